```python
import math
import jax, jax.numpy as jnp
from jax import lax
import numpy as np

D_MODEL = 1024
BATCH = 4
SEQ = 4096
DEPTH = 2
DEC_BATCH = 128
DEC_SEQ = 8
PAST_LEN = 8192
PAGE_SIZE = 128

LRU_WIDTH = 256
LRU_BLOCKS = 4
LRU_BW = LRU_WIDTH // LRU_BLOCKS
CONV_WIDTH = 4
LRU_C = 8.0
MLA_HEADS = 8
Q_RANK = 256
KV_RANK = 128
QK_NOPE = 64
QK_ROPE = 32
V_HEAD = 64
ROPE_THETA = 10000.0
Q_BLOCK = 128
GLA_HEADS = 4
GLA_DK = 32
GLA_DV = 64
GLA_GATE_RANK = 16
GLA_TAU = 16.0
GLA_CHUNK = 16
MLA_WIDTH = MLA_HEADS * V_HEAD
GLA_WIDTH = GLA_HEADS * GLA_DV
MIX_WIDTH = LRU_WIDTH + MLA_WIDTH + GLA_WIDTH
IN_SIZES = (LRU_WIDTH, LRU_WIDTH, Q_RANK, KV_RANK, QK_ROPE,
            GLA_HEADS * GLA_DK, GLA_HEADS * GLA_DK, GLA_WIDTH, GLA_GATE_RANK, GLA_WIDTH)
IN_COLS = sum(IN_SIZES)
N_GROUPS = 4
EXPERTS_PER_GROUP = 8
N_EXPERTS = N_GROUPS * EXPERTS_PER_GROUP
TOP_K = 2
D_EXPERT = 256
EPS = 1e-6

kernel_name = 'hymba_rglru_mla_gla_hmoe_step'

F32 = jnp.float32


def rmsnorm(x, g):
    xf = x.astype(F32)
    y = xf * lax.rsqrt(jnp.mean(xf * xf, axis=-1, keepdims=True) + EPS)
    return (y * g.astype(F32)).astype(x.dtype)


def rope(x, pos):
    half = x.shape[-1] // 2
    inv = ROPE_THETA ** (-jnp.arange(half, dtype=F32) / half)
    ang = pos[:, None] * inv
    shape = (1, pos.shape[0]) + (1,) * (x.ndim - 3) + (half,)
    cos = jnp.cos(ang).reshape(shape)
    sin = jnp.sin(ang).reshape(shape)
    x1, x2 = x[..., :half], x[..., half:]
    return jnp.concatenate([x1 * cos - x2 * sin, x1 * sin + x2 * cos], axis=-1).astype(x.dtype)


def causal_conv(x, buf, w, b):
    S = x.shape[1]
    xp = jnp.concatenate([buf.astype(x.dtype), x], axis=1)
    y = b
    for k in range(CONV_WIDTH):
        y = y + w[k] * xp[:, k:k + S]
    return y, xp[:, -(CONV_WIDTH - 1):]


def rg_lru(x, h0, wa, ba, wi, bi, lam):
    B, S, C = x.shape
    xf = x.astype(F32)
    xb = xf.reshape(B, S, LRU_BLOCKS, LRU_BW)
    r = jax.nn.sigmoid(jnp.einsum('bsnc,ncd->bsnd', xb, wa.astype(F32)).reshape(B, S, C) + ba)
    i = jax.nn.sigmoid(jnp.einsum('bsnc,ncd->bsnd', xb, wi.astype(F32)).reshape(B, S, C) + bi)
    log_a = -LRU_C * r * jax.nn.softplus(-lam.astype(F32))
    a = jnp.exp(log_a)
    u = jnp.sqrt(-jnp.expm1(2.0 * log_a)) * (i * xf)

    def combine(left, right):
        a1, b1 = left
        a2, b2 = right
        return a1 * a2, a2 * b1 + b2

    a_cum, b_cum = lax.associative_scan(combine, (a, u), axis=1)
    h = a_cum * h0.astype(F32)[:, None] + b_cum
    return h, h[:, -1]


def mla_attend(q_abs, q_pe, ckv, kpe, offset):
    B, Sq, H, R = q_abs.shape
    Sk = ckv.shape[1]
    blk = math.gcd(Q_BLOCK, Sq)
    nb = Sq // blk
    scale = (QK_NOPE + QK_ROPE) ** -0.5
    k_pos = jnp.arange(Sk)

    def one_block(args):
        qa, qp, qpos = args
        s = (jnp.einsum('bqhr,bkr->bhqk', qa, ckv) + jnp.einsum('bqhr,bkr->bhqk', qp, kpe)).astype(F32) * scale
        s = jnp.where(k_pos[None, :] <= qpos[:, None], s, -jnp.inf)
        p = jax.nn.softmax(s, axis=-1).astype(ckv.dtype)
        return jnp.einsum('bhqk,bkr->bqhr', p, ckv)

    qa_b = q_abs.reshape(B, nb, blk, H, R).transpose(1, 0, 2, 3, 4)
    qp_b = q_pe.reshape(B, nb, blk, H, QK_ROPE).transpose(1, 0, 2, 3, 4)
    qpos_b = (offset + jnp.arange(Sq)).reshape(nb, blk)
    o = lax.map(one_block, (qa_b, qp_b, qpos_b))
    return o.transpose(1, 0, 2, 3, 4).reshape(B, Sq, H, R)


def gla_scan(q, k, v, log_alpha, s0):
    B, S, H, _ = q.shape
    C = math.gcd(GLA_CHUNK, S)
    N = S // C

    def to_chunks(t):
        return t.astype(F32).reshape(B, N, C, H, t.shape[-1]).transpose(1, 0, 3, 2, 4)

    qc, kc, vc, gc = to_chunks(q), to_chunks(k), to_chunks(v), to_chunks(log_alpha)
    causal = jnp.tril(jnp.ones((C, C), dtype=bool))[:, :, None]

    def step(st, inp):
        qi, ki, vi, gi = inp
        cum = jnp.cumsum(gi, axis=2)
        o_inter = jnp.einsum('bhtd,bhde->bhte', qi * jnp.exp(cum), st)
        diff = cum[:, :, :, None, :] - cum[:, :, None, :, :]
        decay = jnp.exp(jnp.where(causal, diff, -jnp.inf))
        att = jnp.einsum('bhtd,bhsd,bhtsd->bhts', qi, ki, decay)
        o = o_inter + jnp.einsum('bhts,bhse->bhte', att, vi)
        last = cum[:, :, -1:, :]
        k_dec = ki * jnp.exp(last - cum)
        st_new = jnp.exp(last[:, :, 0, :])[..., None] * st + jnp.einsum('bhsd,bhse->bhde', k_dec, vi)
        return st_new, o

    s_fin, o = lax.scan(step, s0.astype(F32), (qc, kc, vc, gc))
    return o.transpose(1, 0, 3, 2, 4).reshape(B, S, H, GLA_DV), s_fin


def token_mixers(hn, past_ckv, past_kpe, conv_buf, lru_h0, gla_s0, lw):
    B, S, _ = hn.shape
    offset = 0 if past_ckv is None else past_ckv.shape[1]
    pos = offset + jnp.arange(S, dtype=F32)
    z = hn @ lw['w_in']
    x_lru, y_lru, c_q, c_kv, k_pe, g_q, g_k, g_v, g_a, g_o = jnp.split(
        z, list(np.cumsum(IN_SIZES)[:-1]), axis=-1)

    xc, conv_new = causal_conv(x_lru, conv_buf, lw['conv_w'], lw['conv_b'])
    h, lru_new = rg_lru(xc, lru_h0, lw['lru_wa'], lw['lru_ba'], lw['lru_wi'], lw['lru_bi'], lw['lru_lambda'])
    out_a = h * jax.nn.gelu(y_lru.astype(F32))

    q = (rmsnorm(c_q, lw['q_norm_g']) @ lw['w_uq']).reshape(B, S, MLA_HEADS, QK_NOPE + QK_ROPE)
    q_nope, q_pe = q[..., :QK_NOPE], rope(q[..., QK_NOPE:], pos)
    q_abs = jnp.einsum('bshn,rhn->bshr', q_nope, lw['w_uk'])
    ckv_new = rmsnorm(c_kv, lw['kv_norm_g'])
    kpe_new = rope(k_pe, pos)
    if past_ckv is None:
        ckv_all, kpe_all = ckv_new, kpe_new
    else:
        ckv_all = jnp.concatenate([past_ckv.astype(ckv_new.dtype), ckv_new], axis=1)
        kpe_all = jnp.concatenate([past_kpe.astype(kpe_new.dtype), kpe_new], axis=1)
    o_lat = mla_attend(q_abs, q_pe, ckv_all, kpe_all, offset)
    out_b = jnp.einsum('bshr,rhv->bshv', o_lat, lw['w_uv']).reshape(B, S, MLA_WIDTH)

    gq = g_q.reshape(B, S, GLA_HEADS, GLA_DK) * (GLA_DK ** -0.5)
    gk = g_k.reshape(B, S, GLA_HEADS, GLA_DK)
    gv = g_v.reshape(B, S, GLA_HEADS, GLA_DV)
    log_alpha = (jax.nn.log_sigmoid((g_a @ lw['gla_wa2'] + lw['gla_ba']).astype(F32)) / GLA_TAU
                 ).reshape(B, S, GLA_HEADS, GLA_DK)
    o_c, gla_new = gla_scan(gq, gk, gv, log_alpha, gla_s0)
    out_c = (rmsnorm(o_c, lw['gla_norm_g']) * jax.nn.silu(g_o.reshape(B, S, GLA_HEADS, GLA_DV).astype(F32))
             ).reshape(B, S, GLA_WIDTH)

    n_a, n_b, n_c = jnp.split(lw['out_norm_g'], [LRU_WIDTH, LRU_WIDTH + MLA_WIDTH])
    merged = jnp.concatenate([rmsnorm(out_a, n_a), rmsnorm(out_b, n_b), rmsnorm(out_c, n_c)],
                             axis=-1).astype(hn.dtype)
    return merged @ lw['w_out'], ckv_new, kpe_new, conv_new, lru_new, gla_new


def hier_moe(h, lw):
    B, S, D = h.shape
    T = B * S
    t = h.reshape(T, D)
    rows = jnp.arange(T)
    g_logits = (t @ lw['router_wg']).astype(F32) + lw['router_bg']
    g_prob = jax.nn.softmax(g_logits, axis=-1)
    g_sel = jnp.argmax(g_logits, axis=-1)
    e_logits = ((t @ lw['router_we']).astype(F32) + lw['router_be']).reshape(T, N_GROUPS, EXPERTS_PER_GROUP)
    e_in = e_logits[rows, g_sel]
    top_v, top_i = lax.top_k(e_in, TOP_K)
    w_sel = jax.nn.softmax(top_v, axis=-1) * g_prob[rows, g_sel][:, None]
    idx = g_sel[:, None] * EXPERTS_PER_GROUP + top_i
    gates = jnp.einsum('tk,tke->te', w_sel, jax.nn.one_hot(idx, N_EXPERTS, dtype=F32))
    hg = jnp.einsum('td,edf->tef', t, lw['w_gate'])
    hu = jnp.einsum('td,edf->tef', t, lw['w_up'])
    a = jax.nn.silu(hg) * hu * gates[:, :, None].astype(hg.dtype)
    y = jnp.einsum('tef,efd->td', a, lw['w_down'])
    return y.reshape(B, S, D).astype(h.dtype)


def decoder(x, cache_ckv, cache_kpe, page_table, conv0, lru0, gla0, final_norm_g, w):
    new_ckv, new_kpe, new_conv, new_lru, new_gla = [], [], [], [], []
    for l in range(DEPTH):
        lw = {name: arr[l] for name, arr in w.items()}
        if page_table is None:
            past_ckv = past_kpe = None
        else:
            nb = page_table.shape[0]
            past_ckv = cache_ckv[l][page_table].reshape(nb, -1, KV_RANK)
            past_kpe = cache_kpe[l][page_table].reshape(nb, -1, QK_ROPE)
        mix, ckv, kpe, conv, lru, gla = token_mixers(rmsnorm(x, lw['norm1_g']), past_ckv, past_kpe,
                                                     conv0[l], lru0[l], gla0[l], lw)
        x = x + mix.astype(x.dtype)
        x = x + hier_moe(rmsnorm(x, lw['norm2_g']), lw)
        new_ckv.append(ckv)
        new_kpe.append(kpe)
        new_conv.append(conv)
        new_lru.append(lru)
        new_gla.append(gla)
    return (rmsnorm(x, final_norm_g), jnp.stack(new_ckv), jnp.stack(new_kpe), jnp.stack(new_conv),
            jnp.stack(new_lru), jnp.stack(new_gla))


def setup_inputs(seed: int = 0) -> dict:
    key = jax.random.key(seed)
    ks = iter(jax.random.split(key, 64))

    def nrm(shape, scale):
        return jax.random.normal(next(ks), shape, F32) * scale

    def gain(shape):
        return 1.0 + nrm(shape, 0.02)

    n_pages = PAST_LEN // PAGE_SIZE
    n_used = DEC_BATCH * n_pages
    n_phys = n_used + n_used // 4
    page_table = jax.random.permutation(next(ks), n_phys)[:n_used].reshape(DEC_BATCH, n_pages).astype(jnp.int32)
    u = jax.random.uniform(next(ks), (DEPTH, LRU_WIDTH), F32, minval=0.9, maxval=0.999)
    s = u ** (1.0 / LRU_C)
    lru_lambda = jnp.log(s) - jnp.log1p(-s)
    return {
        'x_prompt': nrm((BATCH, SEQ, D_MODEL), 1.0),
        'x_sample': nrm((DEC_BATCH, DEC_SEQ, D_MODEL), 1.0),
        'cache_ckv': nrm((DEPTH, n_phys, PAGE_SIZE, KV_RANK), 1.0),
        'cache_kpe': nrm((DEPTH, n_phys, PAGE_SIZE, QK_ROPE), 1.0),
        'page_table': page_table,
        'state_conv': nrm((DEPTH, DEC_BATCH, CONV_WIDTH - 1, LRU_WIDTH), 1.0),
        'state_lru': nrm((DEPTH, DEC_BATCH, LRU_WIDTH), 0.5),
        'state_gla': nrm((DEPTH, DEC_BATCH, GLA_HEADS, GLA_DK, GLA_DV), 0.3),
        'norm1_g': gain((DEPTH, D_MODEL)),
        'w_in': nrm((DEPTH, D_MODEL, IN_COLS), D_MODEL ** -0.5),
        'conv_w': nrm((DEPTH, CONV_WIDTH, LRU_WIDTH), CONV_WIDTH ** -0.5),
        'conv_b': nrm((DEPTH, LRU_WIDTH), 0.05),
        'lru_wa': nrm((DEPTH, LRU_BLOCKS, LRU_BW, LRU_BW), LRU_BW ** -0.5),
        'lru_ba': nrm((DEPTH, LRU_WIDTH), 0.05),
        'lru_wi': nrm((DEPTH, LRU_BLOCKS, LRU_BW, LRU_BW), LRU_BW ** -0.5),
        'lru_bi': nrm((DEPTH, LRU_WIDTH), 0.05),
        'lru_lambda': lru_lambda,
        'q_norm_g': gain((DEPTH, Q_RANK)),
        'w_uq': nrm((DEPTH, Q_RANK, MLA_HEADS * (QK_NOPE + QK_ROPE)), Q_RANK ** -0.5),
        'kv_norm_g': gain((DEPTH, KV_RANK)),
        'w_uk': nrm((DEPTH, KV_RANK, MLA_HEADS, QK_NOPE), KV_RANK ** -0.5),
        'w_uv': nrm((DEPTH, KV_RANK, MLA_HEADS, V_HEAD), KV_RANK ** -0.5),
        'gla_wa2': nrm((DEPTH, GLA_GATE_RANK, GLA_HEADS * GLA_DK), GLA_GATE_RANK ** -0.5),
        'gla_ba': nrm((DEPTH, GLA_HEADS * GLA_DK), 0.1),
        'gla_norm_g': gain((DEPTH, GLA_DV)),
        'out_norm_g': gain((DEPTH, MIX_WIDTH)),
        'w_out': nrm((DEPTH, MIX_WIDTH, D_MODEL), MIX_WIDTH ** -0.5),
        'norm2_g': gain((DEPTH, D_MODEL)),
        'router_wg': nrm((DEPTH, D_MODEL, N_GROUPS), D_MODEL ** -0.5),
        'router_bg': nrm((DEPTH, N_GROUPS), 0.01),
        'router_we': nrm((DEPTH, D_MODEL, N_EXPERTS), D_MODEL ** -0.5),
        'router_be': nrm((DEPTH, N_EXPERTS), 0.01),
        'w_gate': nrm((DEPTH, N_EXPERTS, D_MODEL, D_EXPERT), D_MODEL ** -0.5),
        'w_up': nrm((DEPTH, N_EXPERTS, D_MODEL, D_EXPERT), D_MODEL ** -0.5),
        'w_down': nrm((DEPTH, N_EXPERTS, D_EXPERT, D_MODEL), D_EXPERT ** -0.5),
        'final_norm_g': gain((D_MODEL,)),
    }


def reference(x_prompt, x_sample, cache_ckv, cache_kpe, page_table, state_conv, state_lru, state_gla,
              norm1_g, w_in, conv_w, conv_b, lru_wa, lru_ba, lru_wi, lru_bi, lru_lambda,
              q_norm_g, w_uq, kv_norm_g, w_uk, w_uv, gla_wa2, gla_ba, gla_norm_g, out_norm_g, w_out,
              norm2_g, router_wg, router_bg, router_we, router_be, w_gate, w_up, w_down, final_norm_g):
    w = dict(norm1_g=norm1_g, w_in=w_in, conv_w=conv_w, conv_b=conv_b, lru_wa=lru_wa, lru_ba=lru_ba,
             lru_wi=lru_wi, lru_bi=lru_bi, lru_lambda=lru_lambda, q_norm_g=q_norm_g, w_uq=w_uq,
             kv_norm_g=kv_norm_g, w_uk=w_uk, w_uv=w_uv, gla_wa2=gla_wa2, gla_ba=gla_ba,
             gla_norm_g=gla_norm_g, out_norm_g=out_norm_g, w_out=w_out, norm2_g=norm2_g,
             router_wg=router_wg, router_bg=router_bg, router_we=router_we, router_be=router_be,
             w_gate=w_gate, w_up=w_up, w_down=w_down)
    bp = x_prompt.shape[0]
    conv0 = jnp.zeros((DEPTH, bp, CONV_WIDTH - 1, LRU_WIDTH), x_prompt.dtype)
    lru0 = jnp.zeros((DEPTH, bp, LRU_WIDTH), F32)
    gla0 = jnp.zeros((DEPTH, bp, GLA_HEADS, GLA_DK, GLA_DV), F32)
    y_prompt, ckv_p, kpe_p, conv_p, lru_p, gla_p = decoder(
        x_prompt, None, None, None, conv0, lru0, gla0, final_norm_g, w)
    y_sample, ckv_s, kpe_s, conv_s, lru_s, gla_s = decoder(
        x_sample, cache_ckv, cache_kpe, page_table, state_conv, state_lru, state_gla, final_norm_g, w)
    return (y_prompt, y_sample, ckv_p, kpe_p, ckv_s, kpe_s, conv_p, conv_s, lru_p, lru_s, gla_p, gla_s)
```

```python
import functools
import math

import numpy as np
import jax
import jax.numpy as jnp
from jax import lax
from jax.experimental import pallas as pl
from jax.experimental.pallas import tpu as pltpu

F32 = jnp.float32
BF16 = jnp.bfloat16

LRU_BLOCKS = 4
CONV_WIDTH = 4
LRU_C = 8.0
MLA_HEADS = 8
QK_NOPE = 64
QK_ROPE = 32
V_HEAD = 64
ROPE_THETA = 10000.0
GLA_HEADS = 4
GLA_DK = 32
GLA_DV = 64
GLA_TAU = 16.0
GLA_CHUNK = 16
N_GROUPS = 4
EXPERTS_PER_GROUP = 8
N_EXPERTS = N_GROUPS * EXPERTS_PER_GROUP
TOP_K = 2
EPS = 1e-6

LANES = 128
SUBLANES = 8
VMEM_LIMIT_BYTES = 56 * 1024 * 1024

TOKEN_TILE = 256
SEQ_TILE = 256
MOE_TILE = 256
PAGES_PER_STEP = 8
NEG_BIG = -1e30


def _cparams(*sem):
    return pltpu.CompilerParams(dimension_semantics=sem, vmem_limit_bytes=VMEM_LIMIT_BYTES)


def _rms(x, g):
    return x * lax.rsqrt(jnp.mean(x * x, axis=-1, keepdims=True) + EPS) * g


def _dot(a, b):
    return jnp.dot(a, b, preferred_element_type=F32)


def _dot_nt(a, b):
    return lax.dot_general(a, b, (((1,), (1,)), ((), ())), preferred_element_type=F32)


def _softplus(x):
    return jnp.maximum(x, 0.0) + jnp.log1p(jnp.exp(-jnp.abs(x)))


def _shift_rows(x, d, fill):
    row = lax.broadcasted_iota(jnp.int32, x.shape, 0)
    return jnp.where(row >= d, pltpu.roll(x, d, 0), fill)


def _pre_kernel(x_ref, g1_ref, win_ref, qg_ref, wq_ref, wabs_ref, kvg_ref, wa2_ref, ba_ref,
                cos_ref, sin_ref, zl_ref, q_ref, k_ref, ckv_ref, kpe_ref, g_ref, *, dims):
    lw, qr, kvr, gk_w, gv_w = dims
    xn = _rms(x_ref[...], g1_ref[...])
    z = _dot(xn.astype(BF16), win_ref[...])
    o = 2 * lw
    zl_ref[...] = z[:, :o]
    cq = z[:, o:o + qr]
    o += qr
    ckv = z[:, o:o + kvr]
    o += kvr
    gqk = z[:, o:o + 2 * gk_w]
    o += 2 * gk_w
    gvo = z[:, o:o + 2 * gv_w]
    o += 2 * gv_w
    tail = z[:, o:o + LANES]
    cos = cos_ref[...]
    sin = sin_ref[...]
    scale = (QK_NOPE + QK_ROPE) ** -0.5

    ckv_n = _rms(ckv, kvg_ref[...])
    kpe = tail * cos + pltpu.roll(tail, LANES - QK_ROPE, 1) * sin
    ckv_ref[...] = ckv_n
    kpe_ref[...] = kpe[:, :QK_ROPE]
    k_ref[:, :kvr] = ckv_n.astype(BF16)
    k_ref[:, kvr:] = kpe.astype(BF16)

    cqn = _rms(cq, qg_ref[...]).astype(BF16)
    qall = _dot(cqn, wq_ref[...])
    n_nope = MLA_HEADS * QK_NOPE
    q_abs = _dot(qall[:, :n_nope].astype(BF16), wabs_ref[...]) * scale
    for h in range(MLA_HEADS):
        pe = qall[:, n_nope + h * LANES:n_nope + (h + 1) * LANES]
        sw = qall[:, n_nope + (MLA_HEADS + h) * LANES:n_nope + (MLA_HEADS + h + 1) * LANES]
        q_ref[:, 2 * h * LANES:(2 * h + 1) * LANES] = q_abs[:, h * kvr:(h + 1) * kvr].astype(BF16)
        q_ref[:, (2 * h + 1) * LANES:(2 * h + 2) * LANES] = ((pe * cos + sw * sin) * scale).astype(BF16)

    la_pre = _dot(tail.astype(BF16), wa2_ref[...]) + ba_ref[...]
    la = -_softplus(-la_pre) * (1.0 / GLA_TAU)
    g_ref[:, :gk_w] = gqk[:, :gk_w] * (GLA_DK ** -0.5)
    g_ref[:, gk_w:2 * gk_w] = gqk[:, gk_w:]
    g_ref[:, 2 * gk_w:3 * gk_w] = la
    g_ref[:, 3 * gk_w:] = gvo


def _pre_call(x, lw_, cos_t, sin_t, dims):
    T, D = x.shape
    lw, qr, kvr, gk_w, gv_w = dims
    tm = TOKEN_TILE
    assert T % tm == 0
    row = lambda i: (i, 0)
    full = lambda i: (0, 0)
    wspec = lambda a: pl.BlockSpec(a.shape, full)
    ins = [x, lw_['g1'], lw_['w_in'], lw_['q_g'], lw_['w_q'], lw_['w_abs'], lw_['kv_g'], lw_['wa2'], lw_['ba']]
    in_specs = [pl.BlockSpec((tm, D), row)] + [wspec(a) for a in ins[1:]]
    in_specs += [pl.BlockSpec((tm, LANES), row), pl.BlockSpec((tm, LANES), row)]
    out_shape = (
        jax.ShapeDtypeStruct((T, 2 * lw), F32),
        jax.ShapeDtypeStruct((T, 2 * LANES * MLA_HEADS), BF16),
        jax.ShapeDtypeStruct((T, 2 * LANES), BF16),
        jax.ShapeDtypeStruct((T, kvr), F32),
        jax.ShapeDtypeStruct((T, QK_ROPE), F32),
        jax.ShapeDtypeStruct((T, 3 * gk_w + 2 * gv_w), F32),
    )
    out_specs = tuple(pl.BlockSpec((tm, s.shape[1]), row) for s in out_shape)
    return pl.pallas_call(
        functools.partial(_pre_kernel, dims=dims),
        grid=(T // tm,), in_specs=in_specs, out_specs=out_specs, out_shape=out_shape,
        compiler_params=_cparams("parallel"), name="pre_proj",
    )(*ins, cos_t, sin_t)


def _lru_kernel(zl_ref, cbuf_ref, h0_ref, cw_ref, cb_ref, wa_ref, ba_ref, wi_ref, bi_ref, lam_ref, na_ref,
                a_ref, conv_ref, hout_ref, xbuf, hcar, *, ts, lw):
    i = pl.program_id(1)
    last = pl.num_programs(1) - 1
    pad = SUBLANES
    nbuf = CONV_WIDTH - 1

    @pl.when(i == 0)
    def _():
        xbuf[0:pad, :] = jnp.zeros((pad, lw), F32)
        xbuf[pad - nbuf:pad, :] = cbuf_ref[...]
        hcar[...] = h0_ref[...]

    x = zl_ref[:, :lw]
    y = zl_ref[:, lw:]
    xbuf[pad:pad + ts, :] = x
    xc = cb_ref[...] + cw_ref[nbuf:nbuf + 1, :] * x
    for k in range(nbuf):
        xc = xc + cw_ref[k:k + 1, :] * xbuf[pad - nbuf + k:pad - nbuf + k + ts, :]

    @pl.when(i == last)
    def _():
        conv_ref[...] = xbuf[pad + ts - nbuf:pad + ts, :]

    xbuf[0:pad, :] = xbuf[ts:ts + pad, :]

    xb = xc.astype(BF16)
    r = jax.nn.sigmoid(_dot(xb, wa_ref[...]) + ba_ref[...])
    gi = jax.nn.sigmoid(_dot(xb, wi_ref[...]) + bi_ref[...])
    log_a = (-LRU_C) * r * _softplus(-lam_ref[...])
    a = jnp.exp(log_a)
    th = jnp.tanh(log_a)
    u = jnp.sqrt(-2.0 * th / (1.0 - th)) * (gi * xc)

    d = 1
    while d < ts:
        u = a * _shift_rows(u, d, 0.0) + u
        a = a * _shift_rows(a, d, 1.0)
        d *= 2
    h = a * hcar[...] + u
    hcar[...] = h[ts - 1:ts, :]

    @pl.when(i == last)
    def _():
        hout_ref[...] = h[ts - 1:ts, :]

    out_a = h * jax.nn.gelu(y)
    a_ref[...] = _rms(out_a, na_ref[...]).astype(a_ref.dtype)


def _lru_call(zl, row_off, B, S, cbuf, h0, lw_, out_dtype):
    lw = h0.shape[-1]
    ts = min(S, SEQ_TILE)
    assert S % ts == 0 and row_off % ts == 0 and S >= CONV_WIDTH - 1
    n = S // ts
    off = row_off // ts
    full = lambda b, i: (0, 0)
    ws = [lw_['conv_w'], lw_['conv_b'], lw_['lru_wa'], lw_['lru_ba'], lw_['lru_wi'], lw_['lru_bi'],
          lw_['lru_lam'], lw_['n_a']]
    in_specs = [
        pl.BlockSpec((ts, 2 * lw), lambda b, i: (off + b * n + i, 0)),
        pl.BlockSpec((None, CONV_WIDTH - 1, lw), lambda b, i: (b, 0, 0)),
        pl.BlockSpec((None, 1, lw), lambda b, i: (b, 0, 0)),
    ] + [pl.BlockSpec(a.shape, full) for a in ws]
    out_shape = (
        jax.ShapeDtypeStruct((B * S, lw), out_dtype),
        jax.ShapeDtypeStruct((B, CONV_WIDTH - 1, lw), F32),
        jax.ShapeDtypeStruct((B, 1, lw), F32),
    )
    out_specs = (
        pl.BlockSpec((ts, lw), lambda b, i: (b * n + i, 0)),
        pl.BlockSpec((None, CONV_WIDTH - 1, lw), lambda b, i: (b, 0, 0)),
        pl.BlockSpec((None, 1, lw), lambda b, i: (b, 0, 0)),
    )
    return pl.pallas_call(
        functools.partial(_lru_kernel, ts=ts, lw=lw),
        grid=(B, n), in_specs=in_specs, out_specs=out_specs, out_shape=out_shape,
        scratch_shapes=[pltpu.VMEM((ts + SUBLANES, lw), F32), pltpu.VMEM((1, lw), F32)],
        compiler_params=_cparams("parallel", "arbitrary"), name="rg_lru",
    )(zl, cbuf, h0.reshape(B, 1, lw), *ws)


def _gla_kernel(g_ref, s0_ref, gn_ref, nc_ref, c_ref, sout_ref, st, obuf, *, ts, gk_w, gv_w):
    i = pl.program_id(1)
    last = pl.num_programs(1) - 1
    C = GLA_CHUNK
    rows = min(ts, C)
    n_chunks = max(ts // C, 1)
    log2c = int(math.log2(C))

    @pl.when(i == 0)
    def _():
        st[...] = s0_ref[...]

    dk_sh, dv_sh = int(math.log2(GLA_DK)), int(math.log2(GLA_DV))
    hd = lax.broadcasted_iota(jnp.int32, (gk_w, gv_w), 0) >> dk_sh
    he = lax.broadcasted_iota(jnp.int32, (gk_w, gv_w), 1) >> dv_sh
    same = (hd == he).astype(BF16)
    he_t = lax.broadcasted_iota(jnp.int32, (gv_w, gk_w), 0) >> dv_sh
    hd_t = lax.broadcasted_iota(jnp.int32, (gv_w, gk_w), 1) >> dk_sh
    same_t = (he_t == hd_t).astype(F32)
    sel_t = lax.broadcasted_iota(jnp.int32, (C, C * C), 0)
    sel_r = lax.broadcasted_iota(jnp.int32, (C, C * C), 1) >> log2c
    sel = (sel_t == sel_r).astype(BF16)
    srow = lax.broadcasted_iota(jnp.int32, (C, 1), 0)

    def chunk(c, carry):
        r0 = pl.multiple_of(c * rows, rows)
        blk = g_ref[pl.ds(r0, rows), :]
        if rows < C:
            blk = jnp.concatenate([blk, jnp.zeros((C - rows, blk.shape[1]), F32)], axis=0)
        q = blk[:, :gk_w]
        k = blk[:, gk_w:2 * gk_w]
        la = blk[:, 2 * gk_w:3 * gk_w]
        v = blk[:, 3 * gk_w:3 * gk_w + gv_w]
        cum = la
        d = 1
        while d < C:
            cum = cum + _shift_rows(cum, d, 0.0)
            d *= 2
        s_t = st[...]
        o_inter = _dot_nt((q * jnp.exp(cum)).astype(BF16), s_t.astype(BF16))
        pieces = []
        for t in range(C):
            diff = jnp.where(srow <= t, cum[t:t + 1, :] - cum, NEG_BIG)
            pieces.append(q[t:t + 1, :] * k * jnp.exp(diff))
        w = jnp.concatenate(pieces, axis=0)
        att = _dot(w.astype(BF16), same)
        xv = att * jnp.concatenate([v] * C, axis=0)
        o = o_inter + _dot(sel, xv.astype(BF16))
        obuf[pl.ds(r0, rows), :] = o[:rows, :]
        cl = cum[C - 1:C, :]
        kdec = k * jnp.exp(cl - cum)
        upd = lax.dot_general(v.astype(BF16), kdec.astype(BF16), (((0,), (0,)), ((), ())),
                              preferred_element_type=F32)
        st[...] = s_t * jnp.exp(cl) + upd * same_t
        return carry

    lax.fori_loop(0, n_chunks, chunk, 0)

    @pl.when(i == last)
    def _():
        sout_ref[...] = st[...]

    o = obuf[...]
    go = g_ref[:, 3 * gk_w + gv_w:]
    e64 = (lax.broadcasted_iota(jnp.int32, (gv_w, gv_w), 0) >> dv_sh
           == lax.broadcasted_iota(jnp.int32, (gv_w, gv_w), 1) >> dv_sh).astype(BF16)
    osq = o * o
    osq_hi = osq.astype(BF16)
    osq_lo = (osq - osq_hi.astype(F32)).astype(BF16)
    ms = (_dot(osq_hi, e64) + _dot(osq_lo, e64)) * (1.0 / GLA_DV)
    out_c = o * lax.rsqrt(ms + EPS) * gn_ref[...] * (go * jax.nn.sigmoid(go))
    c_ref[...] = _rms(out_c, nc_ref[...]).astype(c_ref.dtype)


def _gla_call(g, row_off, B, S, s0t, lw_, out_dtype, gk_w, gv_w):
    ts = min(S, SEQ_TILE)
    assert S % ts == 0 and row_off % ts == 0 and (ts % GLA_CHUNK == 0 or ts < GLA_CHUNK)
    n = S // ts
    off = row_off // ts
    gw = g.shape[1]
    full = lambda b, i: (0, 0)
    out_shape = (jax.ShapeDtypeStruct((B * S, gv_w), out_dtype),
                 jax.ShapeDtypeStruct((B, gv_w, gk_w), F32))
    return pl.pallas_call(
        functools.partial(_gla_kernel, ts=ts, gk_w=gk_w, gv_w=gv_w),
        grid=(B, n),
        in_specs=[pl.BlockSpec((ts, gw), lambda b, i: (off + b * n + i, 0)),
                  pl.BlockSpec((None, gv_w, gk_w), lambda b, i: (b, 0, 0)),
                  pl.BlockSpec((1, gv_w), full), pl.BlockSpec((1, gv_w), full)],
        out_specs=(pl.BlockSpec((ts, gv_w), lambda b, i: (b * n + i, 0)),
                   pl.BlockSpec((None, gv_w, gk_w), lambda b, i: (b, 0, 0))),
        out_shape=out_shape,
        scratch_shapes=[pltpu.VMEM((gv_w, gk_w), F32), pltpu.VMEM((ts, gv_w), F32)],
        compiler_params=_cparams("parallel", "arbitrary"), name="gla",
    )(g, s0t, lw_['gla_g'], lw_['n_c'])


def _online_update(carry, s, v):
    m, l, acc = carry
    m_new = jnp.maximum(m, jnp.max(s, axis=-1, keepdims=True))
    alpha = jnp.exp(m - m_new)
    p = jnp.exp(s - m_new)
    l = alpha * l + jnp.sum(p, axis=-1, keepdims=True)
    acc = alpha * acc + _dot(p.astype(BF16), v)
    return m_new, l, acc


def _attn_prompt_kernel(q_ref, k_ref, wuv_ref, nb_ref, o_ref, *, tq, kvr):
    i = pl.program_id(1)
    row = lax.broadcasted_iota(jnp.int32, (tq, tq), 0)
    col = lax.broadcasted_iota(jnp.int32, (tq, tq), 1)
    causal = col <= row
    out = jnp.zeros((tq, wuv_ref.shape[1]), F32)
    for h in range(MLA_HEADS):
        q = q_ref[:, 2 * h * LANES:(2 * h + 2) * LANES]

        def body(j, carry, q=q):
            kb = k_ref[pl.ds(pl.multiple_of(j * tq, tq), tq), :]
            return _online_update(carry, _dot_nt(q, kb), kb[:, :kvr])

        init = (jnp.full((tq, 1), NEG_BIG, F32), jnp.zeros((tq, 1), F32), jnp.zeros((tq, kvr), F32))
        carry = lax.fori_loop(0, i, body, init)
        kb = k_ref[pl.ds(pl.multiple_of(i * tq, tq), tq), :]
        s = jnp.where(causal, _dot_nt(q, kb), NEG_BIG)
        _, l, acc = _online_update(carry, s, kb[:, :kvr])
        o_h = (acc / l).astype(BF16)
        out = out + _dot(o_h, wuv_ref[h * kvr:(h + 1) * kvr, :])
    o_ref[...] = _rms(out, nb_ref[...]).astype(o_ref.dtype)


def _attn_prompt_call(qcat, kcat, B, S, lw_, kvr):
    tq = min(S, SEQ_TILE)
    assert S % tq == 0
    n = S // tq
    mw = lw_['w_uv_rows'].shape[1]
    full = lambda b, i: (0, 0)
    return pl.pallas_call(
        functools.partial(_attn_prompt_kernel, tq=tq, kvr=kvr),
        grid=(B, n),
        in_specs=[pl.BlockSpec((tq, qcat.shape[1]), lambda b, i: (b * n + i, 0)),
                  pl.BlockSpec((S, kcat.shape[1]), lambda b, i: (b, 0)),
                  pl.BlockSpec(lw_['w_uv_rows'].shape, full), pl.BlockSpec((1, mw), full)],
        out_specs=pl.BlockSpec((tq, mw), lambda b, i: (b * n + i, 0)),
        out_shape=jax.ShapeDtypeStruct((B * S, mw), BF16),
        compiler_params=_cparams("parallel", "arbitrary"), name="attn_prompt",
    )(qcat, kcat, lw_['w_uv_rows'], lw_['n_b'])


def _attn_sample_kernel(pt_ref, q_ref, nckv_ref, nkpe_ref, *rest, n_pp, sq, kvr, page):
    ckv_refs = rest[:n_pp]
    kpe_refs = rest[n_pp:2 * n_pp]
    wuv_ref, nb_ref, o_ref, q_scr, m_scr, l_scr, acc_scr = rest[2 * n_pp:]
    g = pl.program_id(1)
    last = pl.num_programs(1) - 1
    nrow = MLA_HEADS * sq

    @pl.when(g == 0)
    def _():
        for h in range(MLA_HEADS):
            q_scr[h * sq:(h + 1) * sq, :] = q_ref[:, 2 * h * LANES:(2 * h + 2) * LANES].astype(F32)
        m_scr[...] = jnp.full((nrow, 1), NEG_BIG, F32)
        l_scr[...] = jnp.zeros((nrow, 1), F32)
        acc_scr[...] = jnp.zeros((nrow, kvr), F32)

    qa = q_scr[:, :kvr].astype(BF16)
    qp = q_scr[:, kvr:kvr + QK_ROPE].astype(BF16)
    carry = (m_scr[...], l_scr[...], acc_scr[...])
    for j in range(n_pp):
        ck = ckv_refs[j][...].astype(BF16)
        kp = kpe_refs[j][...].astype(BF16)
        carry = _online_update(carry, _dot_nt(qa, ck) + _dot_nt(qp, kp), ck)
    m_scr[...], l_scr[...], acc_scr[...] = carry

    @pl.when(g == last)
    def _():
        zpad = lambda a: jnp.concatenate([a, jnp.zeros((page - sq, a.shape[1]), F32)], axis=0)
        ck = zpad(nckv_ref[...]).astype(BF16)
        kp = zpad(nkpe_ref[...]).astype(BF16)
        tok = lax.broadcasted_iota(jnp.int32, (nrow, page), 0) & (sq - 1)
        key = lax.broadcasted_iota(jnp.int32, (nrow, page), 1)
        s = jnp.where(key <= tok, _dot_nt(qa, ck) + _dot_nt(qp, kp), NEG_BIG)
        _, l, acc = _online_update((m_scr[...], l_scr[...], acc_scr[...]), s, ck)
        o_lat = (acc / l).astype(BF16)
        mw = wuv_ref.shape[1]
        r = _dot(o_lat, wuv_ref[...])
        rh = lax.broadcasted_iota(jnp.int32, (nrow, mw), 0) >> int(math.log2(sq))
        ch = lax.broadcasted_iota(jnp.int32, (nrow, mw), 1) >> int(math.log2(V_HEAD))
        r = jnp.where(rh == ch, r, 0.0)
        out = r[0:sq, :]
        for h in range(1, MLA_HEADS):
            out = out + r[h * sq:(h + 1) * sq, :]
        o_ref[...] = _rms(out, nb_ref[...])


def _attn_sample_call(page_table, qs, nckv, nkpe, cache_ckv, cache_kpe, layer, lw_, kvr):
    B, sq, qw = qs.shape
    n_pages = page_table.shape[1]
    page = cache_ckv.shape[2]
    n_pp = math.gcd(PAGES_PER_STEP, n_pages)
    n_steps = n_pages // n_pp
    mw = lw_['w_uv_cat'].shape[1]
    nrow = MLA_HEADS * sq
    assert sq == SUBLANES and sq <= page

    def page_spec(width, j):
        return pl.BlockSpec((None, None, page, width),
                            lambda b, g, pt: (layer, pt[b * n_pages + g * n_pp + j], 0, 0))

    in_specs = [pl.BlockSpec((None, sq, qw), lambda b, g, pt: (b, 0, 0)),
                pl.BlockSpec((None, sq, kvr), lambda b, g, pt: (b, 0, 0)),
                pl.BlockSpec((None, sq, QK_ROPE), lambda b, g, pt: (b, 0, 0))]
    in_specs += [page_spec(kvr, j) for j in range(n_pp)]
    in_specs += [page_spec(QK_ROPE, j) for j in range(n_pp)]
    in_specs += [pl.BlockSpec(lw_['w_uv_cat'].shape, lambda b, g, pt: (0, 0)),
                 pl.BlockSpec((1, mw), lambda b, g, pt: (0, 0))]
    grid_spec = pltpu.PrefetchScalarGridSpec(
        num_scalar_prefetch=1, grid=(B, n_steps), in_specs=in_specs,
        out_specs=pl.BlockSpec((None, sq, mw), lambda b, g, pt: (b, 0, 0)),
        scratch_shapes=[pltpu.VMEM((nrow, 2 * LANES), F32), pltpu.VMEM((nrow, 1), F32),
                        pltpu.VMEM((nrow, 1), F32), pltpu.VMEM((nrow, kvr), F32)])
    return pl.pallas_call(
        functools.partial(_attn_sample_kernel, n_pp=n_pp, sq=sq, kvr=kvr, page=page),
        grid_spec=grid_spec, out_shape=jax.ShapeDtypeStruct((B, sq, mw), F32),
        compiler_params=_cparams("parallel", "arbitrary"), name="attn_sample",
    )(page_table.reshape(-1), qs, nckv, nkpe, *([cache_ckv] * n_pp), *([cache_kpe] * n_pp),
      lw_['w_uv_cat'], lw_['n_b'])


def _post_kernel(a_ref, b_ref, c_ref, x_ref, wo_ref, g2_ref, wr_ref, br_ref, x1_ref, hn_ref, rt_ref, *, widths):
    wa, wb, wc = widths
    mix = _dot(a_ref[...], wo_ref[:wa, :])
    mix = mix + _dot(b_ref[...], wo_ref[wa:wa + wb, :])
    mix = mix + _dot(c_ref[...], wo_ref[wa + wb:, :])
    x1 = x_ref[...] + mix
    x1_ref[...] = x1
    hn = _rms(x1, g2_ref[...])
    hn_ref[...] = hn.astype(BF16)
    logits = jnp.dot(hn, wr_ref[...], precision=lax.Precision.HIGHEST, preferred_element_type=F32) + br_ref[...]

    lane = lax.broadcasted_iota(jnp.int32, logits.shape, 1)
    lane_f = lane.astype(F32)
    big = float(LANES)
    is_g = lane < N_GROUPS
    gl = jnp.where(is_g, logits, NEG_BIG)
    gmax = jnp.max(gl, axis=-1, keepdims=True)
    gsel = jnp.min(jnp.where(gl == gmax, lane_f, big), axis=-1, keepdims=True)
    gprob = 1.0 / jnp.sum(jnp.where(is_g, jnp.exp(gl - gmax), 0.0), axis=-1, keepdims=True)
    lo = N_GROUPS + gsel * EXPERTS_PER_GROUP
    el = jnp.where((lane_f >= lo) & (lane_f < lo + EXPERTS_PER_GROUP), logits, NEG_BIG)
    v1 = jnp.max(el, axis=-1, keepdims=True)
    i1 = jnp.min(jnp.where(el == v1, lane_f, big), axis=-1, keepdims=True)
    el2 = jnp.where(lane_f == i1, NEG_BIG, el)
    v2 = jnp.max(el2, axis=-1, keepdims=True)
    i2 = jnp.min(jnp.where(el2 == v2, lane_f, big), axis=-1, keepdims=True)
    e21 = jnp.exp(v2 - v1)
    w1 = gprob / (1.0 + e21)
    w2 = w1 * e21
    rt = jnp.where(lane == 0, i1 - N_GROUPS, 0.0)
    rt = jnp.where(lane == 1, i2 - N_GROUPS, rt)
    rt = jnp.where(lane == 2, w1, rt)
    rt = jnp.where(lane == 3, w2, rt)
    rt_ref[...] = rt


def _post_call(a, b, c, x, lw_):
    T, D = x.shape
    tm = TOKEN_TILE
    row = lambda i: (i, 0)
    full = lambda i: (0, 0)
    widths = (a.shape[1], b.shape[1], c.shape[1])
    ws = [lw_['w_out'], lw_['g2'], lw_['w_r'], lw_['b_r']]
    out_shape = (jax.ShapeDtypeStruct((T, D), F32), jax.ShapeDtypeStruct((T, D), BF16),
                 jax.ShapeDtypeStruct((T, LANES), F32))
    return pl.pallas_call(
        functools.partial(_post_kernel, widths=widths),
        grid=(T // tm,),
        in_specs=[pl.BlockSpec((tm, w), row) for w in widths] + [pl.BlockSpec((tm, D), row)]
        + [pl.BlockSpec(w.shape, full) for w in ws],
        out_specs=tuple(pl.BlockSpec((tm, s.shape[1]), row) for s in out_shape),
        out_shape=out_shape, compiler_params=_cparams("parallel"), name="post_proj",
    )(a, b, c, x, *ws)


def _moe_kernel(te_ref, nt_ref, xs_ref, w_ref, wg_ref, wu_ref, wd_ref, y_ref):
    i = pl.program_id(0)

    @pl.when(i < nt_ref[0])
    def _():
        x = xs_ref[...]
        hg = _dot(x, wg_ref[...])
        hu = _dot(x, wu_ref[...])
        act = hg * jax.nn.sigmoid(hg) * hu * w_ref[...]
        y_ref[...] = _dot(act.astype(BF16), wd_ref[...])

    @pl.when(i >= nt_ref[0])
    def _():
        y_ref[...] = jnp.zeros(y_ref.shape, F32)


def _moe_call(xs, row_w, tile_e, n_used, w_gate, w_up, w_down, layer):
    R, D = xs.shape
    tm = MOE_TILE
    F = w_gate.shape[-1]
    grid_spec = pltpu.PrefetchScalarGridSpec(
        num_scalar_prefetch=2, grid=(R // tm,),
        in_specs=[pl.BlockSpec((tm, D), lambda i, te, nt: (i, 0)),
                  pl.BlockSpec((tm, 1), lambda i, te, nt: (i, 0)),
                  pl.BlockSpec((None, None, D, F), lambda i, te, nt: (layer, te[i], 0, 0)),
                  pl.BlockSpec((None, None, D, F), lambda i, te, nt: (layer, te[i], 0, 0)),
                  pl.BlockSpec((None, None, F, D), lambda i, te, nt: (layer, te[i], 0, 0))],
        out_specs=pl.BlockSpec((tm, D), lambda i, te, nt: (i, 0)))
    return pl.pallas_call(
        _moe_kernel, grid_spec=grid_spec, out_shape=jax.ShapeDtypeStruct((R, D), F32),
        compiler_params=_cparams("arbitrary"), name="moe_experts",
    )(tile_e, n_used, xs, row_w, w_gate, w_up, w_down)


def _route_meta(route, tm, n_tiles):
    T = route.shape[0]
    n_assign = TOP_K * T
    e = route[:, :TOP_K].astype(jnp.int32).reshape(-1)
    w = route[:, TOP_K:2 * TOP_K].reshape(-1)
    order = jnp.argsort(e, stable=True).astype(jnp.int32)
    se = e[order]
    start = jnp.searchsorted(se, jnp.arange(N_EXPERTS + 1, dtype=jnp.int32), side='left').astype(jnp.int32)
    counts = start[1:] - start[:-1]
    pstart = jnp.concatenate([jnp.zeros((1,), jnp.int32), jnp.cumsum(((counts + tm - 1) // tm) * tm)]).astype(jnp.int32)
    dest = pstart[se] + jnp.arange(n_assign, dtype=jnp.int32) - start[se]
    R = n_tiles * tm
    row_tok = jnp.zeros((R,), jnp.int32).at[dest].set(order // TOP_K)
    row_w = jnp.zeros((R,), F32).at[dest].set(w[order])
    pos = jnp.zeros((n_assign,), jnp.int32).at[order].set(dest)
    n_used = (pstart[N_EXPERTS] // tm).reshape(1)
    tile_e = jnp.searchsorted(pstart[1:], jnp.arange(n_tiles, dtype=jnp.int32) * tm, side='right')
    tile_e = jnp.minimum(tile_e, N_EXPERTS - 1).astype(jnp.int32)
    return row_tok, row_w.reshape(R, 1), pos.reshape(T, TOP_K), tile_e, n_used


def _final_kernel(x_ref, g_ref, o_ref):
    o_ref[...] = _rms(x_ref[...], g_ref[...])


def _final_call(x, g):
    T, D = x.shape
    tm = TOKEN_TILE
    return pl.pallas_call(
        _final_kernel, grid=(T // tm,),
        in_specs=[pl.BlockSpec((tm, D), lambda i: (i, 0)), pl.BlockSpec((1, D), lambda i: (0, 0))],
        out_specs=pl.BlockSpec((tm, D), lambda i: (i, 0)),
        out_shape=jax.ShapeDtypeStruct((T, D), F32), compiler_params=_cparams("parallel"), name="final_norm",
    )(x, g)


def _block_diag(w):
    n, c, d = w.shape
    return jnp.einsum('ncd,nm->ncmd', w, jnp.eye(n, dtype=w.dtype)).reshape(n * c, n * d)


def _half_swap(w):
    half = w.shape[-1] // 2
    return jnp.concatenate([w[..., half:], w[..., :half]], axis=-1)


def _layer_weights(l, p, dims):
    lw, qr, kvr, gk_w, gv_w = dims
    D = p['w_in'].shape[1]
    row = lambda a: a.reshape(1, -1).astype(F32)
    w_in = p['w_in'][l]
    sizes = (lw, lw, qr, kvr, QK_ROPE, gk_w, gk_w, gv_w, p['gla_wa2'].shape[1], gv_w)
    x_lru, y_lru, c_q, c_kv, k_pe, g_q, g_k, g_v, g_a, g_o = jnp.split(w_in, list(np.cumsum(sizes)[:-1]), axis=1)
    tail_pad = LANES - 2 * QK_ROPE - g_a.shape[1]
    tail = jnp.concatenate([k_pe, _half_swap(k_pe), g_a, jnp.zeros((D, tail_pad), F32)], axis=1)
    w_in_p = jnp.concatenate([x_lru, y_lru, c_q, c_kv, g_q, g_k, g_v, g_o, tail], axis=1).astype(BF16)

    w_uq = p['w_uq'][l].reshape(qr, MLA_HEADS, QK_NOPE + QK_ROPE)
    nope = w_uq[:, :, :QK_NOPE].reshape(qr, MLA_HEADS * QK_NOPE)
    pe = w_uq[:, :, QK_NOPE:]
    widen = lambda a: jnp.pad(a, ((0, 0), (0, 0), (0, LANES - QK_ROPE))).reshape(qr, MLA_HEADS * LANES)
    w_q = jnp.concatenate([nope, widen(pe), widen(_half_swap(pe))], axis=1).astype(BF16)
    w_abs = _block_diag(jnp.transpose(p['w_uk'][l], (1, 2, 0))).astype(BF16)
    wa2 = jnp.zeros((LANES, gk_w), F32).at[2 * QK_ROPE:2 * QK_ROPE + g_a.shape[1]].set(p['gla_wa2'][l]).astype(BF16)

    w_uv = p['w_uv'][l]
    mw = MLA_HEADS * V_HEAD
    w_uv_cat = w_uv.reshape(kvr, mw)
    w_uv_rows = _block_diag(jnp.transpose(w_uv, (1, 0, 2)))
    n_a, n_b, n_c = jnp.split(p['out_norm_g'][l], [lw, lw + mw])
    n_r = N_GROUPS + N_EXPERTS
    w_r = jnp.concatenate([p['router_wg'][l], p['router_we'][l], jnp.zeros((D, LANES - n_r), F32)], axis=1)
    b_r = jnp.concatenate([p['router_bg'][l], p['router_be'][l], jnp.zeros((LANES - n_r,), F32)])
    return dict(
        g1=row(p['norm1_g'][l]), w_in=w_in_p, q_g=row(p['q_norm_g'][l]), w_q=w_q, w_abs=w_abs,
        kv_g=row(p['kv_norm_g'][l]), wa2=wa2, ba=row(p['gla_ba'][l]),
        conv_w=p['conv_w'][l], conv_b=row(p['conv_b'][l]),
        lru_wa=_block_diag(p['lru_wa'][l]).astype(BF16), lru_ba=row(p['lru_ba'][l]),
        lru_wi=_block_diag(p['lru_wi'][l]).astype(BF16), lru_bi=row(p['lru_bi'][l]),
        lru_lam=row(p['lru_lambda'][l]), n_a=row(n_a), n_b=row(n_b), n_c=row(n_c),
        gla_g=row(jnp.tile(p['gla_norm_g'][l], GLA_HEADS)),
        w_uv_cat=w_uv_cat.astype(BF16), w_uv_rows=w_uv_rows.astype(BF16),
        w_out=p['w_out'][l].astype(BF16), g2=row(p['norm2_g'][l]), w_r=w_r, b_r=row(b_r),
    )


def _rope_tables(positions):
    half = QK_ROPE // 2
    inv = ROPE_THETA ** (-np.arange(half, dtype=np.float64) / half)
    ang = np.asarray(positions, np.float64)[:, None] * inv
    zeros = np.zeros((ang.shape[0], LANES - QK_ROPE))
    cos = np.concatenate([np.cos(ang), np.cos(ang), zeros], axis=1)
    sin = np.concatenate([-np.sin(ang), np.sin(ang), zeros], axis=1)
    return jnp.asarray(cos, F32), jnp.asarray(sin, F32)


def _gla_state_to_rows(s):
    B = s.shape[0]
    eye = jnp.eye(GLA_HEADS, dtype=s.dtype)
    return jnp.einsum('bhde,hg->bhegd', s, eye).reshape(B, GLA_HEADS * GLA_DV, GLA_HEADS * GLA_DK)


def _gla_rows_to_state(st):
    B = st.shape[0]
    s5 = st.reshape(B, GLA_HEADS, GLA_DV, GLA_HEADS, GLA_DK)
    blocks = jnp.stack([s5[:, h, :, h, :] for h in range(GLA_HEADS)], axis=1)
    return jnp.swapaxes(blocks, -1, -2)


def kernel(x_prompt, x_sample, cache_ckv, cache_kpe, page_table, state_conv, state_lru, state_gla, norm1_g, w_in, conv_w, conv_b, lru_wa, lru_ba, lru_wi, lru_bi, lru_lambda, q_norm_g, w_uq, kv_norm_g, w_uk, w_uv, gla_wa2, gla_ba, gla_norm_g, out_norm_g, w_out, norm2_g, router_wg, router_bg, router_we, router_be, w_gate, w_up, w_down, final_norm_g):
    p = dict(norm1_g=norm1_g, w_in=w_in, conv_w=conv_w, conv_b=conv_b, lru_wa=lru_wa, lru_ba=lru_ba,
             lru_wi=lru_wi, lru_bi=lru_bi, lru_lambda=lru_lambda, q_norm_g=q_norm_g, w_uq=w_uq,
             kv_norm_g=kv_norm_g, w_uk=w_uk, w_uv=w_uv, gla_wa2=gla_wa2, gla_ba=gla_ba,
             gla_norm_g=gla_norm_g, out_norm_g=out_norm_g, w_out=w_out, norm2_g=norm2_g,
             router_wg=router_wg, router_bg=router_bg, router_we=router_we, router_be=router_be)
    Bp, Sp, D = x_prompt.shape
    Bs, Ss, _ = x_sample.shape
    depth = w_in.shape[0]
    lw = state_lru.shape[-1]
    qr = q_norm_g.shape[-1]
    kvr = kv_norm_g.shape[-1]
    gk_w = GLA_HEADS * GLA_DK
    gv_w = GLA_HEADS * GLA_DV
    dims = (lw, qr, kvr, gk_w, gv_w)
    assert kvr == LANES and gk_w == LANES
    Tp, Ts = Bp * Sp, Bs * Ss
    T = Tp + Ts
    past = page_table.shape[1] * cache_ckv.shape[2]

    cos_t, sin_t = _rope_tables(np.concatenate([np.tile(np.arange(Sp), Bp), np.tile(past + np.arange(Ss), Bs)]))
    wg_b, wu_b, wd_b = w_gate.astype(BF16), w_up.astype(BF16), w_down.astype(BF16)
    n_tiles = (TOP_K * T + N_EXPERTS * (MOE_TILE - 1)) // MOE_TILE + 1

    x = jnp.concatenate([x_prompt.reshape(Tp, D), x_sample.reshape(Ts, D)], axis=0)
    outs = {k: [] for k in ('ckv_p', 'kpe_p', 'ckv_s', 'kpe_s', 'conv_p', 'conv_s', 'lru_p', 'lru_s', 'gla_p', 'gla_s')}
    for l in range(depth):
        lw_ = _layer_weights(l, p, dims)
        zl, qcat, kcat, ckv_n, kpe_n, g = _pre_call(x, lw_, cos_t, sin_t, dims)

        a_p, conv_p, lru_p = _lru_call(zl, 0, Bp, Sp, jnp.zeros((Bp, CONV_WIDTH - 1, lw), F32),
                                       jnp.zeros((Bp, lw), F32), lw_, BF16)
        a_s, conv_s, lru_s = _lru_call(zl, Tp, Bs, Ss, state_conv[l], state_lru[l], lw_, F32)

        c_p, gla_p = _gla_call(g, 0, Bp, Sp, jnp.zeros((Bp, gv_w, gk_w), F32), lw_, BF16, gk_w, gv_w)
        c_s, gla_s = _gla_call(g, Tp, Bs, Ss, _gla_state_to_rows(state_gla[l]), lw_, F32, gk_w, gv_w)

        b_p = _attn_prompt_call(qcat, kcat, Bp, Sp, lw_, kvr)
        b_s = _attn_sample_call(page_table, qcat[Tp:].reshape(Bs, Ss, -1), ckv_n[Tp:].reshape(Bs, Ss, kvr),
                                kpe_n[Tp:].reshape(Bs, Ss, QK_ROPE), cache_ckv, cache_kpe, l, lw_, kvr)

        a = jnp.concatenate([a_p, a_s.astype(BF16)], axis=0)
        b = jnp.concatenate([b_p, b_s.reshape(Ts, -1).astype(BF16)], axis=0)
        c = jnp.concatenate([c_p, c_s.astype(BF16)], axis=0)
        x1, hn, route = _post_call(a, b, c, x, lw_)

        row_tok, row_w, pos, tile_e, n_used = _route_meta(route, MOE_TILE, n_tiles)
        ys = _moe_call(hn[row_tok], row_w, tile_e, n_used, wg_b, wu_b, wd_b, l)
        x = x1 + ys[pos[:, 0]] + ys[pos[:, 1]]

        outs['ckv_p'].append(ckv_n[:Tp].reshape(Bp, Sp, kvr))
        outs['kpe_p'].append(kpe_n[:Tp].reshape(Bp, Sp, QK_ROPE))
        outs['ckv_s'].append(ckv_n[Tp:].reshape(Bs, Ss, kvr))
        outs['kpe_s'].append(kpe_n[Tp:].reshape(Bs, Ss, QK_ROPE))
        outs['conv_p'].append(conv_p)
        outs['conv_s'].append(conv_s)
        outs['lru_p'].append(lru_p.reshape(Bp, lw))
        outs['lru_s'].append(lru_s.reshape(Bs, lw))
        outs['gla_p'].append(_gla_rows_to_state(gla_p))
        outs['gla_s'].append(_gla_rows_to_state(gla_s))

    y = _final_call(x, final_norm_g.reshape(1, D))
    st = {k: jnp.stack(v) for k, v in outs.items()}
    return (y[:Tp].reshape(Bp, Sp, D), y[Tp:].reshape(Bs, Ss, D),
            st['ckv_p'], st['kpe_p'], st['ckv_s'], st['kpe_s'], st['conv_p'], st['conv_s'],
            st['lru_p'], st['lru_s'], st['gla_p'], st['gla_s'])
```

```python
import functools
import math

import numpy as np
import jax
import jax.numpy as jnp
from jax import lax
from jax.experimental import pallas as pl
from jax.experimental.pallas import tpu as pltpu

F32 = jnp.float32
BF16 = jnp.bfloat16

LRU_BLOCKS = 4
CONV_WIDTH = 4
LRU_C = 8.0
MLA_HEADS = 8
QK_NOPE = 64
QK_ROPE = 32
V_HEAD = 64
ROPE_THETA = 10000.0
GLA_HEADS = 4
GLA_DK = 32
GLA_DV = 64
GLA_TAU = 16.0
GLA_CHUNK = 16
N_GROUPS = 4
EXPERTS_PER_GROUP = 8
N_EXPERTS = N_GROUPS * EXPERTS_PER_GROUP
TOP_K = 2
EPS = 1e-6

LANES = 128
SUBLANES = 8
VMEM_LIMIT_BYTES = 56 * 1024 * 1024

TOKEN_TILE = 256
SEQ_TILE = 256
MOE_TILE = 256
PAGES_PER_STEP = 16
NEG_BIG = -1e30


def _cparams(*sem):
    return pltpu.CompilerParams(dimension_semantics=sem, vmem_limit_bytes=VMEM_LIMIT_BYTES)


def _rms(x, g):
    return x * lax.rsqrt(jnp.mean(x * x, axis=-1, keepdims=True) + EPS) * g


def _dot(a, b):
    return jnp.dot(a, b, preferred_element_type=F32)


def _dot_nt(a, b):
    return lax.dot_general(a, b, (((1,), (1,)), ((), ())), preferred_element_type=F32)


def _softplus(x):
    return jnp.maximum(x, 0.0) + jnp.log1p(jnp.exp(-jnp.abs(x)))


def _shift_rows(x, d, fill):
    row = lax.broadcasted_iota(jnp.int32, x.shape, 0)
    return jnp.where(row >= d, pltpu.roll(x, d, 0), fill)


def _pre_kernel(x_ref, g1_ref, win_ref, qg_ref, wq_ref, wabs_ref, kvg_ref, wa2_ref, ba_ref,
                cos_ref, sin_ref, zl_ref, q_ref, k_ref, kt_ref, ckv_ref, kpe_ref, g_ref, *, dims):
    lw, qr, kvr, gk_w, gv_w = dims
    xn = _rms(x_ref[...], g1_ref[...])
    z = _dot(xn.astype(BF16), win_ref[...])
    o = 2 * lw
    zl_ref[...] = z[:, :o]
    cq = z[:, o:o + qr]
    o += qr
    ckv = z[:, o:o + kvr]
    o += kvr
    gqk = z[:, o:o + 2 * gk_w]
    o += 2 * gk_w
    gvo = z[:, o:o + 2 * gv_w]
    o += 2 * gv_w
    tail = z[:, o:o + LANES]
    cos = cos_ref[...]
    sin = sin_ref[...]
    scale = (QK_NOPE + QK_ROPE) ** -0.5 * math.log2(math.e)

    ckv_n = _rms(ckv, kvg_ref[...])
    kpe = tail * cos + pltpu.roll(tail, LANES - QK_ROPE, 1) * sin
    ckv_ref[...] = ckv_n
    kpe_ref[...] = kpe[:, :QK_ROPE]
    lane = lax.broadcasted_iota(jnp.int32, kpe.shape, 1)
    kcat = jnp.concatenate([ckv_n, jnp.where(lane == LANES - 1, 1.0, kpe)], axis=1)
    k_ref[...] = kcat.astype(BF16)
    kt_ref[...] = kcat.T.astype(BF16)

    cqn = _rms(cq, qg_ref[...]).astype(BF16)
    qall = _dot(cqn, wq_ref[...])
    n_nope = MLA_HEADS * QK_NOPE
    q_abs = _dot(qall[:, :n_nope].astype(BF16), wabs_ref[...]) * scale
    for h in range(MLA_HEADS):
        pe = qall[:, n_nope + h * LANES:n_nope + (h + 1) * LANES]
        sw = qall[:, n_nope + (MLA_HEADS + h) * LANES:n_nope + (MLA_HEADS + h + 1) * LANES]
        q_ref[:, 2 * h * LANES:(2 * h + 1) * LANES] = q_abs[:, h * kvr:(h + 1) * kvr].astype(BF16)
        q_ref[:, (2 * h + 1) * LANES:(2 * h + 2) * LANES] = ((pe * cos + sw * sin) * scale).astype(BF16)

    la_pre = _dot(tail.astype(BF16), wa2_ref[...]) + ba_ref[...]
    la = -_softplus(-la_pre) * (1.0 / GLA_TAU)
    g_ref[:, :gk_w] = gqk[:, :gk_w] * (GLA_DK ** -0.5)
    g_ref[:, gk_w:2 * gk_w] = gqk[:, gk_w:]
    g_ref[:, 2 * gk_w:3 * gk_w] = la
    g_ref[:, 3 * gk_w:] = gvo


def _pre_call(x, lw_, cos_t, sin_t, dims):
    T, D = x.shape
    lw, qr, kvr, gk_w, gv_w = dims
    tm = TOKEN_TILE
    assert T % tm == 0
    row = lambda i: (i, 0)
    full = lambda i: (0, 0)
    wspec = lambda a: pl.BlockSpec(a.shape, full)
    ins = [x, lw_['g1'], lw_['w_in'], lw_['q_g'], lw_['w_q'], lw_['w_abs'], lw_['kv_g'], lw_['wa2'], lw_['ba']]
    in_specs = [pl.BlockSpec((tm, D), row)] + [wspec(a) for a in ins[1:]]
    in_specs += [pl.BlockSpec((tm, LANES), row), pl.BlockSpec((tm, LANES), row)]
    out_shape = (
        jax.ShapeDtypeStruct((T, 2 * lw), F32),
        jax.ShapeDtypeStruct((T, 2 * LANES * MLA_HEADS), BF16),
        jax.ShapeDtypeStruct((T, 2 * LANES), BF16),
        jax.ShapeDtypeStruct((T // tm, 2 * LANES, tm), BF16),
        jax.ShapeDtypeStruct((T, kvr), F32),
        jax.ShapeDtypeStruct((T, QK_ROPE), F32),
        jax.ShapeDtypeStruct((T, 3 * gk_w + 2 * gv_w), F32),
    )
    out_specs = tuple(pl.BlockSpec((None, 2 * LANES, tm), lambda i: (i, 0, 0)) if len(s.shape) == 3
                      else pl.BlockSpec((tm, s.shape[1]), row) for s in out_shape)
    return pl.pallas_call(
        functools.partial(_pre_kernel, dims=dims),
        grid=(T // tm,), in_specs=in_specs, out_specs=out_specs, out_shape=out_shape,
        compiler_params=_cparams("parallel"), name="pre_proj",
    )(*ins, cos_t, sin_t)


def _lru_kernel(zl_ref, cbuf_ref, h0_ref, cw_ref, cb_ref, wa_ref, ba_ref, wi_ref, bi_ref, lam_ref, na_ref,
                a_ref, conv_ref, hout_ref, xbuf, hcar, *, ts, lw):
    i = pl.program_id(1)
    last = pl.num_programs(1) - 1
    pad = SUBLANES
    nbuf = CONV_WIDTH - 1

    @pl.when(i == 0)
    def _():
        xbuf[0:pad, :] = jnp.zeros((pad, lw), F32)
        xbuf[pad - nbuf:pad, :] = cbuf_ref[...]
        hcar[...] = h0_ref[...]

    x = zl_ref[:, :lw]
    y = zl_ref[:, lw:]
    xbuf[pad:pad + ts, :] = x
    xc = cb_ref[...] + cw_ref[nbuf:nbuf + 1, :] * x
    for k in range(nbuf):
        xc = xc + cw_ref[k:k + 1, :] * xbuf[pad - nbuf + k:pad - nbuf + k + ts, :]

    @pl.when(i == last)
    def _():
        conv_ref[...] = xbuf[pad + ts - nbuf:pad + ts, :]

    xbuf[0:pad, :] = xbuf[ts:ts + pad, :]

    xb = xc.astype(BF16)
    r = jax.nn.sigmoid(_dot(xb, wa_ref[...]) + ba_ref[...])
    gi = jax.nn.sigmoid(_dot(xb, wi_ref[...]) + bi_ref[...])
    log_a = (-LRU_C) * r * _softplus(-lam_ref[...])
    a = jnp.exp(log_a)
    th = jnp.tanh(log_a)
    u = jnp.sqrt(-2.0 * th / (1.0 - th)) * (gi * xc)

    d = 1
    while d < ts:
        u = a * _shift_rows(u, d, 0.0) + u
        a = a * _shift_rows(a, d, 1.0)
        d *= 2
    h = a * hcar[...] + u
    hcar[...] = h[ts - 1:ts, :]

    @pl.when(i == last)
    def _():
        hout_ref[...] = h[ts - 1:ts, :]

    out_a = h * jax.nn.gelu(y)
    a_ref[...] = _rms(out_a, na_ref[...]).astype(a_ref.dtype)


def _lru_call(zl, row_off, B, S, cbuf, h0, lw_, out_dtype):
    lw = h0.shape[-1]
    ts = min(S, SEQ_TILE)
    assert S % ts == 0 and row_off % ts == 0 and S >= CONV_WIDTH - 1
    n = S // ts
    off = row_off // ts
    full = lambda b, i: (0, 0)
    ws = [lw_['conv_w'], lw_['conv_b'], lw_['lru_wa'], lw_['lru_ba'], lw_['lru_wi'], lw_['lru_bi'],
          lw_['lru_lam'], lw_['n_a']]
    in_specs = [
        pl.BlockSpec((ts, 2 * lw), lambda b, i: (off + b * n + i, 0)),
        pl.BlockSpec((None, CONV_WIDTH - 1, lw), lambda b, i: (b, 0, 0)),
        pl.BlockSpec((None, 1, lw), lambda b, i: (b, 0, 0)),
    ] + [pl.BlockSpec(a.shape, full) for a in ws]
    out_shape = (
        jax.ShapeDtypeStruct((B * S, lw), out_dtype),
        jax.ShapeDtypeStruct((B, CONV_WIDTH - 1, lw), F32),
        jax.ShapeDtypeStruct((B, 1, lw), F32),
    )
    out_specs = (
        pl.BlockSpec((ts, lw), lambda b, i: (b * n + i, 0)),
        pl.BlockSpec((None, CONV_WIDTH - 1, lw), lambda b, i: (b, 0, 0)),
        pl.BlockSpec((None, 1, lw), lambda b, i: (b, 0, 0)),
    )
    return pl.pallas_call(
        functools.partial(_lru_kernel, ts=ts, lw=lw),
        grid=(B, n), in_specs=in_specs, out_specs=out_specs, out_shape=out_shape,
        scratch_shapes=[pltpu.VMEM((ts + SUBLANES, lw), F32), pltpu.VMEM((1, lw), F32)],
        compiler_params=_cparams("parallel", "arbitrary"), name="rg_lru",
    )(zl, cbuf, h0.reshape(B, 1, lw), *ws)


def _gla_kernel(g_ref, s0_ref, gn_ref, nc_ref, c_ref, sout_ref, st, obuf, *, ts, gk_w, gv_w):
    i = pl.program_id(1)
    last = pl.num_programs(1) - 1
    C = GLA_CHUNK
    rows = min(ts, C)
    n_chunks = max(ts // C, 1)
    log2c = int(math.log2(C))

    @pl.when(i == 0)
    def _():
        st[...] = s0_ref[...]

    dk_sh, dv_sh = int(math.log2(GLA_DK)), int(math.log2(GLA_DV))
    hd = lax.broadcasted_iota(jnp.int32, (gk_w, gv_w), 0) >> dk_sh
    he = lax.broadcasted_iota(jnp.int32, (gk_w, gv_w), 1) >> dv_sh
    same = (hd == he).astype(BF16)
    he_t = lax.broadcasted_iota(jnp.int32, (gv_w, gk_w), 0) >> dv_sh
    hd_t = lax.broadcasted_iota(jnp.int32, (gv_w, gk_w), 1) >> dk_sh
    same_t = (he_t == hd_t).astype(F32)
    sel_t = lax.broadcasted_iota(jnp.int32, (C, C * C), 0)
    sel_r = lax.broadcasted_iota(jnp.int32, (C, C * C), 1) >> log2c
    sel = (sel_t == sel_r).astype(BF16)
    srow = lax.broadcasted_iota(jnp.int32, (C, 1), 0)

    def chunk(c, carry):
        r0 = pl.multiple_of(c * rows, rows)
        blk = g_ref[pl.ds(r0, rows), :]
        if rows < C:
            blk = jnp.concatenate([blk, jnp.zeros((C - rows, blk.shape[1]), F32)], axis=0)
        q = blk[:, :gk_w]
        k = blk[:, gk_w:2 * gk_w]
        la = blk[:, 2 * gk_w:3 * gk_w]
        v = blk[:, 3 * gk_w:3 * gk_w + gv_w]
        cum = la
        d = 1
        while d < C:
            cum = cum + _shift_rows(cum, d, 0.0)
            d *= 2
        s_t = st[...]
        o_inter = _dot_nt((q * jnp.exp(cum)).astype(BF16), s_t.astype(BF16))
        pieces = []
        for t in range(C):
            diff = jnp.where(srow <= t, cum[t:t + 1, :] - cum, NEG_BIG)
            pieces.append(q[t:t + 1, :] * k * jnp.exp(diff))
        w = jnp.concatenate(pieces, axis=0)
        att = _dot(w.astype(BF16), same)
        xv = att * jnp.concatenate([v] * C, axis=0)
        o = o_inter + _dot(sel, xv.astype(BF16))
        obuf[pl.ds(r0, rows), :] = o[:rows, :]
        cl = cum[C - 1:C, :]
        kdec = k * jnp.exp(cl - cum)
        upd = lax.dot_general(v.astype(BF16), kdec.astype(BF16), (((0,), (0,)), ((), ())),
                              preferred_element_type=F32)
        st[...] = s_t * jnp.exp(cl) + upd * same_t
        return carry

    lax.fori_loop(0, n_chunks, chunk, 0)

    @pl.when(i == last)
    def _():
        sout_ref[...] = st[...]

    o = obuf[...]
    go = g_ref[:, 3 * gk_w + gv_w:]
    e64 = (lax.broadcasted_iota(jnp.int32, (gv_w, gv_w), 0) >> dv_sh
           == lax.broadcasted_iota(jnp.int32, (gv_w, gv_w), 1) >> dv_sh).astype(BF16)
    osq = o * o
    osq_hi = osq.astype(BF16)
    osq_lo = (osq - osq_hi.astype(F32)).astype(BF16)
    ms = (_dot(osq_hi, e64) + _dot(osq_lo, e64)) * (1.0 / GLA_DV)
    out_c = o * lax.rsqrt(ms + EPS) * gn_ref[...] * (go * jax.nn.sigmoid(go))
    c_ref[...] = _rms(out_c, nc_ref[...]).astype(c_ref.dtype)


def _gla_call(g, row_off, B, S, s0t, lw_, out_dtype, gk_w, gv_w):
    ts = min(S, SEQ_TILE)
    assert S % ts == 0 and row_off % ts == 0 and (ts % GLA_CHUNK == 0 or ts < GLA_CHUNK)
    n = S // ts
    off = row_off // ts
    gw = g.shape[1]
    full = lambda b, i: (0, 0)
    out_shape = (jax.ShapeDtypeStruct((B * S, gv_w), out_dtype),
                 jax.ShapeDtypeStruct((B, gv_w, gk_w), F32))
    return pl.pallas_call(
        functools.partial(_gla_kernel, ts=ts, gk_w=gk_w, gv_w=gv_w),
        grid=(B, n),
        in_specs=[pl.BlockSpec((ts, gw), lambda b, i: (off + b * n + i, 0)),
                  pl.BlockSpec((None, gv_w, gk_w), lambda b, i: (b, 0, 0)),
                  pl.BlockSpec((1, gv_w), full), pl.BlockSpec((1, gv_w), full)],
        out_specs=(pl.BlockSpec((ts, gv_w), lambda b, i: (b * n + i, 0)),
                   pl.BlockSpec((None, gv_w, gk_w), lambda b, i: (b, 0, 0))),
        out_shape=out_shape,
        scratch_shapes=[pltpu.VMEM((gv_w, gk_w), F32), pltpu.VMEM((ts, gv_w), F32)],
        compiler_params=_cparams("parallel", "arbitrary"), name="gla",
    )(g, s0t, lw_['gla_g'], lw_['n_c'])


def _tree(op, xs):
    xs = list(xs)
    while len(xs) > 1:
        xs = [op(xs[i], xs[i + 1]) if i + 1 < len(xs) else xs[i] for i in range(0, len(xs), 2)]
    return xs[0]


def _attn_prompt_kernel(q_ref, k_ref, kt_ref, wuvt_ref, nb_ref, o_ref, m_scr, acc_scr, *, tq, kvr):
    i = pl.program_id(1)
    kw = 2 * LANES
    m_scr[...] = jnp.full(m_scr.shape, NEG_BIG, F32)
    acc_scr[...] = jnp.zeros(acc_scr.shape, F32)

    def block(j, masked):
        kb = k_ref[pl.ds(pl.multiple_of(j * tq, tq), tq), :]
        kbt = kt_ref[j]
        if masked:
            key = lax.broadcasted_iota(jnp.int32, (tq, tq), 0)
            qry = lax.broadcasted_iota(jnp.int32, (tq, tq), 1)
            keep = key <= qry
        for h in range(MLA_HEADS):
            st = _dot_nt(kb, q_ref[:, h * kw:(h + 1) * kw])
            if masked:
                st = jnp.where(keep, st, NEG_BIG)
            m_old = m_scr[h]
            m_new = jnp.maximum(m_old, jnp.max(st, axis=0, keepdims=True))
            alpha = jnp.exp2(m_old - m_new)
            pt = jnp.exp2(st - m_new).astype(BF16)
            acc_scr[h] = alpha * acc_scr[h] + _dot(kbt, pt)
            m_scr[h] = m_new

    def body(j, carry):
        block(j, False)
        return carry

    lax.fori_loop(0, i, body, 0)
    block(i, True)

    out_t = jnp.zeros((wuvt_ref.shape[0], tq), F32)
    for h in range(MLA_HEADS):
        acc = acc_scr[h]
        o_t = (acc[:kvr, :] / acc[kw - 1:kw, :]).astype(BF16)
        out_t = out_t + _dot(wuvt_ref[:, h * kvr:(h + 1) * kvr], o_t)
    ms = jnp.mean(out_t * out_t, axis=0, keepdims=True)
    out_t = out_t * lax.rsqrt(ms + EPS) * nb_ref[...]
    o_ref[...] = out_t.T.astype(o_ref.dtype)


def _attn_prompt_call(qcat, kcat, kcat_t, B, S, lw_, kvr):
    tq = kcat_t.shape[2]
    assert S % tq == 0
    n = S // tq
    mw = lw_['w_uv_rows_t'].shape[0]
    full = lambda b, i: (0, 0)
    return pl.pallas_call(
        functools.partial(_attn_prompt_kernel, tq=tq, kvr=kvr),
        grid=(B, n),
        in_specs=[pl.BlockSpec((tq, qcat.shape[1]), lambda b, i: (b * n + i, 0)),
                  pl.BlockSpec((S, kcat.shape[1]), lambda b, i: (b, 0)),
                  pl.BlockSpec((n, kcat_t.shape[1], tq), lambda b, i: (b, 0, 0)),
                  pl.BlockSpec(lw_['w_uv_rows_t'].shape, full), pl.BlockSpec((mw, 1), full)],
        out_specs=pl.BlockSpec((tq, mw), lambda b, i: (b * n + i, 0)),
        out_shape=jax.ShapeDtypeStruct((B * S, mw), BF16),
        scratch_shapes=[pltpu.VMEM((MLA_HEADS, 1, tq), F32),
                        pltpu.VMEM((MLA_HEADS, kcat_t.shape[1], tq), F32)],
        compiler_params=_cparams("parallel", "arbitrary"), name="attn_prompt",
    )(qcat, kcat, kcat_t, lw_['w_uv_rows_t'], lw_['n_b'].reshape(mw, 1))


def _softmax_step(m_scr, l_scr, acc_scr, scores, values):
    m_old = m_scr[...]
    m_new = jnp.maximum(m_old, jnp.max(_tree(jnp.maximum, scores), axis=-1, keepdims=True))
    alpha = jnp.exp2(m_old - m_new)
    ps = [jnp.exp2(s - m_new) for s in scores]
    l_scr[...] = alpha * l_scr[...] + jnp.sum(_tree(jnp.add, ps), axis=-1, keepdims=True)
    pv = _tree(jnp.add, [_dot(p.astype(BF16), v) for p, v in zip(ps, values)])
    acc_scr[...] = alpha * acc_scr[...] + pv
    m_scr[...] = m_new


def _attn_sample_kernel(pt_ref, q_ref, nckv_ref, nkpe_ref, *rest, n_pp, sq, kvr, page):
    ckv_refs = rest[:n_pp]
    kpe_refs = rest[n_pp:2 * n_pp]
    wuv_ref, nb_ref, o_ref, q_scr, m_scr, l_scr, acc_scr = rest[2 * n_pp:]
    g = pl.program_id(1)
    last = pl.num_programs(1) - 1
    nrow = MLA_HEADS * sq

    @pl.when(g == 0)
    def _():
        for h in range(MLA_HEADS):
            q_scr[h * sq:(h + 1) * sq, :] = q_ref[:, 2 * h * LANES:(2 * h + 2) * LANES].astype(F32)
        m_scr[...] = jnp.full((nrow, 1), NEG_BIG, F32)
        l_scr[...] = jnp.zeros((nrow, 1), F32)
        acc_scr[...] = jnp.zeros((nrow, kvr), F32)

    qa = q_scr[:, :kvr].astype(BF16)
    qp = q_scr[:, kvr:kvr + QK_ROPE].astype(BF16)
    cks = [ckv_refs[j][...].astype(BF16) for j in range(n_pp)]
    scores = [_dot_nt(qa, cks[j]) + _dot(qp, kpe_refs[j][...].astype(BF16)) for j in range(n_pp)]
    _softmax_step(m_scr, l_scr, acc_scr, scores, cks)

    @pl.when(g == last)
    def _():
        zpad = lambda a: jnp.concatenate([a, jnp.zeros((page - sq, a.shape[1]), F32)], axis=0)
        ck = zpad(nckv_ref[...]).astype(BF16)
        kp = zpad(nkpe_ref[...]).astype(BF16)
        tok = lax.broadcasted_iota(jnp.int32, (nrow, page), 0) & (sq - 1)
        key = lax.broadcasted_iota(jnp.int32, (nrow, page), 1)
        s = jnp.where(key <= tok, _dot_nt(qa, ck) + _dot_nt(qp, kp), NEG_BIG)
        _softmax_step(m_scr, l_scr, acc_scr, [s], [ck])
        o_lat = (acc_scr[...] / l_scr[...]).astype(BF16)
        mw = wuv_ref.shape[1]
        r = _dot(o_lat, wuv_ref[...])
        rh = lax.broadcasted_iota(jnp.int32, (nrow, mw), 0) >> int(math.log2(sq))
        ch = lax.broadcasted_iota(jnp.int32, (nrow, mw), 1) >> int(math.log2(V_HEAD))
        r = jnp.where(rh == ch, r, 0.0)
        out = r[0:sq, :]
        for h in range(1, MLA_HEADS):
            out = out + r[h * sq:(h + 1) * sq, :]
        o_ref[...] = _rms(out, nb_ref[...])


def _attn_sample_call(page_table, qs, nckv, nkpe, cache_ckv, cache_kpe_t, layer, lw_, kvr):
    B, sq, qw = qs.shape
    n_pages = page_table.shape[1]
    page = cache_ckv.shape[2]
    n_pp = math.gcd(PAGES_PER_STEP, n_pages)
    n_steps = n_pages // n_pp
    mw = lw_['w_uv_cat'].shape[1]
    nrow = MLA_HEADS * sq
    assert sq == SUBLANES and sq <= page

    def page_spec(shape, j):
        return pl.BlockSpec((None, None) + shape,
                            lambda b, g, pt: (layer, pt[b * n_pages + g * n_pp + j], 0, 0))

    in_specs = [pl.BlockSpec((None, sq, qw), lambda b, g, pt: (b, 0, 0)),
                pl.BlockSpec((None, sq, kvr), lambda b, g, pt: (b, 0, 0)),
                pl.BlockSpec((None, sq, QK_ROPE), lambda b, g, pt: (b, 0, 0))]
    in_specs += [page_spec((page, kvr), j) for j in range(n_pp)]
    in_specs += [page_spec((QK_ROPE, page), j) for j in range(n_pp)]
    in_specs += [pl.BlockSpec(lw_['w_uv_cat'].shape, lambda b, g, pt: (0, 0)),
                 pl.BlockSpec((1, mw), lambda b, g, pt: (0, 0))]
    grid_spec = pltpu.PrefetchScalarGridSpec(
        num_scalar_prefetch=1, grid=(B, n_steps), in_specs=in_specs,
        out_specs=pl.BlockSpec((None, sq, mw), lambda b, g, pt: (b, 0, 0)),
        scratch_shapes=[pltpu.VMEM((nrow, 2 * LANES), F32), pltpu.VMEM((nrow, 1), F32),
                        pltpu.VMEM((nrow, 1), F32), pltpu.VMEM((nrow, kvr), F32)])
    return pl.pallas_call(
        functools.partial(_attn_sample_kernel, n_pp=n_pp, sq=sq, kvr=kvr, page=page),
        grid_spec=grid_spec, out_shape=jax.ShapeDtypeStruct((B, sq, mw), F32),
        compiler_params=_cparams("parallel", "arbitrary"), name="attn_sample",
    )(page_table.reshape(-1), qs, nckv, nkpe, *([cache_ckv] * n_pp), *([cache_kpe_t] * n_pp),
      lw_['w_uv_cat'], lw_['n_b'])


def _post_kernel(a_ref, b_ref, c_ref, x_ref, wo_ref, g2_ref, wr_ref, br_ref, x1_ref, hn_ref, rt_ref, *, widths):
    wa, wb, wc = widths
    mix = _dot(a_ref[...], wo_ref[:wa, :])
    mix = mix + _dot(b_ref[...], wo_ref[wa:wa + wb, :])
    mix = mix + _dot(c_ref[...], wo_ref[wa + wb:, :])
    x1 = x_ref[...] + mix
    x1_ref[...] = x1
    hn = _rms(x1, g2_ref[...])
    hn_ref[...] = hn
    logits = jnp.dot(hn, wr_ref[...], precision=lax.Precision.HIGHEST, preferred_element_type=F32) + br_ref[...]

    lane = lax.broadcasted_iota(jnp.int32, logits.shape, 1)
    lane_f = lane.astype(F32)
    big = float(LANES)
    is_g = lane < N_GROUPS
    gl = jnp.where(is_g, logits, NEG_BIG)
    gmax = jnp.max(gl, axis=-1, keepdims=True)
    gsel = jnp.min(jnp.where(gl == gmax, lane_f, big), axis=-1, keepdims=True)
    gprob = 1.0 / jnp.sum(jnp.where(is_g, jnp.exp(gl - gmax), 0.0), axis=-1, keepdims=True)
    lo = N_GROUPS + gsel * EXPERTS_PER_GROUP
    el = jnp.where((lane_f >= lo) & (lane_f < lo + EXPERTS_PER_GROUP), logits, NEG_BIG)
    v1 = jnp.max(el, axis=-1, keepdims=True)
    i1 = jnp.min(jnp.where(el == v1, lane_f, big), axis=-1, keepdims=True)
    el2 = jnp.where(lane_f == i1, NEG_BIG, el)
    v2 = jnp.max(el2, axis=-1, keepdims=True)
    i2 = jnp.min(jnp.where(el2 == v2, lane_f, big), axis=-1, keepdims=True)
    e21 = jnp.exp(v2 - v1)
    w1 = gprob / (1.0 + e21)
    w2 = w1 * e21
    rt = jnp.where(lane == 0, i1 - N_GROUPS, 0.0)
    rt = jnp.where(lane == 1, i2 - N_GROUPS, rt)
    rt = jnp.where(lane == 2, w1, rt)
    rt = jnp.where(lane == 3, w2, rt)
    rt_ref[...] = rt


def _post_call(a, b, c, x, lw_):
    T, D = x.shape
    tm = TOKEN_TILE
    row = lambda i: (i, 0)
    full = lambda i: (0, 0)
    widths = (a.shape[1], b.shape[1], c.shape[1])
    ws = [lw_['w_out'], lw_['g2'], lw_['w_r'], lw_['b_r']]
    out_shape = (jax.ShapeDtypeStruct((T, D), F32), jax.ShapeDtypeStruct((T, D), F32),
                 jax.ShapeDtypeStruct((T, LANES), F32))
    return pl.pallas_call(
        functools.partial(_post_kernel, widths=widths),
        grid=(T // tm,),
        in_specs=[pl.BlockSpec((tm, w), row) for w in widths] + [pl.BlockSpec((tm, D), row)]
        + [pl.BlockSpec(w.shape, full) for w in ws],
        out_specs=tuple(pl.BlockSpec((tm, s.shape[1]), row) for s in out_shape),
        out_shape=out_shape, compiler_params=_cparams("parallel"), name="post_proj",
    )(a, b, c, x, *ws)


def _moe_kernel(te_ref, nt_ref, xs_ref, w_ref, wg_ref, wu_ref, wd_ref, y_ref):
    i = pl.program_id(0)

    @pl.when(i < nt_ref[0])
    def _():
        x = xs_ref[...].astype(BF16)
        hg = _dot(x, wg_ref[...])
        hu = _dot(x, wu_ref[...])
        act = hg * jax.nn.sigmoid(hg) * hu * w_ref[...]
        y_ref[...] = _dot(act.astype(BF16), wd_ref[...])

    @pl.when(i >= nt_ref[0])
    def _():
        y_ref[...] = jnp.zeros(y_ref.shape, F32)


def _moe_call(xs, row_w, tile_e, n_used, w_gate, w_up, w_down, layer):
    R, D = xs.shape
    tm = MOE_TILE
    F = w_gate.shape[-1]
    grid_spec = pltpu.PrefetchScalarGridSpec(
        num_scalar_prefetch=2, grid=(R // tm,),
        in_specs=[pl.BlockSpec((tm, D), lambda i, te, nt: (i, 0)),
                  pl.BlockSpec((tm, 1), lambda i, te, nt: (i, 0)),
                  pl.BlockSpec((None, None, D, F), lambda i, te, nt: (layer, te[i], 0, 0)),
                  pl.BlockSpec((None, None, D, F), lambda i, te, nt: (layer, te[i], 0, 0)),
                  pl.BlockSpec((None, None, F, D), lambda i, te, nt: (layer, te[i], 0, 0))],
        out_specs=pl.BlockSpec((tm, D), lambda i, te, nt: (i, 0)))
    return pl.pallas_call(
        _moe_kernel, grid_spec=grid_spec, out_shape=jax.ShapeDtypeStruct((R, D), F32),
        compiler_params=_cparams("arbitrary"), name="moe_experts",
    )(tile_e, n_used, xs, row_w, w_gate, w_up, w_down)


def _route_meta(route, tm, n_tiles):
    T = route.shape[0]
    n_assign = TOP_K * T
    e = route[:, :TOP_K].astype(jnp.int32).reshape(-1)
    w = route[:, TOP_K:2 * TOP_K].reshape(-1)
    order = jnp.argsort(e, stable=True).astype(jnp.int32)
    se = e[order]
    start = jnp.searchsorted(se, jnp.arange(N_EXPERTS + 1, dtype=jnp.int32), side='left').astype(jnp.int32)
    counts = start[1:] - start[:-1]
    pstart = jnp.concatenate([jnp.zeros((1,), jnp.int32), jnp.cumsum(((counts + tm - 1) // tm) * tm)]).astype(jnp.int32)
    n_used = (pstart[N_EXPERTS] // tm).reshape(1)
    tile_e = jnp.searchsorted(pstart[1:], jnp.arange(n_tiles, dtype=jnp.int32) * tm, side='right')
    tile_e = jnp.minimum(tile_e, N_EXPERTS - 1).astype(jnp.int32)
    r = jnp.arange(n_tiles * tm, dtype=jnp.int32)
    re = jnp.repeat(tile_e, tm)
    k = r - pstart[re]
    valid = (k >= 0) & (k < counts[re])
    src = order[jnp.clip(start[re] + k, 0, n_assign - 1)]
    row_tok = jnp.where(valid, src // TOP_K, 0)
    row_w = jnp.where(valid, w[src], 0.0)
    dest_sorted = pstart[se] + jnp.arange(n_assign, dtype=jnp.int32) - start[se]
    pos = dest_sorted[jnp.argsort(order).astype(jnp.int32)]
    return row_tok, row_w.reshape(-1, 1), pos.reshape(T, TOP_K), tile_e, n_used


def _final_kernel(x_ref, g_ref, o_ref):
    o_ref[...] = _rms(x_ref[...], g_ref[...])


def _final_call(x, g):
    T, D = x.shape
    tm = TOKEN_TILE
    return pl.pallas_call(
        _final_kernel, grid=(T // tm,),
        in_specs=[pl.BlockSpec((tm, D), lambda i: (i, 0)), pl.BlockSpec((1, D), lambda i: (0, 0))],
        out_specs=pl.BlockSpec((tm, D), lambda i: (i, 0)),
        out_shape=jax.ShapeDtypeStruct((T, D), F32), compiler_params=_cparams("parallel"), name="final_norm",
    )(x, g)


def _block_diag(w):
    n, c, d = w.shape
    return jnp.einsum('ncd,nm->ncmd', w, jnp.eye(n, dtype=w.dtype)).reshape(n * c, n * d)


def _half_swap(w):
    half = w.shape[-1] // 2
    return jnp.concatenate([w[..., half:], w[..., :half]], axis=-1)


def _layer_weights(l, p, dims):
    lw, qr, kvr, gk_w, gv_w = dims
    D = p['w_in'].shape[1]
    row = lambda a: a.reshape(1, -1).astype(F32)
    w_in = p['w_in'][l]
    sizes = (lw, lw, qr, kvr, QK_ROPE, gk_w, gk_w, gv_w, p['gla_wa2'].shape[1], gv_w)
    x_lru, y_lru, c_q, c_kv, k_pe, g_q, g_k, g_v, g_a, g_o = jnp.split(w_in, list(np.cumsum(sizes)[:-1]), axis=1)
    tail_pad = LANES - 2 * QK_ROPE - g_a.shape[1]
    tail = jnp.concatenate([k_pe, _half_swap(k_pe), g_a, jnp.zeros((D, tail_pad), F32)], axis=1)
    w_in_p = jnp.concatenate([x_lru, y_lru, c_q, c_kv, g_q, g_k, g_v, g_o, tail], axis=1).astype(BF16)

    w_uq = p['w_uq'][l].reshape(qr, MLA_HEADS, QK_NOPE + QK_ROPE)
    nope = w_uq[:, :, :QK_NOPE].reshape(qr, MLA_HEADS * QK_NOPE)
    pe = w_uq[:, :, QK_NOPE:]
    widen = lambda a: jnp.pad(a, ((0, 0), (0, 0), (0, LANES - QK_ROPE))).reshape(qr, MLA_HEADS * LANES)
    w_q = jnp.concatenate([nope, widen(pe), widen(_half_swap(pe))], axis=1).astype(BF16)
    w_abs = _block_diag(jnp.transpose(p['w_uk'][l], (1, 2, 0))).astype(BF16)
    wa2 = jnp.zeros((LANES, gk_w), F32).at[2 * QK_ROPE:2 * QK_ROPE + g_a.shape[1]].set(p['gla_wa2'][l]).astype(BF16)

    w_uv = p['w_uv'][l]
    mw = MLA_HEADS * V_HEAD
    w_uv_cat = w_uv.reshape(kvr, mw)
    w_uv_rows = _block_diag(jnp.transpose(w_uv, (1, 0, 2)))
    n_a, n_b, n_c = jnp.split(p['out_norm_g'][l], [lw, lw + mw])
    n_r = N_GROUPS + N_EXPERTS
    w_r = jnp.concatenate([p['router_wg'][l], p['router_we'][l], jnp.zeros((D, LANES - n_r), F32)], axis=1)
    b_r = jnp.concatenate([p['router_bg'][l], p['router_be'][l], jnp.zeros((LANES - n_r,), F32)])
    return dict(
        g1=row(p['norm1_g'][l]), w_in=w_in_p, q_g=row(p['q_norm_g'][l]), w_q=w_q, w_abs=w_abs,
        kv_g=row(p['kv_norm_g'][l]), wa2=wa2, ba=row(p['gla_ba'][l]),
        conv_w=p['conv_w'][l], conv_b=row(p['conv_b'][l]),
        lru_wa=_block_diag(p['lru_wa'][l]).astype(BF16), lru_ba=row(p['lru_ba'][l]),
        lru_wi=_block_diag(p['lru_wi'][l]).astype(BF16), lru_bi=row(p['lru_bi'][l]),
        lru_lam=row(p['lru_lambda'][l]), n_a=row(n_a), n_b=row(n_b), n_c=row(n_c),
        gla_g=row(jnp.tile(p['gla_norm_g'][l], GLA_HEADS)),
        w_uv_cat=w_uv_cat.astype(BF16), w_uv_rows_t=w_uv_rows.T.astype(BF16),
        w_out=p['w_out'][l].astype(BF16), g2=row(p['norm2_g'][l]), w_r=w_r, b_r=row(b_r),
    )


def _rope_tables(positions):
    half = QK_ROPE // 2
    inv = ROPE_THETA ** (-np.arange(half, dtype=np.float64) / half)
    ang = np.asarray(positions, np.float64)[:, None] * inv
    zeros = np.zeros((ang.shape[0], LANES - QK_ROPE))
    cos = np.concatenate([np.cos(ang), np.cos(ang), zeros], axis=1)
    sin = np.concatenate([-np.sin(ang), np.sin(ang), zeros], axis=1)
    return jnp.asarray(cos, F32), jnp.asarray(sin, F32)


def _gla_state_to_rows(s):
    B = s.shape[0]
    eye = jnp.eye(GLA_HEADS, dtype=s.dtype)
    return jnp.einsum('bhde,hg->bhegd', s, eye).reshape(B, GLA_HEADS * GLA_DV, GLA_HEADS * GLA_DK)


def _gla_rows_to_state(st):
    B = st.shape[0]
    s5 = st.reshape(B, GLA_HEADS, GLA_DV, GLA_HEADS, GLA_DK)
    blocks = jnp.stack([s5[:, h, :, h, :] for h in range(GLA_HEADS)], axis=1)
    return jnp.swapaxes(blocks, -1, -2)


def kernel(x_prompt, x_sample, cache_ckv, cache_kpe, page_table, state_conv, state_lru, state_gla, norm1_g, w_in, conv_w, conv_b, lru_wa, lru_ba, lru_wi, lru_bi, lru_lambda, q_norm_g, w_uq, kv_norm_g, w_uk, w_uv, gla_wa2, gla_ba, gla_norm_g, out_norm_g, w_out, norm2_g, router_wg, router_bg, router_we, router_be, w_gate, w_up, w_down, final_norm_g):
    p = dict(norm1_g=norm1_g, w_in=w_in, conv_w=conv_w, conv_b=conv_b, lru_wa=lru_wa, lru_ba=lru_ba,
             lru_wi=lru_wi, lru_bi=lru_bi, lru_lambda=lru_lambda, q_norm_g=q_norm_g, w_uq=w_uq,
             kv_norm_g=kv_norm_g, w_uk=w_uk, w_uv=w_uv, gla_wa2=gla_wa2, gla_ba=gla_ba,
             gla_norm_g=gla_norm_g, out_norm_g=out_norm_g, w_out=w_out, norm2_g=norm2_g,
             router_wg=router_wg, router_bg=router_bg, router_we=router_we, router_be=router_be)
    Bp, Sp, D = x_prompt.shape
    Bs, Ss, _ = x_sample.shape
    depth = w_in.shape[0]
    lw = state_lru.shape[-1]
    qr = q_norm_g.shape[-1]
    kvr = kv_norm_g.shape[-1]
    gk_w = GLA_HEADS * GLA_DK
    gv_w = GLA_HEADS * GLA_DV
    dims = (lw, qr, kvr, gk_w, gv_w)
    assert kvr == LANES and gk_w == LANES
    Tp, Ts = Bp * Sp, Bs * Ss
    T = Tp + Ts
    past = page_table.shape[1] * cache_ckv.shape[2]

    cos_t, sin_t = _rope_tables(np.concatenate([np.tile(np.arange(Sp), Bp), np.tile(past + np.arange(Ss), Bs)]))
    wg_b, wu_b, wd_b = w_gate.astype(BF16), w_up.astype(BF16), w_down.astype(BF16)
    cache_kpe_t = jnp.swapaxes(cache_kpe, 2, 3)
    n_tiles = (TOP_K * T + N_EXPERTS * (MOE_TILE - 1)) // MOE_TILE + 1

    x = jnp.concatenate([x_prompt.reshape(Tp, D), x_sample.reshape(Ts, D)], axis=0)
    outs = {k: [] for k in ('ckv_p', 'kpe_p', 'ckv_s', 'kpe_s', 'conv_p', 'conv_s', 'lru_p', 'lru_s', 'gla_p', 'gla_s')}
    for l in range(depth):
        lw_ = _layer_weights(l, p, dims)
        zl, qcat, kcat, kcat_t, ckv_n, kpe_n, g = _pre_call(x, lw_, cos_t, sin_t, dims)

        a_p, conv_p, lru_p = _lru_call(zl, 0, Bp, Sp, jnp.zeros((Bp, CONV_WIDTH - 1, lw), F32),
                                       jnp.zeros((Bp, lw), F32), lw_, BF16)
        a_s, conv_s, lru_s = _lru_call(zl, Tp, Bs, Ss, state_conv[l], state_lru[l], lw_, F32)

        c_p, gla_p = _gla_call(g, 0, Bp, Sp, jnp.zeros((Bp, gv_w, gk_w), F32), lw_, BF16, gk_w, gv_w)
        c_s, gla_s = _gla_call(g, Tp, Bs, Ss, _gla_state_to_rows(state_gla[l]), lw_, F32, gk_w, gv_w)

        b_p = _attn_prompt_call(qcat, kcat, kcat_t, Bp, Sp, lw_, kvr)
        b_s = _attn_sample_call(page_table, qcat[Tp:].reshape(Bs, Ss, -1), ckv_n[Tp:].reshape(Bs, Ss, kvr),
                                kpe_n[Tp:].reshape(Bs, Ss, QK_ROPE), cache_ckv, cache_kpe_t, l, lw_, kvr)

        a = jnp.concatenate([a_p, a_s.astype(BF16)], axis=0)
        b = jnp.concatenate([b_p, b_s.reshape(Ts, -1).astype(BF16)], axis=0)
        c = jnp.concatenate([c_p, c_s.astype(BF16)], axis=0)
        x1, hn, route = _post_call(a, b, c, x, lw_)

        row_tok, row_w, pos, tile_e, n_used = _route_meta(route, MOE_TILE, n_tiles)
        ys = _moe_call(hn[row_tok], row_w, tile_e, n_used, wg_b, wu_b, wd_b, l)
        x = x1 + ys[pos[:, 0]] + ys[pos[:, 1]]

        outs['ckv_p'].append(ckv_n[:Tp].reshape(Bp, Sp, kvr))
        outs['kpe_p'].append(kpe_n[:Tp].reshape(Bp, Sp, QK_ROPE))
        outs['ckv_s'].append(ckv_n[Tp:].reshape(Bs, Ss, kvr))
        outs['kpe_s'].append(kpe_n[Tp:].reshape(Bs, Ss, QK_ROPE))
        outs['conv_p'].append(conv_p)
        outs['conv_s'].append(conv_s)
        outs['lru_p'].append(lru_p.reshape(Bp, lw))
        outs['lru_s'].append(lru_s.reshape(Bs, lw))
        outs['gla_p'].append(_gla_rows_to_state(gla_p))
        outs['gla_s'].append(_gla_rows_to_state(gla_s))

    y = _final_call(x, final_norm_g.reshape(1, D))
    st = {k: jnp.stack(v) for k, v in outs.items()}
    return (y[:Tp].reshape(Bp, Sp, D), y[Tp:].reshape(Bs, Ss, D),
            st['ckv_p'], st['kpe_p'], st['ckv_s'], st['kpe_s'], st['conv_p'], st['conv_s'],
            st['lru_p'], st['lru_s'], st['gla_p'], st['gla_s'])
```

```python
import functools
import math

import numpy as np
import jax
import jax.numpy as jnp
from jax import lax
from jax.experimental import pallas as pl
from jax.experimental.pallas import tpu as pltpu

F32 = jnp.float32
BF16 = jnp.bfloat16

LRU_BLOCKS = 4
CONV_WIDTH = 4
LRU_C = 8.0
MLA_HEADS = 8
QK_NOPE = 64
QK_ROPE = 32
V_HEAD = 64
ROPE_THETA = 10000.0
GLA_HEADS = 4
GLA_DK = 32
GLA_DV = 64
GLA_TAU = 16.0
GLA_CHUNK = 16
N_GROUPS = 4
EXPERTS_PER_GROUP = 8
N_EXPERTS = N_GROUPS * EXPERTS_PER_GROUP
TOP_K = 2
EPS = 1e-6

LANES = 128
SUBLANES = 8
BF16_ROWS = 16
VMEM_LIMIT_BYTES = 56 * 1024 * 1024

TOKEN_TILE = 512
SEQ_TILE = 256
MOE_TILE = 256
PAGES_PER_STEP = 16
NEG_BIG = -1e30


def _cparams(*sem):
    return pltpu.CompilerParams(dimension_semantics=sem, vmem_limit_bytes=VMEM_LIMIT_BYTES)


def _rms(x, g):
    return x * lax.rsqrt(jnp.mean(x * x, axis=-1, keepdims=True) + EPS) * g


def _dot(a, b):
    return jnp.dot(a, b, preferred_element_type=F32)


def _dot_nt(a, b):
    return lax.dot_general(a, b, (((1,), (1,)), ((), ())), preferred_element_type=F32)


def _softplus(x):
    return jnp.maximum(x, 0.0) + jnp.log1p(jnp.exp(-jnp.abs(x)))


def _shift_rows(x, d, fill):
    row = lax.broadcasted_iota(jnp.int32, x.shape, 0)
    return jnp.where(row >= d, pltpu.roll(x, d, 0), fill)


def _pre_kernel(x_ref, g1_ref, win_ref, qg_ref, wq_ref, wabs_ref, kvg_ref, wa2_ref, ba_ref,
                cos_ref, sin_ref, zl_ref, q_ref, k_ref, kt_ref, ckv_ref, kpe_ref, g_ref, *, dims):
    lw, qr, kvr, gk_w, gv_w = dims
    xn = _rms(x_ref[...], g1_ref[...])
    z = _dot(xn.astype(BF16), win_ref[...])
    o = 2 * lw
    zl_ref[...] = z[:, :o]
    cq = z[:, o:o + qr]
    o += qr
    ckv = z[:, o:o + kvr]
    o += kvr
    gqk = z[:, o:o + 2 * gk_w]
    o += 2 * gk_w
    gvo = z[:, o:o + 2 * gv_w]
    o += 2 * gv_w
    tail = z[:, o:o + LANES]
    cos = cos_ref[...]
    sin = sin_ref[...]
    scale = (QK_NOPE + QK_ROPE) ** -0.5 * math.log2(math.e)

    ckv_n = _rms(ckv, kvg_ref[...])
    kpe = tail * cos + pltpu.roll(tail, LANES - QK_ROPE, 1) * sin
    ckv_ref[...] = ckv_n
    kpe_ref[...] = kpe[:, :QK_ROPE]
    k_ref[:, :kvr] = ckv_n.astype(BF16)
    k_ref[:, kvr:] = kpe.astype(BF16)
    ones = jnp.ones((kt_ref.shape[0] - kvr, ckv_n.shape[0]), F32)
    kt_ref[...] = jnp.concatenate([ckv_n.T, ones], axis=0).astype(BF16)

    cqn = _rms(cq, qg_ref[...]).astype(BF16)
    qall = _dot(cqn, wq_ref[...])
    n_nope = MLA_HEADS * QK_NOPE
    q_abs = _dot(qall[:, :n_nope].astype(BF16), wabs_ref[...]) * scale
    for h in range(MLA_HEADS):
        pe = qall[:, n_nope + h * LANES:n_nope + (h + 1) * LANES]
        sw = qall[:, n_nope + (MLA_HEADS + h) * LANES:n_nope + (MLA_HEADS + h + 1) * LANES]
        q_ref[:, 2 * h * LANES:(2 * h + 1) * LANES] = q_abs[:, h * kvr:(h + 1) * kvr].astype(BF16)
        q_ref[:, (2 * h + 1) * LANES:(2 * h + 2) * LANES] = ((pe * cos + sw * sin) * scale).astype(BF16)

    la_pre = _dot(tail.astype(BF16), wa2_ref[...]) + ba_ref[...]
    la = -_softplus(-la_pre) * (1.0 / GLA_TAU)
    g_ref[:, :gk_w] = gqk[:, :gk_w] * (GLA_DK ** -0.5)
    g_ref[:, gk_w:2 * gk_w] = gqk[:, gk_w:]
    g_ref[:, 2 * gk_w:3 * gk_w] = la
    g_ref[:, 3 * gk_w:] = gvo


def _pre_call(x, lw_, cos_t, sin_t, dims):
    T, D = x.shape
    lw, qr, kvr, gk_w, gv_w = dims
    tm = TOKEN_TILE
    assert T % tm == 0
    row = lambda i: (i, 0)
    full = lambda i: (0, 0)
    wspec = lambda a: pl.BlockSpec(a.shape, full)
    ins = [x, lw_['g1'], lw_['w_in'], lw_['q_g'], lw_['w_q'], lw_['w_abs'], lw_['kv_g'], lw_['wa2'], lw_['ba']]
    in_specs = [pl.BlockSpec((tm, D), row)] + [wspec(a) for a in ins[1:]]
    in_specs += [pl.BlockSpec((tm, LANES), row), pl.BlockSpec((tm, LANES), row)]
    out_shape = (
        jax.ShapeDtypeStruct((T, 2 * lw), F32),
        jax.ShapeDtypeStruct((T, 2 * LANES * MLA_HEADS), BF16),
        jax.ShapeDtypeStruct((T, 2 * LANES), BF16),
        jax.ShapeDtypeStruct((T // tm, kvr + BF16_ROWS, tm), BF16),
        jax.ShapeDtypeStruct((T, kvr), F32),
        jax.ShapeDtypeStruct((T, QK_ROPE), F32),
        jax.ShapeDtypeStruct((T, 3 * gk_w + 2 * gv_w), F32),
    )
    out_specs = tuple(pl.BlockSpec((None, s.shape[1], tm), lambda i: (i, 0, 0)) if len(s.shape) == 3
                      else pl.BlockSpec((tm, s.shape[1]), row) for s in out_shape)
    return pl.pallas_call(
        functools.partial(_pre_kernel, dims=dims),
        grid=(T // tm,), in_specs=in_specs, out_specs=out_specs, out_shape=out_shape,
        compiler_params=_cparams("parallel"), name="pre_proj",
    )(*ins, cos_t, sin_t)


def _lru_kernel(zl_ref, cbuf_ref, h0_ref, cw_ref, cb_ref, wa_ref, ba_ref, wi_ref, bi_ref, lam_ref, na_ref,
                a_ref, conv_ref, hout_ref, xbuf, hcar, *, ts, lw):
    i = pl.program_id(1)
    last = pl.num_programs(1) - 1
    pad = SUBLANES
    nbuf = CONV_WIDTH - 1

    @pl.when(i == 0)
    def _():
        xbuf[0:pad, :] = jnp.zeros((pad, lw), F32)
        xbuf[pad - nbuf:pad, :] = cbuf_ref[...]
        hcar[...] = h0_ref[...]

    x = zl_ref[:, :lw]
    y = zl_ref[:, lw:]
    xbuf[pad:pad + ts, :] = x
    xc = cb_ref[...] + cw_ref[nbuf:nbuf + 1, :] * x
    for k in range(nbuf):
        xc = xc + cw_ref[k:k + 1, :] * xbuf[pad - nbuf + k:pad - nbuf + k + ts, :]

    @pl.when(i == last)
    def _():
        conv_ref[...] = xbuf[pad + ts - nbuf:pad + ts, :]

    xbuf[0:pad, :] = xbuf[ts:ts + pad, :]

    xb = xc.astype(BF16)
    r = jax.nn.sigmoid(_dot(xb, wa_ref[...]) + ba_ref[...])
    gi = jax.nn.sigmoid(_dot(xb, wi_ref[...]) + bi_ref[...])
    log_a = (-LRU_C) * r * _softplus(-lam_ref[...])
    a = jnp.exp(log_a)
    th = jnp.tanh(log_a)
    u = jnp.sqrt(-2.0 * th / (1.0 - th)) * (gi * xc)

    d = 1
    while d < ts:
        u = a * _shift_rows(u, d, 0.0) + u
        a = a * _shift_rows(a, d, 1.0)
        d *= 2
    h = a * hcar[...] + u
    hcar[...] = h[ts - 1:ts, :]

    @pl.when(i == last)
    def _():
        hout_ref[...] = h[ts - 1:ts, :]

    out_a = h * jax.nn.gelu(y)
    a_ref[...] = _rms(out_a, na_ref[...]).astype(a_ref.dtype)


def _lru_call(zl, row_off, B, S, cbuf, h0, lw_, out_dtype):
    lw = h0.shape[-1]
    ts = min(S, SEQ_TILE)
    assert S % ts == 0 and row_off % ts == 0 and S >= CONV_WIDTH - 1
    n = S // ts
    off = row_off // ts
    full = lambda b, i: (0, 0)
    ws = [lw_['conv_w'], lw_['conv_b'], lw_['lru_wa'], lw_['lru_ba'], lw_['lru_wi'], lw_['lru_bi'],
          lw_['lru_lam'], lw_['n_a']]
    in_specs = [
        pl.BlockSpec((ts, 2 * lw), lambda b, i: (off + b * n + i, 0)),
        pl.BlockSpec((None, CONV_WIDTH - 1, lw), lambda b, i: (b, 0, 0)),
        pl.BlockSpec((None, 1, lw), lambda b, i: (b, 0, 0)),
    ] + [pl.BlockSpec(a.shape, full) for a in ws]
    out_shape = (
        jax.ShapeDtypeStruct((B * S, lw), out_dtype),
        jax.ShapeDtypeStruct((B, CONV_WIDTH - 1, lw), F32),
        jax.ShapeDtypeStruct((B, 1, lw), F32),
    )
    out_specs = (
        pl.BlockSpec((ts, lw), lambda b, i: (b * n + i, 0)),
        pl.BlockSpec((None, CONV_WIDTH - 1, lw), lambda b, i: (b, 0, 0)),
        pl.BlockSpec((None, 1, lw), lambda b, i: (b, 0, 0)),
    )
    return pl.pallas_call(
        functools.partial(_lru_kernel, ts=ts, lw=lw),
        grid=(B, n), in_specs=in_specs, out_specs=out_specs, out_shape=out_shape,
        scratch_shapes=[pltpu.VMEM((ts + SUBLANES, lw), F32), pltpu.VMEM((1, lw), F32)],
        compiler_params=_cparams("parallel", "arbitrary"), name="rg_lru",
    )(zl, cbuf, h0.reshape(B, 1, lw), *ws)


def _gla_kernel(g_ref, s0_ref, gn_ref, nc_ref, c_ref, sout_ref, st, obuf, *, ts, gk_w, gv_w):
    i = pl.program_id(1)
    last = pl.num_programs(1) - 1
    C = GLA_CHUNK
    rows = min(ts, C)
    n_chunks = max(ts // C, 1)
    log2c = int(math.log2(C))

    @pl.when(i == 0)
    def _():
        st[...] = s0_ref[...]

    dk_sh, dv_sh = int(math.log2(GLA_DK)), int(math.log2(GLA_DV))
    hd = lax.broadcasted_iota(jnp.int32, (gk_w, gv_w), 0) >> dk_sh
    he = lax.broadcasted_iota(jnp.int32, (gk_w, gv_w), 1) >> dv_sh
    same = (hd == he).astype(BF16)
    he_t = lax.broadcasted_iota(jnp.int32, (gv_w, gk_w), 0) >> dv_sh
    hd_t = lax.broadcasted_iota(jnp.int32, (gv_w, gk_w), 1) >> dk_sh
    same_t = (he_t == hd_t).astype(F32)
    sel_t = lax.broadcasted_iota(jnp.int32, (C, C * C), 0)
    sel_r = lax.broadcasted_iota(jnp.int32, (C, C * C), 1) >> log2c
    sel = (sel_t == sel_r).astype(BF16)
    srow = lax.broadcasted_iota(jnp.int32, (C, 1), 0)

    def chunk(c, carry):
        r0 = pl.multiple_of(c * rows, rows)
        blk = g_ref[pl.ds(r0, rows), :]
        if rows < C:
            blk = jnp.concatenate([blk, jnp.zeros((C - rows, blk.shape[1]), F32)], axis=0)
        q = blk[:, :gk_w]
        k = blk[:, gk_w:2 * gk_w]
        la = blk[:, 2 * gk_w:3 * gk_w]
        v = blk[:, 3 * gk_w:3 * gk_w + gv_w]
        cum = la
        d = 1
        while d < C:
            cum = cum + _shift_rows(cum, d, 0.0)
            d *= 2
        s_t = st[...]
        o_inter = _dot_nt((q * jnp.exp(cum)).astype(BF16), s_t.astype(BF16))
        pieces = []
        for t in range(C):
            diff = jnp.where(srow <= t, cum[t:t + 1, :] - cum, NEG_BIG)
            pieces.append(q[t:t + 1, :] * k * jnp.exp(diff))
        w = jnp.concatenate(pieces, axis=0)
        att = _dot(w.astype(BF16), same)
        xv = att * jnp.concatenate([v] * C, axis=0)
        o = o_inter + _dot(sel, xv.astype(BF16))
        obuf[pl.ds(r0, rows), :] = o[:rows, :]
        cl = cum[C - 1:C, :]
        kdec = k * jnp.exp(cl - cum)
        upd = lax.dot_general(v.astype(BF16), kdec.astype(BF16), (((0,), (0,)), ((), ())),
                              preferred_element_type=F32)
        st[...] = s_t * jnp.exp(cl) + upd * same_t
        return carry

    lax.fori_loop(0, n_chunks, chunk, 0)

    @pl.when(i == last)
    def _():
        sout_ref[...] = st[...]

    o = obuf[...]
    go = g_ref[:, 3 * gk_w + gv_w:]
    e64 = (lax.broadcasted_iota(jnp.int32, (gv_w, gv_w), 0) >> dv_sh
           == lax.broadcasted_iota(jnp.int32, (gv_w, gv_w), 1) >> dv_sh).astype(BF16)
    osq = o * o
    osq_hi = osq.astype(BF16)
    osq_lo = (osq - osq_hi.astype(F32)).astype(BF16)
    ms = (_dot(osq_hi, e64) + _dot(osq_lo, e64)) * (1.0 / GLA_DV)
    out_c = o * lax.rsqrt(ms + EPS) * gn_ref[...] * (go * jax.nn.sigmoid(go))
    c_ref[...] = _rms(out_c, nc_ref[...]).astype(c_ref.dtype)


def _gla_call(g, row_off, B, S, s0t, lw_, out_dtype, gk_w, gv_w):
    ts = min(S, SEQ_TILE)
    assert S % ts == 0 and row_off % ts == 0 and (ts % GLA_CHUNK == 0 or ts < GLA_CHUNK)
    n = S // ts
    off = row_off // ts
    gw = g.shape[1]
    full = lambda b, i: (0, 0)
    out_shape = (jax.ShapeDtypeStruct((B * S, gv_w), out_dtype),
                 jax.ShapeDtypeStruct((B, gv_w, gk_w), F32))
    return pl.pallas_call(
        functools.partial(_gla_kernel, ts=ts, gk_w=gk_w, gv_w=gv_w),
        grid=(B, n),
        in_specs=[pl.BlockSpec((ts, gw), lambda b, i: (off + b * n + i, 0)),
                  pl.BlockSpec((None, gv_w, gk_w), lambda b, i: (b, 0, 0)),
                  pl.BlockSpec((1, gv_w), full), pl.BlockSpec((1, gv_w), full)],
        out_specs=(pl.BlockSpec((ts, gv_w), lambda b, i: (b * n + i, 0)),
                   pl.BlockSpec((None, gv_w, gk_w), lambda b, i: (b, 0, 0))),
        out_shape=out_shape,
        scratch_shapes=[pltpu.VMEM((gv_w, gk_w), F32), pltpu.VMEM((ts, gv_w), F32)],
        compiler_params=_cparams("parallel", "arbitrary"), name="gla",
    )(g, s0t, lw_['gla_g'], lw_['n_c'])


def _tree(op, xs):
    xs = list(xs)
    while len(xs) > 1:
        xs = [op(xs[i], xs[i + 1]) if i + 1 < len(xs) else xs[i] for i in range(0, len(xs), 2)]
    return xs[0]


def _attn_prompt_kernel(q_ref, k_ref, kt_ref, wuvt_ref, nb_ref, o_ref, m_scr, acc_scr, *, tq, kvr):
    i = pl.program_id(1)
    kw = 2 * LANES
    m_scr[...] = jnp.full(m_scr.shape, NEG_BIG, F32)
    acc_scr[...] = jnp.zeros(acc_scr.shape, F32)

    def block(j, masked):
        kb = k_ref[pl.ds(pl.multiple_of(j * tq, tq), tq), :]
        kbt = kt_ref[j]
        if masked:
            key = lax.broadcasted_iota(jnp.int32, (tq, tq), 0)
            qry = lax.broadcasted_iota(jnp.int32, (tq, tq), 1)
            keep = key <= qry
        qk = lambda h: _dot_nt(kb, q_ref[:, h * kw:(h + 1) * kw])
        st_next = qk(0)
        for h in range(MLA_HEADS):
            st = st_next
            if h + 1 < MLA_HEADS:
                st_next = qk(h + 1)
            if masked:
                st = jnp.where(keep, st, NEG_BIG)
            m_old = m_scr[h]
            m_new = jnp.maximum(m_old, jnp.max(st, axis=0, keepdims=True))
            alpha = jnp.exp2(m_old - m_new)
            pt = jnp.exp2(st - m_new).astype(BF16)
            acc_scr[h] = alpha * acc_scr[h] + _dot(kbt, pt)
            m_scr[h] = m_new

    def body(j, carry):
        block(j, False)
        return carry

    lax.fori_loop(0, i, body, 0)
    block(i, True)

    out_t = jnp.zeros((wuvt_ref.shape[0], tq), F32)
    for h in range(MLA_HEADS):
        acc = acc_scr[h]
        o_t = (acc[:kvr, :] / acc[kvr:kvr + 1, :]).astype(BF16)
        out_t = out_t + _dot(wuvt_ref[:, h * kvr:(h + 1) * kvr], o_t)
    ms = jnp.mean(out_t * out_t, axis=0, keepdims=True)
    out_t = out_t * lax.rsqrt(ms + EPS) * nb_ref[...]
    o_ref[...] = out_t.T.astype(o_ref.dtype)


def _attn_prompt_call(qcat, kcat, kcat_t, B, S, lw_, kvr):
    tq = kcat_t.shape[2]
    assert S % tq == 0
    n = S // tq
    mw = lw_['w_uv_rows_t'].shape[0]
    full = lambda b, i: (0, 0)
    return pl.pallas_call(
        functools.partial(_attn_prompt_kernel, tq=tq, kvr=kvr),
        grid=(B, n),
        in_specs=[pl.BlockSpec((tq, qcat.shape[1]), lambda b, i: (b * n + i, 0)),
                  pl.BlockSpec((S, kcat.shape[1]), lambda b, i: (b, 0)),
                  pl.BlockSpec((n, kcat_t.shape[1], tq), lambda b, i: (b, 0, 0)),
                  pl.BlockSpec(lw_['w_uv_rows_t'].shape, full), pl.BlockSpec((mw, 1), full)],
        out_specs=pl.BlockSpec((tq, mw), lambda b, i: (b * n + i, 0)),
        out_shape=jax.ShapeDtypeStruct((B * S, mw), BF16),
        scratch_shapes=[pltpu.VMEM((MLA_HEADS, 1, tq), F32),
                        pltpu.VMEM((MLA_HEADS, kcat_t.shape[1], tq), F32)],
        compiler_params=_cparams("parallel", "arbitrary"), name="attn_prompt",
    )(qcat, kcat, kcat_t, lw_['w_uv_rows_t'], lw_['n_b'].reshape(mw, 1))


def _attn_sample_kernel(pt_ref, q_ref, nckv_ref, nkpe_ref, *rest, n_pp, n_steps, sq, kvr, page):
    ckv_refs = rest[:n_pp]
    kpe_refs = rest[n_pp:2 * n_pp]
    wuv_ref, nb_ref, o_ref, q_scr, s_scr, v_scr = rest[2 * n_pp:]
    g = pl.program_id(1)
    nrow = MLA_HEADS * sq

    @pl.when(g == 0)
    def _():
        for h in range(MLA_HEADS):
            q_scr[h * sq:(h + 1) * sq, :] = q_ref[:, 2 * h * LANES:(2 * h + 2) * LANES].astype(F32)

    qa = q_scr[:, :kvr].astype(BF16)
    qp = q_scr[:, kvr:kvr + QK_ROPE].astype(BF16)
    for j in range(n_pp):
        ck = ckv_refs[j][...].astype(BF16)
        v_scr[g * n_pp + j] = ck
        s_scr[g, :, j * page:(j + 1) * page] = _dot_nt(qa, ck) + _dot(qp, kpe_refs[j][...].astype(BF16))

    @pl.when(g == n_steps - 1)
    def _():
        zpad = lambda a: jnp.concatenate([a, jnp.zeros((page - sq, a.shape[1]), F32)], axis=0)
        ck_new = zpad(nckv_ref[...]).astype(BF16)
        kp_new = zpad(nkpe_ref[...]).astype(BF16)
        tok = lax.broadcasted_iota(jnp.int32, (nrow, page), 0) & (sq - 1)
        key = lax.broadcasted_iota(jnp.int32, (nrow, page), 1)
        s_new = jnp.where(key <= tok, _dot_nt(qa, ck_new) + _dot_nt(qp, kp_new), NEG_BIG)
        tile = lambda i: s_scr[i // n_pp, :, (i % n_pp) * page:(i % n_pp + 1) * page]
        n_past = n_steps * n_pp
        m = jnp.max(_tree(jnp.maximum, [tile(i) for i in range(n_past)] + [s_new]), axis=-1, keepdims=True)
        p_new = jnp.exp2(s_new - m)
        acc = _dot(p_new.astype(BF16), ck_new)
        psum = p_new
        for i in range(n_past):
            p = jnp.exp2(tile(i) - m)
            psum = psum + p
            acc = acc + _dot(p.astype(BF16), v_scr[i])
        l = jnp.sum(psum, axis=-1, keepdims=True)
        o_lat = (acc / l).astype(BF16)
        mw = wuv_ref.shape[1]
        r = _dot(o_lat, wuv_ref[...])
        rh = lax.broadcasted_iota(jnp.int32, (nrow, mw), 0) >> int(math.log2(sq))
        ch = lax.broadcasted_iota(jnp.int32, (nrow, mw), 1) >> int(math.log2(V_HEAD))
        r = jnp.where(rh == ch, r, 0.0)
        out = r[0:sq, :]
        for h in range(1, MLA_HEADS):
            out = out + r[h * sq:(h + 1) * sq, :]
        o_ref[...] = _rms(out, nb_ref[...])


def _attn_sample_call(page_table, qs, nckv, nkpe, cache_ckv, cache_kpe_t, layer, lw_, kvr):
    B, sq, qw = qs.shape
    n_pages = page_table.shape[1]
    page = cache_ckv.shape[2]
    n_pp = math.gcd(PAGES_PER_STEP, n_pages)
    n_steps = n_pages // n_pp
    mw = lw_['w_uv_cat'].shape[1]
    nrow = MLA_HEADS * sq
    assert sq == SUBLANES and sq <= page

    def page_spec(shape, j):
        return pl.BlockSpec((None, None) + shape,
                            lambda b, g, pt: (layer, pt[b * n_pages + g * n_pp + j], 0, 0))

    in_specs = [pl.BlockSpec((None, sq, qw), lambda b, g, pt: (b, 0, 0)),
                pl.BlockSpec((None, sq, kvr), lambda b, g, pt: (b, 0, 0)),
                pl.BlockSpec((None, sq, QK_ROPE), lambda b, g, pt: (b, 0, 0))]
    in_specs += [page_spec((page, kvr), j) for j in range(n_pp)]
    in_specs += [page_spec((QK_ROPE, page), j) for j in range(n_pp)]
    in_specs += [pl.BlockSpec(lw_['w_uv_cat'].shape, lambda b, g, pt: (0, 0)),
                 pl.BlockSpec((1, mw), lambda b, g, pt: (0, 0))]
    grid_spec = pltpu.PrefetchScalarGridSpec(
        num_scalar_prefetch=1, grid=(B, n_steps), in_specs=in_specs,
        out_specs=pl.BlockSpec((None, sq, mw), lambda b, g, pt: (b, 0, 0)),
        scratch_shapes=[pltpu.VMEM((nrow, 2 * LANES), F32), pltpu.VMEM((n_steps, nrow, n_pp * page), F32),
                        pltpu.VMEM((n_pages, page, kvr), BF16)])
    return pl.pallas_call(
        functools.partial(_attn_sample_kernel, n_pp=n_pp, n_steps=n_steps, sq=sq, kvr=kvr, page=page),
        grid_spec=grid_spec, out_shape=jax.ShapeDtypeStruct((B, sq, mw), F32),
        compiler_params=_cparams("parallel", "arbitrary"), name="attn_sample",
    )(page_table.reshape(-1), qs, nckv, nkpe, *([cache_ckv] * n_pp), *([cache_kpe_t] * n_pp),
      lw_['w_uv_cat'], lw_['n_b'])


def _post_kernel(a_ref, b_ref, c_ref, x_ref, wo_ref, g2_ref, wr_ref, br_ref, x1_ref, hn_ref, rt_ref, cnt_ref,
                 cnt_scr, *, widths):
    wa, wb, wc = widths

    @pl.when(pl.program_id(0) == 0)
    def _():
        cnt_scr[...] = jnp.zeros(cnt_scr.shape, F32)

    mix = _dot(a_ref[...], wo_ref[:wa, :])
    mix = mix + _dot(b_ref[...], wo_ref[wa:wa + wb, :])
    mix = mix + _dot(c_ref[...], wo_ref[wa + wb:, :])
    x1 = x_ref[...] + mix
    x1_ref[...] = x1
    hn = _rms(x1, g2_ref[...])
    hn_ref[...] = hn
    logits = jnp.dot(hn, wr_ref[...], precision=lax.Precision.HIGHEST, preferred_element_type=F32) + br_ref[...]

    lane = lax.broadcasted_iota(jnp.int32, logits.shape, 1)
    lane_f = lane.astype(F32)
    big = float(LANES)
    is_g = lane < N_GROUPS
    gl = jnp.where(is_g, logits, NEG_BIG)
    gmax = jnp.max(gl, axis=-1, keepdims=True)
    gsel = jnp.min(jnp.where(gl == gmax, lane_f, big), axis=-1, keepdims=True)
    gprob = 1.0 / jnp.sum(jnp.where(is_g, jnp.exp(gl - gmax), 0.0), axis=-1, keepdims=True)
    lo = N_GROUPS + gsel * EXPERTS_PER_GROUP
    el = jnp.where((lane_f >= lo) & (lane_f < lo + EXPERTS_PER_GROUP), logits, NEG_BIG)
    v1 = jnp.max(el, axis=-1, keepdims=True)
    i1 = jnp.min(jnp.where(el == v1, lane_f, big), axis=-1, keepdims=True)
    el2 = jnp.where(lane_f == i1, NEG_BIG, el)
    v2 = jnp.max(el2, axis=-1, keepdims=True)
    i2 = jnp.min(jnp.where(el2 == v2, lane_f, big), axis=-1, keepdims=True)
    e21 = jnp.exp(v2 - v1)
    w1 = gprob / (1.0 + e21)
    w2 = w1 * e21
    e1 = i1 - N_GROUPS
    e2 = i2 - N_GROUPS

    oh1 = (lane_f == e1).astype(F32)
    oh2 = (lane_f == e2).astype(F32)
    both = oh1 + oh2
    tm = both.shape[0]
    tri = (lax.broadcasted_iota(jnp.int32, (tm, tm), 0) >= lax.broadcasted_iota(jnp.int32, (tm, tm), 1))
    incl = _dot(tri.astype(BF16), both.astype(BF16))
    base = cnt_scr[0:1, :] + incl - both
    r1 = jnp.sum(oh1 * base, axis=-1, keepdims=True)
    r2 = jnp.sum(oh2 * base, axis=-1, keepdims=True)
    cnt_scr[...] = cnt_scr[...] + incl[tm - 1:tm, :]
    cnt_ref[...] = cnt_scr[...]

    rt = jnp.where(lane == 0, e1, 0.0)
    rt = jnp.where(lane == 1, e2, rt)
    rt = jnp.where(lane == 2, w1, rt)
    rt = jnp.where(lane == 3, w2, rt)
    rt = jnp.where(lane == 4, r1, rt)
    rt = jnp.where(lane == 5, r2, rt)
    rt_ref[...] = rt


def _post_call(a, b, c, x, lw_):
    T, D = x.shape
    tm = TOKEN_TILE
    row = lambda i: (i, 0)
    full = lambda i: (0, 0)
    widths = (a.shape[1], b.shape[1], c.shape[1])
    ws = [lw_['w_out'], lw_['g2'], lw_['w_r'], lw_['b_r']]
    out_shape = (jax.ShapeDtypeStruct((T, D), F32), jax.ShapeDtypeStruct((T, D), F32),
                 jax.ShapeDtypeStruct((T, LANES), F32), jax.ShapeDtypeStruct((SUBLANES, LANES), F32))
    return pl.pallas_call(
        functools.partial(_post_kernel, widths=widths),
        grid=(T // tm,),
        in_specs=[pl.BlockSpec((tm, w), row) for w in widths] + [pl.BlockSpec((tm, D), row)]
        + [pl.BlockSpec(w.shape, full) for w in ws],
        out_specs=tuple(pl.BlockSpec((tm, s.shape[1]), row) for s in out_shape[:3])
        + (pl.BlockSpec((SUBLANES, LANES), full),),
        out_shape=out_shape, scratch_shapes=[pltpu.VMEM((SUBLANES, LANES), F32)],
        compiler_params=_cparams("arbitrary"), name="post_proj",
    )(a, b, c, x, *ws)


def _dest_kernel(rt_ref, ps_ref, d_ref):
    rt = rt_ref[...]
    lane = lax.broadcasted_iota(jnp.int32, rt.shape, 1)
    lane_f = lane.astype(F32)
    pick = lambda k: jnp.sum(jnp.where(lane == k, rt, 0.0), axis=-1, keepdims=True)
    ps = ps_ref[...]
    d1 = jnp.sum(jnp.where(lane_f == pick(0), ps, 0.0), axis=-1, keepdims=True) + pick(2 * TOP_K)
    d2 = jnp.sum(jnp.where(lane_f == pick(1), ps, 0.0), axis=-1, keepdims=True) + pick(2 * TOP_K + 1)
    d_ref[...] = jnp.where(lane == 0, d1, jnp.where(lane == 1, d2, 0.0))


def _dest_call(route, pstart_row):
    T = route.shape[0]
    tm = TOKEN_TILE
    return pl.pallas_call(
        _dest_kernel, grid=(T // tm,),
        in_specs=[pl.BlockSpec((tm, LANES), lambda i: (i, 0)), pl.BlockSpec((1, LANES), lambda i: (0, 0))],
        out_specs=pl.BlockSpec((tm, LANES), lambda i: (i, 0)),
        out_shape=jax.ShapeDtypeStruct((T, LANES), F32), compiler_params=_cparams("parallel"), name="moe_dest",
    )(route, pstart_row)


def _moe_kernel(te_ref, nt_ref, xs_ref, w_ref, wg_ref, wu_ref, wd_ref, y_ref):
    i = pl.program_id(0)

    @pl.when(i < nt_ref[0])
    def _():
        x = xs_ref[...].astype(BF16)
        hg = _dot(x, wg_ref[...].astype(BF16))
        hu = _dot(x, wu_ref[...].astype(BF16))
        act = hg * jax.nn.sigmoid(hg) * hu * w_ref[...]
        y_ref[...] = _dot(act.astype(BF16), wd_ref[...].astype(BF16))

    @pl.when(i >= nt_ref[0])
    def _():
        y_ref[...] = jnp.zeros(y_ref.shape, F32)


def _moe_call(xs, row_w, tile_e, n_used, w_gate, w_up, w_down, layer):
    R, D = xs.shape
    tm = MOE_TILE
    F = w_gate.shape[-1]
    grid_spec = pltpu.PrefetchScalarGridSpec(
        num_scalar_prefetch=2, grid=(R // tm,),
        in_specs=[pl.BlockSpec((tm, D), lambda i, te, nt: (i, 0)),
                  pl.BlockSpec((tm, 1), lambda i, te, nt: (i, 0)),
                  pl.BlockSpec((None, None, D, F), lambda i, te, nt: (layer, te[i], 0, 0)),
                  pl.BlockSpec((None, None, D, F), lambda i, te, nt: (layer, te[i], 0, 0)),
                  pl.BlockSpec((None, None, F, D), lambda i, te, nt: (layer, te[i], 0, 0))],
        out_specs=pl.BlockSpec((tm, D), lambda i, te, nt: (i, 0)))
    return pl.pallas_call(
        _moe_kernel, grid_spec=grid_spec, out_shape=jax.ShapeDtypeStruct((R, D), F32),
        compiler_params=_cparams("arbitrary"), name="moe_experts",
    )(tile_e, n_used, xs, row_w, w_gate, w_up, w_down)


def _route_meta(route, counts_f, tm, n_tiles):
    T = route.shape[0]
    n_assign = TOP_K * T
    e = route[:, :TOP_K].astype(jnp.int32).reshape(-1)
    w = route[:, TOP_K:2 * TOP_K].reshape(-1)
    order = jnp.argsort(e, stable=True).astype(jnp.int32)
    counts = counts_f.astype(jnp.int32)
    zero = jnp.zeros((1,), jnp.int32)
    start = jnp.concatenate([zero, jnp.cumsum(counts)])
    pstart = jnp.concatenate([zero, jnp.cumsum(((counts + tm - 1) // tm) * tm)])
    n_used = (pstart[N_EXPERTS] // tm).reshape(1)
    tile_lo = jnp.arange(n_tiles, dtype=jnp.int32) * tm
    tile_e = jnp.sum((pstart[None, 1:] <= tile_lo[:, None]).astype(jnp.int32), axis=1)
    tile_e = jnp.minimum(tile_e, N_EXPERTS - 1)
    oh = (tile_e[:, None] == jnp.arange(N_EXPERTS, dtype=jnp.int32)[None, :]).astype(jnp.int32)
    pick = lambda tab: jnp.sum(oh * tab[None, :N_EXPERTS], axis=1)
    k = (tile_lo - pick(pstart))[:, None] + jnp.arange(tm, dtype=jnp.int32)[None, :]
    valid = (k < pick(counts)[:, None]).reshape(-1)
    idx = jnp.clip(pick(start)[:, None] + k, 0, n_assign - 1).reshape(-1)
    src = order[idx]
    row_tok = jnp.where(valid, src // TOP_K, 0)
    row_w = jnp.where(valid, w[src], 0.0)
    ps_row = jnp.pad(pstart[:N_EXPERTS].astype(F32), (0, LANES - N_EXPERTS)).reshape(1, LANES)
    pos = _dest_call(route, ps_row)[:, :TOP_K].astype(jnp.int32)
    return row_tok, row_w.reshape(-1, 1), pos, tile_e, n_used


def _final_kernel(x_ref, g_ref, o_ref):
    o_ref[...] = _rms(x_ref[...], g_ref[...])


def _final_call(x, g):
    T, D = x.shape
    tm = TOKEN_TILE
    return pl.pallas_call(
        _final_kernel, grid=(T // tm,),
        in_specs=[pl.BlockSpec((tm, D), lambda i: (i, 0)), pl.BlockSpec((1, D), lambda i: (0, 0))],
        out_specs=pl.BlockSpec((tm, D), lambda i: (i, 0)),
        out_shape=jax.ShapeDtypeStruct((T, D), F32), compiler_params=_cparams("parallel"), name="final_norm",
    )(x, g)


def _block_diag(w):
    n, c, d = w.shape
    return jnp.einsum('ncd,nm->ncmd', w, jnp.eye(n, dtype=w.dtype)).reshape(n * c, n * d)


def _half_swap(w):
    half = w.shape[-1] // 2
    return jnp.concatenate([w[..., half:], w[..., :half]], axis=-1)


def _layer_weights(l, p, dims):
    lw, qr, kvr, gk_w, gv_w = dims
    D = p['w_in'].shape[1]
    row = lambda a: a.reshape(1, -1).astype(F32)
    w_in = p['w_in'][l]
    sizes = (lw, lw, qr, kvr, QK_ROPE, gk_w, gk_w, gv_w, p['gla_wa2'].shape[1], gv_w)
    x_lru, y_lru, c_q, c_kv, k_pe, g_q, g_k, g_v, g_a, g_o = jnp.split(w_in, list(np.cumsum(sizes)[:-1]), axis=1)
    tail_pad = LANES - 2 * QK_ROPE - g_a.shape[1]
    tail = jnp.concatenate([k_pe, _half_swap(k_pe), g_a, jnp.zeros((D, tail_pad), F32)], axis=1)
    w_in_p = jnp.concatenate([x_lru, y_lru, c_q, c_kv, g_q, g_k, g_v, g_o, tail], axis=1).astype(BF16)

    w_uq = p['w_uq'][l].reshape(qr, MLA_HEADS, QK_NOPE + QK_ROPE)
    nope = w_uq[:, :, :QK_NOPE].reshape(qr, MLA_HEADS * QK_NOPE)
    pe = w_uq[:, :, QK_NOPE:]
    widen = lambda a: jnp.pad(a, ((0, 0), (0, 0), (0, LANES - QK_ROPE))).reshape(qr, MLA_HEADS * LANES)
    w_q = jnp.concatenate([nope, widen(pe), widen(_half_swap(pe))], axis=1).astype(BF16)
    w_abs = _block_diag(jnp.transpose(p['w_uk'][l], (1, 2, 0))).astype(BF16)
    wa2 = jnp.zeros((LANES, gk_w), F32).at[2 * QK_ROPE:2 * QK_ROPE + g_a.shape[1]].set(p['gla_wa2'][l]).astype(BF16)

    w_uv = p['w_uv'][l]
    mw = MLA_HEADS * V_HEAD
    w_uv_cat = w_uv.reshape(kvr, mw)
    w_uv_rows = _block_diag(jnp.transpose(w_uv, (1, 0, 2)))
    n_a, n_b, n_c = jnp.split(p['out_norm_g'][l], [lw, lw + mw])
    n_r = N_GROUPS + N_EXPERTS
    w_r = jnp.concatenate([p['router_wg'][l], p['router_we'][l], jnp.zeros((D, LANES - n_r), F32)], axis=1)
    b_r = jnp.concatenate([p['router_bg'][l], p['router_be'][l], jnp.zeros((LANES - n_r,), F32)])
    return dict(
        g1=row(p['norm1_g'][l]), w_in=w_in_p, q_g=row(p['q_norm_g'][l]), w_q=w_q, w_abs=w_abs,
        kv_g=row(p['kv_norm_g'][l]), wa2=wa2, ba=row(p['gla_ba'][l]),
        conv_w=p['conv_w'][l], conv_b=row(p['conv_b'][l]),
        lru_wa=_block_diag(p['lru_wa'][l]).astype(BF16), lru_ba=row(p['lru_ba'][l]),
        lru_wi=_block_diag(p['lru_wi'][l]).astype(BF16), lru_bi=row(p['lru_bi'][l]),
        lru_lam=row(p['lru_lambda'][l]), n_a=row(n_a), n_b=row(n_b), n_c=row(n_c),
        gla_g=row(jnp.tile(p['gla_norm_g'][l], GLA_HEADS)),
        w_uv_cat=w_uv_cat.astype(BF16), w_uv_rows_t=w_uv_rows.T.astype(BF16),
        w_out=p['w_out'][l].astype(BF16), g2=row(p['norm2_g'][l]), w_r=w_r, b_r=row(b_r),
    )


def _rope_tables(positions):
    half = QK_ROPE // 2
    inv = ROPE_THETA ** (-np.arange(half, dtype=np.float64) / half)
    ang = np.asarray(positions, np.float64)[:, None] * inv
    zeros = np.zeros((ang.shape[0], LANES - QK_ROPE))
    cos = np.concatenate([np.cos(ang), np.cos(ang), zeros], axis=1)
    sin = np.concatenate([-np.sin(ang), np.sin(ang), zeros], axis=1)
    return jnp.asarray(cos, F32), jnp.asarray(sin, F32)


def _gla_state_to_rows(s):
    B = s.shape[0]
    eye = jnp.eye(GLA_HEADS, dtype=s.dtype)
    return jnp.einsum('bhde,hg->bhegd', s, eye).reshape(B, GLA_HEADS * GLA_DV, GLA_HEADS * GLA_DK)


def _gla_rows_to_state(st):
    B = st.shape[0]
    s5 = st.reshape(B, GLA_HEADS, GLA_DV, GLA_HEADS, GLA_DK)
    blocks = jnp.stack([s5[:, h, :, h, :] for h in range(GLA_HEADS)], axis=1)
    return jnp.swapaxes(blocks, -1, -2)


def kernel(x_prompt, x_sample, cache_ckv, cache_kpe, page_table, state_conv, state_lru, state_gla, norm1_g, w_in, conv_w, conv_b, lru_wa, lru_ba, lru_wi, lru_bi, lru_lambda, q_norm_g, w_uq, kv_norm_g, w_uk, w_uv, gla_wa2, gla_ba, gla_norm_g, out_norm_g, w_out, norm2_g, router_wg, router_bg, router_we, router_be, w_gate, w_up, w_down, final_norm_g):
    p = dict(norm1_g=norm1_g, w_in=w_in, conv_w=conv_w, conv_b=conv_b, lru_wa=lru_wa, lru_ba=lru_ba,
             lru_wi=lru_wi, lru_bi=lru_bi, lru_lambda=lru_lambda, q_norm_g=q_norm_g, w_uq=w_uq,
             kv_norm_g=kv_norm_g, w_uk=w_uk, w_uv=w_uv, gla_wa2=gla_wa2, gla_ba=gla_ba,
             gla_norm_g=gla_norm_g, out_norm_g=out_norm_g, w_out=w_out, norm2_g=norm2_g,
             router_wg=router_wg, router_bg=router_bg, router_we=router_we, router_be=router_be)
    Bp, Sp, D = x_prompt.shape
    Bs, Ss, _ = x_sample.shape
    depth = w_in.shape[0]
    lw = state_lru.shape[-1]
    qr = q_norm_g.shape[-1]
    kvr = kv_norm_g.shape[-1]
    gk_w = GLA_HEADS * GLA_DK
    gv_w = GLA_HEADS * GLA_DV
    dims = (lw, qr, kvr, gk_w, gv_w)
    assert kvr == LANES and gk_w == LANES
    Tp, Ts = Bp * Sp, Bs * Ss
    T = Tp + Ts
    past = page_table.shape[1] * cache_ckv.shape[2]

    cos_t, sin_t = _rope_tables(np.concatenate([np.tile(np.arange(Sp), Bp), np.tile(past + np.arange(Ss), Bs)]))
    cache_kpe_t = jnp.swapaxes(cache_kpe, 2, 3)
    n_tiles = (TOP_K * T + N_EXPERTS * (MOE_TILE - 1)) // MOE_TILE + 1

    x = jnp.concatenate([x_prompt.reshape(Tp, D), x_sample.reshape(Ts, D)], axis=0)
    outs = {k: [] for k in ('ckv_p', 'kpe_p', 'ckv_s', 'kpe_s', 'conv_p', 'conv_s', 'lru_p', 'lru_s', 'gla_p', 'gla_s')}
    for l in range(depth):
        lw_ = _layer_weights(l, p, dims)
        zl, qcat, kcat, kcat_t, ckv_n, kpe_n, g = _pre_call(x, lw_, cos_t, sin_t, dims)

        a_p, conv_p, lru_p = _lru_call(zl, 0, Bp, Sp, jnp.zeros((Bp, CONV_WIDTH - 1, lw), F32),
                                       jnp.zeros((Bp, lw), F32), lw_, BF16)
        a_s, conv_s, lru_s = _lru_call(zl, Tp, Bs, Ss, state_conv[l], state_lru[l], lw_, F32)

        c_p, gla_p = _gla_call(g, 0, Bp, Sp, jnp.zeros((Bp, gv_w, gk_w), F32), lw_, BF16, gk_w, gv_w)
        c_s, gla_s = _gla_call(g, Tp, Bs, Ss, _gla_state_to_rows(state_gla[l]), lw_, F32, gk_w, gv_w)

        b_p = _attn_prompt_call(qcat, kcat, kcat_t, Bp, Sp, lw_, kvr)
        b_s = _attn_sample_call(page_table, qcat[Tp:].reshape(Bs, Ss, -1), ckv_n[Tp:].reshape(Bs, Ss, kvr),
                                kpe_n[Tp:].reshape(Bs, Ss, QK_ROPE), cache_ckv, cache_kpe_t, l, lw_, kvr)

        a = jnp.concatenate([a_p, a_s.astype(BF16)], axis=0)
        b = jnp.concatenate([b_p, b_s.reshape(Ts, -1).astype(BF16)], axis=0)
        c = jnp.concatenate([c_p, c_s.astype(BF16)], axis=0)
        x1, hn, route, cnt = _post_call(a, b, c, x, lw_)

        row_tok, row_w, pos, tile_e, n_used = _route_meta(route, cnt[0, :N_EXPERTS], MOE_TILE, n_tiles)
        ys = _moe_call(hn[row_tok], row_w, tile_e, n_used, w_gate, w_up, w_down, l)
        x = x1 + ys[pos[:, 0]] + ys[pos[:, 1]]

        outs['ckv_p'].append(ckv_n[:Tp].reshape(Bp, Sp, kvr))
        outs['kpe_p'].append(kpe_n[:Tp].reshape(Bp, Sp, QK_ROPE))
        outs['ckv_s'].append(ckv_n[Tp:].reshape(Bs, Ss, kvr))
        outs['kpe_s'].append(kpe_n[Tp:].reshape(Bs, Ss, QK_ROPE))
        outs['conv_p'].append(conv_p)
        outs['conv_s'].append(conv_s)
        outs['lru_p'].append(lru_p.reshape(Bp, lw))
        outs['lru_s'].append(lru_s.reshape(Bs, lw))
        outs['gla_p'].append(_gla_rows_to_state(gla_p))
        outs['gla_s'].append(_gla_rows_to_state(gla_s))

    y = _final_call(x, final_norm_g.reshape(1, D))
    st = {k: jnp.stack(v) for k, v in outs.items()}
    return (y[:Tp].reshape(Bp, Sp, D), y[Tp:].reshape(Bs, Ss, D),
            st['ckv_p'], st['kpe_p'], st['ckv_s'], st['kpe_s'], st['conv_p'], st['conv_s'],
            st['lru_p'], st['lru_s'], st['gla_p'], st['gla_s'])
```

```python
import functools
import math

import numpy as np
import jax
import jax.numpy as jnp
from jax import lax
from jax.experimental import pallas as pl
from jax.experimental.pallas import tpu as pltpu

F32 = jnp.float32
BF16 = jnp.bfloat16

LRU_BLOCKS = 4
CONV_WIDTH = 4
LRU_C = 8.0
MLA_HEADS = 8
QK_NOPE = 64
QK_ROPE = 32
V_HEAD = 64
ROPE_THETA = 10000.0
GLA_HEADS = 4
GLA_DK = 32
GLA_DV = 64
GLA_TAU = 16.0
GLA_CHUNK = 16
N_GROUPS = 4
EXPERTS_PER_GROUP = 8
N_EXPERTS = N_GROUPS * EXPERTS_PER_GROUP
TOP_K = 2
EPS = 1e-6

LANES = 128
SUBLANES = 8
BF16_ROWS = 16
VMEM_LIMIT_BYTES = 56 * 1024 * 1024

TOKEN_TILE = 512
SEQ_TILE = 256
MOE_TILE = 256
PAGES_PER_STEP = 16
NEG_BIG = -1e30


def _cparams(*sem):
    return pltpu.CompilerParams(dimension_semantics=sem, vmem_limit_bytes=VMEM_LIMIT_BYTES)


def _rms(x, g):
    return x * lax.rsqrt(jnp.mean(x * x, axis=-1, keepdims=True) + EPS) * g


def _dot(a, b):
    return jnp.dot(a, b, preferred_element_type=F32)


def _dot_nt(a, b):
    return lax.dot_general(a, b, (((1,), (1,)), ((), ())), preferred_element_type=F32)


def _softplus(x):
    return jnp.maximum(x, 0.0) + jnp.log1p(jnp.exp(-jnp.abs(x)))


def _shift_rows(x, d, fill):
    row = lax.broadcasted_iota(jnp.int32, x.shape, 0)
    return jnp.where(row >= d, pltpu.roll(x, d, 0), fill)


def _pre_kernel(x_ref, g1_ref, win_ref, qg_ref, wq_ref, wabs_ref, kvg_ref, wa2_ref, ba_ref,
                cos_ref, sin_ref, zl_ref, q_ref, k_ref, kt_ref, ckv_ref, kpe_ref, g_ref, *, dims):
    lw, qr, kvr, gk_w, gv_w = dims
    xn = _rms(x_ref[...], g1_ref[...])
    z = _dot(xn.astype(BF16), win_ref[...])
    o = 2 * lw
    zl_ref[...] = z[:, :o]
    cq = z[:, o:o + qr]
    o += qr
    ckv = z[:, o:o + kvr]
    o += kvr
    gqk = z[:, o:o + 2 * gk_w]
    o += 2 * gk_w
    gvo = z[:, o:o + 2 * gv_w]
    o += 2 * gv_w
    tail = z[:, o:o + LANES]
    cos = cos_ref[...]
    sin = sin_ref[...]
    scale = (QK_NOPE + QK_ROPE) ** -0.5 * math.log2(math.e)

    ckv_n = _rms(ckv, kvg_ref[...])
    kpe = tail * cos + pltpu.roll(tail, LANES - QK_ROPE, 1) * sin
    ckv_ref[...] = ckv_n
    kpe_ref[...] = kpe[:, :QK_ROPE]
    k_ref[:, :kvr] = ckv_n.astype(BF16)
    k_ref[:, kvr:] = kpe.astype(BF16)
    ones = jnp.ones((kt_ref.shape[0] - kvr, ckv_n.shape[0]), F32)
    kt_ref[...] = jnp.concatenate([ckv_n.T, ones], axis=0).astype(BF16)

    cqn = _rms(cq, qg_ref[...]).astype(BF16)
    qall = _dot(cqn, wq_ref[...])
    n_nope = MLA_HEADS * QK_NOPE
    q_abs = _dot(qall[:, :n_nope].astype(BF16), wabs_ref[...]) * scale
    for h in range(MLA_HEADS):
        pe = qall[:, n_nope + h * LANES:n_nope + (h + 1) * LANES]
        sw = qall[:, n_nope + (MLA_HEADS + h) * LANES:n_nope + (MLA_HEADS + h + 1) * LANES]
        q_ref[:, 2 * h * LANES:(2 * h + 1) * LANES] = q_abs[:, h * kvr:(h + 1) * kvr].astype(BF16)
        q_ref[:, (2 * h + 1) * LANES:(2 * h + 2) * LANES] = ((pe * cos + sw * sin) * scale).astype(BF16)

    la_pre = _dot(tail.astype(BF16), wa2_ref[...]) + ba_ref[...]
    la = -_softplus(-la_pre) * (1.0 / GLA_TAU)
    g_ref[:, :gk_w] = gqk[:, :gk_w] * (GLA_DK ** -0.5)
    g_ref[:, gk_w:2 * gk_w] = gqk[:, gk_w:]
    g_ref[:, 2 * gk_w:3 * gk_w] = la
    g_ref[:, 3 * gk_w:] = gvo


def _pre_call(x, lw_, cos_t, sin_t, dims):
    T, D = x.shape
    lw, qr, kvr, gk_w, gv_w = dims
    tm = TOKEN_TILE
    assert T % tm == 0
    row = lambda i: (i, 0)
    full = lambda i: (0, 0)
    wspec = lambda a: pl.BlockSpec(a.shape, full)
    ins = [x, lw_['g1'], lw_['w_in'], lw_['q_g'], lw_['w_q'], lw_['w_abs'], lw_['kv_g'], lw_['wa2'], lw_['ba']]
    in_specs = [pl.BlockSpec((tm, D), row)] + [wspec(a) for a in ins[1:]]
    in_specs += [pl.BlockSpec((tm, LANES), row), pl.BlockSpec((tm, LANES), row)]
    out_shape = (
        jax.ShapeDtypeStruct((T, 2 * lw), F32),
        jax.ShapeDtypeStruct((T, 2 * LANES * MLA_HEADS), BF16),
        jax.ShapeDtypeStruct((T, 2 * LANES), BF16),
        jax.ShapeDtypeStruct((T // tm, kvr + BF16_ROWS, tm), BF16),
        jax.ShapeDtypeStruct((T, kvr), F32),
        jax.ShapeDtypeStruct((T, QK_ROPE), F32),
        jax.ShapeDtypeStruct((T, 3 * gk_w + 2 * gv_w), F32),
    )
    out_specs = tuple(pl.BlockSpec((None, s.shape[1], tm), lambda i: (i, 0, 0)) if len(s.shape) == 3
                      else pl.BlockSpec((tm, s.shape[1]), row) for s in out_shape)
    return pl.pallas_call(
        functools.partial(_pre_kernel, dims=dims),
        grid=(T // tm,), in_specs=in_specs, out_specs=out_specs, out_shape=out_shape,
        compiler_params=_cparams("parallel"), name="pre_proj",
    )(*ins, cos_t, sin_t)


def _lru_kernel(zl_ref, cbuf_ref, h0_ref, cw_ref, cb_ref, wa_ref, ba_ref, wi_ref, bi_ref, lam_ref, na_ref,
                a_ref, conv_ref, hout_ref, xbuf, hcar, *, ts, lw):
    i = pl.program_id(1)
    last = pl.num_programs(1) - 1
    pad = SUBLANES
    nbuf = CONV_WIDTH - 1

    @pl.when(i == 0)
    def _():
        xbuf[0:pad, :] = jnp.zeros((pad, lw), F32)
        xbuf[pad - nbuf:pad, :] = cbuf_ref[...]
        hcar[...] = h0_ref[...]

    x = zl_ref[:, :lw]
    y = zl_ref[:, lw:]
    xbuf[pad:pad + ts, :] = x
    xc = cb_ref[...] + cw_ref[nbuf:nbuf + 1, :] * x
    for k in range(nbuf):
        xc = xc + cw_ref[k:k + 1, :] * xbuf[pad - nbuf + k:pad - nbuf + k + ts, :]

    @pl.when(i == last)
    def _():
        conv_ref[...] = xbuf[pad + ts - nbuf:pad + ts, :]

    xbuf[0:pad, :] = xbuf[ts:ts + pad, :]

    xb = xc.astype(BF16)
    r = jax.nn.sigmoid(_dot(xb, wa_ref[...]) + ba_ref[...])
    gi = jax.nn.sigmoid(_dot(xb, wi_ref[...]) + bi_ref[...])
    log_a = (-LRU_C) * r * _softplus(-lam_ref[...])
    a = jnp.exp(log_a)
    th = jnp.tanh(log_a)
    u = jnp.sqrt(-2.0 * th / (1.0 - th)) * (gi * xc)

    d = 1
    while d < ts:
        u = a * _shift_rows(u, d, 0.0) + u
        a = a * _shift_rows(a, d, 1.0)
        d *= 2
    h = a * hcar[...] + u
    hcar[...] = h[ts - 1:ts, :]

    @pl.when(i == last)
    def _():
        hout_ref[...] = h[ts - 1:ts, :]

    out_a = h * jax.nn.gelu(y)
    a_ref[...] = _rms(out_a, na_ref[...]).astype(a_ref.dtype)


def _lru_call(zl, row_off, B, S, cbuf, h0, lw_, out_dtype):
    lw = h0.shape[-1]
    ts = min(S, SEQ_TILE)
    assert S % ts == 0 and row_off % ts == 0 and S >= CONV_WIDTH - 1
    n = S // ts
    off = row_off // ts
    full = lambda b, i: (0, 0)
    ws = [lw_['conv_w'], lw_['conv_b'], lw_['lru_wa'], lw_['lru_ba'], lw_['lru_wi'], lw_['lru_bi'],
          lw_['lru_lam'], lw_['n_a']]
    in_specs = [
        pl.BlockSpec((ts, 2 * lw), lambda b, i: (off + b * n + i, 0)),
        pl.BlockSpec((None, CONV_WIDTH - 1, lw), lambda b, i: (b, 0, 0)),
        pl.BlockSpec((None, 1, lw), lambda b, i: (b, 0, 0)),
    ] + [pl.BlockSpec(a.shape, full) for a in ws]
    out_shape = (
        jax.ShapeDtypeStruct((B * S, lw), out_dtype),
        jax.ShapeDtypeStruct((B, CONV_WIDTH - 1, lw), F32),
        jax.ShapeDtypeStruct((B, 1, lw), F32),
    )
    out_specs = (
        pl.BlockSpec((ts, lw), lambda b, i: (b * n + i, 0)),
        pl.BlockSpec((None, CONV_WIDTH - 1, lw), lambda b, i: (b, 0, 0)),
        pl.BlockSpec((None, 1, lw), lambda b, i: (b, 0, 0)),
    )
    return pl.pallas_call(
        functools.partial(_lru_kernel, ts=ts, lw=lw),
        grid=(B, n), in_specs=in_specs, out_specs=out_specs, out_shape=out_shape,
        scratch_shapes=[pltpu.VMEM((ts + SUBLANES, lw), F32), pltpu.VMEM((1, lw), F32)],
        compiler_params=_cparams("parallel", "arbitrary"), name="rg_lru",
    )(zl, cbuf, h0.reshape(B, 1, lw), *ws)


def _gla_kernel(g_ref, s0_ref, gn_ref, nc_ref, c_ref, sout_ref, st, *, ts, gk_w, gv_w):
    i = pl.program_id(1)
    last = pl.num_programs(1) - 1
    C = GLA_CHUNK
    rows = min(ts, C)
    n_chunks = max(ts // C, 1)
    log2c = int(math.log2(C))

    @pl.when(i == 0)
    def _():
        st[...] = s0_ref[...]

    dk_sh, dv_sh = int(math.log2(GLA_DK)), int(math.log2(GLA_DV))
    hd = lax.broadcasted_iota(jnp.int32, (gk_w, gv_w), 0) >> dk_sh
    he = lax.broadcasted_iota(jnp.int32, (gk_w, gv_w), 1) >> dv_sh
    same = (hd == he).astype(BF16)
    he_t = lax.broadcasted_iota(jnp.int32, (gv_w, gk_w), 0) >> dv_sh
    hd_t = lax.broadcasted_iota(jnp.int32, (gv_w, gk_w), 1) >> dk_sh
    same_t = (he_t == hd_t).astype(F32)
    sel_t = lax.broadcasted_iota(jnp.int32, (C, C * C), 0)
    sel_r = lax.broadcasted_iota(jnp.int32, (C, C * C), 1) >> log2c
    sel = (sel_t == sel_r).astype(BF16)
    srow = lax.broadcasted_iota(jnp.int32, (C, 1), 0)

    blk = g_ref[...]
    if rows < C:
        blk = jnp.concatenate([blk, jnp.zeros((C - rows, blk.shape[1]), F32)], axis=0)
    q = blk[:, :gk_w]
    k = blk[:, gk_w:2 * gk_w]
    la = blk[:, 2 * gk_w:3 * gk_w]
    v = blk[:, 3 * gk_w:3 * gk_w + gv_w]
    nrow = n_chunks * C
    pos = lax.broadcasted_iota(jnp.int32, (nrow, 1), 0) & (C - 1)
    cum = la
    d = 1
    while d < C:
        cum = cum + jnp.where(pos >= d, pltpu.roll(cum, d, 0), 0.0)
        d *= 2
    tot = jnp.where(pos == C - 1, cum, 0.0)
    d = 1
    while d < C:
        tot = tot + jnp.where(pos < C - d, pltpu.roll(tot, nrow - d, 0), 0.0)
        d *= 2
    qe = (q * jnp.exp(cum)).astype(BF16)
    kdec = (k * jnp.exp(tot - cum)).astype(BF16)
    dec = jnp.exp(tot)
    vb = v.astype(BF16)

    o_intra, upd = [], []
    for c in range(n_chunks):
        sl = slice(c * C, (c + 1) * C)
        cum_c, q_c, k_c, v_c = cum[sl], q[sl], k[sl], v[sl]
        pieces = []
        for t in range(C):
            diff = jnp.where(srow <= t, cum_c[t:t + 1, :] - cum_c, NEG_BIG)
            pieces.append(q_c[t:t + 1, :] * k_c * jnp.exp(diff))
        w = jnp.concatenate(pieces, axis=0)
        att = _dot(w.astype(BF16), same)
        xv = att * jnp.concatenate([v_c] * C, axis=0)
        o_intra.append(_dot(sel, xv.astype(BF16)))
        upd.append(lax.dot_general(vb[sl], kdec[sl], (((0,), (0,)), ((), ())),
                                   preferred_element_type=F32) * same_t)

    s_t = st[...]
    outs = []
    for c in range(n_chunks):
        sl = slice(c * C, (c + 1) * C)
        outs.append(o_intra[c] + _dot_nt(qe[sl], s_t.astype(BF16)))
        s_t = s_t * dec[c * C:c * C + 1, :] + upd[c]
    st[...] = s_t
    o = jnp.concatenate(outs, axis=0)[:ts] if n_chunks > 1 else outs[0][:ts]

    @pl.when(i == last)
    def _():
        sout_ref[...] = st[...]

    go = g_ref[:, 3 * gk_w + gv_w:]
    e64 = (lax.broadcasted_iota(jnp.int32, (gv_w, gv_w), 0) >> dv_sh
           == lax.broadcasted_iota(jnp.int32, (gv_w, gv_w), 1) >> dv_sh).astype(BF16)
    osq = o * o
    osq_hi = osq.astype(BF16)
    osq_lo = (osq - osq_hi.astype(F32)).astype(BF16)
    ms = (_dot(osq_hi, e64) + _dot(osq_lo, e64)) * (1.0 / GLA_DV)
    out_c = o * lax.rsqrt(ms + EPS) * gn_ref[...] * (go * jax.nn.sigmoid(go))
    c_ref[...] = _rms(out_c, nc_ref[...]).astype(c_ref.dtype)


def _gla_call(g, row_off, B, S, s0t, lw_, out_dtype, gk_w, gv_w):
    ts = min(S, SEQ_TILE)
    assert S % ts == 0 and row_off % ts == 0 and (ts % GLA_CHUNK == 0 or ts < GLA_CHUNK)
    n = S // ts
    off = row_off // ts
    gw = g.shape[1]
    full = lambda b, i: (0, 0)
    out_shape = (jax.ShapeDtypeStruct((B * S, gv_w), out_dtype),
                 jax.ShapeDtypeStruct((B, gv_w, gk_w), F32))
    return pl.pallas_call(
        functools.partial(_gla_kernel, ts=ts, gk_w=gk_w, gv_w=gv_w),
        grid=(B, n),
        in_specs=[pl.BlockSpec((ts, gw), lambda b, i: (off + b * n + i, 0)),
                  pl.BlockSpec((None, gv_w, gk_w), lambda b, i: (b, 0, 0)),
                  pl.BlockSpec((1, gv_w), full), pl.BlockSpec((1, gv_w), full)],
        out_specs=(pl.BlockSpec((ts, gv_w), lambda b, i: (b * n + i, 0)),
                   pl.BlockSpec((None, gv_w, gk_w), lambda b, i: (b, 0, 0))),
        out_shape=out_shape,
        scratch_shapes=[pltpu.VMEM((gv_w, gk_w), F32)],
        compiler_params=_cparams("parallel", "arbitrary"), name="gla",
    )(g, s0t, lw_['gla_g'], lw_['n_c'])


def _tree(op, xs):
    xs = list(xs)
    while len(xs) > 1:
        xs = [op(xs[i], xs[i + 1]) if i + 1 < len(xs) else xs[i] for i in range(0, len(xs), 2)]
    return xs[0]


def _attn_prompt_kernel(q_ref, k_ref, kt_ref, wuvt_ref, nb_ref, o_ref, m_scr, acc_scr, *, tq, kvr):
    i = pl.program_id(1)
    kw = 2 * LANES
    m_scr[...] = jnp.full(m_scr.shape, NEG_BIG, F32)
    acc_scr[...] = jnp.zeros(acc_scr.shape, F32)

    def block(j, masked):
        kb = k_ref[pl.ds(pl.multiple_of(j * tq, tq), tq), :]
        kbt = kt_ref[j]
        if masked:
            key = lax.broadcasted_iota(jnp.int32, (tq, tq), 0)
            qry = lax.broadcasted_iota(jnp.int32, (tq, tq), 1)
            keep = key <= qry
        qk = lambda h: _dot_nt(kb, q_ref[:, h * kw:(h + 1) * kw])
        st_next = qk(0)
        for h in range(MLA_HEADS):
            st = st_next
            if h + 1 < MLA_HEADS:
                st_next = qk(h + 1)
            if masked:
                st = jnp.where(keep, st, NEG_BIG)
            m_old = m_scr[h]
            m_new = jnp.maximum(m_old, jnp.max(st, axis=0, keepdims=True))
            alpha = jnp.exp2(m_old - m_new)
            pt = jnp.exp2(st - m_new).astype(BF16)
            acc_scr[h] = alpha * acc_scr[h] + _dot(kbt, pt)
            m_scr[h] = m_new

    def body(j, carry):
        block(j, False)
        return carry

    lax.fori_loop(0, i, body, 0)
    block(i, True)

    out_t = jnp.zeros((wuvt_ref.shape[0], tq), F32)
    for h in range(MLA_HEADS):
        acc = acc_scr[h]
        o_t = (acc[:kvr, :] / acc[kvr:kvr + 1, :]).astype(BF16)
        out_t = out_t + _dot(wuvt_ref[:, h * kvr:(h + 1) * kvr], o_t)
    ms = jnp.mean(out_t * out_t, axis=0, keepdims=True)
    out_t = out_t * lax.rsqrt(ms + EPS) * nb_ref[...]
    o_ref[...] = out_t.T.astype(o_ref.dtype)


def _attn_prompt_call(qcat, kcat, kcat_t, B, S, lw_, kvr):
    tq = kcat_t.shape[2]
    assert S % tq == 0
    n = S // tq
    mw = lw_['w_uv_rows_t'].shape[0]
    full = lambda b, i: (0, 0)
    return pl.pallas_call(
        functools.partial(_attn_prompt_kernel, tq=tq, kvr=kvr),
        grid=(B, n),
        in_specs=[pl.BlockSpec((tq, qcat.shape[1]), lambda b, i: (b * n + i, 0)),
                  pl.BlockSpec((S, kcat.shape[1]), lambda b, i: (b, 0)),
                  pl.BlockSpec((n, kcat_t.shape[1], tq), lambda b, i: (b, 0, 0)),
                  pl.BlockSpec(lw_['w_uv_rows_t'].shape, full), pl.BlockSpec((mw, 1), full)],
        out_specs=pl.BlockSpec((tq, mw), lambda b, i: (b * n + i, 0)),
        out_shape=jax.ShapeDtypeStruct((B * S, mw), BF16),
        scratch_shapes=[pltpu.VMEM((MLA_HEADS, 1, tq), F32),
                        pltpu.VMEM((MLA_HEADS, kcat_t.shape[1], tq), F32)],
        compiler_params=_cparams("parallel", "arbitrary"), name="attn_prompt",
    )(qcat, kcat, kcat_t, lw_['w_uv_rows_t'], lw_['n_b'].reshape(mw, 1))


def _attn_sample_kernel(pt_ref, q_ref, nckv_ref, nkpe_ref, ckv_hbm, kpe_hbm, wuv_ref, nb_ref, o_ref,
                        q_scr, s_scr, v_scr, ckv_buf, kpe_buf, sem, *, layer, n_pp, n_steps, sq, kvr, page):
    b = pl.program_id(0)
    nrow = MLA_HEADS * sq
    n_pages = n_steps * n_pp

    def group_copies(bb, g):
        slot = g % 2
        out = []
        for j in range(n_pp):
            pg = pt_ref[bb * n_pages + g * n_pp + j]
            out.append(pltpu.make_async_copy(ckv_hbm.at[layer, pg], ckv_buf.at[slot, j], sem.at[0, slot]))
            out.append(pltpu.make_async_copy(kpe_hbm.at[layer, pg], kpe_buf.at[slot, j], sem.at[1, slot]))
        return out

    @pl.when(b == 0)
    def _():
        for c in group_copies(b, 0):
            c.start()

    for h in range(MLA_HEADS):
        q_scr[h * sq:(h + 1) * sq, :] = q_ref[:, 2 * h * LANES:(2 * h + 2) * LANES].astype(F32)
    qa = q_scr[:, :kvr].astype(BF16)
    qp = q_scr[:, kvr:kvr + QK_ROPE].astype(BF16)

    for g in range(n_steps):
        if g + 1 < n_steps:
            for c in group_copies(b, g + 1):
                c.start()
        else:
            @pl.when(b + 1 < pl.num_programs(0))
            def _():
                for c in group_copies(b + 1, 0):
                    c.start()
        for c in group_copies(b, g):
            c.wait()
        for j in range(n_pp):
            ck = ckv_buf[g % 2, j].astype(BF16)
            v_scr[g * n_pp + j] = ck
            s_scr[g, :, j * page:(j + 1) * page] = _dot_nt(qa, ck) + _dot(qp, kpe_buf[g % 2, j].astype(BF16))

    zpad = lambda a: jnp.concatenate([a, jnp.zeros((page - sq, a.shape[1]), F32)], axis=0)
    ck_new = zpad(nckv_ref[...]).astype(BF16)
    kp_new = zpad(nkpe_ref[...]).astype(BF16)
    tok = lax.broadcasted_iota(jnp.int32, (nrow, page), 0) & (sq - 1)
    key = lax.broadcasted_iota(jnp.int32, (nrow, page), 1)
    s_new = jnp.where(key <= tok, _dot_nt(qa, ck_new) + _dot_nt(qp, kp_new), NEG_BIG)
    tile = lambda i: s_scr[i // n_pp, :, (i % n_pp) * page:(i % n_pp + 1) * page]
    m = jnp.max(_tree(jnp.maximum, [tile(i) for i in range(n_pages)] + [s_new]), axis=-1, keepdims=True)
    p_new = jnp.exp2(s_new - m)
    acc = _dot(p_new.astype(BF16), ck_new)
    psum = p_new
    for i in range(n_pages):
        p = jnp.exp2(tile(i) - m)
        psum = psum + p
        acc = acc + _dot(p.astype(BF16), v_scr[i])
    l = jnp.sum(psum, axis=-1, keepdims=True)
    o_lat = (acc / l).astype(BF16)
    mw = wuv_ref.shape[1]
    r = _dot(o_lat, wuv_ref[...])
    rh = lax.broadcasted_iota(jnp.int32, (nrow, mw), 0) >> int(math.log2(sq))
    ch = lax.broadcasted_iota(jnp.int32, (nrow, mw), 1) >> int(math.log2(V_HEAD))
    r = jnp.where(rh == ch, r, 0.0)
    out = r[0:sq, :]
    for h in range(1, MLA_HEADS):
        out = out + r[h * sq:(h + 1) * sq, :]
    o_ref[...] = _rms(out, nb_ref[...])


def _attn_sample_call(page_table, qs, nckv, nkpe, cache_ckv, cache_kpe_t, layer, lw_, kvr):
    B, sq, qw = qs.shape
    n_pages = page_table.shape[1]
    page = cache_ckv.shape[2]
    n_pp = math.gcd(PAGES_PER_STEP, n_pages)
    n_steps = n_pages // n_pp
    mw = lw_['w_uv_cat'].shape[1]
    nrow = MLA_HEADS * sq
    assert sq == SUBLANES and sq <= page and (n_steps % 2 == 0 or n_steps == 1)
    hbm = pl.BlockSpec(memory_space=pl.ANY)
    in_specs = [pl.BlockSpec((None, sq, qw), lambda b, pt: (b, 0, 0)),
                pl.BlockSpec((None, sq, kvr), lambda b, pt: (b, 0, 0)),
                pl.BlockSpec((None, sq, QK_ROPE), lambda b, pt: (b, 0, 0)),
                hbm, hbm,
                pl.BlockSpec(lw_['w_uv_cat'].shape, lambda b, pt: (0, 0)),
                pl.BlockSpec((1, mw), lambda b, pt: (0, 0))]
    grid_spec = pltpu.PrefetchScalarGridSpec(
        num_scalar_prefetch=1, grid=(B,), in_specs=in_specs,
        out_specs=pl.BlockSpec((None, sq, mw), lambda b, pt: (b, 0, 0)),
        scratch_shapes=[pltpu.VMEM((nrow, 2 * LANES), F32), pltpu.VMEM((n_steps, nrow, n_pp * page), F32),
                        pltpu.VMEM((n_pages, page, kvr), BF16),
                        pltpu.VMEM((2, n_pp, page, kvr), F32), pltpu.VMEM((2, n_pp, QK_ROPE, page), F32),
                        pltpu.SemaphoreType.DMA((2, 2))])
    return pl.pallas_call(
        functools.partial(_attn_sample_kernel, layer=layer, n_pp=n_pp, n_steps=n_steps, sq=sq, kvr=kvr, page=page),
        grid_spec=grid_spec, out_shape=jax.ShapeDtypeStruct((B, sq, mw), F32),
        compiler_params=_cparams("arbitrary"), name="attn_sample",
    )(page_table.reshape(-1), qs, nckv, nkpe, cache_ckv, cache_kpe_t, lw_['w_uv_cat'], lw_['n_b'])


def _post_kernel(a_ref, b_ref, c_ref, x_ref, wo_ref, g2_ref, wr_ref, br_ref, x1_ref, hn_ref, rt_ref, cnt_ref,
                 cnt_scr, *, widths):
    wa, wb, wc = widths

    @pl.when(pl.program_id(0) == 0)
    def _():
        cnt_scr[...] = jnp.zeros(cnt_scr.shape, F32)

    mix = _dot(a_ref[...], wo_ref[:wa, :])
    mix = mix + _dot(b_ref[...], wo_ref[wa:wa + wb, :])
    mix = mix + _dot(c_ref[...], wo_ref[wa + wb:, :])
    x1 = x_ref[...] + mix
    x1_ref[...] = x1
    hn = _rms(x1, g2_ref[...])
    half = hn.shape[1] // 2
    bits = lambda a: lax.bitcast_convert_type(a.astype(BF16).astype(F32), jnp.uint32)
    hn_ref[...] = (bits(hn[:, :half]) >> 16) | bits(hn[:, half:])
    logits = jnp.dot(hn, wr_ref[...], precision=lax.Precision.HIGHEST, preferred_element_type=F32) + br_ref[...]

    lane = lax.broadcasted_iota(jnp.int32, logits.shape, 1)
    lane_f = lane.astype(F32)
    big = float(LANES)
    is_g = lane < N_GROUPS
    gl = jnp.where(is_g, logits, NEG_BIG)
    gmax = jnp.max(gl, axis=-1, keepdims=True)
    gsel = jnp.min(jnp.where(gl == gmax, lane_f, big), axis=-1, keepdims=True)
    gprob = 1.0 / jnp.sum(jnp.where(is_g, jnp.exp(gl - gmax), 0.0), axis=-1, keepdims=True)
    lo = N_GROUPS + gsel * EXPERTS_PER_GROUP
    el = jnp.where((lane_f >= lo) & (lane_f < lo + EXPERTS_PER_GROUP), logits, NEG_BIG)
    v1 = jnp.max(el, axis=-1, keepdims=True)
    i1 = jnp.min(jnp.where(el == v1, lane_f, big), axis=-1, keepdims=True)
    el2 = jnp.where(lane_f == i1, NEG_BIG, el)
    v2 = jnp.max(el2, axis=-1, keepdims=True)
    i2 = jnp.min(jnp.where(el2 == v2, lane_f, big), axis=-1, keepdims=True)
    e21 = jnp.exp(v2 - v1)
    w1 = gprob / (1.0 + e21)
    w2 = w1 * e21
    e1 = i1 - N_GROUPS
    e2 = i2 - N_GROUPS

    oh1 = (lane_f == e1).astype(F32)
    oh2 = (lane_f == e2).astype(F32)
    both = oh1 + oh2
    tm = both.shape[0]
    tri = (lax.broadcasted_iota(jnp.int32, (tm, tm), 0) >= lax.broadcasted_iota(jnp.int32, (tm, tm), 1))
    incl = _dot(tri.astype(BF16), both.astype(BF16))
    base = cnt_scr[0:1, :] + incl - both
    r1 = jnp.sum(oh1 * base, axis=-1, keepdims=True)
    r2 = jnp.sum(oh2 * base, axis=-1, keepdims=True)
    cnt_scr[...] = cnt_scr[...] + incl[tm - 1:tm, :]
    cnt_ref[...] = cnt_scr[...]

    rt = jnp.where(lane == 0, e1, 0.0)
    rt = jnp.where(lane == 1, e2, rt)
    rt = jnp.where(lane == 2, w1, rt)
    rt = jnp.where(lane == 3, w2, rt)
    rt = jnp.where(lane == 4, r1, rt)
    rt = jnp.where(lane == 5, r2, rt)
    rt_ref[...] = rt


def _post_call(a, b, c, x, lw_):
    T, D = x.shape
    tm = TOKEN_TILE
    row = lambda i: (i, 0)
    full = lambda i: (0, 0)
    widths = (a.shape[1], b.shape[1], c.shape[1])
    ws = [lw_['w_out'], lw_['g2'], lw_['w_r'], lw_['b_r']]
    out_shape = (jax.ShapeDtypeStruct((T, D), F32), jax.ShapeDtypeStruct((T, D // 2), jnp.uint32),
                 jax.ShapeDtypeStruct((T, LANES), F32), jax.ShapeDtypeStruct((SUBLANES, LANES), F32))
    return pl.pallas_call(
        functools.partial(_post_kernel, widths=widths),
        grid=(T // tm,),
        in_specs=[pl.BlockSpec((tm, w), row) for w in widths] + [pl.BlockSpec((tm, D), row)]
        + [pl.BlockSpec(w.shape, full) for w in ws],
        out_specs=tuple(pl.BlockSpec((tm, s.shape[1]), row) for s in out_shape[:3])
        + (pl.BlockSpec((SUBLANES, LANES), full),),
        out_shape=out_shape, scratch_shapes=[pltpu.VMEM((SUBLANES, LANES), F32)],
        compiler_params=_cparams("arbitrary"), name="post_proj",
    )(a, b, c, x, *ws)


def _dest_kernel(rt_ref, ps_ref, d_ref):
    rt = rt_ref[...]
    lane = lax.broadcasted_iota(jnp.int32, rt.shape, 1)
    lane_f = lane.astype(F32)
    pick = lambda k: jnp.sum(jnp.where(lane == k, rt, 0.0), axis=-1, keepdims=True)
    ps = ps_ref[...]
    d1 = jnp.sum(jnp.where(lane_f == pick(0), ps, 0.0), axis=-1, keepdims=True) + pick(2 * TOP_K)
    d2 = jnp.sum(jnp.where(lane_f == pick(1), ps, 0.0), axis=-1, keepdims=True) + pick(2 * TOP_K + 1)
    d_ref[...] = jnp.where(lane == 0, d1, jnp.where(lane == 1, d2, 0.0))


def _dest_call(route, pstart_row):
    T = route.shape[0]
    tm = TOKEN_TILE
    return pl.pallas_call(
        _dest_kernel, grid=(T // tm,),
        in_specs=[pl.BlockSpec((tm, LANES), lambda i: (i, 0)), pl.BlockSpec((1, LANES), lambda i: (0, 0))],
        out_specs=pl.BlockSpec((tm, LANES), lambda i: (i, 0)),
        out_shape=jax.ShapeDtypeStruct((T, LANES), F32), compiler_params=_cparams("parallel"), name="moe_dest",
    )(route, pstart_row)


def _moe_kernel(te_ref, nt_ref, xs_ref, w_ref, wg_ref, wu_ref, wd_ref, y_ref):
    i = pl.program_id(0)

    @pl.when(i < nt_ref[0])
    def _():
        word = xs_ref[...]
        lo = lax.bitcast_convert_type(word << 16, F32)
        hi = lax.bitcast_convert_type(word & jnp.uint32(0xFFFF0000), F32)
        x = jnp.concatenate([lo, hi], axis=1).astype(BF16)
        hg = _dot(x, wg_ref[...].astype(BF16))
        hu = _dot(x, wu_ref[...].astype(BF16))
        act = hg * jax.nn.sigmoid(hg) * hu * w_ref[...]
        y_ref[...] = _dot(act.astype(BF16), wd_ref[...].astype(BF16))

    @pl.when(i >= nt_ref[0])
    def _():
        y_ref[...] = jnp.zeros(y_ref.shape, F32)


def _moe_call(xs, row_w, tile_e, n_used, w_gate, w_up, w_down, layer):
    R, D = xs.shape[0], w_gate.shape[-2]
    tm = MOE_TILE
    F = w_gate.shape[-1]
    grid_spec = pltpu.PrefetchScalarGridSpec(
        num_scalar_prefetch=2, grid=(R // tm,),
        in_specs=[pl.BlockSpec((tm, xs.shape[1]), lambda i, te, nt: (i, 0)),
                  pl.BlockSpec((tm, 1), lambda i, te, nt: (i, 0)),
                  pl.BlockSpec((None, None, D, F), lambda i, te, nt: (layer, te[i], 0, 0)),
                  pl.BlockSpec((None, None, D, F), lambda i, te, nt: (layer, te[i], 0, 0)),
                  pl.BlockSpec((None, None, F, D), lambda i, te, nt: (layer, te[i], 0, 0))],
        out_specs=pl.BlockSpec((tm, D), lambda i, te, nt: (i, 0)))
    return pl.pallas_call(
        _moe_kernel, grid_spec=grid_spec, out_shape=jax.ShapeDtypeStruct((R, D), F32),
        compiler_params=_cparams("arbitrary"), name="moe_experts",
    )(tile_e, n_used, xs, row_w, w_gate, w_up, w_down)


def _route_meta(route, counts_f, tm, n_tiles):
    T = route.shape[0]
    n_assign = TOP_K * T
    e = route[:, :TOP_K].astype(jnp.int32).reshape(-1)
    w = route[:, TOP_K:2 * TOP_K].reshape(-1)
    order = jnp.argsort(e, stable=True).astype(jnp.int32)
    counts = counts_f.astype(jnp.int32)
    zero = jnp.zeros((1,), jnp.int32)
    start = jnp.concatenate([zero, jnp.cumsum(counts)])
    pstart = jnp.concatenate([zero, jnp.cumsum(((counts + tm - 1) // tm) * tm)])
    n_used = (pstart[N_EXPERTS] // tm).reshape(1)
    tile_lo = jnp.arange(n_tiles, dtype=jnp.int32) * tm
    tile_e = jnp.sum((pstart[None, 1:] <= tile_lo[:, None]).astype(jnp.int32), axis=1)
    tile_e = jnp.minimum(tile_e, N_EXPERTS - 1)
    oh = (tile_e[:, None] == jnp.arange(N_EXPERTS, dtype=jnp.int32)[None, :]).astype(jnp.int32)
    pick = lambda tab: jnp.sum(oh * tab[None, :N_EXPERTS], axis=1)
    k = (tile_lo - pick(pstart))[:, None] + jnp.arange(tm, dtype=jnp.int32)[None, :]
    valid = (k < pick(counts)[:, None]).reshape(-1)
    idx = jnp.clip(pick(start)[:, None] + k, 0, n_assign - 1).reshape(-1)
    src = order[idx]
    row_tok = jnp.where(valid, src // TOP_K, 0)
    row_w = jnp.where(valid, w[src], 0.0)
    ps_row = jnp.pad(pstart[:N_EXPERTS].astype(F32), (0, LANES - N_EXPERTS)).reshape(1, LANES)
    pos = _dest_call(route, ps_row)[:, :TOP_K].astype(jnp.int32)
    return row_tok, row_w.reshape(-1, 1), pos, tile_e, n_used


def _final_kernel(x_ref, g_ref, o_ref):
    o_ref[...] = _rms(x_ref[...], g_ref[...])


def _final_call(x, g):
    T, D = x.shape
    tm = TOKEN_TILE
    return pl.pallas_call(
        _final_kernel, grid=(T // tm,),
        in_specs=[pl.BlockSpec((tm, D), lambda i: (i, 0)), pl.BlockSpec((1, D), lambda i: (0, 0))],
        out_specs=pl.BlockSpec((tm, D), lambda i: (i, 0)),
        out_shape=jax.ShapeDtypeStruct((T, D), F32), compiler_params=_cparams("parallel"), name="final_norm",
    )(x, g)


def _block_diag(w):
    n, c, d = w.shape
    return jnp.einsum('ncd,nm->ncmd', w, jnp.eye(n, dtype=w.dtype)).reshape(n * c, n * d)


def _half_swap(w):
    half = w.shape[-1] // 2
    return jnp.concatenate([w[..., half:], w[..., :half]], axis=-1)


def _layer_weights(l, p, dims):
    lw, qr, kvr, gk_w, gv_w = dims
    D = p['w_in'].shape[1]
    row = lambda a: a.reshape(1, -1).astype(F32)
    w_in = p['w_in'][l]
    sizes = (lw, lw, qr, kvr, QK_ROPE, gk_w, gk_w, gv_w, p['gla_wa2'].shape[1], gv_w)
    x_lru, y_lru, c_q, c_kv, k_pe, g_q, g_k, g_v, g_a, g_o = jnp.split(w_in, list(np.cumsum(sizes)[:-1]), axis=1)
    tail_pad = LANES - 2 * QK_ROPE - g_a.shape[1]
    tail = jnp.concatenate([k_pe, _half_swap(k_pe), g_a, jnp.zeros((D, tail_pad), F32)], axis=1)
    w_in_p = jnp.concatenate([x_lru, y_lru, c_q, c_kv, g_q, g_k, g_v, g_o, tail], axis=1).astype(BF16)

    w_uq = p['w_uq'][l].reshape(qr, MLA_HEADS, QK_NOPE + QK_ROPE)
    nope = w_uq[:, :, :QK_NOPE].reshape(qr, MLA_HEADS * QK_NOPE)
    pe = w_uq[:, :, QK_NOPE:]
    widen = lambda a: jnp.pad(a, ((0, 0), (0, 0), (0, LANES - QK_ROPE))).reshape(qr, MLA_HEADS * LANES)
    w_q = jnp.concatenate([nope, widen(pe), widen(_half_swap(pe))], axis=1).astype(BF16)
    w_abs = _block_diag(jnp.transpose(p['w_uk'][l], (1, 2, 0))).astype(BF16)
    wa2 = jnp.zeros((LANES, gk_w), F32).at[2 * QK_ROPE:2 * QK_ROPE + g_a.shape[1]].set(p['gla_wa2'][l]).astype(BF16)

    w_uv = p['w_uv'][l]
    mw = MLA_HEADS * V_HEAD
    w_uv_cat = w_uv.reshape(kvr, mw)
    w_uv_rows = _block_diag(jnp.transpose(w_uv, (1, 0, 2)))
    n_a, n_b, n_c = jnp.split(p['out_norm_g'][l], [lw, lw + mw])
    n_r = N_GROUPS + N_EXPERTS
    w_r = jnp.concatenate([p['router_wg'][l], p['router_we'][l], jnp.zeros((D, LANES - n_r), F32)], axis=1)
    b_r = jnp.concatenate([p['router_bg'][l], p['router_be'][l], jnp.zeros((LANES - n_r,), F32)])
    return dict(
        g1=row(p['norm1_g'][l]), w_in=w_in_p, q_g=row(p['q_norm_g'][l]), w_q=w_q, w_abs=w_abs,
        kv_g=row(p['kv_norm_g'][l]), wa2=wa2, ba=row(p['gla_ba'][l]),
        conv_w=p['conv_w'][l], conv_b=row(p['conv_b'][l]),
        lru_wa=_block_diag(p['lru_wa'][l]).astype(BF16), lru_ba=row(p['lru_ba'][l]),
        lru_wi=_block_diag(p['lru_wi'][l]).astype(BF16), lru_bi=row(p['lru_bi'][l]),
        lru_lam=row(p['lru_lambda'][l]), n_a=row(n_a), n_b=row(n_b), n_c=row(n_c),
        gla_g=row(jnp.tile(p['gla_norm_g'][l], GLA_HEADS)),
        w_uv_cat=w_uv_cat.astype(BF16), w_uv_rows_t=w_uv_rows.T.astype(BF16),
        w_out=p['w_out'][l].astype(BF16), g2=row(p['norm2_g'][l]), w_r=w_r, b_r=row(b_r),
    )


def _rope_tables(positions):
    half = QK_ROPE // 2
    inv = ROPE_THETA ** (-np.arange(half, dtype=np.float64) / half)
    ang = np.asarray(positions, np.float64)[:, None] * inv
    zeros = np.zeros((ang.shape[0], LANES - QK_ROPE))
    cos = np.concatenate([np.cos(ang), np.cos(ang), zeros], axis=1)
    sin = np.concatenate([-np.sin(ang), np.sin(ang), zeros], axis=1)
    return jnp.asarray(cos, F32), jnp.asarray(sin, F32)


def _gla_state_to_rows(s):
    B = s.shape[0]
    eye = jnp.eye(GLA_HEADS, dtype=s.dtype)
    return jnp.einsum('bhde,hg->bhegd', s, eye).reshape(B, GLA_HEADS * GLA_DV, GLA_HEADS * GLA_DK)


def _gla_rows_to_state(st):
    B = st.shape[0]
    s5 = st.reshape(B, GLA_HEADS, GLA_DV, GLA_HEADS, GLA_DK)
    blocks = jnp.stack([s5[:, h, :, h, :] for h in range(GLA_HEADS)], axis=1)
    return jnp.swapaxes(blocks, -1, -2)


def kernel(x_prompt, x_sample, cache_ckv, cache_kpe, page_table, state_conv, state_lru, state_gla, norm1_g, w_in, conv_w, conv_b, lru_wa, lru_ba, lru_wi, lru_bi, lru_lambda, q_norm_g, w_uq, kv_norm_g, w_uk, w_uv, gla_wa2, gla_ba, gla_norm_g, out_norm_g, w_out, norm2_g, router_wg, router_bg, router_we, router_be, w_gate, w_up, w_down, final_norm_g):
    p = dict(norm1_g=norm1_g, w_in=w_in, conv_w=conv_w, conv_b=conv_b, lru_wa=lru_wa, lru_ba=lru_ba,
             lru_wi=lru_wi, lru_bi=lru_bi, lru_lambda=lru_lambda, q_norm_g=q_norm_g, w_uq=w_uq,
             kv_norm_g=kv_norm_g, w_uk=w_uk, w_uv=w_uv, gla_wa2=gla_wa2, gla_ba=gla_ba,
             gla_norm_g=gla_norm_g, out_norm_g=out_norm_g, w_out=w_out, norm2_g=norm2_g,
             router_wg=router_wg, router_bg=router_bg, router_we=router_we, router_be=router_be)
    Bp, Sp, D = x_prompt.shape
    Bs, Ss, _ = x_sample.shape
    depth = w_in.shape[0]
    lw = state_lru.shape[-1]
    qr = q_norm_g.shape[-1]
    kvr = kv_norm_g.shape[-1]
    gk_w = GLA_HEADS * GLA_DK
    gv_w = GLA_HEADS * GLA_DV
    dims = (lw, qr, kvr, gk_w, gv_w)
    assert kvr == LANES and gk_w == LANES
    Tp, Ts = Bp * Sp, Bs * Ss
    T = Tp + Ts
    past = page_table.shape[1] * cache_ckv.shape[2]

    cos_t, sin_t = _rope_tables(np.concatenate([np.tile(np.arange(Sp), Bp), np.tile(past + np.arange(Ss), Bs)]))
    cache_kpe_t = jnp.swapaxes(cache_kpe, 2, 3)
    n_tiles = (TOP_K * T + N_EXPERTS * (MOE_TILE - 1)) // MOE_TILE + 1

    x = jnp.concatenate([x_prompt.reshape(Tp, D), x_sample.reshape(Ts, D)], axis=0)
    outs = {k: [] for k in ('ckv_p', 'kpe_p', 'ckv_s', 'kpe_s', 'conv_p', 'conv_s', 'lru_p', 'lru_s', 'gla_p', 'gla_s')}
    for l in range(depth):
        lw_ = _layer_weights(l, p, dims)
        zl, qcat, kcat, kcat_t, ckv_n, kpe_n, g = _pre_call(x, lw_, cos_t, sin_t, dims)

        a_p, conv_p, lru_p = _lru_call(zl, 0, Bp, Sp, jnp.zeros((Bp, CONV_WIDTH - 1, lw), F32),
                                       jnp.zeros((Bp, lw), F32), lw_, BF16)
        a_s, conv_s, lru_s = _lru_call(zl, Tp, Bs, Ss, state_conv[l], state_lru[l], lw_, F32)

        c_p, gla_p = _gla_call(g, 0, Bp, Sp, jnp.zeros((Bp, gv_w, gk_w), F32), lw_, BF16, gk_w, gv_w)
        c_s, gla_s = _gla_call(g, Tp, Bs, Ss, _gla_state_to_rows(state_gla[l]), lw_, F32, gk_w, gv_w)

        b_p = _attn_prompt_call(qcat, kcat, kcat_t, Bp, Sp, lw_, kvr)
        b_s = _attn_sample_call(page_table, qcat[Tp:].reshape(Bs, Ss, -1), ckv_n[Tp:].reshape(Bs, Ss, kvr),
                                kpe_n[Tp:].reshape(Bs, Ss, QK_ROPE), cache_ckv, cache_kpe_t, l, lw_, kvr)

        a = jnp.concatenate([a_p, a_s.astype(BF16)], axis=0)
        b = jnp.concatenate([b_p, b_s.reshape(Ts, -1).astype(BF16)], axis=0)
        c = jnp.concatenate([c_p, c_s.astype(BF16)], axis=0)
        x1, hn, route, cnt = _post_call(a, b, c, x, lw_)

        row_tok, row_w, pos, tile_e, n_used = _route_meta(route, cnt[0, :N_EXPERTS], MOE_TILE, n_tiles)
        ys = _moe_call(hn[row_tok], row_w, tile_e, n_used, w_gate, w_up, w_down, l)
        x = x1 + ys[pos[:, 0]] + ys[pos[:, 1]]

        outs['ckv_p'].append(ckv_n[:Tp].reshape(Bp, Sp, kvr))
        outs['kpe_p'].append(kpe_n[:Tp].reshape(Bp, Sp, QK_ROPE))
        outs['ckv_s'].append(ckv_n[Tp:].reshape(Bs, Ss, kvr))
        outs['kpe_s'].append(kpe_n[Tp:].reshape(Bs, Ss, QK_ROPE))
        outs['conv_p'].append(conv_p)
        outs['conv_s'].append(conv_s)
        outs['lru_p'].append(lru_p.reshape(Bp, lw))
        outs['lru_s'].append(lru_s.reshape(Bs, lw))
        outs['gla_p'].append(_gla_rows_to_state(gla_p))
        outs['gla_s'].append(_gla_rows_to_state(gla_s))

    y = _final_call(x, final_norm_g.reshape(1, D))
    st = {k: jnp.stack(v) for k, v in outs.items()}
    return (y[:Tp].reshape(Bp, Sp, D), y[Tp:].reshape(Bs, Ss, D),
            st['ckv_p'], st['kpe_p'], st['ckv_s'], st['kpe_s'], st['conv_p'], st['conv_s'],
            st['lru_p'], st['lru_s'], st['gla_p'], st['gla_s'])
```

```python
import functools
import math

import numpy as np
import jax
import jax.numpy as jnp
from jax import lax
from jax.experimental import pallas as pl
from jax.experimental.pallas import tpu as pltpu

F32 = jnp.float32
BF16 = jnp.bfloat16

LRU_BLOCKS = 4
CONV_WIDTH = 4
LRU_C = 8.0
MLA_HEADS = 8
QK_NOPE = 64
QK_ROPE = 32
V_HEAD = 64
ROPE_THETA = 10000.0
GLA_HEADS = 4
GLA_DK = 32
GLA_DV = 64
GLA_TAU = 16.0
GLA_CHUNK = 16
N_GROUPS = 4
EXPERTS_PER_GROUP = 8
N_EXPERTS = N_GROUPS * EXPERTS_PER_GROUP
TOP_K = 2
EPS = 1e-6

LANES = 128
SUBLANES = 8
BF16_ROWS = 16
VMEM_LIMIT_BYTES = 56 * 1024 * 1024

TOKEN_TILE = 512
SEQ_TILE = 256
MOE_TILE = 256
MOE_CHUNKS = 4
PAGES_PER_STEP = 16
NEG_BIG = -1e30


def _cparams(*sem):
    return pltpu.CompilerParams(dimension_semantics=sem, vmem_limit_bytes=VMEM_LIMIT_BYTES)


def _rms(x, g):
    return x * lax.rsqrt(jnp.mean(x * x, axis=-1, keepdims=True) + EPS) * g


def _dot(a, b):
    return jnp.dot(a, b, preferred_element_type=F32)


def _dot_nt(a, b):
    return lax.dot_general(a, b, (((1,), (1,)), ((), ())), preferred_element_type=F32)


def _softplus(x):
    return jnp.maximum(x, 0.0) + jnp.log1p(jnp.exp(-jnp.abs(x)))


def _shift_rows(x, d, fill):
    row = lax.broadcasted_iota(jnp.int32, x.shape, 0)
    return jnp.where(row >= d, pltpu.roll(x, d, 0), fill)


def _pre_kernel(x_ref, g1_ref, win_ref, qg_ref, wq_ref, wabs_ref, kvg_ref, wa2_ref, ba_ref,
                cos_ref, sin_ref, zl_ref, q_ref, k_ref, kt_ref, ckv_ref, kpe_ref, g_ref, *, dims):
    lw, qr, kvr, gk_w, gv_w = dims
    xn = _rms(x_ref[...], g1_ref[...])
    z = _dot(xn.astype(BF16), win_ref[...])
    o = 2 * lw
    zl_ref[...] = z[:, :o]
    cq = z[:, o:o + qr]
    o += qr
    ckv = z[:, o:o + kvr]
    o += kvr
    gqk = z[:, o:o + 2 * gk_w]
    o += 2 * gk_w
    gvo = z[:, o:o + 2 * gv_w]
    o += 2 * gv_w
    tail = z[:, o:o + LANES]
    cos = cos_ref[...]
    sin = sin_ref[...]
    scale = (QK_NOPE + QK_ROPE) ** -0.5 * math.log2(math.e)

    ckv_n = _rms(ckv, kvg_ref[...])
    kpe = tail * cos + pltpu.roll(tail, LANES - QK_ROPE, 1) * sin
    ckv_ref[...] = ckv_n
    kpe_ref[...] = kpe[:, :QK_ROPE]
    k_ref[:, :kvr] = ckv_n.astype(BF16)
    k_ref[:, kvr:] = kpe.astype(BF16)
    ones = jnp.ones((kt_ref.shape[0] - kvr, ckv_n.shape[0]), F32)
    kt_ref[...] = jnp.concatenate([ckv_n.T, ones], axis=0).astype(BF16)

    cqn = _rms(cq, qg_ref[...]).astype(BF16)
    qall = _dot(cqn, wq_ref[...])
    n_nope = MLA_HEADS * QK_NOPE
    q_abs = _dot(qall[:, :n_nope].astype(BF16), wabs_ref[...]) * scale
    for h in range(MLA_HEADS):
        pe = qall[:, n_nope + h * LANES:n_nope + (h + 1) * LANES]
        sw = qall[:, n_nope + (MLA_HEADS + h) * LANES:n_nope + (MLA_HEADS + h + 1) * LANES]
        q_ref[:, 2 * h * LANES:(2 * h + 1) * LANES] = q_abs[:, h * kvr:(h + 1) * kvr].astype(BF16)
        q_ref[:, (2 * h + 1) * LANES:(2 * h + 2) * LANES] = ((pe * cos + sw * sin) * scale).astype(BF16)

    la_pre = _dot(tail.astype(BF16), wa2_ref[...]) + ba_ref[...]
    la = -_softplus(-la_pre) * (1.0 / GLA_TAU)
    g_ref[:, :gk_w] = gqk[:, :gk_w] * (GLA_DK ** -0.5)
    g_ref[:, gk_w:2 * gk_w] = gqk[:, gk_w:]
    g_ref[:, 2 * gk_w:3 * gk_w] = la
    g_ref[:, 3 * gk_w:] = gvo


def _pre_call(x, lw_, cos_t, sin_t, dims):
    T, D = x.shape
    lw, qr, kvr, gk_w, gv_w = dims
    tm = TOKEN_TILE
    assert T % tm == 0
    row = lambda i: (i, 0)
    full = lambda i: (0, 0)
    wspec = lambda a: pl.BlockSpec(a.shape, full)
    ins = [x, lw_['g1'], lw_['w_in'], lw_['q_g'], lw_['w_q'], lw_['w_abs'], lw_['kv_g'], lw_['wa2'], lw_['ba']]
    in_specs = [pl.BlockSpec((tm, D), row)] + [wspec(a) for a in ins[1:]]
    in_specs += [pl.BlockSpec((tm, LANES), row), pl.BlockSpec((tm, LANES), row)]
    out_shape = (
        jax.ShapeDtypeStruct((T, 2 * lw), F32),
        jax.ShapeDtypeStruct((T, 2 * LANES * MLA_HEADS), BF16),
        jax.ShapeDtypeStruct((T, 2 * LANES), BF16),
        jax.ShapeDtypeStruct((T // tm, kvr + BF16_ROWS, tm), BF16),
        jax.ShapeDtypeStruct((T, kvr), F32),
        jax.ShapeDtypeStruct((T, QK_ROPE), F32),
        jax.ShapeDtypeStruct((T, 3 * gk_w + 2 * gv_w), F32),
    )
    out_specs = tuple(pl.BlockSpec((None, s.shape[1], tm), lambda i: (i, 0, 0)) if len(s.shape) == 3
                      else pl.BlockSpec((tm, s.shape[1]), row) for s in out_shape)
    return pl.pallas_call(
        functools.partial(_pre_kernel, dims=dims),
        grid=(T // tm,), in_specs=in_specs, out_specs=out_specs, out_shape=out_shape,
        compiler_params=_cparams("parallel"), name="pre_proj",
    )(*ins, cos_t, sin_t)


def _lru_kernel(zl_ref, cbuf_ref, h0_ref, cw_ref, cb_ref, wa_ref, ba_ref, wi_ref, bi_ref, lam_ref, na_ref,
                a_ref, conv_ref, hout_ref, xbuf, hcar, *, ts, lw):
    i = pl.program_id(1)
    last = pl.num_programs(1) - 1
    pad = SUBLANES
    nbuf = CONV_WIDTH - 1

    @pl.when(i == 0)
    def _():
        xbuf[0:pad, :] = jnp.zeros((pad, lw), F32)
        xbuf[pad - nbuf:pad, :] = cbuf_ref[...]
        hcar[...] = h0_ref[...]

    x = zl_ref[:, :lw]
    y = zl_ref[:, lw:]
    xbuf[pad:pad + ts, :] = x
    xc = cb_ref[...] + cw_ref[nbuf:nbuf + 1, :] * x
    for k in range(nbuf):
        xc = xc + cw_ref[k:k + 1, :] * xbuf[pad - nbuf + k:pad - nbuf + k + ts, :]

    @pl.when(i == last)
    def _():
        conv_ref[...] = xbuf[pad + ts - nbuf:pad + ts, :]

    xbuf[0:pad, :] = xbuf[ts:ts + pad, :]

    xb = xc.astype(BF16)
    r = jax.nn.sigmoid(_dot(xb, wa_ref[...]) + ba_ref[...])
    gi = jax.nn.sigmoid(_dot(xb, wi_ref[...]) + bi_ref[...])
    log_a = (-LRU_C) * r * _softplus(-lam_ref[...])
    a = jnp.exp(log_a)
    th = jnp.tanh(log_a)
    u = jnp.sqrt(-2.0 * th / (1.0 - th)) * (gi * xc)

    d = 1
    while d < ts:
        u = a * _shift_rows(u, d, 0.0) + u
        a = a * _shift_rows(a, d, 1.0)
        d *= 2
    h = a * hcar[...] + u
    hcar[...] = h[ts - 1:ts, :]

    @pl.when(i == last)
    def _():
        hout_ref[...] = h[ts - 1:ts, :]

    out_a = h * jax.nn.gelu(y)
    a_ref[...] = _rms(out_a, na_ref[...]).astype(a_ref.dtype)


def _lru_call(zl, row_off, B, S, cbuf, h0, lw_, out_dtype):
    lw = h0.shape[-1]
    ts = min(S, SEQ_TILE)
    assert S % ts == 0 and row_off % ts == 0 and S >= CONV_WIDTH - 1
    n = S // ts
    off = row_off // ts
    full = lambda b, i: (0, 0)
    ws = [lw_['conv_w'], lw_['conv_b'], lw_['lru_wa'], lw_['lru_ba'], lw_['lru_wi'], lw_['lru_bi'],
          lw_['lru_lam'], lw_['n_a']]
    in_specs = [
        pl.BlockSpec((ts, 2 * lw), lambda b, i: (off + b * n + i, 0)),
        pl.BlockSpec((None, CONV_WIDTH - 1, lw), lambda b, i: (b, 0, 0)),
        pl.BlockSpec((None, 1, lw), lambda b, i: (b, 0, 0)),
    ] + [pl.BlockSpec(a.shape, full) for a in ws]
    out_shape = (
        jax.ShapeDtypeStruct((B * S, lw), out_dtype),
        jax.ShapeDtypeStruct((B, CONV_WIDTH - 1, lw), F32),
        jax.ShapeDtypeStruct((B, 1, lw), F32),
    )
    out_specs = (
        pl.BlockSpec((ts, lw), lambda b, i: (b * n + i, 0)),
        pl.BlockSpec((None, CONV_WIDTH - 1, lw), lambda b, i: (b, 0, 0)),
        pl.BlockSpec((None, 1, lw), lambda b, i: (b, 0, 0)),
    )
    return pl.pallas_call(
        functools.partial(_lru_kernel, ts=ts, lw=lw),
        grid=(B, n), in_specs=in_specs, out_specs=out_specs, out_shape=out_shape,
        scratch_shapes=[pltpu.VMEM((ts + SUBLANES, lw), F32), pltpu.VMEM((1, lw), F32)],
        compiler_params=_cparams("parallel", "arbitrary"), name="rg_lru",
    )(zl, cbuf, h0.reshape(B, 1, lw), *ws)


def _lru_seg_kernel(zl_ref, hist_ref, h0_ref, cw_ref, cb_ref, wa_ref, ba_ref, wi_ref, bi_ref, lam_ref, na_ref,
                    a_ref, h_ref, *, seg, lw):
    nbuf = CONV_WIDTH - 1
    x = zl_ref[:, :lw]
    y = zl_ref[:, lw:]
    n = x.shape[0]
    pos = lax.broadcasted_iota(jnp.int32, (n, 1), 0) & (seg - 1)
    hist = hist_ref[...]
    xc = cb_ref[...] + cw_ref[nbuf:nbuf + 1, :] * x
    for k in range(nbuf):
        j = nbuf - k
        prev = jnp.where(pos >= j, pltpu.roll(x, j, 0), pltpu.roll(hist, (j - seg) % n, 0))
        xc = xc + cw_ref[k:k + 1, :] * prev
    xb = xc.astype(BF16)
    r = jax.nn.sigmoid(_dot(xb, wa_ref[...]) + ba_ref[...])
    gi = jax.nn.sigmoid(_dot(xb, wi_ref[...]) + bi_ref[...])
    log_a = (-LRU_C) * r * _softplus(-lam_ref[...])
    a = jnp.exp(log_a)
    th = jnp.tanh(log_a)
    u = jnp.sqrt(-2.0 * th / (1.0 - th)) * (gi * xc)
    d = 1
    while d < seg:
        u = a * jnp.where(pos >= d, pltpu.roll(u, d, 0), 0.0) + u
        a = a * jnp.where(pos >= d, pltpu.roll(a, d, 0), 1.0)
        d *= 2
    h = a * h0_ref[...] + u
    h_ref[...] = h
    a_ref[...] = _rms(h * jax.nn.gelu(y), na_ref[...]).astype(a_ref.dtype)


def _lru_seg_call(zl, row_off, B, S, cbuf, h0, lw_):
    lw = h0.shape[-1]
    n = B * S
    tm = min(n, TOKEN_TILE)
    assert S & (S - 1) == 0 and S >= CONV_WIDTH - 1 and tm % S == 0 and n % tm == 0 and row_off % tm == 0
    off = row_off // tm
    hist = jnp.pad(cbuf, ((0, 0), (S - (CONV_WIDTH - 1), 0), (0, 0))).reshape(n, lw)
    h0_rows = jnp.repeat(h0, S, axis=0)
    full = lambda i: (0, 0)
    row = lambda i: (i, 0)
    ws = [lw_['conv_w'], lw_['conv_b'], lw_['lru_wa'], lw_['lru_ba'], lw_['lru_wi'], lw_['lru_bi'],
          lw_['lru_lam'], lw_['n_a']]
    a, h = pl.pallas_call(
        functools.partial(_lru_seg_kernel, seg=S, lw=lw),
        grid=(n // tm,),
        in_specs=[pl.BlockSpec((tm, 2 * lw), lambda i: (off + i, 0)), pl.BlockSpec((tm, lw), row),
                  pl.BlockSpec((tm, lw), row)] + [pl.BlockSpec(w.shape, full) for w in ws],
        out_specs=(pl.BlockSpec((tm, lw), row), pl.BlockSpec((tm, lw), row)),
        out_shape=(jax.ShapeDtypeStruct((n, lw), BF16), jax.ShapeDtypeStruct((n, lw), F32)),
        compiler_params=_cparams("parallel"), name="rg_lru_seg",
    )(zl, hist, h0_rows, *ws)
    return a, h.reshape(B, S, lw)[:, S - 1]


def _gla_kernel(g_ref, s0_ref, gn_ref, nc_ref, c_ref, sout_ref, st, *, ts, gk_w, gv_w):
    i = pl.program_id(1)
    last = pl.num_programs(1) - 1
    C = GLA_CHUNK
    rows = min(ts, C)
    n_chunks = max(ts // C, 1)
    log2c = int(math.log2(C))

    @pl.when(i == 0)
    def _():
        st[...] = s0_ref[...]

    dk_sh, dv_sh = int(math.log2(GLA_DK)), int(math.log2(GLA_DV))
    hd = lax.broadcasted_iota(jnp.int32, (gk_w, gv_w), 0) >> dk_sh
    he = lax.broadcasted_iota(jnp.int32, (gk_w, gv_w), 1) >> dv_sh
    same = (hd == he).astype(BF16)
    he_t = lax.broadcasted_iota(jnp.int32, (gv_w, gk_w), 0) >> dv_sh
    hd_t = lax.broadcasted_iota(jnp.int32, (gv_w, gk_w), 1) >> dk_sh
    same_t = (he_t == hd_t).astype(F32)
    sel_t = lax.broadcasted_iota(jnp.int32, (C, C * C), 0)
    sel_r = lax.broadcasted_iota(jnp.int32, (C, C * C), 1) >> log2c
    sel = (sel_t == sel_r).astype(BF16)
    srow = lax.broadcasted_iota(jnp.int32, (C, 1), 0)

    blk = g_ref[...]
    if rows < C:
        blk = jnp.concatenate([blk, jnp.zeros((C - rows, blk.shape[1]), F32)], axis=0)
    q = blk[:, :gk_w]
    k = blk[:, gk_w:2 * gk_w]
    la = blk[:, 2 * gk_w:3 * gk_w]
    v = blk[:, 3 * gk_w:3 * gk_w + gv_w]
    nrow = n_chunks * C
    pos = lax.broadcasted_iota(jnp.int32, (nrow, 1), 0) & (C - 1)
    cum = la
    d = 1
    while d < C:
        cum = cum + jnp.where(pos >= d, pltpu.roll(cum, d, 0), 0.0)
        d *= 2
    tot = jnp.where(pos == C - 1, cum, 0.0)
    d = 1
    while d < C:
        tot = tot + jnp.where(pos < C - d, pltpu.roll(tot, nrow - d, 0), 0.0)
        d *= 2
    qe = (q * jnp.exp(cum)).astype(BF16)
    kdec = (k * jnp.exp(tot - cum)).astype(BF16)
    dec = jnp.exp(tot)
    vb = v.astype(BF16)

    o_intra, upd = [], []
    for c in range(n_chunks):
        sl = slice(c * C, (c + 1) * C)
        cum_c, q_c, k_c, v_c = cum[sl], q[sl], k[sl], v[sl]
        pieces = []
        for t in range(C):
            diff = jnp.where(srow <= t, cum_c[t:t + 1, :] - cum_c, NEG_BIG)
            pieces.append(q_c[t:t + 1, :] * k_c * jnp.exp(diff))
        w = jnp.concatenate(pieces, axis=0)
        att = _dot(w.astype(BF16), same)
        xv = att * jnp.concatenate([v_c] * C, axis=0)
        o_intra.append(_dot(sel, xv.astype(BF16)))
        upd.append(lax.dot_general(vb[sl], kdec[sl], (((0,), (0,)), ((), ())),
                                   preferred_element_type=F32) * same_t)

    s_t = st[...]
    outs = []
    for c in range(n_chunks):
        sl = slice(c * C, (c + 1) * C)
        outs.append(o_intra[c] + _dot_nt(qe[sl], s_t.astype(BF16)))
        s_t = s_t * dec[c * C:c * C + 1, :] + upd[c]
    st[...] = s_t
    o = jnp.concatenate(outs, axis=0)[:ts] if n_chunks > 1 else outs[0][:ts]

    @pl.when(i == last)
    def _():
        sout_ref[...] = st[...]

    go = g_ref[:, 3 * gk_w + gv_w:]
    e64 = (lax.broadcasted_iota(jnp.int32, (gv_w, gv_w), 0) >> dv_sh
           == lax.broadcasted_iota(jnp.int32, (gv_w, gv_w), 1) >> dv_sh).astype(BF16)
    osq = o * o
    osq_hi = osq.astype(BF16)
    osq_lo = (osq - osq_hi.astype(F32)).astype(BF16)
    ms = (_dot(osq_hi, e64) + _dot(osq_lo, e64)) * (1.0 / GLA_DV)
    out_c = o * lax.rsqrt(ms + EPS) * gn_ref[...] * (go * jax.nn.sigmoid(go))
    c_ref[...] = _rms(out_c, nc_ref[...]).astype(c_ref.dtype)


def _gla_call(g, row_off, B, S, s0t, lw_, out_dtype, gk_w, gv_w):
    ts = min(S, SEQ_TILE)
    assert S % ts == 0 and row_off % ts == 0 and (ts % GLA_CHUNK == 0 or ts < GLA_CHUNK)
    n = S // ts
    off = row_off // ts
    gw = g.shape[1]
    full = lambda b, i: (0, 0)
    out_shape = (jax.ShapeDtypeStruct((B * S, gv_w), out_dtype),
                 jax.ShapeDtypeStruct((B, gv_w, gk_w), F32))
    return pl.pallas_call(
        functools.partial(_gla_kernel, ts=ts, gk_w=gk_w, gv_w=gv_w),
        grid=(B, n),
        in_specs=[pl.BlockSpec((ts, gw), lambda b, i: (off + b * n + i, 0)),
                  pl.BlockSpec((None, gv_w, gk_w), lambda b, i: (b, 0, 0)),
                  pl.BlockSpec((1, gv_w), full), pl.BlockSpec((1, gv_w), full)],
        out_specs=(pl.BlockSpec((ts, gv_w), lambda b, i: (b * n + i, 0)),
                   pl.BlockSpec((None, gv_w, gk_w), lambda b, i: (b, 0, 0))),
        out_shape=out_shape,
        scratch_shapes=[pltpu.VMEM((gv_w, gk_w), F32)],
        compiler_params=_cparams("parallel", "arbitrary"), name="gla",
    )(g, s0t, lw_['gla_g'], lw_['n_c'])


def _tree(op, xs):
    xs = list(xs)
    while len(xs) > 1:
        xs = [op(xs[i], xs[i + 1]) if i + 1 < len(xs) else xs[i] for i in range(0, len(xs), 2)]
    return xs[0]


def _attn_prompt_kernel(q_ref, k_ref, kt_ref, wuvt_ref, nb_ref, o_ref, m_scr, acc_scr, *, tq, kvr):
    i = pl.program_id(1)
    kw = 2 * LANES
    m_scr[...] = jnp.full(m_scr.shape, NEG_BIG, F32)
    acc_scr[...] = jnp.zeros(acc_scr.shape, F32)

    def block(j, masked):
        kb = k_ref[pl.ds(pl.multiple_of(j * tq, tq), tq), :]
        kbt = kt_ref[j]
        if masked:
            key = lax.broadcasted_iota(jnp.int32, (tq, tq), 0)
            qry = lax.broadcasted_iota(jnp.int32, (tq, tq), 1)
            keep = key <= qry
        qk = lambda h: _dot_nt(kb, q_ref[:, h * kw:(h + 1) * kw])
        st_next = qk(0)
        for h in range(MLA_HEADS):
            st = st_next
            if h + 1 < MLA_HEADS:
                st_next = qk(h + 1)
            if masked:
                st = jnp.where(keep, st, NEG_BIG)
            m_old = m_scr[h]
            m_new = jnp.maximum(m_old, jnp.max(st, axis=0, keepdims=True))
            alpha = jnp.exp2(m_old - m_new)
            pt = jnp.exp2(st - m_new).astype(BF16)
            acc_scr[h] = alpha * acc_scr[h] + _dot(kbt, pt)
            m_scr[h] = m_new

    def body(j, carry):
        block(j, False)
        return carry

    lax.fori_loop(0, i, body, 0)
    block(i, True)

    out_t = jnp.zeros((wuvt_ref.shape[0], tq), F32)
    for h in range(MLA_HEADS):
        acc = acc_scr[h]
        o_t = (acc[:kvr, :] / acc[kvr:kvr + 1, :]).astype(BF16)
        out_t = out_t + _dot(wuvt_ref[:, h * kvr:(h + 1) * kvr], o_t)
    ms = jnp.mean(out_t * out_t, axis=0, keepdims=True)
    out_t = out_t * lax.rsqrt(ms + EPS) * nb_ref[...]
    o_ref[...] = out_t.T.astype(o_ref.dtype)


def _attn_prompt_call(qcat, kcat, kcat_t, B, S, lw_, kvr):
    tq = kcat_t.shape[2]
    assert S % tq == 0
    n = S // tq
    mw = lw_['w_uv_rows_t'].shape[0]
    full = lambda b, i: (0, 0)
    return pl.pallas_call(
        functools.partial(_attn_prompt_kernel, tq=tq, kvr=kvr),
        grid=(B, n),
        in_specs=[pl.BlockSpec((tq, qcat.shape[1]), lambda b, i: (b * n + i, 0)),
                  pl.BlockSpec((S, kcat.shape[1]), lambda b, i: (b, 0)),
                  pl.BlockSpec((n, kcat_t.shape[1], tq), lambda b, i: (b, 0, 0)),
                  pl.BlockSpec(lw_['w_uv_rows_t'].shape, full), pl.BlockSpec((mw, 1), full)],
        out_specs=pl.BlockSpec((tq, mw), lambda b, i: (b * n + i, 0)),
        out_shape=jax.ShapeDtypeStruct((B * S, mw), BF16),
        scratch_shapes=[pltpu.VMEM((MLA_HEADS, 1, tq), F32),
                        pltpu.VMEM((MLA_HEADS, kcat_t.shape[1], tq), F32)],
        compiler_params=_cparams("parallel", "arbitrary"), name="attn_prompt",
    )(qcat, kcat, kcat_t, lw_['w_uv_rows_t'], lw_['n_b'].reshape(mw, 1))


def _attn_sample_kernel(pt_ref, q_ref, nckv_ref, nkpe_ref, ckv_hbm, kpe_hbm, wuv_ref, nb_ref, o_ref,
                        q_scr, s_scr, v_scr, ckv_buf, kpe_buf, sem, *, layer, n_pp, n_steps, sq, kvr, page):
    b = pl.program_id(0)
    nrow = MLA_HEADS * sq
    n_pages = n_steps * n_pp

    def group_copies(bb, g):
        slot = g % 2
        out = []
        for j in range(n_pp):
            pg = pt_ref[bb * n_pages + g * n_pp + j]
            out.append(pltpu.make_async_copy(ckv_hbm.at[layer, pg], ckv_buf.at[slot, j], sem.at[0, slot]))
            out.append(pltpu.make_async_copy(kpe_hbm.at[layer, pg], kpe_buf.at[slot, j], sem.at[1, slot]))
        return out

    @pl.when(b == 0)
    def _():
        for c in group_copies(b, 0):
            c.start()

    for h in range(MLA_HEADS):
        q_scr[h * sq:(h + 1) * sq, :] = q_ref[:, 2 * h * LANES:(2 * h + 2) * LANES].astype(F32)
    qa = q_scr[:, :kvr].astype(BF16)
    qp = q_scr[:, kvr:kvr + QK_ROPE].astype(BF16)

    for g in range(n_steps):
        if g + 1 < n_steps:
            for c in group_copies(b, g + 1):
                c.start()
        else:
            @pl.when(b + 1 < pl.num_programs(0))
            def _():
                for c in group_copies(b + 1, 0):
                    c.start()
        for c in group_copies(b, g):
            c.wait()
        for j in range(n_pp):
            ck = ckv_buf[g % 2, j].astype(BF16)
            v_scr[g * n_pp + j] = ck
            s_scr[g, :, j * page:(j + 1) * page] = _dot_nt(qa, ck) + _dot(qp, kpe_buf[g % 2, j].astype(BF16))

    zpad = lambda a: jnp.concatenate([a, jnp.zeros((page - sq, a.shape[1]), F32)], axis=0)
    ck_new = zpad(nckv_ref[...]).astype(BF16)
    kp_new = zpad(nkpe_ref[...]).astype(BF16)
    tok = lax.broadcasted_iota(jnp.int32, (nrow, page), 0) & (sq - 1)
    key = lax.broadcasted_iota(jnp.int32, (nrow, page), 1)
    s_new = jnp.where(key <= tok, _dot_nt(qa, ck_new) + _dot_nt(qp, kp_new), NEG_BIG)
    tile = lambda i: s_scr[i // n_pp, :, (i % n_pp) * page:(i % n_pp + 1) * page]
    m = jnp.max(_tree(jnp.maximum, [tile(i) for i in range(n_pages)] + [s_new]), axis=-1, keepdims=True)
    p_new = jnp.exp2(s_new - m)
    acc = _dot(p_new.astype(BF16), ck_new)
    psum = p_new
    for i in range(n_pages):
        p = jnp.exp2(tile(i) - m)
        psum = psum + p
        acc = acc + _dot(p.astype(BF16), v_scr[i])
    l = jnp.sum(psum, axis=-1, keepdims=True)
    o_lat = (acc / l).astype(BF16)
    mw = wuv_ref.shape[1]
    r = _dot(o_lat, wuv_ref[...])
    rh = lax.broadcasted_iota(jnp.int32, (nrow, mw), 0) >> int(math.log2(sq))
    ch = lax.broadcasted_iota(jnp.int32, (nrow, mw), 1) >> int(math.log2(V_HEAD))
    r = jnp.where(rh == ch, r, 0.0)
    out = r[0:sq, :]
    for h in range(1, MLA_HEADS):
        out = out + r[h * sq:(h + 1) * sq, :]
    o_ref[...] = _rms(out, nb_ref[...])


def _attn_sample_call(page_table, qs, nckv, nkpe, cache_ckv, cache_kpe_t, layer, lw_, kvr):
    B, sq, qw = qs.shape
    n_pages = page_table.shape[1]
    page = cache_ckv.shape[2]
    n_pp = math.gcd(PAGES_PER_STEP, n_pages)
    n_steps = n_pages // n_pp
    mw = lw_['w_uv_cat'].shape[1]
    nrow = MLA_HEADS * sq
    assert sq == SUBLANES and sq <= page and (n_steps % 2 == 0 or n_steps == 1)
    hbm = pl.BlockSpec(memory_space=pl.ANY)
    in_specs = [pl.BlockSpec((None, sq, qw), lambda b, pt: (b, 0, 0)),
                pl.BlockSpec((None, sq, kvr), lambda b, pt: (b, 0, 0)),
                pl.BlockSpec((None, sq, QK_ROPE), lambda b, pt: (b, 0, 0)),
                hbm, hbm,
                pl.BlockSpec(lw_['w_uv_cat'].shape, lambda b, pt: (0, 0)),
                pl.BlockSpec((1, mw), lambda b, pt: (0, 0))]
    grid_spec = pltpu.PrefetchScalarGridSpec(
        num_scalar_prefetch=1, grid=(B,), in_specs=in_specs,
        out_specs=pl.BlockSpec((None, sq, mw), lambda b, pt: (b, 0, 0)),
        scratch_shapes=[pltpu.VMEM((nrow, 2 * LANES), F32), pltpu.VMEM((n_steps, nrow, n_pp * page), F32),
                        pltpu.VMEM((n_pages, page, kvr), BF16),
                        pltpu.VMEM((2, n_pp, page, kvr), F32), pltpu.VMEM((2, n_pp, QK_ROPE, page), F32),
                        pltpu.SemaphoreType.DMA((2, 2))])
    return pl.pallas_call(
        functools.partial(_attn_sample_kernel, layer=layer, n_pp=n_pp, n_steps=n_steps, sq=sq, kvr=kvr, page=page),
        grid_spec=grid_spec, out_shape=jax.ShapeDtypeStruct((B, sq, mw), F32),
        compiler_params=_cparams("arbitrary"), name="attn_sample",
    )(page_table.reshape(-1), qs, nckv, nkpe, cache_ckv, cache_kpe_t, lw_['w_uv_cat'], lw_['n_b'])


def _post_kernel(a_ref, b_ref, c_ref, x_ref, wo_ref, g2_ref, wr_ref, wrl_ref, br_ref, x1_ref, hn_ref, rt_ref, cnt_ref,
                 cnt_scr, *, widths):
    wa, wb, wc = widths

    @pl.when(pl.program_id(0) == 0)
    def _():
        cnt_scr[...] = jnp.zeros(cnt_scr.shape, F32)

    mix = _dot(a_ref[...], wo_ref[:wa, :])
    mix = mix + _dot(b_ref[...], wo_ref[wa:wa + wb, :])
    mix = mix + _dot(c_ref[...], wo_ref[wa + wb:, :])
    x1 = x_ref[...] + mix
    x1_ref[...] = x1
    hn = _rms(x1, g2_ref[...])
    hn_ref[...] = hn
    hn_hi = hn.astype(BF16)
    hn_lo = (hn - hn_hi.astype(F32)).astype(BF16)
    logits = (_dot(hn_hi, wr_ref[...]) + _dot(hn_hi, wrl_ref[...]) + _dot(hn_lo, wr_ref[...])) + br_ref[...]

    lane = lax.broadcasted_iota(jnp.int32, logits.shape, 1)
    lane_f = lane.astype(F32)
    big = float(LANES)
    is_g = lane < N_GROUPS
    gl = jnp.where(is_g, logits, NEG_BIG)
    gmax = jnp.max(gl, axis=-1, keepdims=True)
    gsel = jnp.min(jnp.where(gl == gmax, lane_f, big), axis=-1, keepdims=True)
    gprob = 1.0 / jnp.sum(jnp.where(is_g, jnp.exp(gl - gmax), 0.0), axis=-1, keepdims=True)
    lo = N_GROUPS + gsel * EXPERTS_PER_GROUP
    el = jnp.where((lane_f >= lo) & (lane_f < lo + EXPERTS_PER_GROUP), logits, NEG_BIG)
    v1 = jnp.max(el, axis=-1, keepdims=True)
    i1 = jnp.min(jnp.where(el == v1, lane_f, big), axis=-1, keepdims=True)
    el2 = jnp.where(lane_f == i1, NEG_BIG, el)
    v2 = jnp.max(el2, axis=-1, keepdims=True)
    i2 = jnp.min(jnp.where(el2 == v2, lane_f, big), axis=-1, keepdims=True)
    e21 = jnp.exp(v2 - v1)
    w1 = gprob / (1.0 + e21)
    w2 = w1 * e21
    e1 = i1 - N_GROUPS
    e2 = i2 - N_GROUPS

    oh1 = (lane_f == e1).astype(F32)
    oh2 = (lane_f == e2).astype(F32)
    both = oh1 + oh2
    tm = both.shape[0]
    tri = (lax.broadcasted_iota(jnp.int32, (tm, tm), 0) >= lax.broadcasted_iota(jnp.int32, (tm, tm), 1))
    incl = _dot(tri.astype(BF16), both.astype(BF16))
    base = cnt_scr[0:1, :] + incl - both
    r1 = jnp.sum(oh1 * base, axis=-1, keepdims=True)
    r2 = jnp.sum(oh2 * base, axis=-1, keepdims=True)
    cnt_scr[...] = cnt_scr[...] + incl[tm - 1:tm, :]
    cnt_ref[...] = cnt_scr[...]

    rt = jnp.where(lane == 0, e1, 0.0)
    rt = jnp.where(lane == 1, e2, rt)
    rt = jnp.where(lane == 2, w1, rt)
    rt = jnp.where(lane == 3, w2, rt)
    rt = jnp.where(lane == 4, r1, rt)
    rt = jnp.where(lane == 5, r2, rt)
    rt_ref[...] = rt


def _post_call(a, b, c, x, lw_):
    T, D = x.shape
    tm = TOKEN_TILE
    row = lambda i: (i, 0)
    full = lambda i: (0, 0)
    widths = (a.shape[1], b.shape[1], c.shape[1])
    ws = [lw_['w_out'], lw_['g2'], lw_['w_r'], lw_['w_r_lo'], lw_['b_r']]
    out_shape = (jax.ShapeDtypeStruct((T, D), F32), jax.ShapeDtypeStruct((T, D), F32),
                 jax.ShapeDtypeStruct((T, LANES), F32), jax.ShapeDtypeStruct((SUBLANES, LANES), F32))
    return pl.pallas_call(
        functools.partial(_post_kernel, widths=widths),
        grid=(T // tm,),
        in_specs=[pl.BlockSpec((tm, w), row) for w in widths] + [pl.BlockSpec((tm, D), row)]
        + [pl.BlockSpec(w.shape, full) for w in ws],
        out_specs=tuple(pl.BlockSpec((tm, s.shape[1]), row) for s in out_shape[:3])
        + (pl.BlockSpec((SUBLANES, LANES), full),),
        out_shape=out_shape, scratch_shapes=[pltpu.VMEM((SUBLANES, LANES), F32)],
        compiler_params=_cparams("arbitrary"), name="post_proj",
    )(a, b, c, x, *ws)


def _dest_kernel(rt_ref, ps_ref, d_ref):
    rt = rt_ref[...]
    lane = lax.broadcasted_iota(jnp.int32, rt.shape, 1)
    lane_f = lane.astype(F32)
    pick = lambda k: jnp.sum(jnp.where(lane == k, rt, 0.0), axis=-1, keepdims=True)
    ps = ps_ref[...]
    d1 = jnp.sum(jnp.where(lane_f == pick(0), ps, 0.0), axis=-1, keepdims=True) + pick(2 * TOP_K)
    d2 = jnp.sum(jnp.where(lane_f == pick(1), ps, 0.0), axis=-1, keepdims=True) + pick(2 * TOP_K + 1)
    d_ref[...] = jnp.where(lane == 0, d1, jnp.where(lane == 1, d2, 0.0))


def _dest_call(route, pstart_row):
    T = route.shape[0]
    tm = TOKEN_TILE
    return pl.pallas_call(
        _dest_kernel, grid=(T // tm,),
        in_specs=[pl.BlockSpec((tm, LANES), lambda i: (i, 0)), pl.BlockSpec((1, LANES), lambda i: (0, 0))],
        out_specs=pl.BlockSpec((tm, LANES), lambda i: (i, 0)),
        out_shape=jax.ShapeDtypeStruct((T, LANES), F32), compiler_params=_cparams("parallel"), name="moe_dest",
    )(route, pstart_row)


def _moe_kernel(te_ref, nt_ref, xs_ref, w_ref, wg_ref, wu_ref, wd_ref, *rest):
    y_ref = rest[-1]
    i = pl.program_id(0)

    @pl.when(i < nt_ref[0])
    def _():
        x = xs_ref[...].astype(BF16)
        hg = _dot(x, wg_ref[...].astype(BF16))
        hu = _dot(x, wu_ref[...].astype(BF16))
        act = hg * jax.nn.sigmoid(hg) * hu * w_ref[...]
        y_ref[...] = _dot(act.astype(BF16), wd_ref[...].astype(BF16))

    @pl.when(i >= nt_ref[0])
    def _():
        y_ref[...] = jnp.zeros(y_ref.shape, F32)


def _moe_call(xs, row_w, tile_e, n_used, w_gate, w_up, w_down, layer, ys_prev, tile_off, n_rows):
    D = xs.shape[1]
    tm = MOE_TILE
    F = w_gate.shape[-1]
    in_specs = [pl.BlockSpec((tm, D), lambda i, te, nt: (i, 0)),
                pl.BlockSpec((tm, 1), lambda i, te, nt: (i, 0)),
                pl.BlockSpec((None, None, D, F), lambda i, te, nt: (layer, te[i], 0, 0)),
                pl.BlockSpec((None, None, D, F), lambda i, te, nt: (layer, te[i], 0, 0)),
                pl.BlockSpec((None, None, F, D), lambda i, te, nt: (layer, te[i], 0, 0))]
    args = [tile_e, n_used, xs, row_w, w_gate, w_up, w_down]
    aliases = {}
    if ys_prev is not None:
        in_specs.append(pl.BlockSpec(memory_space=pl.ANY))
        aliases = {len(args): 0}
        args.append(ys_prev)
    grid_spec = pltpu.PrefetchScalarGridSpec(
        num_scalar_prefetch=2, grid=(tile_e.shape[0],), in_specs=in_specs,
        out_specs=pl.BlockSpec((tm, D), lambda i, te, nt: (tile_off + i, 0)))
    return pl.pallas_call(
        _moe_kernel, grid_spec=grid_spec, out_shape=jax.ShapeDtypeStruct((n_rows, D), F32),
        input_output_aliases=aliases, compiler_params=_cparams("arbitrary"), name="moe_experts",
    )(*args)


def _moe_dispatch(hn, row_tok, row_w, tile_e, n_used, w_gate, w_up, w_down, layer):
    n_tiles = tile_e.shape[0]
    per = n_tiles // MOE_CHUNKS
    rows = per * MOE_TILE
    ys = jnp.zeros((n_tiles * MOE_TILE, hn.shape[1]), F32)
    for c in range(MOE_CHUNKS):
        xs = hn[row_tok[c * rows:(c + 1) * rows]]
        used = jnp.clip(n_used - c * per, 0, per)
        ys = _moe_call(xs, row_w[c * rows:(c + 1) * rows], tile_e[c * per:(c + 1) * per], used,
                       w_gate, w_up, w_down, layer, ys, c * per, n_tiles * MOE_TILE)
    return ys


def _route_meta(route, counts_f, tm, n_tiles):
    T = route.shape[0]
    n_assign = TOP_K * T
    e = route[:, :TOP_K].astype(jnp.int32).reshape(-1)
    w = route[:, TOP_K:2 * TOP_K].reshape(-1)
    order = jnp.argsort(e, stable=True).astype(jnp.int32)
    counts = counts_f.astype(jnp.int32)
    zero = jnp.zeros((1,), jnp.int32)
    start = jnp.concatenate([zero, jnp.cumsum(counts)])
    pstart = jnp.concatenate([zero, jnp.cumsum(((counts + tm - 1) // tm) * tm)])
    n_used = (pstart[N_EXPERTS] // tm).reshape(1)
    tile_lo = jnp.arange(n_tiles, dtype=jnp.int32) * tm
    tile_e = jnp.sum((pstart[None, 1:] <= tile_lo[:, None]).astype(jnp.int32), axis=1)
    tile_e = jnp.minimum(tile_e, N_EXPERTS - 1)
    oh = (tile_e[:, None] == jnp.arange(N_EXPERTS, dtype=jnp.int32)[None, :]).astype(jnp.int32)
    pick = lambda tab: jnp.sum(oh * tab[None, :N_EXPERTS], axis=1)
    k = (tile_lo - pick(pstart))[:, None] + jnp.arange(tm, dtype=jnp.int32)[None, :]
    valid = (k < pick(counts)[:, None]).reshape(-1)
    idx = jnp.clip(pick(start)[:, None] + k, 0, n_assign - 1).reshape(-1)
    src = order[idx]
    row_tok = jnp.where(valid, src // TOP_K, 0)
    row_w = jnp.where(valid, w[src], 0.0)
    ps_row = jnp.pad(pstart[:N_EXPERTS].astype(F32), (0, LANES - N_EXPERTS)).reshape(1, LANES)
    pos = _dest_call(route, ps_row)[:, :TOP_K].astype(jnp.int32)
    return row_tok, row_w.reshape(-1, 1), pos, tile_e, n_used


def _final_kernel(x_ref, g_ref, o_ref):
    o_ref[...] = _rms(x_ref[...], g_ref[...])


def _final_call(x, g):
    T, D = x.shape
    tm = TOKEN_TILE
    return pl.pallas_call(
        _final_kernel, grid=(T // tm,),
        in_specs=[pl.BlockSpec((tm, D), lambda i: (i, 0)), pl.BlockSpec((1, D), lambda i: (0, 0))],
        out_specs=pl.BlockSpec((tm, D), lambda i: (i, 0)),
        out_shape=jax.ShapeDtypeStruct((T, D), F32), compiler_params=_cparams("parallel"), name="final_norm",
    )(x, g)


def _block_diag(w):
    n, c, d = w.shape
    return jnp.einsum('ncd,nm->ncmd', w, jnp.eye(n, dtype=w.dtype)).reshape(n * c, n * d)


def _half_swap(w):
    half = w.shape[-1] // 2
    return jnp.concatenate([w[..., half:], w[..., :half]], axis=-1)


def _layer_weights(l, p, dims):
    lw, qr, kvr, gk_w, gv_w = dims
    D = p['w_in'].shape[1]
    row = lambda a: a.reshape(1, -1).astype(F32)
    w_in = p['w_in'][l]
    sizes = (lw, lw, qr, kvr, QK_ROPE, gk_w, gk_w, gv_w, p['gla_wa2'].shape[1], gv_w)
    x_lru, y_lru, c_q, c_kv, k_pe, g_q, g_k, g_v, g_a, g_o = jnp.split(w_in, list(np.cumsum(sizes)[:-1]), axis=1)
    tail_pad = LANES - 2 * QK_ROPE - g_a.shape[1]
    tail = jnp.concatenate([k_pe, _half_swap(k_pe), g_a, jnp.zeros((D, tail_pad), F32)], axis=1)
    w_in_p = jnp.concatenate([x_lru, y_lru, c_q, c_kv, g_q, g_k, g_v, g_o, tail], axis=1).astype(BF16)

    w_uq = p['w_uq'][l].reshape(qr, MLA_HEADS, QK_NOPE + QK_ROPE)
    nope = w_uq[:, :, :QK_NOPE].reshape(qr, MLA_HEADS * QK_NOPE)
    pe = w_uq[:, :, QK_NOPE:]
    widen = lambda a: jnp.pad(a, ((0, 0), (0, 0), (0, LANES - QK_ROPE))).reshape(qr, MLA_HEADS * LANES)
    w_q = jnp.concatenate([nope, widen(pe), widen(_half_swap(pe))], axis=1).astype(BF16)
    w_abs = _block_diag(jnp.transpose(p['w_uk'][l], (1, 2, 0))).astype(BF16)
    wa2 = jnp.zeros((LANES, gk_w), F32).at[2 * QK_ROPE:2 * QK_ROPE + g_a.shape[1]].set(p['gla_wa2'][l]).astype(BF16)

    w_uv = p['w_uv'][l]
    mw = MLA_HEADS * V_HEAD
    w_uv_cat = w_uv.reshape(kvr, mw)
    w_uv_rows = _block_diag(jnp.transpose(w_uv, (1, 0, 2)))
    n_a, n_b, n_c = jnp.split(p['out_norm_g'][l], [lw, lw + mw])
    n_r = N_GROUPS + N_EXPERTS
    w_r = jnp.concatenate([p['router_wg'][l], p['router_we'][l], jnp.zeros((D, LANES - n_r), F32)], axis=1)
    b_r = jnp.concatenate([p['router_bg'][l], p['router_be'][l], jnp.zeros((LANES - n_r,), F32)])
    return dict(
        g1=row(p['norm1_g'][l]), w_in=w_in_p, q_g=row(p['q_norm_g'][l]), w_q=w_q, w_abs=w_abs,
        kv_g=row(p['kv_norm_g'][l]), wa2=wa2, ba=row(p['gla_ba'][l]),
        conv_w=p['conv_w'][l], conv_b=row(p['conv_b'][l]),
        lru_wa=_block_diag(p['lru_wa'][l]).astype(BF16), lru_ba=row(p['lru_ba'][l]),
        lru_wi=_block_diag(p['lru_wi'][l]).astype(BF16), lru_bi=row(p['lru_bi'][l]),
        lru_lam=row(p['lru_lambda'][l]), n_a=row(n_a), n_b=row(n_b), n_c=row(n_c),
        gla_g=row(jnp.tile(p['gla_norm_g'][l], GLA_HEADS)),
        w_uv_cat=w_uv_cat.astype(BF16), w_uv_rows_t=w_uv_rows.T.astype(BF16),
        w_out=p['w_out'][l].astype(BF16), g2=row(p['norm2_g'][l]), w_r=w_r.astype(BF16),
        w_r_lo=(w_r - w_r.astype(BF16).astype(F32)).astype(BF16), b_r=row(b_r),
    )


def _rope_tables(positions):
    half = QK_ROPE // 2
    inv = ROPE_THETA ** (-np.arange(half, dtype=np.float64) / half)
    ang = np.asarray(positions, np.float64)[:, None] * inv
    zeros = np.zeros((ang.shape[0], LANES - QK_ROPE))
    cos = np.concatenate([np.cos(ang), np.cos(ang), zeros], axis=1)
    sin = np.concatenate([-np.sin(ang), np.sin(ang), zeros], axis=1)
    return jnp.asarray(cos, F32), jnp.asarray(sin, F32)


def _gla_state_to_rows(s):
    B = s.shape[0]
    eye = jnp.eye(GLA_HEADS, dtype=s.dtype)
    return jnp.einsum('bhde,hg->bhegd', s, eye).reshape(B, GLA_HEADS * GLA_DV, GLA_HEADS * GLA_DK)


def _gla_rows_to_state(st):
    B = st.shape[0]
    s5 = st.reshape(B, GLA_HEADS, GLA_DV, GLA_HEADS, GLA_DK)
    blocks = jnp.stack([s5[:, h, :, h, :] for h in range(GLA_HEADS)], axis=1)
    return jnp.swapaxes(blocks, -1, -2)


def kernel(x_prompt, x_sample, cache_ckv, cache_kpe, page_table, state_conv, state_lru, state_gla, norm1_g, w_in, conv_w, conv_b, lru_wa, lru_ba, lru_wi, lru_bi, lru_lambda, q_norm_g, w_uq, kv_norm_g, w_uk, w_uv, gla_wa2, gla_ba, gla_norm_g, out_norm_g, w_out, norm2_g, router_wg, router_bg, router_we, router_be, w_gate, w_up, w_down, final_norm_g):
    p = dict(norm1_g=norm1_g, w_in=w_in, conv_w=conv_w, conv_b=conv_b, lru_wa=lru_wa, lru_ba=lru_ba,
             lru_wi=lru_wi, lru_bi=lru_bi, lru_lambda=lru_lambda, q_norm_g=q_norm_g, w_uq=w_uq,
             kv_norm_g=kv_norm_g, w_uk=w_uk, w_uv=w_uv, gla_wa2=gla_wa2, gla_ba=gla_ba,
             gla_norm_g=gla_norm_g, out_norm_g=out_norm_g, w_out=w_out, norm2_g=norm2_g,
             router_wg=router_wg, router_bg=router_bg, router_we=router_we, router_be=router_be)
    Bp, Sp, D = x_prompt.shape
    Bs, Ss, _ = x_sample.shape
    depth = w_in.shape[0]
    lw = state_lru.shape[-1]
    qr = q_norm_g.shape[-1]
    kvr = kv_norm_g.shape[-1]
    gk_w = GLA_HEADS * GLA_DK
    gv_w = GLA_HEADS * GLA_DV
    dims = (lw, qr, kvr, gk_w, gv_w)
    assert kvr == LANES and gk_w == LANES
    Tp, Ts = Bp * Sp, Bs * Ss
    T = Tp + Ts
    past = page_table.shape[1] * cache_ckv.shape[2]

    cos_t, sin_t = _rope_tables(np.concatenate([np.tile(np.arange(Sp), Bp), np.tile(past + np.arange(Ss), Bs)]))
    cache_kpe_t = jnp.swapaxes(cache_kpe, 2, 3)
    n_tiles = -(-((TOP_K * T + N_EXPERTS * (MOE_TILE - 1)) // MOE_TILE + 1) // MOE_CHUNKS) * MOE_CHUNKS

    x = jnp.concatenate([x_prompt.reshape(Tp, D), x_sample.reshape(Ts, D)], axis=0)
    outs = {k: [] for k in ('ckv_p', 'kpe_p', 'ckv_s', 'kpe_s', 'conv_p', 'conv_s', 'lru_p', 'lru_s', 'gla_p', 'gla_s')}
    for l in range(depth):
        lw_ = _layer_weights(l, p, dims)
        zl, qcat, kcat, kcat_t, ckv_n, kpe_n, g = _pre_call(x, lw_, cos_t, sin_t, dims)

        a_p, conv_p, lru_p = _lru_call(zl, 0, Bp, Sp, jnp.zeros((Bp, CONV_WIDTH - 1, lw), F32),
                                       jnp.zeros((Bp, lw), F32), lw_, BF16)
        a_s, lru_s = _lru_seg_call(zl, Tp, Bs, Ss, state_conv[l], state_lru[l], lw_)
        conv_s = zl[Tp:, :lw].reshape(Bs, Ss, lw)[:, Ss - (CONV_WIDTH - 1):]

        c_p, gla_p = _gla_call(g, 0, Bp, Sp, jnp.zeros((Bp, gv_w, gk_w), F32), lw_, BF16, gk_w, gv_w)
        c_s, gla_s = _gla_call(g, Tp, Bs, Ss, _gla_state_to_rows(state_gla[l]), lw_, F32, gk_w, gv_w)

        b_p = _attn_prompt_call(qcat, kcat, kcat_t, Bp, Sp, lw_, kvr)
        b_s = _attn_sample_call(page_table, qcat[Tp:].reshape(Bs, Ss, -1), ckv_n[Tp:].reshape(Bs, Ss, kvr),
                                kpe_n[Tp:].reshape(Bs, Ss, QK_ROPE), cache_ckv, cache_kpe_t, l, lw_, kvr)

        a = jnp.concatenate([a_p, a_s], axis=0)
        b = jnp.concatenate([b_p, b_s.reshape(Ts, -1).astype(BF16)], axis=0)
        c = jnp.concatenate([c_p, c_s.astype(BF16)], axis=0)
        x1, hn, route, cnt = _post_call(a, b, c, x, lw_)

        row_tok, row_w, pos, tile_e, n_used = _route_meta(route, cnt[0, :N_EXPERTS], MOE_TILE, n_tiles)
        ys = _moe_dispatch(hn, row_tok, row_w, tile_e, n_used, w_gate, w_up, w_down, l)
        x = x1 + ys[pos[:, 0]] + ys[pos[:, 1]]

        outs['ckv_p'].append(ckv_n[:Tp].reshape(Bp, Sp, kvr))
        outs['kpe_p'].append(kpe_n[:Tp].reshape(Bp, Sp, QK_ROPE))
        outs['ckv_s'].append(ckv_n[Tp:].reshape(Bs, Ss, kvr))
        outs['kpe_s'].append(kpe_n[Tp:].reshape(Bs, Ss, QK_ROPE))
        outs['conv_p'].append(conv_p)
        outs['conv_s'].append(conv_s)
        outs['lru_p'].append(lru_p.reshape(Bp, lw))
        outs['lru_s'].append(lru_s.reshape(Bs, lw))
        outs['gla_p'].append(_gla_rows_to_state(gla_p))
        outs['gla_s'].append(_gla_rows_to_state(gla_s))

    y = _final_call(x, final_norm_g.reshape(1, D))
    st = {k: jnp.stack(v) for k, v in outs.items()}
    return (y[:Tp].reshape(Bp, Sp, D), y[Tp:].reshape(Bs, Ss, D),
            st['ckv_p'], st['kpe_p'], st['ckv_s'], st['kpe_s'], st['conv_p'], st['conv_s'],
            st['lru_p'], st['lru_s'], st['gla_p'], st['gla_s'])
```

```python
import functools
import math

import numpy as np
import jax
import jax.numpy as jnp
from jax import lax
from jax.experimental import pallas as pl
from jax.experimental.pallas import tpu as pltpu

F32 = jnp.float32
BF16 = jnp.bfloat16

LRU_BLOCKS = 4
CONV_WIDTH = 4
LRU_C = 8.0
MLA_HEADS = 8
QK_NOPE = 64
QK_ROPE = 32
V_HEAD = 64
ROPE_THETA = 10000.0
GLA_HEADS = 4
GLA_DK = 32
GLA_DV = 64
GLA_TAU = 16.0
GLA_CHUNK = 16
N_GROUPS = 4
EXPERTS_PER_GROUP = 8
N_EXPERTS = N_GROUPS * EXPERTS_PER_GROUP
TOP_K = 2
EPS = 1e-6

LANES = 128
SUBLANES = 8
BF16_ROWS = 16
VMEM_LIMIT_BYTES = 56 * 1024 * 1024

TOKEN_TILE = 512
SEQ_TILE = 256
MOE_TILE = 256
MOE_CHUNKS = 4
SCORE_PAGES = 4
VALUE_PAGES = 2
PAGES_PER_STEP = 16
NEG_BIG = -1e30


def _cparams(*sem):
    return pltpu.CompilerParams(dimension_semantics=sem, vmem_limit_bytes=VMEM_LIMIT_BYTES)


def _rms(x, g):
    return x * lax.rsqrt(jnp.mean(x * x, axis=-1, keepdims=True) + EPS) * g


def _dot(a, b):
    return jnp.dot(a, b, preferred_element_type=F32)


def _dot_nt(a, b):
    return lax.dot_general(a, b, (((1,), (1,)), ((), ())), preferred_element_type=F32)


def _softplus(x):
    return jnp.maximum(x, 0.0) + jnp.log1p(jnp.exp(-jnp.abs(x)))


def _shift_rows(x, d, fill):
    row = lax.broadcasted_iota(jnp.int32, x.shape, 0)
    return jnp.where(row >= d, pltpu.roll(x, d, 0), fill)


def _pre_kernel(x_ref, g1_ref, win_ref, qg_ref, wq_ref, wabs_ref, kvg_ref, wa2_ref, ba_ref,
                cos_ref, sin_ref, zl_ref, q_ref, k_ref, kt_ref, ckv_ref, kpe_ref, g_ref, *, dims):
    lw, qr, kvr, gk_w, gv_w = dims
    xn = _rms(x_ref[...], g1_ref[...])
    z = _dot(xn.astype(BF16), win_ref[...])
    o = 2 * lw
    zl_ref[...] = z[:, :o]
    cq = z[:, o:o + qr]
    o += qr
    ckv = z[:, o:o + kvr]
    o += kvr
    gqk = z[:, o:o + 2 * gk_w]
    o += 2 * gk_w
    gvo = z[:, o:o + 2 * gv_w]
    o += 2 * gv_w
    tail = z[:, o:o + LANES]
    cos = cos_ref[...]
    sin = sin_ref[...]
    scale = (QK_NOPE + QK_ROPE) ** -0.5 * math.log2(math.e)

    ckv_n = _rms(ckv, kvg_ref[...])
    kpe = tail * cos + pltpu.roll(tail, LANES - QK_ROPE, 1) * sin
    ckv_ref[...] = ckv_n
    kpe_ref[...] = kpe[:, :QK_ROPE]
    k_ref[:, :kvr] = ckv_n.astype(BF16)
    k_ref[:, kvr:] = kpe.astype(BF16)
    ones = jnp.ones((kt_ref.shape[0] - kvr, ckv_n.shape[0]), F32)
    kt_ref[...] = jnp.concatenate([ckv_n.T, ones], axis=0).astype(BF16)

    cqn = _rms(cq, qg_ref[...]).astype(BF16)
    qall = _dot(cqn, wq_ref[...])
    n_nope = MLA_HEADS * QK_NOPE
    q_abs = _dot(qall[:, :n_nope].astype(BF16), wabs_ref[...]) * scale
    for h in range(MLA_HEADS):
        pe = qall[:, n_nope + h * LANES:n_nope + (h + 1) * LANES]
        sw = qall[:, n_nope + (MLA_HEADS + h) * LANES:n_nope + (MLA_HEADS + h + 1) * LANES]
        q_ref[:, 2 * h * LANES:(2 * h + 1) * LANES] = q_abs[:, h * kvr:(h + 1) * kvr].astype(BF16)
        q_ref[:, (2 * h + 1) * LANES:(2 * h + 2) * LANES] = ((pe * cos + sw * sin) * scale).astype(BF16)

    la_pre = _dot(tail.astype(BF16), wa2_ref[...]) + ba_ref[...]
    la = -_softplus(-la_pre) * (1.0 / GLA_TAU)
    g_ref[:, :gk_w] = gqk[:, :gk_w] * (GLA_DK ** -0.5)
    g_ref[:, gk_w:2 * gk_w] = gqk[:, gk_w:]
    g_ref[:, 2 * gk_w:3 * gk_w] = la
    g_ref[:, 3 * gk_w:] = gvo


def _pre_call(x, lw_, cos_t, sin_t, dims):
    T, D = x.shape
    lw, qr, kvr, gk_w, gv_w = dims
    tm = TOKEN_TILE
    assert T % tm == 0
    row = lambda i: (i, 0)
    full = lambda i: (0, 0)
    wspec = lambda a: pl.BlockSpec(a.shape, full)
    ins = [x, lw_['g1'], lw_['w_in'], lw_['q_g'], lw_['w_q'], lw_['w_abs'], lw_['kv_g'], lw_['wa2'], lw_['ba']]
    in_specs = [pl.BlockSpec((tm, D), row)] + [wspec(a) for a in ins[1:]]
    in_specs += [pl.BlockSpec((tm, LANES), row), pl.BlockSpec((tm, LANES), row)]
    out_shape = (
        jax.ShapeDtypeStruct((T, 2 * lw), F32),
        jax.ShapeDtypeStruct((T, 2 * LANES * MLA_HEADS), BF16),
        jax.ShapeDtypeStruct((T, 2 * LANES), BF16),
        jax.ShapeDtypeStruct((T // tm, kvr + BF16_ROWS, tm), BF16),
        jax.ShapeDtypeStruct((T, kvr), F32),
        jax.ShapeDtypeStruct((T, QK_ROPE), F32),
        jax.ShapeDtypeStruct((T, 3 * gk_w + 2 * gv_w), F32),
    )
    out_specs = tuple(pl.BlockSpec((None, s.shape[1], tm), lambda i: (i, 0, 0)) if len(s.shape) == 3
                      else pl.BlockSpec((tm, s.shape[1]), row) for s in out_shape)
    return pl.pallas_call(
        functools.partial(_pre_kernel, dims=dims),
        grid=(T // tm,), in_specs=in_specs, out_specs=out_specs, out_shape=out_shape,
        compiler_params=_cparams("parallel"), name="pre_proj",
    )(*ins, cos_t, sin_t)


def _lru_kernel(zl_ref, cbuf_ref, h0_ref, cw_ref, cb_ref, wa_ref, ba_ref, wi_ref, bi_ref, lam_ref, na_ref,
                a_ref, conv_ref, hout_ref, xbuf, hcar, *, ts, lw):
    i = pl.program_id(1)
    last = pl.num_programs(1) - 1
    pad = SUBLANES
    nbuf = CONV_WIDTH - 1

    @pl.when(i == 0)
    def _():
        xbuf[0:pad, :] = jnp.zeros((pad, lw), F32)
        xbuf[pad - nbuf:pad, :] = cbuf_ref[...]
        hcar[...] = h0_ref[...]

    x = zl_ref[:, :lw]
    y = zl_ref[:, lw:]
    xbuf[pad:pad + ts, :] = x
    xc = cb_ref[...] + cw_ref[nbuf:nbuf + 1, :] * x
    for k in range(nbuf):
        xc = xc + cw_ref[k:k + 1, :] * xbuf[pad - nbuf + k:pad - nbuf + k + ts, :]

    @pl.when(i == last)
    def _():
        conv_ref[...] = xbuf[pad + ts - nbuf:pad + ts, :]

    xbuf[0:pad, :] = xbuf[ts:ts + pad, :]

    xb = xc.astype(BF16)
    r = jax.nn.sigmoid(_dot(xb, wa_ref[...]) + ba_ref[...])
    gi = jax.nn.sigmoid(_dot(xb, wi_ref[...]) + bi_ref[...])
    log_a = (-LRU_C) * r * _softplus(-lam_ref[...])
    a = jnp.exp(log_a)
    th = jnp.tanh(log_a)
    u = jnp.sqrt(-2.0 * th / (1.0 - th)) * (gi * xc)

    d = 1
    while d < ts:
        u = a * _shift_rows(u, d, 0.0) + u
        a = a * _shift_rows(a, d, 1.0)
        d *= 2
    h = a * hcar[...] + u
    hcar[...] = h[ts - 1:ts, :]

    @pl.when(i == last)
    def _():
        hout_ref[...] = h[ts - 1:ts, :]

    out_a = h * jax.nn.gelu(y)
    a_ref[...] = _rms(out_a, na_ref[...]).astype(a_ref.dtype)


def _lru_call(zl, row_off, B, S, cbuf, h0, lw_, out_dtype):
    lw = h0.shape[-1]
    ts = min(S, SEQ_TILE)
    assert S % ts == 0 and row_off % ts == 0 and S >= CONV_WIDTH - 1
    n = S // ts
    off = row_off // ts
    full = lambda b, i: (0, 0)
    ws = [lw_['conv_w'], lw_['conv_b'], lw_['lru_wa'], lw_['lru_ba'], lw_['lru_wi'], lw_['lru_bi'],
          lw_['lru_lam'], lw_['n_a']]
    in_specs = [
        pl.BlockSpec((ts, 2 * lw), lambda b, i: (off + b * n + i, 0)),
        pl.BlockSpec((None, CONV_WIDTH - 1, lw), lambda b, i: (b, 0, 0)),
        pl.BlockSpec((None, 1, lw), lambda b, i: (b, 0, 0)),
    ] + [pl.BlockSpec(a.shape, full) for a in ws]
    out_shape = (
        jax.ShapeDtypeStruct((B * S, lw), out_dtype),
        jax.ShapeDtypeStruct((B, CONV_WIDTH - 1, lw), F32),
        jax.ShapeDtypeStruct((B, 1, lw), F32),
    )
    out_specs = (
        pl.BlockSpec((ts, lw), lambda b, i: (b * n + i, 0)),
        pl.BlockSpec((None, CONV_WIDTH - 1, lw), lambda b, i: (b, 0, 0)),
        pl.BlockSpec((None, 1, lw), lambda b, i: (b, 0, 0)),
    )
    return pl.pallas_call(
        functools.partial(_lru_kernel, ts=ts, lw=lw),
        grid=(B, n), in_specs=in_specs, out_specs=out_specs, out_shape=out_shape,
        scratch_shapes=[pltpu.VMEM((ts + SUBLANES, lw), F32), pltpu.VMEM((1, lw), F32)],
        compiler_params=_cparams("parallel", "arbitrary"), name="rg_lru",
    )(zl, cbuf, h0.reshape(B, 1, lw), *ws)


def _lru_seg_kernel(zl_ref, hist_ref, h0_ref, cw_ref, cb_ref, wa_ref, ba_ref, wi_ref, bi_ref, lam_ref, na_ref,
                    a_ref, h_ref, *, seg, lw):
    nbuf = CONV_WIDTH - 1
    x = zl_ref[:, :lw]
    y = zl_ref[:, lw:]
    n = x.shape[0]
    pos = lax.broadcasted_iota(jnp.int32, (n, 1), 0) & (seg - 1)
    hist = hist_ref[...]
    xc = cb_ref[...] + cw_ref[nbuf:nbuf + 1, :] * x
    for k in range(nbuf):
        j = nbuf - k
        prev = jnp.where(pos >= j, pltpu.roll(x, j, 0), pltpu.roll(hist, (j - seg) % n, 0))
        xc = xc + cw_ref[k:k + 1, :] * prev
    xb = xc.astype(BF16)
    r = jax.nn.sigmoid(_dot(xb, wa_ref[...]) + ba_ref[...])
    gi = jax.nn.sigmoid(_dot(xb, wi_ref[...]) + bi_ref[...])
    log_a = (-LRU_C) * r * _softplus(-lam_ref[...])
    a = jnp.exp(log_a)
    th = jnp.tanh(log_a)
    u = jnp.sqrt(-2.0 * th / (1.0 - th)) * (gi * xc)
    d = 1
    while d < seg:
        u = a * jnp.where(pos >= d, pltpu.roll(u, d, 0), 0.0) + u
        a = a * jnp.where(pos >= d, pltpu.roll(a, d, 0), 1.0)
        d *= 2
    h = a * h0_ref[...] + u
    h_ref[...] = h
    a_ref[...] = _rms(h * jax.nn.gelu(y), na_ref[...]).astype(a_ref.dtype)


def _lru_seg_call(zl, row_off, B, S, cbuf, h0, lw_):
    lw = h0.shape[-1]
    n = B * S
    tm = min(n, TOKEN_TILE)
    assert S & (S - 1) == 0 and S >= CONV_WIDTH - 1 and tm % S == 0 and n % tm == 0 and row_off % tm == 0
    off = row_off // tm
    hist = jnp.pad(cbuf, ((0, 0), (S - (CONV_WIDTH - 1), 0), (0, 0))).reshape(n, lw)
    h0_rows = jnp.repeat(h0, S, axis=0)
    full = lambda i: (0, 0)
    row = lambda i: (i, 0)
    ws = [lw_['conv_w'], lw_['conv_b'], lw_['lru_wa'], lw_['lru_ba'], lw_['lru_wi'], lw_['lru_bi'],
          lw_['lru_lam'], lw_['n_a']]
    a, h = pl.pallas_call(
        functools.partial(_lru_seg_kernel, seg=S, lw=lw),
        grid=(n // tm,),
        in_specs=[pl.BlockSpec((tm, 2 * lw), lambda i: (off + i, 0)), pl.BlockSpec((tm, lw), row),
                  pl.BlockSpec((tm, lw), row)] + [pl.BlockSpec(w.shape, full) for w in ws],
        out_specs=(pl.BlockSpec((tm, lw), row), pl.BlockSpec((tm, lw), row)),
        out_shape=(jax.ShapeDtypeStruct((n, lw), BF16), jax.ShapeDtypeStruct((n, lw), F32)),
        compiler_params=_cparams("parallel"), name="rg_lru_seg",
    )(zl, hist, h0_rows, *ws)
    return a, h.reshape(B, S, lw)[:, S - 1]


def _gla_kernel(g_ref, s0_ref, gn_ref, nc_ref, c_ref, sout_ref, st, *, ts, nseq, gk_w, gv_w):
    i = pl.program_id(1)
    last = pl.num_programs(1) - 1
    C = GLA_CHUNK
    rows = min(ts, C)
    n_chunks = max(ts // C, 1)
    log2c = int(math.log2(C))

    @pl.when(i == 0)
    def _():
        st[...] = s0_ref[...]

    dk_sh, dv_sh = int(math.log2(GLA_DK)), int(math.log2(GLA_DV))
    hd = lax.broadcasted_iota(jnp.int32, (gk_w, gv_w), 0) >> dk_sh
    he = lax.broadcasted_iota(jnp.int32, (gk_w, gv_w), 1) >> dv_sh
    same = (hd == he).astype(BF16)
    he_t = lax.broadcasted_iota(jnp.int32, (gv_w, gk_w), 0) >> dv_sh
    hd_t = lax.broadcasted_iota(jnp.int32, (gv_w, gk_w), 1) >> dk_sh
    same_t = (he_t == hd_t).astype(F32)
    sel_t = lax.broadcasted_iota(jnp.int32, (C, C * C), 0)
    sel_r = lax.broadcasted_iota(jnp.int32, (C, C * C), 1) >> log2c
    sel = (sel_t == sel_r).astype(BF16)
    srow = lax.broadcasted_iota(jnp.int32, (C, 1), 0)

    seq_out = []
    for sq in range(nseq):
        blk = g_ref[sq * ts:(sq + 1) * ts, :]
        if rows < C:
            blk = jnp.concatenate([blk, jnp.zeros((C - rows, blk.shape[1]), F32)], axis=0)
        q = blk[:, :gk_w]
        k = blk[:, gk_w:2 * gk_w]
        la = blk[:, 2 * gk_w:3 * gk_w]
        v = blk[:, 3 * gk_w:3 * gk_w + gv_w]
        nrow = n_chunks * C
        pos = lax.broadcasted_iota(jnp.int32, (nrow, 1), 0) & (C - 1)
        cum = la
        d = 1
        while d < C:
            cum = cum + jnp.where(pos >= d, pltpu.roll(cum, d, 0), 0.0)
            d *= 2
        tot = jnp.where(pos == C - 1, cum, 0.0)
        d = 1
        while d < C:
            tot = tot + jnp.where(pos < C - d, pltpu.roll(tot, nrow - d, 0), 0.0)
            d *= 2
        qe = (q * jnp.exp(cum)).astype(BF16)
        kdec = (k * jnp.exp(tot - cum)).astype(BF16)
        dec = jnp.exp(tot)
        vb = v.astype(BF16)

        o_intra, upd = [], []
        for c in range(n_chunks):
            sl = slice(c * C, (c + 1) * C)
            cum_c, q_c, k_c, v_c = cum[sl], q[sl], k[sl], v[sl]
            pieces = []
            for t in range(C):
                diff = jnp.where(srow <= t, cum_c[t:t + 1, :] - cum_c, NEG_BIG)
                pieces.append(q_c[t:t + 1, :] * k_c * jnp.exp(diff))
            w = jnp.concatenate(pieces, axis=0)
            att = _dot(w.astype(BF16), same)
            xv = att * jnp.concatenate([v_c] * C, axis=0)
            o_intra.append(_dot(sel, xv.astype(BF16)))
            upd.append(lax.dot_general(vb[sl], kdec[sl], (((0,), (0,)), ((), ())),
                                       preferred_element_type=F32) * same_t)

        s_t = st[sq]
        outs = []
        for c in range(n_chunks):
            sl = slice(c * C, (c + 1) * C)
            outs.append(o_intra[c] + _dot_nt(qe[sl], s_t.astype(BF16)))
            s_t = s_t * dec[c * C:c * C + 1, :] + upd[c]
        st[sq] = s_t
        seq_out.append(jnp.concatenate(outs, axis=0)[:ts] if n_chunks > 1 else outs[0][:ts])

    o = jnp.concatenate(seq_out, axis=0) if nseq > 1 else seq_out[0]

    @pl.when(i == last)
    def _():
        sout_ref[...] = st[...]

    go = g_ref[:, 3 * gk_w + gv_w:]
    e64 = (lax.broadcasted_iota(jnp.int32, (gv_w, gv_w), 0) >> dv_sh
           == lax.broadcasted_iota(jnp.int32, (gv_w, gv_w), 1) >> dv_sh).astype(BF16)
    osq = o * o
    osq_hi = osq.astype(BF16)
    osq_lo = (osq - osq_hi.astype(F32)).astype(BF16)
    ms = (_dot(osq_hi, e64) + _dot(osq_lo, e64)) * (1.0 / GLA_DV)
    out_c = o * lax.rsqrt(ms + EPS) * gn_ref[...] * (go * jax.nn.sigmoid(go))
    c_ref[...] = _rms(out_c, nc_ref[...]).astype(c_ref.dtype)


def _gla_call(g, row_off, B, S, s0t, lw_, out_dtype, gk_w, gv_w):
    ts = min(S, SEQ_TILE)
    nseq = math.gcd(B, max(SEQ_TILE // (2 * ts), 1)) if ts == S else 1
    rows = nseq * ts
    assert S % ts == 0 and row_off % rows == 0 and (ts % GLA_CHUNK == 0 or ts < GLA_CHUNK)
    n = S // ts
    off = row_off // rows
    gw = g.shape[1]
    full = lambda b, i: (0, 0)
    out_shape = (jax.ShapeDtypeStruct((B * S, gv_w), out_dtype),
                 jax.ShapeDtypeStruct((B, gv_w, gk_w), F32))
    return pl.pallas_call(
        functools.partial(_gla_kernel, ts=ts, nseq=nseq, gk_w=gk_w, gv_w=gv_w),
        grid=(B // nseq, n),
        in_specs=[pl.BlockSpec((rows, gw), lambda b, i: (off + b * n + i, 0)),
                  pl.BlockSpec((nseq, gv_w, gk_w), lambda b, i: (b, 0, 0)),
                  pl.BlockSpec((1, gv_w), full), pl.BlockSpec((1, gv_w), full)],
        out_specs=(pl.BlockSpec((rows, gv_w), lambda b, i: (b * n + i, 0)),
                   pl.BlockSpec((nseq, gv_w, gk_w), lambda b, i: (b, 0, 0))),
        out_shape=out_shape,
        scratch_shapes=[pltpu.VMEM((nseq, gv_w, gk_w), F32)],
        compiler_params=_cparams("parallel", "arbitrary"), name="gla",
    )(g, s0t, lw_['gla_g'], lw_['n_c'])


def _tree(op, xs):
    xs = list(xs)
    while len(xs) > 1:
        xs = [op(xs[i], xs[i + 1]) if i + 1 < len(xs) else xs[i] for i in range(0, len(xs), 2)]
    return xs[0]


def _attn_prompt_kernel(q_ref, k_ref, kt_ref, wuvt_ref, nb_ref, o_ref, m_scr, acc_scr, *, tq, kvr):
    i = pl.program_id(1)
    kw = 2 * LANES
    m_scr[...] = jnp.full(m_scr.shape, NEG_BIG, F32)
    acc_scr[...] = jnp.zeros(acc_scr.shape, F32)

    def block(j, masked):
        kb = k_ref[pl.ds(pl.multiple_of(j * tq, tq), tq), :]
        kbt = kt_ref[j]
        if masked:
            key = lax.broadcasted_iota(jnp.int32, (tq, tq), 0)
            qry = lax.broadcasted_iota(jnp.int32, (tq, tq), 1)
            keep = key <= qry
        qk = lambda h: _dot_nt(kb, q_ref[:, h * kw:(h + 1) * kw])
        st_next = qk(0)
        for h in range(MLA_HEADS):
            st = st_next
            if h + 1 < MLA_HEADS:
                st_next = qk(h + 1)
            if masked:
                st = jnp.where(keep, st, NEG_BIG)
            m_old = m_scr[h]
            m_new = jnp.maximum(m_old, jnp.max(st, axis=0, keepdims=True))
            alpha = jnp.exp2(m_old - m_new)
            pt = jnp.exp2(st - m_new).astype(BF16)
            acc_scr[h] = alpha * acc_scr[h] + _dot(kbt, pt)
            m_scr[h] = m_new

    def body(j, carry):
        block(j, False)
        return carry

    lax.fori_loop(0, i, body, 0)
    block(i, True)

    out_t = jnp.zeros((wuvt_ref.shape[0], tq), F32)
    for h in range(MLA_HEADS):
        acc = acc_scr[h]
        o_t = (acc[:kvr, :] / acc[kvr:kvr + 1, :]).astype(BF16)
        out_t = out_t + _dot(wuvt_ref[:, h * kvr:(h + 1) * kvr], o_t)
    ms = jnp.mean(out_t * out_t, axis=0, keepdims=True)
    out_t = out_t * lax.rsqrt(ms + EPS) * nb_ref[...]
    o_ref[...] = out_t.T.astype(o_ref.dtype)


def _attn_prompt_call(qcat, kcat, kcat_t, B, S, lw_, kvr):
    tq = kcat_t.shape[2]
    assert S % tq == 0
    n = S // tq
    mw = lw_['w_uv_rows_t'].shape[0]
    full = lambda b, i: (0, 0)
    return pl.pallas_call(
        functools.partial(_attn_prompt_kernel, tq=tq, kvr=kvr),
        grid=(B, n),
        in_specs=[pl.BlockSpec((tq, qcat.shape[1]), lambda b, i: (b * n + i, 0)),
                  pl.BlockSpec((S, kcat.shape[1]), lambda b, i: (b, 0)),
                  pl.BlockSpec((n, kcat_t.shape[1], tq), lambda b, i: (b, 0, 0)),
                  pl.BlockSpec(lw_['w_uv_rows_t'].shape, full), pl.BlockSpec((mw, 1), full)],
        out_specs=pl.BlockSpec((tq, mw), lambda b, i: (b * n + i, 0)),
        out_shape=jax.ShapeDtypeStruct((B * S, mw), BF16),
        scratch_shapes=[pltpu.VMEM((MLA_HEADS, 1, tq), F32),
                        pltpu.VMEM((MLA_HEADS, kcat_t.shape[1], tq), F32)],
        compiler_params=_cparams("parallel", "arbitrary"), name="attn_prompt",
    )(qcat, kcat, kcat_t, lw_['w_uv_rows_t'], lw_['n_b'].reshape(mw, 1))


def _attn_sample_kernel(pt_ref, q_ref, nckv_ref, nkpe_ref, ckv_hbm, kpe_hbm, wuv_ref, nb_ref, o_ref,
                        q_scr, s_scr, v_scr, ckv_buf, kpe_buf, sem, *, layer, n_pp, n_steps, sq, kvr, page):
    b = pl.program_id(0)
    nrow = MLA_HEADS * sq
    n_pages = n_steps * n_pp

    def page_copies(bb):
        slot = lax.rem(bb, 2)
        out = []
        for j in range(n_pages):
            pg = pt_ref[bb * n_pages + j]
            out.append(pltpu.make_async_copy(ckv_hbm.at[layer, pg], ckv_buf.at[slot, j], sem.at[0, slot]))
            out.append(pltpu.make_async_copy(kpe_hbm.at[layer, pg], kpe_buf.at[slot, j], sem.at[1, slot]))
        return out

    @pl.when(b == 0)
    def _():
        for c in page_copies(b):
            c.start()

    @pl.when(b + 1 < pl.num_programs(0))
    def _():
        for c in page_copies(b + 1):
            c.start()

    for h in range(MLA_HEADS):
        q_scr[h * sq:(h + 1) * sq, :] = q_ref[:, 2 * h * LANES:(2 * h + 2) * LANES].astype(F32)
    qa = q_scr[:, :kvr].astype(BF16)
    qp = q_scr[:, kvr:kvr + QK_ROPE].astype(BF16)

    for c in page_copies(b):
        c.wait()
    slot = lax.rem(b, 2)
    gp = SCORE_PAGES
    for j in range(0, n_pages, gp):
        ck = ckv_buf[slot, j:j + gp].reshape(gp * page, kvr).astype(BF16)
        kp = jnp.concatenate([kpe_buf[slot, j + t] for t in range(gp)], axis=1).astype(BF16)
        v_scr[j * page:(j + gp) * page, :] = ck
        s_scr[:, j * page:(j + gp) * page] = _dot_nt(qa, ck) + _dot(qp, kp)

    zpad = lambda a: jnp.concatenate([a, jnp.zeros((page - sq, a.shape[1]), F32)], axis=0)
    ck_new = zpad(nckv_ref[...]).astype(BF16)
    kp_new = zpad(nkpe_ref[...]).astype(BF16)
    tok = lax.broadcasted_iota(jnp.int32, (nrow, page), 0) & (sq - 1)
    key = lax.broadcasted_iota(jnp.int32, (nrow, page), 1)
    s_new = jnp.where(key <= tok, _dot_nt(qa, ck_new) + _dot_nt(qp, kp_new), NEG_BIG)
    tile = lambda i: s_scr[:, i * page:(i + 1) * page]
    m = jnp.max(_tree(jnp.maximum, [tile(i) for i in range(n_pages)] + [s_new]), axis=-1, keepdims=True)
    p_new = jnp.exp2(s_new - m)
    acc = _dot(p_new.astype(BF16), ck_new)
    psum = p_new
    vp = VALUE_PAGES
    for i in range(0, n_pages, vp):
        p = jnp.exp2(s_scr[:, i * page:(i + vp) * page] - m)
        psum = psum + _tree(jnp.add, [p[:, t * page:(t + 1) * page] for t in range(vp)])
        acc = acc + _dot(p.astype(BF16), v_scr[i * page:(i + vp) * page, :])
    l = jnp.sum(psum, axis=-1, keepdims=True)
    o_lat = (acc / l).astype(BF16)
    mw = wuv_ref.shape[1]
    r = _dot(o_lat, wuv_ref[...])
    rh = lax.broadcasted_iota(jnp.int32, (nrow, mw), 0) >> int(math.log2(sq))
    ch = lax.broadcasted_iota(jnp.int32, (nrow, mw), 1) >> int(math.log2(V_HEAD))
    r = jnp.where(rh == ch, r, 0.0)
    out = r[0:sq, :]
    for h in range(1, MLA_HEADS):
        out = out + r[h * sq:(h + 1) * sq, :]
    o_ref[...] = _rms(out, nb_ref[...])


def _attn_sample_call(page_table, qs, nckv, nkpe, cache_ckv, cache_kpe_t, layer, lw_, kvr):
    B, sq, qw = qs.shape
    n_pages = page_table.shape[1]
    page = cache_ckv.shape[2]
    n_pp = math.gcd(PAGES_PER_STEP, n_pages)
    n_steps = n_pages // n_pp
    mw = lw_['w_uv_cat'].shape[1]
    nrow = MLA_HEADS * sq
    assert sq == SUBLANES and sq <= page and n_pages % SCORE_PAGES == 0 and n_pages % VALUE_PAGES == 0
    hbm = pl.BlockSpec(memory_space=pl.ANY)
    in_specs = [pl.BlockSpec((None, sq, qw), lambda b, pt: (b, 0, 0)),
                pl.BlockSpec((None, sq, kvr), lambda b, pt: (b, 0, 0)),
                pl.BlockSpec((None, sq, QK_ROPE), lambda b, pt: (b, 0, 0)),
                hbm, hbm,
                pl.BlockSpec(lw_['w_uv_cat'].shape, lambda b, pt: (0, 0)),
                pl.BlockSpec((1, mw), lambda b, pt: (0, 0))]
    grid_spec = pltpu.PrefetchScalarGridSpec(
        num_scalar_prefetch=1, grid=(B,), in_specs=in_specs,
        out_specs=pl.BlockSpec((None, sq, mw), lambda b, pt: (b, 0, 0)),
        scratch_shapes=[pltpu.VMEM((nrow, 2 * LANES), F32), pltpu.VMEM((nrow, n_pages * page), F32),
                        pltpu.VMEM((n_pages * page, kvr), BF16),
                        pltpu.VMEM((2, n_pages, page, kvr), F32), pltpu.VMEM((2, n_pages, QK_ROPE, page), F32),
                        pltpu.SemaphoreType.DMA((2, 2))])
    return pl.pallas_call(
        functools.partial(_attn_sample_kernel, layer=layer, n_pp=n_pp, n_steps=n_steps, sq=sq, kvr=kvr, page=page),
        grid_spec=grid_spec, out_shape=jax.ShapeDtypeStruct((B, sq, mw), F32),
        compiler_params=_cparams("arbitrary"), name="attn_sample",
    )(page_table.reshape(-1), qs, nckv, nkpe, cache_ckv, cache_kpe_t, lw_['w_uv_cat'], lw_['n_b'])


def _post_kernel(a_ref, b_ref, c_ref, x_ref, wo_ref, g2_ref, wr_ref, wrl_ref, br_ref, x1_ref, hn_ref, rt_ref, cnt_ref,
                 cnt_scr, *, widths):
    wa, wb, wc = widths

    @pl.when(pl.program_id(0) == 0)
    def _():
        cnt_scr[...] = jnp.zeros(cnt_scr.shape, F32)

    mix = _dot(a_ref[...], wo_ref[:wa, :])
    mix = mix + _dot(b_ref[...], wo_ref[wa:wa + wb, :])
    mix = mix + _dot(c_ref[...], wo_ref[wa + wb:, :])
    x1 = x_ref[...] + mix
    x1_ref[...] = x1
    hn = _rms(x1, g2_ref[...])
    hn_ref[...] = hn
    hn_hi = hn.astype(BF16)
    hn_lo = (hn - hn_hi.astype(F32)).astype(BF16)
    logits = (_dot(hn_hi, wr_ref[...]) + _dot(hn_hi, wrl_ref[...]) + _dot(hn_lo, wr_ref[...])) + br_ref[...]

    lane = lax.broadcasted_iota(jnp.int32, logits.shape, 1)
    lane_f = lane.astype(F32)
    big = float(LANES)
    is_g = lane < N_GROUPS
    gl = jnp.where(is_g, logits, NEG_BIG)
    gmax = jnp.max(gl, axis=-1, keepdims=True)
    gsel = jnp.min(jnp.where(gl == gmax, lane_f, big), axis=-1, keepdims=True)
    gprob = 1.0 / jnp.sum(jnp.where(is_g, jnp.exp(gl - gmax), 0.0), axis=-1, keepdims=True)
    lo = N_GROUPS + gsel * EXPERTS_PER_GROUP
    el = jnp.where((lane_f >= lo) & (lane_f < lo + EXPERTS_PER_GROUP), logits, NEG_BIG)
    v1 = jnp.max(el, axis=-1, keepdims=True)
    i1 = jnp.min(jnp.where(el == v1, lane_f, big), axis=-1, keepdims=True)
    el2 = jnp.where(lane_f == i1, NEG_BIG, el)
    v2 = jnp.max(el2, axis=-1, keepdims=True)
    i2 = jnp.min(jnp.where(el2 == v2, lane_f, big), axis=-1, keepdims=True)
    e21 = jnp.exp(v2 - v1)
    w1 = gprob / (1.0 + e21)
    w2 = w1 * e21
    e1 = i1 - N_GROUPS
    e2 = i2 - N_GROUPS

    oh1 = (lane_f == e1).astype(F32)
    oh2 = (lane_f == e2).astype(F32)
    both = oh1 + oh2
    tm = both.shape[0]
    tri = (lax.broadcasted_iota(jnp.int32, (tm, tm), 0) >= lax.broadcasted_iota(jnp.int32, (tm, tm), 1))
    incl = _dot(tri.astype(BF16), both.astype(BF16))
    base = cnt_scr[0:1, :] + incl - both
    r1 = jnp.sum(oh1 * base, axis=-1, keepdims=True)
    r2 = jnp.sum(oh2 * base, axis=-1, keepdims=True)
    cnt_scr[...] = cnt_scr[...] + incl[tm - 1:tm, :]
    cnt_ref[...] = cnt_scr[...]

    rt = jnp.where(lane == 0, e1, 0.0)
    rt = jnp.where(lane == 1, e2, rt)
    rt = jnp.where(lane == 2, w1, rt)
    rt = jnp.where(lane == 3, w2, rt)
    rt = jnp.where(lane == 4, r1, rt)
    rt = jnp.where(lane == 5, r2, rt)
    rt_ref[...] = rt


def _post_call(a, b, c, x, lw_):
    T, D = x.shape
    tm = TOKEN_TILE
    row = lambda i: (i, 0)
    full = lambda i: (0, 0)
    widths = (a.shape[1], b.shape[1], c.shape[1])
    ws = [lw_['w_out'], lw_['g2'], lw_['w_r'], lw_['w_r_lo'], lw_['b_r']]
    out_shape = (jax.ShapeDtypeStruct((T, D), F32), jax.ShapeDtypeStruct((T, D), F32),
                 jax.ShapeDtypeStruct((T, LANES), F32), jax.ShapeDtypeStruct((SUBLANES, LANES), F32))
    return pl.pallas_call(
        functools.partial(_post_kernel, widths=widths),
        grid=(T // tm,),
        in_specs=[pl.BlockSpec((tm, w), row) for w in widths] + [pl.BlockSpec((tm, D), row)]
        + [pl.BlockSpec(w.shape, full) for w in ws],
        out_specs=tuple(pl.BlockSpec((tm, s.shape[1]), row) for s in out_shape[:3])
        + (pl.BlockSpec((SUBLANES, LANES), full),),
        out_shape=out_shape, scratch_shapes=[pltpu.VMEM((SUBLANES, LANES), F32)],
        compiler_params=_cparams("arbitrary"), name="post_proj",
    )(a, b, c, x, *ws)


def _dest_kernel(rt_ref, ps_ref, d_ref):
    rt = rt_ref[...]
    lane = lax.broadcasted_iota(jnp.int32, rt.shape, 1)
    lane_f = lane.astype(F32)
    pick = lambda k: jnp.sum(jnp.where(lane == k, rt, 0.0), axis=-1, keepdims=True)
    ps = ps_ref[...]
    d1 = jnp.sum(jnp.where(lane_f == pick(0), ps, 0.0), axis=-1, keepdims=True) + pick(2 * TOP_K)
    d2 = jnp.sum(jnp.where(lane_f == pick(1), ps, 0.0), axis=-1, keepdims=True) + pick(2 * TOP_K + 1)
    d_ref[...] = jnp.where(lane == 0, d1, jnp.where(lane == 1, d2, 0.0))


def _dest_call(route, pstart_row):
    T = route.shape[0]
    tm = TOKEN_TILE
    return pl.pallas_call(
        _dest_kernel, grid=(T // tm,),
        in_specs=[pl.BlockSpec((tm, LANES), lambda i: (i, 0)), pl.BlockSpec((1, LANES), lambda i: (0, 0))],
        out_specs=pl.BlockSpec((tm, LANES), lambda i: (i, 0)),
        out_shape=jax.ShapeDtypeStruct((T, LANES), F32), compiler_params=_cparams("parallel"), name="moe_dest",
    )(route, pstart_row)


def _moe_kernel(te_ref, nt_ref, xs_ref, w_ref, wg_ref, wu_ref, wd_ref, *rest):
    y_ref = rest[-1]
    i = pl.program_id(0)

    @pl.when(i < nt_ref[0])
    def _():
        x = xs_ref[...].astype(BF16)
        hg = _dot(x, wg_ref[...].astype(BF16))
        hu = _dot(x, wu_ref[...].astype(BF16))
        act = hg * jax.nn.sigmoid(hg) * hu * w_ref[...]
        y_ref[...] = _dot(act.astype(BF16), wd_ref[...].astype(BF16))

    @pl.when(i >= nt_ref[0])
    def _():
        y_ref[...] = jnp.zeros(y_ref.shape, F32)


def _moe_call(xs, row_w, tile_e, n_used, w_gate, w_up, w_down, layer, ys_prev, tile_off, n_rows):
    D = xs.shape[1]
    tm = MOE_TILE
    F = w_gate.shape[-1]
    in_specs = [pl.BlockSpec((tm, D), lambda i, te, nt: (i, 0)),
                pl.BlockSpec((tm, 1), lambda i, te, nt: (i, 0)),
                pl.BlockSpec((None, None, D, F), lambda i, te, nt: (layer, te[i], 0, 0)),
                pl.BlockSpec((None, None, D, F), lambda i, te, nt: (layer, te[i], 0, 0)),
                pl.BlockSpec((None, None, F, D), lambda i, te, nt: (layer, te[i], 0, 0))]
    args = [tile_e, n_used, xs, row_w, w_gate, w_up, w_down]
    aliases = {}
    if ys_prev is not None:
        in_specs.append(pl.BlockSpec(memory_space=pl.ANY))
        aliases = {len(args): 0}
        args.append(ys_prev)
    grid_spec = pltpu.PrefetchScalarGridSpec(
        num_scalar_prefetch=2, grid=(tile_e.shape[0],), in_specs=in_specs,
        out_specs=pl.BlockSpec((tm, D), lambda i, te, nt: (tile_off + i, 0)))
    return pl.pallas_call(
        _moe_kernel, grid_spec=grid_spec, out_shape=jax.ShapeDtypeStruct((n_rows, D), F32),
        input_output_aliases=aliases, compiler_params=_cparams("arbitrary"), name="moe_experts",
    )(*args)


def _moe_dispatch(hn, row_tok, row_w, tile_e, n_used, w_gate, w_up, w_down, layer):
    n_tiles = tile_e.shape[0]
    per = n_tiles // MOE_CHUNKS
    rows = per * MOE_TILE
    ys = jnp.zeros((n_tiles * MOE_TILE, hn.shape[1]), F32)
    for c in range(MOE_CHUNKS):
        xs = hn[row_tok[c * rows:(c + 1) * rows]]
        used = jnp.clip(n_used - c * per, 0, per)
        ys = _moe_call(xs, row_w[c * rows:(c + 1) * rows], tile_e[c * per:(c + 1) * per], used,
                       w_gate, w_up, w_down, layer, ys, c * per, n_tiles * MOE_TILE)
    return ys


def _route_meta(route, counts_f, tm, n_tiles):
    T = route.shape[0]
    n_assign = TOP_K * T
    e = route[:, :TOP_K].astype(jnp.int32).reshape(-1)
    w = route[:, TOP_K:2 * TOP_K].reshape(-1)
    order = jnp.argsort(e, stable=True).astype(jnp.int32)
    counts = counts_f.astype(jnp.int32)
    zero = jnp.zeros((1,), jnp.int32)
    start = jnp.concatenate([zero, jnp.cumsum(counts)])
    pstart = jnp.concatenate([zero, jnp.cumsum(((counts + tm - 1) // tm) * tm)])
    n_used = (pstart[N_EXPERTS] // tm).reshape(1)
    tile_lo = jnp.arange(n_tiles, dtype=jnp.int32) * tm
    tile_e = jnp.sum((pstart[None, 1:] <= tile_lo[:, None]).astype(jnp.int32), axis=1)
    tile_e = jnp.minimum(tile_e, N_EXPERTS - 1)
    oh = (tile_e[:, None] == jnp.arange(N_EXPERTS, dtype=jnp.int32)[None, :]).astype(jnp.int32)
    pick = lambda tab: jnp.sum(oh * tab[None, :N_EXPERTS], axis=1)
    k = (tile_lo - pick(pstart))[:, None] + jnp.arange(tm, dtype=jnp.int32)[None, :]
    valid = (k < pick(counts)[:, None]).reshape(-1)
    idx = jnp.clip(pick(start)[:, None] + k, 0, n_assign - 1).reshape(-1)
    src = order[idx]
    row_tok = jnp.where(valid, src // TOP_K, 0)
    row_w = jnp.where(valid, w[src], 0.0)
    ps_row = jnp.pad(pstart[:N_EXPERTS].astype(F32), (0, LANES - N_EXPERTS)).reshape(1, LANES)
    pos = _dest_call(route, ps_row)[:, :TOP_K].astype(jnp.int32)
    return row_tok, row_w.reshape(-1, 1), pos, tile_e, n_used


def _final_kernel(x_ref, g_ref, o_ref):
    o_ref[...] = _rms(x_ref[...], g_ref[...])


def _final_call(x, g):
    T, D = x.shape
    tm = TOKEN_TILE
    return pl.pallas_call(
        _final_kernel, grid=(T // tm,),
        in_specs=[pl.BlockSpec((tm, D), lambda i: (i, 0)), pl.BlockSpec((1, D), lambda i: (0, 0))],
        out_specs=pl.BlockSpec((tm, D), lambda i: (i, 0)),
        out_shape=jax.ShapeDtypeStruct((T, D), F32), compiler_params=_cparams("parallel"), name="final_norm",
    )(x, g)


def _block_diag(w):
    n, c, d = w.shape
    return jnp.einsum('ncd,nm->ncmd', w, jnp.eye(n, dtype=w.dtype)).reshape(n * c, n * d)


def _half_swap(w):
    half = w.shape[-1] // 2
    return jnp.concatenate([w[..., half:], w[..., :half]], axis=-1)


def _layer_weights(l, p, dims):
    lw, qr, kvr, gk_w, gv_w = dims
    D = p['w_in'].shape[1]
    row = lambda a: a.reshape(1, -1).astype(F32)
    w_in = p['w_in'][l]
    sizes = (lw, lw, qr, kvr, QK_ROPE, gk_w, gk_w, gv_w, p['gla_wa2'].shape[1], gv_w)
    x_lru, y_lru, c_q, c_kv, k_pe, g_q, g_k, g_v, g_a, g_o = jnp.split(w_in, list(np.cumsum(sizes)[:-1]), axis=1)
    tail_pad = LANES - 2 * QK_ROPE - g_a.shape[1]
    tail = jnp.concatenate([k_pe, _half_swap(k_pe), g_a, jnp.zeros((D, tail_pad), F32)], axis=1)
    w_in_p = jnp.concatenate([x_lru, y_lru, c_q, c_kv, g_q, g_k, g_v, g_o, tail], axis=1).astype(BF16)

    w_uq = p['w_uq'][l].reshape(qr, MLA_HEADS, QK_NOPE + QK_ROPE)
    nope = w_uq[:, :, :QK_NOPE].reshape(qr, MLA_HEADS * QK_NOPE)
    pe = w_uq[:, :, QK_NOPE:]
    widen = lambda a: jnp.pad(a, ((0, 0), (0, 0), (0, LANES - QK_ROPE))).reshape(qr, MLA_HEADS * LANES)
    w_q = jnp.concatenate([nope, widen(pe), widen(_half_swap(pe))], axis=1).astype(BF16)
    w_abs = _block_diag(jnp.transpose(p['w_uk'][l], (1, 2, 0))).astype(BF16)
    wa2 = jnp.zeros((LANES, gk_w), F32).at[2 * QK_ROPE:2 * QK_ROPE + g_a.shape[1]].set(p['gla_wa2'][l]).astype(BF16)

    w_uv = p['w_uv'][l]
    mw = MLA_HEADS * V_HEAD
    w_uv_cat = w_uv.reshape(kvr, mw)
    w_uv_rows = _block_diag(jnp.transpose(w_uv, (1, 0, 2)))
    n_a, n_b, n_c = jnp.split(p['out_norm_g'][l], [lw, lw + mw])
    n_r = N_GROUPS + N_EXPERTS
    w_r = jnp.concatenate([p['router_wg'][l], p['router_we'][l], jnp.zeros((D, LANES - n_r), F32)], axis=1)
    b_r = jnp.concatenate([p['router_bg'][l], p['router_be'][l], jnp.zeros((LANES - n_r,), F32)])
    return dict(
        g1=row(p['norm1_g'][l]), w_in=w_in_p, q_g=row(p['q_norm_g'][l]), w_q=w_q, w_abs=w_abs,
        kv_g=row(p['kv_norm_g'][l]), wa2=wa2, ba=row(p['gla_ba'][l]),
        conv_w=p['conv_w'][l], conv_b=row(p['conv_b'][l]),
        lru_wa=_block_diag(p['lru_wa'][l]).astype(BF16), lru_ba=row(p['lru_ba'][l]),
        lru_wi=_block_diag(p['lru_wi'][l]).astype(BF16), lru_bi=row(p['lru_bi'][l]),
        lru_lam=row(p['lru_lambda'][l]), n_a=row(n_a), n_b=row(n_b), n_c=row(n_c),
        gla_g=row(jnp.tile(p['gla_norm_g'][l], GLA_HEADS)),
        w_uv_cat=w_uv_cat.astype(BF16), w_uv_rows_t=w_uv_rows.T.astype(BF16),
        w_out=p['w_out'][l].astype(BF16), g2=row(p['norm2_g'][l]), w_r=w_r.astype(BF16),
        w_r_lo=(w_r - w_r.astype(BF16).astype(F32)).astype(BF16), b_r=row(b_r),
    )


def _rope_tables(positions):
    half = QK_ROPE // 2
    inv = ROPE_THETA ** (-np.arange(half, dtype=np.float64) / half)
    ang = np.asarray(positions, np.float64)[:, None] * inv
    zeros = np.zeros((ang.shape[0], LANES - QK_ROPE))
    cos = np.concatenate([np.cos(ang), np.cos(ang), zeros], axis=1)
    sin = np.concatenate([-np.sin(ang), np.sin(ang), zeros], axis=1)
    return jnp.asarray(cos, F32), jnp.asarray(sin, F32)


def _gla_state_to_rows(s):
    B = s.shape[0]
    eye = jnp.eye(GLA_HEADS, dtype=s.dtype)
    return jnp.einsum('bhde,hg->bhegd', s, eye).reshape(B, GLA_HEADS * GLA_DV, GLA_HEADS * GLA_DK)


def _gla_rows_to_state(st):
    B = st.shape[0]
    s5 = st.reshape(B, GLA_HEADS, GLA_DV, GLA_HEADS, GLA_DK)
    blocks = jnp.stack([s5[:, h, :, h, :] for h in range(GLA_HEADS)], axis=1)
    return jnp.swapaxes(blocks, -1, -2)


def kernel(x_prompt, x_sample, cache_ckv, cache_kpe, page_table, state_conv, state_lru, state_gla, norm1_g, w_in, conv_w, conv_b, lru_wa, lru_ba, lru_wi, lru_bi, lru_lambda, q_norm_g, w_uq, kv_norm_g, w_uk, w_uv, gla_wa2, gla_ba, gla_norm_g, out_norm_g, w_out, norm2_g, router_wg, router_bg, router_we, router_be, w_gate, w_up, w_down, final_norm_g):
    p = dict(norm1_g=norm1_g, w_in=w_in, conv_w=conv_w, conv_b=conv_b, lru_wa=lru_wa, lru_ba=lru_ba,
             lru_wi=lru_wi, lru_bi=lru_bi, lru_lambda=lru_lambda, q_norm_g=q_norm_g, w_uq=w_uq,
             kv_norm_g=kv_norm_g, w_uk=w_uk, w_uv=w_uv, gla_wa2=gla_wa2, gla_ba=gla_ba,
             gla_norm_g=gla_norm_g, out_norm_g=out_norm_g, w_out=w_out, norm2_g=norm2_g,
             router_wg=router_wg, router_bg=router_bg, router_we=router_we, router_be=router_be)
    Bp, Sp, D = x_prompt.shape
    Bs, Ss, _ = x_sample.shape
    depth = w_in.shape[0]
    lw = state_lru.shape[-1]
    qr = q_norm_g.shape[-1]
    kvr = kv_norm_g.shape[-1]
    gk_w = GLA_HEADS * GLA_DK
    gv_w = GLA_HEADS * GLA_DV
    dims = (lw, qr, kvr, gk_w, gv_w)
    assert kvr == LANES and gk_w == LANES
    Tp, Ts = Bp * Sp, Bs * Ss
    T = Tp + Ts
    past = page_table.shape[1] * cache_ckv.shape[2]

    cos_t, sin_t = _rope_tables(np.concatenate([np.tile(np.arange(Sp), Bp), np.tile(past + np.arange(Ss), Bs)]))
    cache_kpe_t = jnp.swapaxes(cache_kpe, 2, 3)
    n_tiles = -(-((TOP_K * T + N_EXPERTS * (MOE_TILE - 1)) // MOE_TILE + 1) // MOE_CHUNKS) * MOE_CHUNKS

    x = jnp.concatenate([x_prompt.reshape(Tp, D), x_sample.reshape(Ts, D)], axis=0)
    outs = {k: [] for k in ('ckv_p', 'kpe_p', 'ckv_s', 'kpe_s', 'conv_p', 'conv_s', 'lru_p', 'lru_s', 'gla_p', 'gla_s')}
    for l in range(depth):
        lw_ = _layer_weights(l, p, dims)
        zl, qcat, kcat, kcat_t, ckv_n, kpe_n, g = _pre_call(x, lw_, cos_t, sin_t, dims)

        a_p, conv_p, lru_p = _lru_call(zl, 0, Bp, Sp, jnp.zeros((Bp, CONV_WIDTH - 1, lw), F32),
                                       jnp.zeros((Bp, lw), F32), lw_, BF16)
        a_s, lru_s = _lru_seg_call(zl, Tp, Bs, Ss, state_conv[l], state_lru[l], lw_)
        conv_s = zl[Tp:, :lw].reshape(Bs, Ss, lw)[:, Ss - (CONV_WIDTH - 1):]

        c_p, gla_p = _gla_call(g, 0, Bp, Sp, jnp.zeros((Bp, gv_w, gk_w), F32), lw_, BF16, gk_w, gv_w)
        c_s, gla_s = _gla_call(g, Tp, Bs, Ss, _gla_state_to_rows(state_gla[l]), lw_, BF16, gk_w, gv_w)

        b_p = _attn_prompt_call(qcat, kcat, kcat_t, Bp, Sp, lw_, kvr)
        b_s = _attn_sample_call(page_table, qcat[Tp:].reshape(Bs, Ss, -1), ckv_n[Tp:].reshape(Bs, Ss, kvr),
                                kpe_n[Tp:].reshape(Bs, Ss, QK_ROPE), cache_ckv, cache_kpe_t, l, lw_, kvr)

        a = jnp.concatenate([a_p, a_s], axis=0)
        b = jnp.concatenate([b_p, b_s.reshape(Ts, -1).astype(BF16)], axis=0)
        c = jnp.concatenate([c_p, c_s], axis=0)
        x1, hn, route, cnt = _post_call(a, b, c, x, lw_)

        row_tok, row_w, pos, tile_e, n_used = _route_meta(route, cnt[0, :N_EXPERTS], MOE_TILE, n_tiles)
        ys = _moe_dispatch(hn, row_tok, row_w, tile_e, n_used, w_gate, w_up, w_down, l)
        x = x1 + ys[pos[:, 0]] + ys[pos[:, 1]]

        outs['ckv_p'].append(ckv_n[:Tp].reshape(Bp, Sp, kvr))
        outs['kpe_p'].append(kpe_n[:Tp].reshape(Bp, Sp, QK_ROPE))
        outs['ckv_s'].append(ckv_n[Tp:].reshape(Bs, Ss, kvr))
        outs['kpe_s'].append(kpe_n[Tp:].reshape(Bs, Ss, QK_ROPE))
        outs['conv_p'].append(conv_p)
        outs['conv_s'].append(conv_s)
        outs['lru_p'].append(lru_p.reshape(Bp, lw))
        outs['lru_s'].append(lru_s.reshape(Bs, lw))
        outs['gla_p'].append(_gla_rows_to_state(gla_p))
        outs['gla_s'].append(_gla_rows_to_state(gla_s))

    y = _final_call(x, final_norm_g.reshape(1, D))
    st = {k: jnp.stack(v) for k, v in outs.items()}
    return (y[:Tp].reshape(Bp, Sp, D), y[Tp:].reshape(Bs, Ss, D),
            st['ckv_p'], st['kpe_p'], st['ckv_s'], st['kpe_s'], st['conv_p'], st['conv_s'],
            st['lru_p'], st['lru_s'], st['gla_p'], st['gla_s'])
```

```python
import functools
import math

import numpy as np
import jax
import jax.numpy as jnp
from jax import lax
from jax.experimental import pallas as pl
from jax.experimental.pallas import tpu as pltpu

F32 = jnp.float32
BF16 = jnp.bfloat16

LRU_BLOCKS = 4
CONV_WIDTH = 4
LRU_C = 8.0
MLA_HEADS = 8
QK_NOPE = 64
QK_ROPE = 32
V_HEAD = 64
ROPE_THETA = 10000.0
GLA_HEADS = 4
GLA_DK = 32
GLA_DV = 64
GLA_TAU = 16.0
GLA_CHUNK = 16
N_GROUPS = 4
EXPERTS_PER_GROUP = 8
N_EXPERTS = N_GROUPS * EXPERTS_PER_GROUP
TOP_K = 2
EPS = 1e-6

LANES = 128
SUBLANES = 8
BF16_ROWS = 16
VMEM_LIMIT_BYTES = 56 * 1024 * 1024

TOKEN_TILE = 512
SEQ_TILE = 256
MOE_TILE = 256
SCORE_PAGES = 4
VALUE_PAGES = 2
PAGES_PER_STEP = 16
NEG_BIG = -1e30


def _cparams(*sem):
    return pltpu.CompilerParams(dimension_semantics=sem, vmem_limit_bytes=VMEM_LIMIT_BYTES)


def _rms(x, g):
    return x * lax.rsqrt(jnp.mean(x * x, axis=-1, keepdims=True) + EPS) * g


def _dot(a, b):
    return jnp.dot(a, b, preferred_element_type=F32)


def _dot_nt(a, b):
    return lax.dot_general(a, b, (((1,), (1,)), ((), ())), preferred_element_type=F32)


def _softplus(x):
    return jnp.maximum(x, 0.0) + jnp.log1p(jnp.exp(-jnp.abs(x)))


def _shift_rows(x, d, fill):
    row = lax.broadcasted_iota(jnp.int32, x.shape, 0)
    return jnp.where(row >= d, pltpu.roll(x, d, 0), fill)


def _pre_kernel(x_ref, g1_ref, win_ref, qg_ref, wq_ref, wabs_ref, kvg_ref, wa2_ref, ba_ref,
                cos_ref, sin_ref, zl_ref, q_ref, k_ref, kt_ref, ckv_ref, kpe_ref, g_ref, *, dims):
    lw, qr, kvr, gk_w, gv_w = dims
    xn = _rms(x_ref[...], g1_ref[...])
    z = _dot(xn.astype(BF16), win_ref[...])
    o = 2 * lw
    zl_ref[...] = z[:, :o]
    cq = z[:, o:o + qr]
    o += qr
    ckv = z[:, o:o + kvr]
    o += kvr
    gqk = z[:, o:o + 2 * gk_w]
    o += 2 * gk_w
    gvo = z[:, o:o + 2 * gv_w]
    o += 2 * gv_w
    tail = z[:, o:o + LANES]
    cos = cos_ref[...]
    sin = sin_ref[...]
    scale = (QK_NOPE + QK_ROPE) ** -0.5 * math.log2(math.e)

    ckv_n = _rms(ckv, kvg_ref[...])
    kpe = tail * cos + pltpu.roll(tail, LANES - QK_ROPE, 1) * sin
    ckv_ref[...] = ckv_n
    kpe_ref[...] = kpe[:, :QK_ROPE]
    k_ref[:, :kvr] = ckv_n.astype(BF16)
    k_ref[:, kvr:] = kpe.astype(BF16)
    ones = jnp.ones((kt_ref.shape[0] - kvr, ckv_n.shape[0]), F32)
    kt_ref[...] = jnp.concatenate([ckv_n.T, ones], axis=0).astype(BF16)

    cqn = _rms(cq, qg_ref[...]).astype(BF16)
    qall = _dot(cqn, wq_ref[...])
    n_nope = MLA_HEADS * QK_NOPE
    q_abs = _dot(qall[:, :n_nope].astype(BF16), wabs_ref[...]) * scale
    for h in range(MLA_HEADS):
        pe = qall[:, n_nope + h * LANES:n_nope + (h + 1) * LANES]
        sw = qall[:, n_nope + (MLA_HEADS + h) * LANES:n_nope + (MLA_HEADS + h + 1) * LANES]
        q_ref[:, 2 * h * LANES:(2 * h + 1) * LANES] = q_abs[:, h * kvr:(h + 1) * kvr].astype(BF16)
        q_ref[:, (2 * h + 1) * LANES:(2 * h + 2) * LANES] = ((pe * cos + sw * sin) * scale).astype(BF16)

    la_pre = _dot(tail.astype(BF16), wa2_ref[...]) + ba_ref[...]
    la = -_softplus(-la_pre) * (1.0 / GLA_TAU)
    g_ref[:, :gk_w] = gqk[:, :gk_w] * (GLA_DK ** -0.5)
    g_ref[:, gk_w:2 * gk_w] = gqk[:, gk_w:]
    g_ref[:, 2 * gk_w:3 * gk_w] = la
    g_ref[:, 3 * gk_w:] = gvo


def _pre_call(x, lw_, cos_t, sin_t, dims):
    T, D = x.shape
    lw, qr, kvr, gk_w, gv_w = dims
    tm = TOKEN_TILE
    assert T % tm == 0
    row = lambda i: (i, 0)
    full = lambda i: (0, 0)
    wspec = lambda a: pl.BlockSpec(a.shape, full)
    ins = [x, lw_['g1'], lw_['w_in'], lw_['q_g'], lw_['w_q'], lw_['w_abs'], lw_['kv_g'], lw_['wa2'], lw_['ba']]
    in_specs = [pl.BlockSpec((tm, D), row)] + [wspec(a) for a in ins[1:]]
    in_specs += [pl.BlockSpec((tm, LANES), row), pl.BlockSpec((tm, LANES), row)]
    out_shape = (
        jax.ShapeDtypeStruct((T, 2 * lw), F32),
        jax.ShapeDtypeStruct((T, 2 * LANES * MLA_HEADS), BF16),
        jax.ShapeDtypeStruct((T, 2 * LANES), BF16),
        jax.ShapeDtypeStruct((T // tm, kvr + BF16_ROWS, tm), BF16),
        jax.ShapeDtypeStruct((T, kvr), F32),
        jax.ShapeDtypeStruct((T, QK_ROPE), F32),
        jax.ShapeDtypeStruct((T, 3 * gk_w + 2 * gv_w), F32),
    )
    out_specs = tuple(pl.BlockSpec((None, s.shape[1], tm), lambda i: (i, 0, 0)) if len(s.shape) == 3
                      else pl.BlockSpec((tm, s.shape[1]), row) for s in out_shape)
    return pl.pallas_call(
        functools.partial(_pre_kernel, dims=dims),
        grid=(T // tm,), in_specs=in_specs, out_specs=out_specs, out_shape=out_shape,
        compiler_params=_cparams("parallel"), name="pre_proj",
    )(*ins, cos_t, sin_t)


def _lru_kernel(zl_ref, cbuf_ref, h0_ref, cw_ref, cb_ref, wa_ref, ba_ref, wi_ref, bi_ref, lam_ref, na_ref,
                a_ref, conv_ref, hout_ref, xbuf, hcar, *, ts, lw):
    i = pl.program_id(1)
    last = pl.num_programs(1) - 1
    pad = SUBLANES
    nbuf = CONV_WIDTH - 1

    @pl.when(i == 0)
    def _():
        xbuf[0:pad, :] = jnp.zeros((pad, lw), F32)
        xbuf[pad - nbuf:pad, :] = cbuf_ref[...]
        hcar[...] = h0_ref[...]

    x = zl_ref[:, :lw]
    y = zl_ref[:, lw:]
    xbuf[pad:pad + ts, :] = x
    xc = cb_ref[...] + cw_ref[nbuf:nbuf + 1, :] * x
    for k in range(nbuf):
        xc = xc + cw_ref[k:k + 1, :] * xbuf[pad - nbuf + k:pad - nbuf + k + ts, :]

    @pl.when(i == last)
    def _():
        conv_ref[...] = xbuf[pad + ts - nbuf:pad + ts, :]

    xbuf[0:pad, :] = xbuf[ts:ts + pad, :]

    xb = xc.astype(BF16)
    r = jax.nn.sigmoid(_dot(xb, wa_ref[...]) + ba_ref[...])
    gi = jax.nn.sigmoid(_dot(xb, wi_ref[...]) + bi_ref[...])
    log_a = (-LRU_C) * r * _softplus(-lam_ref[...])
    a = jnp.exp(log_a)
    th = jnp.tanh(log_a)
    u = jnp.sqrt(-2.0 * th / (1.0 - th)) * (gi * xc)

    d = 1
    while d < ts:
        u = a * _shift_rows(u, d, 0.0) + u
        a = a * _shift_rows(a, d, 1.0)
        d *= 2
    h = a * hcar[...] + u
    hcar[...] = h[ts - 1:ts, :]

    @pl.when(i == last)
    def _():
        hout_ref[...] = h[ts - 1:ts, :]

    out_a = h * jax.nn.gelu(y)
    a_ref[...] = _rms(out_a, na_ref[...]).astype(a_ref.dtype)


def _lru_call(zl, row_off, B, S, cbuf, h0, lw_, out_dtype):
    lw = h0.shape[-1]
    ts = min(S, SEQ_TILE)
    assert S % ts == 0 and row_off % ts == 0 and S >= CONV_WIDTH - 1
    n = S // ts
    off = row_off // ts
    full = lambda b, i: (0, 0)
    ws = [lw_['conv_w'], lw_['conv_b'], lw_['lru_wa'], lw_['lru_ba'], lw_['lru_wi'], lw_['lru_bi'],
          lw_['lru_lam'], lw_['n_a']]
    in_specs = [
        pl.BlockSpec((ts, 2 * lw), lambda b, i: (off + b * n + i, 0)),
        pl.BlockSpec((None, CONV_WIDTH - 1, lw), lambda b, i: (b, 0, 0)),
        pl.BlockSpec((None, 1, lw), lambda b, i: (b, 0, 0)),
    ] + [pl.BlockSpec(a.shape, full) for a in ws]
    out_shape = (
        jax.ShapeDtypeStruct((B * S, lw), out_dtype),
        jax.ShapeDtypeStruct((B, CONV_WIDTH - 1, lw), F32),
        jax.ShapeDtypeStruct((B, 1, lw), F32),
    )
    out_specs = (
        pl.BlockSpec((ts, lw), lambda b, i: (b * n + i, 0)),
        pl.BlockSpec((None, CONV_WIDTH - 1, lw), lambda b, i: (b, 0, 0)),
        pl.BlockSpec((None, 1, lw), lambda b, i: (b, 0, 0)),
    )
    return pl.pallas_call(
        functools.partial(_lru_kernel, ts=ts, lw=lw),
        grid=(B, n), in_specs=in_specs, out_specs=out_specs, out_shape=out_shape,
        scratch_shapes=[pltpu.VMEM((ts + SUBLANES, lw), F32), pltpu.VMEM((1, lw), F32)],
        compiler_params=_cparams("parallel", "arbitrary"), name="rg_lru",
    )(zl, cbuf, h0.reshape(B, 1, lw), *ws)


def _lru_seg_kernel(zl_ref, hist_ref, h0_ref, cw_ref, cb_ref, wa_ref, ba_ref, wi_ref, bi_ref, lam_ref, na_ref,
                    a_ref, h_ref, *, seg, lw):
    nbuf = CONV_WIDTH - 1
    x = zl_ref[:, :lw]
    y = zl_ref[:, lw:]
    n = x.shape[0]
    pos = lax.broadcasted_iota(jnp.int32, (n, 1), 0) & (seg - 1)
    hist = hist_ref[...]
    xc = cb_ref[...] + cw_ref[nbuf:nbuf + 1, :] * x
    for k in range(nbuf):
        j = nbuf - k
        prev = jnp.where(pos >= j, pltpu.roll(x, j, 0), pltpu.roll(hist, (j - seg) % n, 0))
        xc = xc + cw_ref[k:k + 1, :] * prev
    xb = xc.astype(BF16)
    r = jax.nn.sigmoid(_dot(xb, wa_ref[...]) + ba_ref[...])
    gi = jax.nn.sigmoid(_dot(xb, wi_ref[...]) + bi_ref[...])
    log_a = (-LRU_C) * r * _softplus(-lam_ref[...])
    a = jnp.exp(log_a)
    th = jnp.tanh(log_a)
    u = jnp.sqrt(-2.0 * th / (1.0 - th)) * (gi * xc)
    d = 1
    while d < seg:
        u = a * jnp.where(pos >= d, pltpu.roll(u, d, 0), 0.0) + u
        a = a * jnp.where(pos >= d, pltpu.roll(a, d, 0), 1.0)
        d *= 2
    h = a * h0_ref[...] + u
    h_ref[...] = h
    a_ref[...] = _rms(h * jax.nn.gelu(y), na_ref[...]).astype(a_ref.dtype)


def _lru_seg_call(zl, row_off, B, S, cbuf, h0, lw_):
    lw = h0.shape[-1]
    n = B * S
    tm = min(n, TOKEN_TILE)
    assert S & (S - 1) == 0 and S >= CONV_WIDTH - 1 and tm % S == 0 and n % tm == 0 and row_off % tm == 0
    off = row_off // tm
    hist = jnp.pad(cbuf, ((0, 0), (S - (CONV_WIDTH - 1), 0), (0, 0))).reshape(n, lw)
    h0_rows = jnp.repeat(h0, S, axis=0)
    full = lambda i: (0, 0)
    row = lambda i: (i, 0)
    ws = [lw_['conv_w'], lw_['conv_b'], lw_['lru_wa'], lw_['lru_ba'], lw_['lru_wi'], lw_['lru_bi'],
          lw_['lru_lam'], lw_['n_a']]
    a, h = pl.pallas_call(
        functools.partial(_lru_seg_kernel, seg=S, lw=lw),
        grid=(n // tm,),
        in_specs=[pl.BlockSpec((tm, 2 * lw), lambda i: (off + i, 0)), pl.BlockSpec((tm, lw), row),
                  pl.BlockSpec((tm, lw), row)] + [pl.BlockSpec(w.shape, full) for w in ws],
        out_specs=(pl.BlockSpec((tm, lw), row), pl.BlockSpec((tm, lw), row)),
        out_shape=(jax.ShapeDtypeStruct((n, lw), BF16), jax.ShapeDtypeStruct((n, lw), F32)),
        compiler_params=_cparams("parallel"), name="rg_lru_seg",
    )(zl, hist, h0_rows, *ws)
    return a, h.reshape(B, S, lw)[:, S - 1]


def _gla_kernel(g_ref, s0_ref, gn_ref, nc_ref, c_ref, sout_ref, st, *, ts, nseq, gk_w, gv_w):
    i = pl.program_id(1)
    last = pl.num_programs(1) - 1
    C = GLA_CHUNK
    rows = min(ts, C)
    n_chunks = max(ts // C, 1)
    log2c = int(math.log2(C))

    @pl.when(i == 0)
    def _():
        st[...] = s0_ref[...]

    dk_sh, dv_sh = int(math.log2(GLA_DK)), int(math.log2(GLA_DV))
    hd = lax.broadcasted_iota(jnp.int32, (gk_w, gv_w), 0) >> dk_sh
    he = lax.broadcasted_iota(jnp.int32, (gk_w, gv_w), 1) >> dv_sh
    same = (hd == he).astype(BF16)
    he_t = lax.broadcasted_iota(jnp.int32, (gv_w, gk_w), 0) >> dv_sh
    hd_t = lax.broadcasted_iota(jnp.int32, (gv_w, gk_w), 1) >> dk_sh
    same_t = (he_t == hd_t).astype(F32)
    sel_t = lax.broadcasted_iota(jnp.int32, (C, C * C), 0)
    sel_r = lax.broadcasted_iota(jnp.int32, (C, C * C), 1) >> log2c
    sel = (sel_t == sel_r).astype(BF16)
    srow = lax.broadcasted_iota(jnp.int32, (C, 1), 0)

    seq_out = []
    for sq in range(nseq):
        blk = g_ref[sq * ts:(sq + 1) * ts, :]
        if rows < C:
            blk = jnp.concatenate([blk, jnp.zeros((C - rows, blk.shape[1]), F32)], axis=0)
        q = blk[:, :gk_w]
        k = blk[:, gk_w:2 * gk_w]
        la = blk[:, 2 * gk_w:3 * gk_w]
        v = blk[:, 3 * gk_w:3 * gk_w + gv_w]
        nrow = n_chunks * C
        pos = lax.broadcasted_iota(jnp.int32, (nrow, 1), 0) & (C - 1)
        cum = la
        d = 1
        while d < C:
            cum = cum + jnp.where(pos >= d, pltpu.roll(cum, d, 0), 0.0)
            d *= 2
        tot = jnp.where(pos == C - 1, cum, 0.0)
        d = 1
        while d < C:
            tot = tot + jnp.where(pos < C - d, pltpu.roll(tot, nrow - d, 0), 0.0)
            d *= 2
        qe = (q * jnp.exp(cum)).astype(BF16)
        kdec = (k * jnp.exp(tot - cum)).astype(BF16)
        dec = jnp.exp(tot)
        vb = v.astype(BF16)

        o_intra, upd = [], []
        for c in range(n_chunks):
            sl = slice(c * C, (c + 1) * C)
            cum_c, q_c, k_c, v_c = cum[sl], q[sl], k[sl], v[sl]
            pieces = []
            for t in range(C):
                diff = jnp.where(srow <= t, cum_c[t:t + 1, :] - cum_c, NEG_BIG)
                pieces.append(q_c[t:t + 1, :] * k_c * jnp.exp(diff))
            w = jnp.concatenate(pieces, axis=0)
            att = _dot(w.astype(BF16), same)
            xv = att * jnp.concatenate([v_c] * C, axis=0)
            o_intra.append(_dot(sel, xv.astype(BF16)))
            upd.append(lax.dot_general(vb[sl], kdec[sl], (((0,), (0,)), ((), ())),
                                       preferred_element_type=F32) * same_t)

        s_t = st[sq]
        outs = []
        for c in range(n_chunks):
            sl = slice(c * C, (c + 1) * C)
            outs.append(o_intra[c] + _dot_nt(qe[sl], s_t.astype(BF16)))
            s_t = s_t * dec[c * C:c * C + 1, :] + upd[c]
        st[sq] = s_t
        seq_out.append(jnp.concatenate(outs, axis=0)[:ts] if n_chunks > 1 else outs[0][:ts])

    o = jnp.concatenate(seq_out, axis=0) if nseq > 1 else seq_out[0]

    @pl.when(i == last)
    def _():
        sout_ref[...] = st[...]

    go = g_ref[:, 3 * gk_w + gv_w:]
    e64 = (lax.broadcasted_iota(jnp.int32, (gv_w, gv_w), 0) >> dv_sh
           == lax.broadcasted_iota(jnp.int32, (gv_w, gv_w), 1) >> dv_sh).astype(BF16)
    osq = o * o
    osq_hi = osq.astype(BF16)
    osq_lo = (osq - osq_hi.astype(F32)).astype(BF16)
    ms = (_dot(osq_hi, e64) + _dot(osq_lo, e64)) * (1.0 / GLA_DV)
    out_c = o * lax.rsqrt(ms + EPS) * gn_ref[...] * (go * jax.nn.sigmoid(go))
    c_ref[...] = _rms(out_c, nc_ref[...]).astype(c_ref.dtype)


def _gla_call(g, row_off, B, S, s0t, lw_, out_dtype, gk_w, gv_w):
    ts = min(S, SEQ_TILE)
    nseq = math.gcd(B, max(SEQ_TILE // (2 * ts), 1)) if ts == S else 1
    rows = nseq * ts
    assert S % ts == 0 and row_off % rows == 0 and (ts % GLA_CHUNK == 0 or ts < GLA_CHUNK)
    n = S // ts
    off = row_off // rows
    gw = g.shape[1]
    full = lambda b, i: (0, 0)
    out_shape = (jax.ShapeDtypeStruct((B * S, gv_w), out_dtype),
                 jax.ShapeDtypeStruct((B, gv_w, gk_w), F32))
    return pl.pallas_call(
        functools.partial(_gla_kernel, ts=ts, nseq=nseq, gk_w=gk_w, gv_w=gv_w),
        grid=(B // nseq, n),
        in_specs=[pl.BlockSpec((rows, gw), lambda b, i: (off + b * n + i, 0)),
                  pl.BlockSpec((nseq, gv_w, gk_w), lambda b, i: (b, 0, 0)),
                  pl.BlockSpec((1, gv_w), full), pl.BlockSpec((1, gv_w), full)],
        out_specs=(pl.BlockSpec((rows, gv_w), lambda b, i: (b * n + i, 0)),
                   pl.BlockSpec((nseq, gv_w, gk_w), lambda b, i: (b, 0, 0))),
        out_shape=out_shape,
        scratch_shapes=[pltpu.VMEM((nseq, gv_w, gk_w), F32)],
        compiler_params=_cparams("parallel", "arbitrary"), name="gla",
    )(g, s0t, lw_['gla_g'], lw_['n_c'])


def _tree(op, xs):
    xs = list(xs)
    while len(xs) > 1:
        xs = [op(xs[i], xs[i + 1]) if i + 1 < len(xs) else xs[i] for i in range(0, len(xs), 2)]
    return xs[0]


def _attn_prompt_kernel(q_ref, k_ref, kt_ref, wuvt_ref, nb_ref, o_ref, m_scr, acc_scr, *, tq, kvr):
    i = pl.program_id(1)
    kw = 2 * LANES
    m_scr[...] = jnp.full(m_scr.shape, NEG_BIG, F32)
    acc_scr[...] = jnp.zeros(acc_scr.shape, F32)

    def block(j, masked):
        kb = k_ref[pl.ds(pl.multiple_of(j * tq, tq), tq), :]
        kbt = kt_ref[j]
        if masked:
            key = lax.broadcasted_iota(jnp.int32, (tq, tq), 0)
            qry = lax.broadcasted_iota(jnp.int32, (tq, tq), 1)
            keep = key <= qry
        qk = lambda h: _dot_nt(kb, q_ref[:, h * kw:(h + 1) * kw])
        st_next = qk(0)
        for h in range(MLA_HEADS):
            st = st_next
            if h + 1 < MLA_HEADS:
                st_next = qk(h + 1)
            if masked:
                st = jnp.where(keep, st, NEG_BIG)
            m_old = m_scr[h]
            m_new = jnp.maximum(m_old, jnp.max(st, axis=0, keepdims=True))
            alpha = jnp.exp2(m_old - m_new)
            pt = jnp.exp2(st - m_new).astype(BF16)
            acc_scr[h] = alpha * acc_scr[h] + _dot(kbt, pt)
            m_scr[h] = m_new

    def body(j, carry):
        block(j, False)
        return carry

    lax.fori_loop(0, i, body, 0)
    block(i, True)

    out_t = jnp.zeros((wuvt_ref.shape[0], tq), F32)
    for h in range(MLA_HEADS):
        acc = acc_scr[h]
        o_t = (acc[:kvr, :] / acc[kvr:kvr + 1, :]).astype(BF16)
        out_t = out_t + _dot(wuvt_ref[:, h * kvr:(h + 1) * kvr], o_t)
    ms = jnp.mean(out_t * out_t, axis=0, keepdims=True)
    out_t = out_t * lax.rsqrt(ms + EPS) * nb_ref[...]
    o_ref[...] = out_t.T.astype(o_ref.dtype)


def _attn_prompt_call(qcat, kcat, kcat_t, B, S, lw_, kvr):
    tq = kcat_t.shape[2]
    assert S % tq == 0
    n = S // tq
    mw = lw_['w_uv_rows_t'].shape[0]
    full = lambda b, i: (0, 0)
    return pl.pallas_call(
        functools.partial(_attn_prompt_kernel, tq=tq, kvr=kvr),
        grid=(B, n),
        in_specs=[pl.BlockSpec((tq, qcat.shape[1]), lambda b, i: (b * n + i, 0)),
                  pl.BlockSpec((S, kcat.shape[1]), lambda b, i: (b, 0)),
                  pl.BlockSpec((n, kcat_t.shape[1], tq), lambda b, i: (b, 0, 0)),
                  pl.BlockSpec(lw_['w_uv_rows_t'].shape, full), pl.BlockSpec((mw, 1), full)],
        out_specs=pl.BlockSpec((tq, mw), lambda b, i: (b * n + i, 0)),
        out_shape=jax.ShapeDtypeStruct((B * S, mw), BF16),
        scratch_shapes=[pltpu.VMEM((MLA_HEADS, 1, tq), F32),
                        pltpu.VMEM((MLA_HEADS, kcat_t.shape[1], tq), F32)],
        compiler_params=_cparams("parallel", "arbitrary"), name="attn_prompt",
    )(qcat, kcat, kcat_t, lw_['w_uv_rows_t'], lw_['n_b'].reshape(mw, 1))


def _attn_sample_kernel(pt_ref, q_ref, nckv_ref, nkpe_ref, ckv_hbm, kpe_hbm, wuv_ref, nb_ref, o_ref,
                        q_scr, s_scr, v_scr, ckv_buf, kpe_buf, sem, *, layer, n_pp, n_steps, sq, kvr, page):
    b = pl.program_id(0)
    nrow = MLA_HEADS * sq
    n_pages = n_steps * n_pp

    def page_copies(bb):
        slot = lax.rem(bb, 2)
        out = []
        for j in range(n_pages):
            pg = pt_ref[bb * n_pages + j]
            out.append(pltpu.make_async_copy(ckv_hbm.at[layer, pg], ckv_buf.at[slot, j], sem.at[0, slot]))
            out.append(pltpu.make_async_copy(kpe_hbm.at[layer, pg], kpe_buf.at[slot, j], sem.at[1, slot]))
        return out

    @pl.when(b == 0)
    def _():
        for c in page_copies(b):
            c.start()

    @pl.when(b + 1 < pl.num_programs(0))
    def _():
        for c in page_copies(b + 1):
            c.start()

    for h in range(MLA_HEADS):
        q_scr[h * sq:(h + 1) * sq, :] = q_ref[:, 2 * h * LANES:(2 * h + 2) * LANES].astype(F32)
    qa = q_scr[:, :kvr].astype(BF16)
    qp = q_scr[:, kvr:kvr + QK_ROPE].astype(BF16)

    for c in page_copies(b):
        c.wait()
    slot = lax.rem(b, 2)
    gp = SCORE_PAGES
    for j in range(0, n_pages, gp):
        ck = ckv_buf[slot, j:j + gp].reshape(gp * page, kvr).astype(BF16)
        kp = jnp.concatenate([kpe_buf[slot, j + t] for t in range(gp)], axis=1).astype(BF16)
        v_scr[j * page:(j + gp) * page, :] = ck
        s_scr[:, j * page:(j + gp) * page] = _dot_nt(qa, ck) + _dot(qp, kp)

    zpad = lambda a: jnp.concatenate([a, jnp.zeros((page - sq, a.shape[1]), F32)], axis=0)
    ck_new = zpad(nckv_ref[...]).astype(BF16)
    kp_new = zpad(nkpe_ref[...]).astype(BF16)
    tok = lax.broadcasted_iota(jnp.int32, (nrow, page), 0) & (sq - 1)
    key = lax.broadcasted_iota(jnp.int32, (nrow, page), 1)
    s_new = jnp.where(key <= tok, _dot_nt(qa, ck_new) + _dot_nt(qp, kp_new), NEG_BIG)
    tile = lambda i: s_scr[:, i * page:(i + 1) * page]
    m = jnp.max(_tree(jnp.maximum, [tile(i) for i in range(n_pages)] + [s_new]), axis=-1, keepdims=True)
    p_new = jnp.exp2(s_new - m)
    acc = _dot(p_new.astype(BF16), ck_new)
    psum = p_new
    vp = VALUE_PAGES
    for i in range(0, n_pages, vp):
        p = jnp.exp2(s_scr[:, i * page:(i + vp) * page] - m)
        psum = psum + _tree(jnp.add, [p[:, t * page:(t + 1) * page] for t in range(vp)])
        acc = acc + _dot(p.astype(BF16), v_scr[i * page:(i + vp) * page, :])
    l = jnp.sum(psum, axis=-1, keepdims=True)
    o_lat = (acc / l).astype(BF16)
    mw = wuv_ref.shape[1]
    r = _dot(o_lat, wuv_ref[...])
    rh = lax.broadcasted_iota(jnp.int32, (nrow, mw), 0) >> int(math.log2(sq))
    ch = lax.broadcasted_iota(jnp.int32, (nrow, mw), 1) >> int(math.log2(V_HEAD))
    r = jnp.where(rh == ch, r, 0.0)
    out = r[0:sq, :]
    for h in range(1, MLA_HEADS):
        out = out + r[h * sq:(h + 1) * sq, :]
    o_ref[...] = _rms(out, nb_ref[...])


def _attn_sample_call(page_table, qs, nckv, nkpe, cache_ckv, cache_kpe_t, layer, lw_, kvr):
    B, sq, qw = qs.shape
    n_pages = page_table.shape[1]
    page = cache_ckv.shape[2]
    n_pp = math.gcd(PAGES_PER_STEP, n_pages)
    n_steps = n_pages // n_pp
    mw = lw_['w_uv_cat'].shape[1]
    nrow = MLA_HEADS * sq
    assert sq == SUBLANES and sq <= page and n_pages % SCORE_PAGES == 0 and n_pages % VALUE_PAGES == 0
    hbm = pl.BlockSpec(memory_space=pl.ANY)
    in_specs = [pl.BlockSpec((None, sq, qw), lambda b, pt: (b, 0, 0)),
                pl.BlockSpec((None, sq, kvr), lambda b, pt: (b, 0, 0)),
                pl.BlockSpec((None, sq, QK_ROPE), lambda b, pt: (b, 0, 0)),
                hbm, hbm,
                pl.BlockSpec(lw_['w_uv_cat'].shape, lambda b, pt: (0, 0)),
                pl.BlockSpec((1, mw), lambda b, pt: (0, 0))]
    grid_spec = pltpu.PrefetchScalarGridSpec(
        num_scalar_prefetch=1, grid=(B,), in_specs=in_specs,
        out_specs=pl.BlockSpec((None, sq, mw), lambda b, pt: (b, 0, 0)),
        scratch_shapes=[pltpu.VMEM((nrow, 2 * LANES), F32), pltpu.VMEM((nrow, n_pages * page), F32),
                        pltpu.VMEM((n_pages * page, kvr), BF16),
                        pltpu.VMEM((2, n_pages, page, kvr), F32), pltpu.VMEM((2, n_pages, QK_ROPE, page), F32),
                        pltpu.SemaphoreType.DMA((2, 2))])
    return pl.pallas_call(
        functools.partial(_attn_sample_kernel, layer=layer, n_pp=n_pp, n_steps=n_steps, sq=sq, kvr=kvr, page=page),
        grid_spec=grid_spec, out_shape=jax.ShapeDtypeStruct((B, sq, mw), F32),
        compiler_params=_cparams("arbitrary"), name="attn_sample",
    )(page_table.reshape(-1), qs, nckv, nkpe, cache_ckv, cache_kpe_t, lw_['w_uv_cat'], lw_['n_b'])


def _post_kernel(a_ref, b_ref, c_ref, x_ref, wo_ref, g2_ref, wr_ref, wrl_ref, br_ref, x1_ref, hn_ref, rt_ref, cnt_ref,
                 cnt_scr, *, widths):
    wa, wb, wc = widths

    @pl.when(pl.program_id(0) == 0)
    def _():
        cnt_scr[...] = jnp.zeros(cnt_scr.shape, F32)

    mix = _dot(a_ref[...], wo_ref[:wa, :])
    mix = mix + _dot(b_ref[...], wo_ref[wa:wa + wb, :])
    mix = mix + _dot(c_ref[...], wo_ref[wa + wb:, :])
    x1 = x_ref[...] + mix
    x1_ref[...] = x1
    hn = _rms(x1, g2_ref[...])
    hn_ref[...] = hn
    hn_hi = hn.astype(BF16)
    hn_lo = (hn - hn_hi.astype(F32)).astype(BF16)
    logits = (_dot(hn_hi, wr_ref[...]) + _dot(hn_hi, wrl_ref[...]) + _dot(hn_lo, wr_ref[...])) + br_ref[...]

    lane = lax.broadcasted_iota(jnp.int32, logits.shape, 1)
    lane_f = lane.astype(F32)
    big = float(LANES)
    is_g = lane < N_GROUPS
    gl = jnp.where(is_g, logits, NEG_BIG)
    gmax = jnp.max(gl, axis=-1, keepdims=True)
    gsel = jnp.min(jnp.where(gl == gmax, lane_f, big), axis=-1, keepdims=True)
    gprob = 1.0 / jnp.sum(jnp.where(is_g, jnp.exp(gl - gmax), 0.0), axis=-1, keepdims=True)
    lo = N_GROUPS + gsel * EXPERTS_PER_GROUP
    el = jnp.where((lane_f >= lo) & (lane_f < lo + EXPERTS_PER_GROUP), logits, NEG_BIG)
    v1 = jnp.max(el, axis=-1, keepdims=True)
    i1 = jnp.min(jnp.where(el == v1, lane_f, big), axis=-1, keepdims=True)
    el2 = jnp.where(lane_f == i1, NEG_BIG, el)
    v2 = jnp.max(el2, axis=-1, keepdims=True)
    i2 = jnp.min(jnp.where(el2 == v2, lane_f, big), axis=-1, keepdims=True)
    e21 = jnp.exp(v2 - v1)
    w1 = gprob / (1.0 + e21)
    w2 = w1 * e21
    e1 = i1 - N_GROUPS
    e2 = i2 - N_GROUPS

    oh1 = (lane_f == e1).astype(F32)
    oh2 = (lane_f == e2).astype(F32)
    both = oh1 + oh2
    tm = both.shape[0]
    tri = (lax.broadcasted_iota(jnp.int32, (tm, tm), 0) >= lax.broadcasted_iota(jnp.int32, (tm, tm), 1))
    incl = _dot(tri.astype(BF16), both.astype(BF16))
    base = cnt_scr[0:1, :] + incl - both
    r1 = jnp.sum(oh1 * base, axis=-1, keepdims=True)
    r2 = jnp.sum(oh2 * base, axis=-1, keepdims=True)
    cnt_scr[...] = cnt_scr[...] + incl[tm - 1:tm, :]
    cnt_ref[...] = cnt_scr[...]

    rt = jnp.where(lane == 0, e1, 0.0)
    rt = jnp.where(lane == 1, e2, rt)
    rt = jnp.where(lane == 2, w1, rt)
    rt = jnp.where(lane == 3, w2, rt)
    rt = jnp.where(lane == 4, r1, rt)
    rt = jnp.where(lane == 5, r2, rt)
    rt_ref[...] = rt


def _post_call(a, b, c, x, lw_):
    T, D = x.shape
    tm = TOKEN_TILE
    row = lambda i: (i, 0)
    full = lambda i: (0, 0)
    widths = (a.shape[1], b.shape[1], c.shape[1])
    ws = [lw_['w_out'], lw_['g2'], lw_['w_r'], lw_['w_r_lo'], lw_['b_r']]
    out_shape = (jax.ShapeDtypeStruct((T, D), F32), jax.ShapeDtypeStruct((T, D), F32),
                 jax.ShapeDtypeStruct((T, LANES), F32), jax.ShapeDtypeStruct((SUBLANES, LANES), F32))
    return pl.pallas_call(
        functools.partial(_post_kernel, widths=widths),
        grid=(T // tm,),
        in_specs=[pl.BlockSpec((tm, w), row) for w in widths] + [pl.BlockSpec((tm, D), row)]
        + [pl.BlockSpec(w.shape, full) for w in ws],
        out_specs=tuple(pl.BlockSpec((tm, s.shape[1]), row) for s in out_shape[:3])
        + (pl.BlockSpec((SUBLANES, LANES), full),),
        out_shape=out_shape, scratch_shapes=[pltpu.VMEM((SUBLANES, LANES), F32)],
        compiler_params=_cparams("arbitrary"), name="post_proj",
    )(a, b, c, x, *ws)


def _dest_kernel(rt_ref, ps_ref, d_ref):
    rt = rt_ref[...]
    lane = lax.broadcasted_iota(jnp.int32, rt.shape, 1)
    lane_f = lane.astype(F32)
    pick = lambda k: jnp.sum(jnp.where(lane == k, rt, 0.0), axis=-1, keepdims=True)
    ps = ps_ref[...]
    d1 = jnp.sum(jnp.where(lane_f == pick(0), ps, 0.0), axis=-1, keepdims=True) + pick(2 * TOP_K)
    d2 = jnp.sum(jnp.where(lane_f == pick(1), ps, 0.0), axis=-1, keepdims=True) + pick(2 * TOP_K + 1)
    d_ref[...] = jnp.where(lane == 0, d1, jnp.where(lane == 1, d2, 0.0))


def _dest_call(route, pstart_row):
    T = route.shape[0]
    tm = TOKEN_TILE
    return pl.pallas_call(
        _dest_kernel, grid=(T // tm,),
        in_specs=[pl.BlockSpec((tm, LANES), lambda i: (i, 0)), pl.BlockSpec((1, LANES), lambda i: (0, 0))],
        out_specs=pl.BlockSpec((tm, LANES), lambda i: (i, 0)),
        out_shape=jax.ShapeDtypeStruct((T, LANES), F32), compiler_params=_cparams("parallel"), name="moe_dest",
    )(route, pstart_row)


def _moe_kernel(te_ref, nt_ref, tok_ref, hn_hbm, w_ref, wg_ref, wu_ref, wd_ref, y_ref, xbuf, sem):
    i = pl.program_id(0)
    tm = xbuf.shape[1]
    nt = nt_ref[0]

    def start_rows(t):
        slot = lax.rem(t, 2)

        def body(r, carry):
            row = tok_ref[t * tm + r]
            pltpu.make_async_copy(hn_hbm.at[pl.ds(row, 1), :], xbuf.at[slot, pl.ds(r, 1), :], sem.at[slot]).start()
            return carry

        lax.fori_loop(0, tm, body, 0, unroll=8)

    @pl.when(i == 0)
    def _():
        start_rows(i)

    @pl.when(i + 1 < nt)
    def _():
        start_rows(i + 1)

    @pl.when(i < nt)
    def _():
        slot = lax.rem(i, 2)
        pltpu.make_async_copy(hn_hbm.at[pl.ds(0, tm), :], xbuf.at[slot], sem.at[slot]).wait()
        x = xbuf[slot].astype(BF16)
        hg = _dot(x, wg_ref[...].astype(BF16))
        hu = _dot(x, wu_ref[...].astype(BF16))
        act = hg * jax.nn.sigmoid(hg) * hu * w_ref[...]
        y_ref[...] = _dot(act.astype(BF16), wd_ref[...].astype(BF16))

    @pl.when(i >= nt)
    def _():
        y_ref[...] = jnp.zeros(y_ref.shape, F32)


def _moe_call(hn, row_tok, row_w, tile_e, n_used, w_gate, w_up, w_down, layer):
    D = hn.shape[1]
    tm = MOE_TILE
    F = w_gate.shape[-1]
    n_tiles = tile_e.shape[0]
    grid_spec = pltpu.PrefetchScalarGridSpec(
        num_scalar_prefetch=3, grid=(n_tiles,),
        in_specs=[pl.BlockSpec(memory_space=pl.ANY),
                  pl.BlockSpec((tm, 1), lambda i, te, nt, tok: (i, 0)),
                  pl.BlockSpec((None, None, D, F), lambda i, te, nt, tok: (layer, te[i], 0, 0)),
                  pl.BlockSpec((None, None, D, F), lambda i, te, nt, tok: (layer, te[i], 0, 0)),
                  pl.BlockSpec((None, None, F, D), lambda i, te, nt, tok: (layer, te[i], 0, 0))],
        out_specs=pl.BlockSpec((tm, D), lambda i, te, nt, tok: (i, 0)),
        scratch_shapes=[pltpu.VMEM((2, tm, D), F32), pltpu.SemaphoreType.DMA((2,))])
    return pl.pallas_call(
        _moe_kernel, grid_spec=grid_spec, out_shape=jax.ShapeDtypeStruct((n_tiles * tm, D), F32),
        compiler_params=_cparams("arbitrary"), name="moe_experts",
    )(tile_e, n_used, row_tok, hn, row_w, w_gate, w_up, w_down)


def _route_meta(route, counts_f, tm, n_tiles):
    T = route.shape[0]
    n_assign = TOP_K * T
    e = route[:, :TOP_K].astype(jnp.int32).reshape(-1)
    w = route[:, TOP_K:2 * TOP_K].reshape(-1)
    order = jnp.argsort(e, stable=True).astype(jnp.int32)
    counts = counts_f.astype(jnp.int32)
    zero = jnp.zeros((1,), jnp.int32)
    start = jnp.concatenate([zero, jnp.cumsum(counts)])
    pstart = jnp.concatenate([zero, jnp.cumsum(((counts + tm - 1) // tm) * tm)])
    n_used = (pstart[N_EXPERTS] // tm).reshape(1)
    tile_lo = jnp.arange(n_tiles, dtype=jnp.int32) * tm
    tile_e = jnp.sum((pstart[None, 1:] <= tile_lo[:, None]).astype(jnp.int32), axis=1)
    tile_e = jnp.minimum(tile_e, N_EXPERTS - 1)
    oh = (tile_e[:, None] == jnp.arange(N_EXPERTS, dtype=jnp.int32)[None, :]).astype(jnp.int32)
    pick = lambda tab: jnp.sum(oh * tab[None, :N_EXPERTS], axis=1)
    k = (tile_lo - pick(pstart))[:, None] + jnp.arange(tm, dtype=jnp.int32)[None, :]
    valid = (k < pick(counts)[:, None]).reshape(-1)
    idx = jnp.clip(pick(start)[:, None] + k, 0, n_assign - 1).reshape(-1)
    src = order[idx]
    row_tok = jnp.where(valid, src // TOP_K, 0)
    row_w = jnp.where(valid, w[src], 0.0)
    ps_row = jnp.pad(pstart[:N_EXPERTS].astype(F32), (0, LANES - N_EXPERTS)).reshape(1, LANES)
    pos = _dest_call(route, ps_row)[:, :TOP_K].astype(jnp.int32)
    return row_tok, row_w.reshape(-1, 1), pos, tile_e, n_used


def _final_kernel(x_ref, g_ref, o_ref):
    o_ref[...] = _rms(x_ref[...], g_ref[...])


def _final_call(x, g):
    T, D = x.shape
    tm = TOKEN_TILE
    return pl.pallas_call(
        _final_kernel, grid=(T // tm,),
        in_specs=[pl.BlockSpec((tm, D), lambda i: (i, 0)), pl.BlockSpec((1, D), lambda i: (0, 0))],
        out_specs=pl.BlockSpec((tm, D), lambda i: (i, 0)),
        out_shape=jax.ShapeDtypeStruct((T, D), F32), compiler_params=_cparams("parallel"), name="final_norm",
    )(x, g)


def _block_diag(w):
    n, c, d = w.shape
    return jnp.einsum('ncd,nm->ncmd', w, jnp.eye(n, dtype=w.dtype)).reshape(n * c, n * d)


def _half_swap(w):
    half = w.shape[-1] // 2
    return jnp.concatenate([w[..., half:], w[..., :half]], axis=-1)


def _layer_weights(l, p, dims):
    lw, qr, kvr, gk_w, gv_w = dims
    D = p['w_in'].shape[1]
    row = lambda a: a.reshape(1, -1).astype(F32)
    w_in = p['w_in'][l]
    sizes = (lw, lw, qr, kvr, QK_ROPE, gk_w, gk_w, gv_w, p['gla_wa2'].shape[1], gv_w)
    x_lru, y_lru, c_q, c_kv, k_pe, g_q, g_k, g_v, g_a, g_o = jnp.split(w_in, list(np.cumsum(sizes)[:-1]), axis=1)
    tail_pad = LANES - 2 * QK_ROPE - g_a.shape[1]
    tail = jnp.concatenate([k_pe, _half_swap(k_pe), g_a, jnp.zeros((D, tail_pad), F32)], axis=1)
    w_in_p = jnp.concatenate([x_lru, y_lru, c_q, c_kv, g_q, g_k, g_v, g_o, tail], axis=1).astype(BF16)

    w_uq = p['w_uq'][l].reshape(qr, MLA_HEADS, QK_NOPE + QK_ROPE)
    nope = w_uq[:, :, :QK_NOPE].reshape(qr, MLA_HEADS * QK_NOPE)
    pe = w_uq[:, :, QK_NOPE:]
    widen = lambda a: jnp.pad(a, ((0, 0), (0, 0), (0, LANES - QK_ROPE))).reshape(qr, MLA_HEADS * LANES)
    w_q = jnp.concatenate([nope, widen(pe), widen(_half_swap(pe))], axis=1).astype(BF16)
    w_abs = _block_diag(jnp.transpose(p['w_uk'][l], (1, 2, 0))).astype(BF16)
    wa2 = jnp.zeros((LANES, gk_w), F32).at[2 * QK_ROPE:2 * QK_ROPE + g_a.shape[1]].set(p['gla_wa2'][l]).astype(BF16)

    w_uv = p['w_uv'][l]
    mw = MLA_HEADS * V_HEAD
    w_uv_cat = w_uv.reshape(kvr, mw)
    w_uv_rows = _block_diag(jnp.transpose(w_uv, (1, 0, 2)))
    n_a, n_b, n_c = jnp.split(p['out_norm_g'][l], [lw, lw + mw])
    n_r = N_GROUPS + N_EXPERTS
    w_r = jnp.concatenate([p['router_wg'][l], p['router_we'][l], jnp.zeros((D, LANES - n_r), F32)], axis=1)
    b_r = jnp.concatenate([p['router_bg'][l], p['router_be'][l], jnp.zeros((LANES - n_r,), F32)])
    return dict(
        g1=row(p['norm1_g'][l]), w_in=w_in_p, q_g=row(p['q_norm_g'][l]), w_q=w_q, w_abs=w_abs,
        kv_g=row(p['kv_norm_g'][l]), wa2=wa2, ba=row(p['gla_ba'][l]),
        conv_w=p['conv_w'][l], conv_b=row(p['conv_b'][l]),
        lru_wa=_block_diag(p['lru_wa'][l]).astype(BF16), lru_ba=row(p['lru_ba'][l]),
        lru_wi=_block_diag(p['lru_wi'][l]).astype(BF16), lru_bi=row(p['lru_bi'][l]),
        lru_lam=row(p['lru_lambda'][l]), n_a=row(n_a), n_b=row(n_b), n_c=row(n_c),
        gla_g=row(jnp.tile(p['gla_norm_g'][l], GLA_HEADS)),
        w_uv_cat=w_uv_cat.astype(BF16), w_uv_rows_t=w_uv_rows.T.astype(BF16),
        w_out=p['w_out'][l].astype(BF16), g2=row(p['norm2_g'][l]), w_r=w_r.astype(BF16),
        w_r_lo=(w_r - w_r.astype(BF16).astype(F32)).astype(BF16), b_r=row(b_r),
    )


def _rope_tables(positions):
    half = QK_ROPE // 2
    inv = ROPE_THETA ** (-np.arange(half, dtype=np.float64) / half)
    ang = np.asarray(positions, np.float64)[:, None] * inv
    zeros = np.zeros((ang.shape[0], LANES - QK_ROPE))
    cos = np.concatenate([np.cos(ang), np.cos(ang), zeros], axis=1)
    sin = np.concatenate([-np.sin(ang), np.sin(ang), zeros], axis=1)
    return jnp.asarray(cos, F32), jnp.asarray(sin, F32)


def _gla_state_to_rows(s):
    B = s.shape[0]
    eye = jnp.eye(GLA_HEADS, dtype=s.dtype)
    return jnp.einsum('bhde,hg->bhegd', s, eye).reshape(B, GLA_HEADS * GLA_DV, GLA_HEADS * GLA_DK)


def _gla_rows_to_state(st):
    B = st.shape[0]
    s5 = st.reshape(B, GLA_HEADS, GLA_DV, GLA_HEADS, GLA_DK)
    blocks = jnp.stack([s5[:, h, :, h, :] for h in range(GLA_HEADS)], axis=1)
    return jnp.swapaxes(blocks, -1, -2)


def kernel(x_prompt, x_sample, cache_ckv, cache_kpe, page_table, state_conv, state_lru, state_gla, norm1_g, w_in, conv_w, conv_b, lru_wa, lru_ba, lru_wi, lru_bi, lru_lambda, q_norm_g, w_uq, kv_norm_g, w_uk, w_uv, gla_wa2, gla_ba, gla_norm_g, out_norm_g, w_out, norm2_g, router_wg, router_bg, router_we, router_be, w_gate, w_up, w_down, final_norm_g):
    p = dict(norm1_g=norm1_g, w_in=w_in, conv_w=conv_w, conv_b=conv_b, lru_wa=lru_wa, lru_ba=lru_ba,
             lru_wi=lru_wi, lru_bi=lru_bi, lru_lambda=lru_lambda, q_norm_g=q_norm_g, w_uq=w_uq,
             kv_norm_g=kv_norm_g, w_uk=w_uk, w_uv=w_uv, gla_wa2=gla_wa2, gla_ba=gla_ba,
             gla_norm_g=gla_norm_g, out_norm_g=out_norm_g, w_out=w_out, norm2_g=norm2_g,
             router_wg=router_wg, router_bg=router_bg, router_we=router_we, router_be=router_be)
    Bp, Sp, D = x_prompt.shape
    Bs, Ss, _ = x_sample.shape
    depth = w_in.shape[0]
    lw = state_lru.shape[-1]
    qr = q_norm_g.shape[-1]
    kvr = kv_norm_g.shape[-1]
    gk_w = GLA_HEADS * GLA_DK
    gv_w = GLA_HEADS * GLA_DV
    dims = (lw, qr, kvr, gk_w, gv_w)
    assert kvr == LANES and gk_w == LANES
    Tp, Ts = Bp * Sp, Bs * Ss
    T = Tp + Ts
    past = page_table.shape[1] * cache_ckv.shape[2]

    cos_t, sin_t = _rope_tables(np.concatenate([np.tile(np.arange(Sp), Bp), np.tile(past + np.arange(Ss), Bs)]))
    cache_kpe_t = jnp.swapaxes(cache_kpe, 2, 3)
    n_tiles = (TOP_K * T + N_EXPERTS * (MOE_TILE - 1)) // MOE_TILE + 1

    x = jnp.concatenate([x_prompt.reshape(Tp, D), x_sample.reshape(Ts, D)], axis=0)
    outs = {k: [] for k in ('ckv_p', 'kpe_p', 'ckv_s', 'kpe_s', 'conv_p', 'conv_s', 'lru_p', 'lru_s', 'gla_p', 'gla_s')}
    for l in range(depth):
        lw_ = _layer_weights(l, p, dims)
        zl, qcat, kcat, kcat_t, ckv_n, kpe_n, g = _pre_call(x, lw_, cos_t, sin_t, dims)

        a_p, conv_p, lru_p = _lru_call(zl, 0, Bp, Sp, jnp.zeros((Bp, CONV_WIDTH - 1, lw), F32),
                                       jnp.zeros((Bp, lw), F32), lw_, BF16)
        a_s, lru_s = _lru_seg_call(zl, Tp, Bs, Ss, state_conv[l], state_lru[l], lw_)
        conv_s = zl[Tp:, :lw].reshape(Bs, Ss, lw)[:, Ss - (CONV_WIDTH - 1):]

        c_p, gla_p = _gla_call(g, 0, Bp, Sp, jnp.zeros((Bp, gv_w, gk_w), F32), lw_, BF16, gk_w, gv_w)
        c_s, gla_s = _gla_call(g, Tp, Bs, Ss, _gla_state_to_rows(state_gla[l]), lw_, BF16, gk_w, gv_w)

        b_p = _attn_prompt_call(qcat, kcat, kcat_t, Bp, Sp, lw_, kvr)
        b_s = _attn_sample_call(page_table, qcat[Tp:].reshape(Bs, Ss, -1), ckv_n[Tp:].reshape(Bs, Ss, kvr),
                                kpe_n[Tp:].reshape(Bs, Ss, QK_ROPE), cache_ckv, cache_kpe_t, l, lw_, kvr)

        a = jnp.concatenate([a_p, a_s], axis=0)
        b = jnp.concatenate([b_p, b_s.reshape(Ts, -1).astype(BF16)], axis=0)
        c = jnp.concatenate([c_p, c_s], axis=0)
        x1, hn, route, cnt = _post_call(a, b, c, x, lw_)

        row_tok, row_w, pos, tile_e, n_used = _route_meta(route, cnt[0, :N_EXPERTS], MOE_TILE, n_tiles)
        ys = _moe_call(hn, row_tok, row_w, tile_e, n_used, w_gate, w_up, w_down, l)
        x = x1 + ys[pos[:, 0]] + ys[pos[:, 1]]

        outs['ckv_p'].append(ckv_n[:Tp].reshape(Bp, Sp, kvr))
        outs['kpe_p'].append(kpe_n[:Tp].reshape(Bp, Sp, QK_ROPE))
        outs['ckv_s'].append(ckv_n[Tp:].reshape(Bs, Ss, kvr))
        outs['kpe_s'].append(kpe_n[Tp:].reshape(Bs, Ss, QK_ROPE))
        outs['conv_p'].append(conv_p)
        outs['conv_s'].append(conv_s)
        outs['lru_p'].append(lru_p.reshape(Bp, lw))
        outs['lru_s'].append(lru_s.reshape(Bs, lw))
        outs['gla_p'].append(_gla_rows_to_state(gla_p))
        outs['gla_s'].append(_gla_rows_to_state(gla_s))

    y = _final_call(x, final_norm_g.reshape(1, D))
    st = {k: jnp.stack(v) for k, v in outs.items()}
    return (y[:Tp].reshape(Bp, Sp, D), y[Tp:].reshape(Bs, Ss, D),
            st['ckv_p'], st['kpe_p'], st['ckv_s'], st['kpe_s'], st['conv_p'], st['conv_s'],
            st['lru_p'], st['lru_s'], st['gla_p'], st['gla_s'])
```

```python
import functools
import math

import numpy as np
import jax
import jax.numpy as jnp
from jax import lax
from jax.experimental import pallas as pl
from jax.experimental.pallas import tpu as pltpu

F32 = jnp.float32
BF16 = jnp.bfloat16

LRU_BLOCKS = 4
CONV_WIDTH = 4
LRU_C = 8.0
MLA_HEADS = 8
QK_NOPE = 64
QK_ROPE = 32
V_HEAD = 64
ROPE_THETA = 10000.0
GLA_HEADS = 4
GLA_DK = 32
GLA_DV = 64
GLA_TAU = 16.0
GLA_CHUNK = 16
N_GROUPS = 4
EXPERTS_PER_GROUP = 8
N_EXPERTS = N_GROUPS * EXPERTS_PER_GROUP
TOP_K = 2
EPS = 1e-6

LANES = 128
SUBLANES = 8
BF16_ROWS = 16
VMEM_LIMIT_BYTES = 56 * 1024 * 1024

TOKEN_TILE = 512
SEQ_TILE = 256
MOE_TILE = 256
SCORE_PAGES = 4
VALUE_PAGES = 2
PAGES_PER_STEP = 16
NEG_BIG = -1e30


def _cparams(*sem):
    return pltpu.CompilerParams(dimension_semantics=sem, vmem_limit_bytes=VMEM_LIMIT_BYTES)


def _rms(x, g):
    return x * lax.rsqrt(jnp.mean(x * x, axis=-1, keepdims=True) + EPS) * g


def _dot(a, b):
    return jnp.dot(a, b, preferred_element_type=F32)


def _dot_nt(a, b):
    return lax.dot_general(a, b, (((1,), (1,)), ((), ())), preferred_element_type=F32)


def _softplus(x):
    return jnp.maximum(x, 0.0) + jnp.log1p(jnp.exp(-jnp.abs(x)))


def _shift_rows(x, d, fill):
    row = lax.broadcasted_iota(jnp.int32, x.shape, 0)
    return jnp.where(row >= d, pltpu.roll(x, d, 0), fill)


def _pre_kernel(x_ref, g1_ref, win_ref, qg_ref, wq_ref, wabs_ref, kvg_ref, wa2_ref, ba_ref,
                cos_ref, sin_ref, zl_ref, q_ref, k_ref, kt_ref, ckv_ref, kpe_ref, g_ref, *, dims):
    lw, qr, kvr, gk_w, gv_w = dims
    xn = _rms(x_ref[...], g1_ref[...])
    z = _dot(xn.astype(BF16), win_ref[...])
    o = 2 * lw
    zl_ref[...] = z[:, :o]
    cq = z[:, o:o + qr]
    o += qr
    ckv = z[:, o:o + kvr]
    o += kvr
    gqk = z[:, o:o + 2 * gk_w]
    o += 2 * gk_w
    gvo = z[:, o:o + 2 * gv_w]
    o += 2 * gv_w
    tail = z[:, o:o + LANES]
    cos = cos_ref[...]
    sin = sin_ref[...]
    scale = (QK_NOPE + QK_ROPE) ** -0.5 * math.log2(math.e)

    ckv_n = _rms(ckv, kvg_ref[...])
    kpe = tail * cos + pltpu.roll(tail, LANES - QK_ROPE, 1) * sin
    ckv_ref[...] = ckv_n
    kpe_ref[...] = kpe[:, :QK_ROPE]
    k_ref[:, :kvr] = ckv_n.astype(BF16)
    k_ref[:, kvr:] = kpe.astype(BF16)
    ones = jnp.ones((kt_ref.shape[0] - kvr, ckv_n.shape[0]), F32)
    kt_ref[...] = jnp.concatenate([ckv_n.T, ones], axis=0).astype(BF16)

    cqn = _rms(cq, qg_ref[...]).astype(BF16)
    qall = _dot(cqn, wq_ref[...])
    n_nope = MLA_HEADS * QK_NOPE
    q_abs = _dot(qall[:, :n_nope].astype(BF16), wabs_ref[...]) * scale
    for h in range(MLA_HEADS):
        pe = qall[:, n_nope + h * LANES:n_nope + (h + 1) * LANES]
        sw = qall[:, n_nope + (MLA_HEADS + h) * LANES:n_nope + (MLA_HEADS + h + 1) * LANES]
        q_ref[:, 2 * h * LANES:(2 * h + 1) * LANES] = q_abs[:, h * kvr:(h + 1) * kvr].astype(BF16)
        q_ref[:, (2 * h + 1) * LANES:(2 * h + 2) * LANES] = ((pe * cos + sw * sin) * scale).astype(BF16)

    la_pre = _dot(tail.astype(BF16), wa2_ref[...]) + ba_ref[...]
    la = -_softplus(-la_pre) * (1.0 / GLA_TAU)
    g_ref[:, :gk_w] = gqk[:, :gk_w] * (GLA_DK ** -0.5)
    g_ref[:, gk_w:2 * gk_w] = gqk[:, gk_w:]
    g_ref[:, 2 * gk_w:3 * gk_w] = la
    g_ref[:, 3 * gk_w:] = gvo


def _pre_call(x, lw_, cos_t, sin_t, dims):
    T, D = x.shape
    lw, qr, kvr, gk_w, gv_w = dims
    tm = TOKEN_TILE
    assert T % tm == 0
    row = lambda i: (i, 0)
    full = lambda i: (0, 0)
    wspec = lambda a: pl.BlockSpec(a.shape, full)
    ins = [x, lw_['g1'], lw_['w_in'], lw_['q_g'], lw_['w_q'], lw_['w_abs'], lw_['kv_g'], lw_['wa2'], lw_['ba']]
    in_specs = [pl.BlockSpec((tm, D), row)] + [wspec(a) for a in ins[1:]]
    in_specs += [pl.BlockSpec((tm, LANES), row), pl.BlockSpec((tm, LANES), row)]
    out_shape = (
        jax.ShapeDtypeStruct((T, 2 * lw), F32),
        jax.ShapeDtypeStruct((T, 2 * LANES * MLA_HEADS), BF16),
        jax.ShapeDtypeStruct((T, 2 * LANES), BF16),
        jax.ShapeDtypeStruct((T // tm, kvr + BF16_ROWS, tm), BF16),
        jax.ShapeDtypeStruct((T, kvr), F32),
        jax.ShapeDtypeStruct((T, QK_ROPE), F32),
        jax.ShapeDtypeStruct((T, 3 * gk_w + 2 * gv_w), F32),
    )
    out_specs = tuple(pl.BlockSpec((None, s.shape[1], tm), lambda i: (i, 0, 0)) if len(s.shape) == 3
                      else pl.BlockSpec((tm, s.shape[1]), row) for s in out_shape)
    return pl.pallas_call(
        functools.partial(_pre_kernel, dims=dims),
        grid=(T // tm,), in_specs=in_specs, out_specs=out_specs, out_shape=out_shape,
        compiler_params=_cparams("parallel"), name="pre_proj",
    )(*ins, cos_t, sin_t)


def _lru_kernel(zl_ref, cbuf_ref, h0_ref, cw_ref, cb_ref, wa_ref, ba_ref, wi_ref, bi_ref, lam_ref, na_ref,
                a_ref, conv_ref, hout_ref, xbuf, hcar, *, ts, lw):
    i = pl.program_id(1)
    last = pl.num_programs(1) - 1
    pad = SUBLANES
    nbuf = CONV_WIDTH - 1

    @pl.when(i == 0)
    def _():
        xbuf[0:pad, :] = jnp.zeros((pad, lw), F32)
        xbuf[pad - nbuf:pad, :] = cbuf_ref[...]
        hcar[...] = h0_ref[...]

    x = zl_ref[:, :lw]
    y = zl_ref[:, lw:]
    xbuf[pad:pad + ts, :] = x
    xc = cb_ref[...] + cw_ref[nbuf:nbuf + 1, :] * x
    for k in range(nbuf):
        xc = xc + cw_ref[k:k + 1, :] * xbuf[pad - nbuf + k:pad - nbuf + k + ts, :]

    @pl.when(i == last)
    def _():
        conv_ref[...] = xbuf[pad + ts - nbuf:pad + ts, :]

    xbuf[0:pad, :] = xbuf[ts:ts + pad, :]

    xb = xc.astype(BF16)
    r = jax.nn.sigmoid(_dot(xb, wa_ref[...]) + ba_ref[...])
    gi = jax.nn.sigmoid(_dot(xb, wi_ref[...]) + bi_ref[...])
    log_a = (-LRU_C) * r * _softplus(-lam_ref[...])
    a = jnp.exp(log_a)
    th = jnp.tanh(log_a)
    u = jnp.sqrt(-2.0 * th / (1.0 - th)) * (gi * xc)

    d = 1
    while d < ts:
        u = a * _shift_rows(u, d, 0.0) + u
        a = a * _shift_rows(a, d, 1.0)
        d *= 2
    h = a * hcar[...] + u
    hcar[...] = h[ts - 1:ts, :]

    @pl.when(i == last)
    def _():
        hout_ref[...] = h[ts - 1:ts, :]

    out_a = h * jax.nn.gelu(y)
    a_ref[...] = _rms(out_a, na_ref[...]).astype(a_ref.dtype)


def _lru_call(zl, row_off, B, S, cbuf, h0, lw_, out_dtype):
    lw = h0.shape[-1]
    ts = min(S, SEQ_TILE)
    assert S % ts == 0 and row_off % ts == 0 and S >= CONV_WIDTH - 1
    n = S // ts
    off = row_off // ts
    full = lambda b, i: (0, 0)
    ws = [lw_['conv_w'], lw_['conv_b'], lw_['lru_wa'], lw_['lru_ba'], lw_['lru_wi'], lw_['lru_bi'],
          lw_['lru_lam'], lw_['n_a']]
    in_specs = [
        pl.BlockSpec((ts, 2 * lw), lambda b, i: (off + b * n + i, 0)),
        pl.BlockSpec((None, CONV_WIDTH - 1, lw), lambda b, i: (b, 0, 0)),
        pl.BlockSpec((None, 1, lw), lambda b, i: (b, 0, 0)),
    ] + [pl.BlockSpec(a.shape, full) for a in ws]
    out_shape = (
        jax.ShapeDtypeStruct((B * S, lw), out_dtype),
        jax.ShapeDtypeStruct((B, CONV_WIDTH - 1, lw), F32),
        jax.ShapeDtypeStruct((B, 1, lw), F32),
    )
    out_specs = (
        pl.BlockSpec((ts, lw), lambda b, i: (b * n + i, 0)),
        pl.BlockSpec((None, CONV_WIDTH - 1, lw), lambda b, i: (b, 0, 0)),
        pl.BlockSpec((None, 1, lw), lambda b, i: (b, 0, 0)),
    )
    return pl.pallas_call(
        functools.partial(_lru_kernel, ts=ts, lw=lw),
        grid=(B, n), in_specs=in_specs, out_specs=out_specs, out_shape=out_shape,
        scratch_shapes=[pltpu.VMEM((ts + SUBLANES, lw), F32), pltpu.VMEM((1, lw), F32)],
        compiler_params=_cparams("parallel", "arbitrary"), name="rg_lru",
    )(zl, cbuf, h0.reshape(B, 1, lw), *ws)


def _lru_seg_kernel(zl_ref, hist_ref, h0_ref, cw_ref, cb_ref, wa_ref, ba_ref, wi_ref, bi_ref, lam_ref, na_ref,
                    a_ref, h_ref, *, seg, lw):
    nbuf = CONV_WIDTH - 1
    x = zl_ref[:, :lw]
    y = zl_ref[:, lw:]
    n = x.shape[0]
    pos = lax.broadcasted_iota(jnp.int32, (n, 1), 0) & (seg - 1)
    hist = hist_ref[...]
    xc = cb_ref[...] + cw_ref[nbuf:nbuf + 1, :] * x
    for k in range(nbuf):
        j = nbuf - k
        prev = jnp.where(pos >= j, pltpu.roll(x, j, 0), pltpu.roll(hist, (j - seg) % n, 0))
        xc = xc + cw_ref[k:k + 1, :] * prev
    xb = xc.astype(BF16)
    r = jax.nn.sigmoid(_dot(xb, wa_ref[...]) + ba_ref[...])
    gi = jax.nn.sigmoid(_dot(xb, wi_ref[...]) + bi_ref[...])
    log_a = (-LRU_C) * r * _softplus(-lam_ref[...])
    a = jnp.exp(log_a)
    th = jnp.tanh(log_a)
    u = jnp.sqrt(-2.0 * th / (1.0 - th)) * (gi * xc)
    d = 1
    while d < seg:
        u = a * jnp.where(pos >= d, pltpu.roll(u, d, 0), 0.0) + u
        a = a * jnp.where(pos >= d, pltpu.roll(a, d, 0), 1.0)
        d *= 2
    h = a * h0_ref[...] + u
    h_ref[...] = h
    a_ref[...] = _rms(h * jax.nn.gelu(y), na_ref[...]).astype(a_ref.dtype)


def _lru_seg_call(zl, row_off, B, S, cbuf, h0, lw_):
    lw = h0.shape[-1]
    n = B * S
    tm = min(n, TOKEN_TILE)
    assert S & (S - 1) == 0 and S >= CONV_WIDTH - 1 and tm % S == 0 and n % tm == 0 and row_off % tm == 0
    off = row_off // tm
    hist = jnp.pad(cbuf, ((0, 0), (S - (CONV_WIDTH - 1), 0), (0, 0))).reshape(n, lw)
    h0_rows = jnp.repeat(h0, S, axis=0)
    full = lambda i: (0, 0)
    row = lambda i: (i, 0)
    ws = [lw_['conv_w'], lw_['conv_b'], lw_['lru_wa'], lw_['lru_ba'], lw_['lru_wi'], lw_['lru_bi'],
          lw_['lru_lam'], lw_['n_a']]
    a, h = pl.pallas_call(
        functools.partial(_lru_seg_kernel, seg=S, lw=lw),
        grid=(n // tm,),
        in_specs=[pl.BlockSpec((tm, 2 * lw), lambda i: (off + i, 0)), pl.BlockSpec((tm, lw), row),
                  pl.BlockSpec((tm, lw), row)] + [pl.BlockSpec(w.shape, full) for w in ws],
        out_specs=(pl.BlockSpec((tm, lw), row), pl.BlockSpec((tm, lw), row)),
        out_shape=(jax.ShapeDtypeStruct((n, lw), BF16), jax.ShapeDtypeStruct((n, lw), F32)),
        compiler_params=_cparams("parallel"), name="rg_lru_seg",
    )(zl, hist, h0_rows, *ws)
    return a, h.reshape(B, S, lw)[:, S - 1]


def _gla_kernel(g_ref, s0_ref, gn_ref, nc_ref, c_ref, sout_ref, st, *, ts, nseq, gk_w, gv_w):
    i = pl.program_id(1)
    last = pl.num_programs(1) - 1
    C = GLA_CHUNK
    rows = min(ts, C)
    n_chunks = max(ts // C, 1)
    log2c = int(math.log2(C))

    @pl.when(i == 0)
    def _():
        st[...] = s0_ref[...]

    dk_sh, dv_sh = int(math.log2(GLA_DK)), int(math.log2(GLA_DV))
    hd = lax.broadcasted_iota(jnp.int32, (gk_w, gv_w), 0) >> dk_sh
    he = lax.broadcasted_iota(jnp.int32, (gk_w, gv_w), 1) >> dv_sh
    same = (hd == he).astype(BF16)
    he_t = lax.broadcasted_iota(jnp.int32, (gv_w, gk_w), 0) >> dv_sh
    hd_t = lax.broadcasted_iota(jnp.int32, (gv_w, gk_w), 1) >> dk_sh
    same_t = (he_t == hd_t).astype(F32)
    sel_t = lax.broadcasted_iota(jnp.int32, (C, C * C), 0)
    sel_r = lax.broadcasted_iota(jnp.int32, (C, C * C), 1) >> log2c
    sel = (sel_t == sel_r).astype(BF16)
    srow = lax.broadcasted_iota(jnp.int32, (C, 1), 0)

    seq_out = []
    for sq in range(nseq):
        blk = g_ref[sq * ts:(sq + 1) * ts, :]
        if rows < C:
            blk = jnp.concatenate([blk, jnp.zeros((C - rows, blk.shape[1]), F32)], axis=0)
        q = blk[:, :gk_w]
        k = blk[:, gk_w:2 * gk_w]
        la = blk[:, 2 * gk_w:3 * gk_w]
        v = blk[:, 3 * gk_w:3 * gk_w + gv_w]
        nrow = n_chunks * C
        pos = lax.broadcasted_iota(jnp.int32, (nrow, 1), 0) & (C - 1)
        cum = la
        d = 1
        while d < C:
            cum = cum + jnp.where(pos >= d, pltpu.roll(cum, d, 0), 0.0)
            d *= 2
        tot = jnp.where(pos == C - 1, cum, 0.0)
        d = 1
        while d < C:
            tot = tot + jnp.where(pos < C - d, pltpu.roll(tot, nrow - d, 0), 0.0)
            d *= 2
        qe = (q * jnp.exp(cum)).astype(BF16)
        kdec = (k * jnp.exp(tot - cum)).astype(BF16)
        dec = jnp.exp(tot)
        vb = v.astype(BF16)

        o_intra, upd = [], []
        for c in range(n_chunks):
            sl = slice(c * C, (c + 1) * C)
            cum_c, q_c, k_c, v_c = cum[sl], q[sl], k[sl], v[sl]
            pieces = []
            for t in range(C):
                diff = jnp.where(srow <= t, cum_c[t:t + 1, :] - cum_c, NEG_BIG)
                pieces.append(q_c[t:t + 1, :] * k_c * jnp.exp(diff))
            w = jnp.concatenate(pieces, axis=0)
            att = _dot(w.astype(BF16), same)
            xv = att * jnp.concatenate([v_c] * C, axis=0)
            o_intra.append(_dot(sel, xv.astype(BF16)))
            upd.append(lax.dot_general(vb[sl], kdec[sl], (((0,), (0,)), ((), ())),
                                       preferred_element_type=F32) * same_t)

        s_t = st[sq]
        outs = []
        for c in range(n_chunks):
            sl = slice(c * C, (c + 1) * C)
            outs.append(o_intra[c] + _dot_nt(qe[sl], s_t.astype(BF16)))
            s_t = s_t * dec[c * C:c * C + 1, :] + upd[c]
        st[sq] = s_t
        seq_out.append(jnp.concatenate(outs, axis=0)[:ts] if n_chunks > 1 else outs[0][:ts])

    o = jnp.concatenate(seq_out, axis=0) if nseq > 1 else seq_out[0]

    @pl.when(i == last)
    def _():
        sout_ref[...] = st[...]

    go = g_ref[:, 3 * gk_w + gv_w:]
    e64 = (lax.broadcasted_iota(jnp.int32, (gv_w, gv_w), 0) >> dv_sh
           == lax.broadcasted_iota(jnp.int32, (gv_w, gv_w), 1) >> dv_sh).astype(BF16)
    osq = o * o
    osq_hi = osq.astype(BF16)
    osq_lo = (osq - osq_hi.astype(F32)).astype(BF16)
    ms = (_dot(osq_hi, e64) + _dot(osq_lo, e64)) * (1.0 / GLA_DV)
    out_c = o * lax.rsqrt(ms + EPS) * gn_ref[...] * (go * jax.nn.sigmoid(go))
    c_ref[...] = _rms(out_c, nc_ref[...]).astype(c_ref.dtype)


def _gla_call(g, row_off, B, S, s0t, lw_, out_dtype, gk_w, gv_w):
    ts = min(S, SEQ_TILE)
    nseq = math.gcd(B, max(SEQ_TILE // (2 * ts), 1)) if ts == S else 1
    rows = nseq * ts
    assert S % ts == 0 and row_off % rows == 0 and (ts % GLA_CHUNK == 0 or ts < GLA_CHUNK)
    n = S // ts
    off = row_off // rows
    gw = g.shape[1]
    full = lambda b, i: (0, 0)
    out_shape = (jax.ShapeDtypeStruct((B * S, gv_w), out_dtype),
                 jax.ShapeDtypeStruct((B, gv_w, gk_w), F32))
    return pl.pallas_call(
        functools.partial(_gla_kernel, ts=ts, nseq=nseq, gk_w=gk_w, gv_w=gv_w),
        grid=(B // nseq, n),
        in_specs=[pl.BlockSpec((rows, gw), lambda b, i: (off + b * n + i, 0)),
                  pl.BlockSpec((nseq, gv_w, gk_w), lambda b, i: (b, 0, 0)),
                  pl.BlockSpec((1, gv_w), full), pl.BlockSpec((1, gv_w), full)],
        out_specs=(pl.BlockSpec((rows, gv_w), lambda b, i: (b * n + i, 0)),
                   pl.BlockSpec((nseq, gv_w, gk_w), lambda b, i: (b, 0, 0))),
        out_shape=out_shape,
        scratch_shapes=[pltpu.VMEM((nseq, gv_w, gk_w), F32)],
        compiler_params=_cparams("parallel", "arbitrary"), name="gla",
    )(g, s0t, lw_['gla_g'], lw_['n_c'])


def _tree(op, xs):
    xs = list(xs)
    while len(xs) > 1:
        xs = [op(xs[i], xs[i + 1]) if i + 1 < len(xs) else xs[i] for i in range(0, len(xs), 2)]
    return xs[0]


def _attn_prompt_kernel(q_ref, k_ref, kt_ref, wuvt_ref, nb_ref, o_ref, m_scr, acc_scr, *, tq, kvr):
    i = pl.program_id(1)
    kw = 2 * LANES
    m_scr[...] = jnp.full(m_scr.shape, NEG_BIG, F32)
    acc_scr[...] = jnp.zeros(acc_scr.shape, F32)

    def block(j, masked):
        kb = k_ref[pl.ds(pl.multiple_of(j * tq, tq), tq), :]
        kbt = kt_ref[j]
        if masked:
            key = lax.broadcasted_iota(jnp.int32, (tq, tq), 0)
            qry = lax.broadcasted_iota(jnp.int32, (tq, tq), 1)
            keep = key <= qry
        qk = lambda h: _dot_nt(kb, q_ref[:, h * kw:(h + 1) * kw])
        st_next = qk(0)
        for h in range(MLA_HEADS):
            st = st_next
            if h + 1 < MLA_HEADS:
                st_next = qk(h + 1)
            if masked:
                st = jnp.where(keep, st, NEG_BIG)
            m_old = m_scr[h]
            m_new = jnp.maximum(m_old, jnp.max(st, axis=0, keepdims=True))
            alpha = jnp.exp2(m_old - m_new)
            pt = jnp.exp2(st - m_new).astype(BF16)
            acc_scr[h] = alpha * acc_scr[h] + _dot(kbt, pt)
            m_scr[h] = m_new

    def body(j, carry):
        block(j, False)
        return carry

    lax.fori_loop(0, i, body, 0)
    block(i, True)

    out_t = jnp.zeros((wuvt_ref.shape[0], tq), F32)
    for h in range(MLA_HEADS):
        acc = acc_scr[h]
        o_t = (acc[:kvr, :] / acc[kvr:kvr + 1, :]).astype(BF16)
        out_t = out_t + _dot(wuvt_ref[:, h * kvr:(h + 1) * kvr], o_t)
    ms = jnp.mean(out_t * out_t, axis=0, keepdims=True)
    out_t = out_t * lax.rsqrt(ms + EPS) * nb_ref[...]
    o_ref[...] = out_t.T.astype(o_ref.dtype)


def _attn_prompt_call(qcat, kcat, kcat_t, B, S, lw_, kvr):
    tq = kcat_t.shape[2]
    assert S % tq == 0
    n = S // tq
    mw = lw_['w_uv_rows_t'].shape[0]
    full = lambda b, i: (0, 0)
    return pl.pallas_call(
        functools.partial(_attn_prompt_kernel, tq=tq, kvr=kvr),
        grid=(B, n),
        in_specs=[pl.BlockSpec((tq, qcat.shape[1]), lambda b, i: (b * n + i, 0)),
                  pl.BlockSpec((S, kcat.shape[1]), lambda b, i: (b, 0)),
                  pl.BlockSpec((n, kcat_t.shape[1], tq), lambda b, i: (b, 0, 0)),
                  pl.BlockSpec(lw_['w_uv_rows_t'].shape, full), pl.BlockSpec((mw, 1), full)],
        out_specs=pl.BlockSpec((tq, mw), lambda b, i: (b * n + i, 0)),
        out_shape=jax.ShapeDtypeStruct((B * S, mw), BF16),
        scratch_shapes=[pltpu.VMEM((MLA_HEADS, 1, tq), F32),
                        pltpu.VMEM((MLA_HEADS, kcat_t.shape[1], tq), F32)],
        compiler_params=_cparams("parallel", "arbitrary"), name="attn_prompt",
    )(qcat, kcat, kcat_t, lw_['w_uv_rows_t'], lw_['n_b'].reshape(mw, 1))


def _attn_sample_kernel(pt_ref, q_ref, nckv_ref, nkpe_ref, ckv_hbm, kpe_hbm, wuv_ref, nb_ref, o_ref,
                        q_scr, s_scr, v_scr, ckv_buf, kpe_buf, sem, *, layer, n_pp, n_steps, sq, kvr, page):
    b = pl.program_id(0)
    nrow = MLA_HEADS * sq
    n_pages = n_steps * n_pp

    def page_copies(bb):
        slot = lax.rem(bb, 2)
        out = []
        for j in range(n_pages):
            pg = pt_ref[bb * n_pages + j]
            out.append(pltpu.make_async_copy(ckv_hbm.at[layer, pg], ckv_buf.at[slot, j], sem.at[0, slot]))
            out.append(pltpu.make_async_copy(kpe_hbm.at[layer, pg], kpe_buf.at[slot, j], sem.at[1, slot]))
        return out

    @pl.when(b == 0)
    def _():
        for c in page_copies(b):
            c.start()

    @pl.when(b + 1 < pl.num_programs(0))
    def _():
        for c in page_copies(b + 1):
            c.start()

    for h in range(MLA_HEADS):
        q_scr[h * sq:(h + 1) * sq, :] = q_ref[:, 2 * h * LANES:(2 * h + 2) * LANES].astype(F32)
    qa = q_scr[:, :kvr].astype(BF16)
    qp = q_scr[:, kvr:kvr + QK_ROPE].astype(BF16)

    for c in page_copies(b):
        c.wait()
    slot = lax.rem(b, 2)
    gp = SCORE_PAGES
    for j in range(0, n_pages, gp):
        ck = ckv_buf[slot, j:j + gp].reshape(gp * page, kvr).astype(BF16)
        kp = jnp.concatenate([kpe_buf[slot, j + t] for t in range(gp)], axis=1).astype(BF16)
        v_scr[j * page:(j + gp) * page, :] = ck
        s_scr[:, j * page:(j + gp) * page] = _dot_nt(qa, ck) + _dot(qp, kp)

    zpad = lambda a: jnp.concatenate([a, jnp.zeros((page - sq, a.shape[1]), F32)], axis=0)
    ck_new = zpad(nckv_ref[...]).astype(BF16)
    kp_new = zpad(nkpe_ref[...]).astype(BF16)
    tok = lax.broadcasted_iota(jnp.int32, (nrow, page), 0) & (sq - 1)
    key = lax.broadcasted_iota(jnp.int32, (nrow, page), 1)
    s_new = jnp.where(key <= tok, _dot_nt(qa, ck_new) + _dot_nt(qp, kp_new), NEG_BIG)
    tile = lambda i: s_scr[:, i * page:(i + 1) * page]
    m = jnp.max(_tree(jnp.maximum, [tile(i) for i in range(n_pages)] + [s_new]), axis=-1, keepdims=True)
    p_new = jnp.exp2(s_new - m)
    acc = _dot(p_new.astype(BF16), ck_new)
    psum = p_new
    vp = VALUE_PAGES
    for i in range(0, n_pages, vp):
        p = jnp.exp2(s_scr[:, i * page:(i + vp) * page] - m)
        psum = psum + _tree(jnp.add, [p[:, t * page:(t + 1) * page] for t in range(vp)])
        acc = acc + _dot(p.astype(BF16), v_scr[i * page:(i + vp) * page, :])
    l = jnp.sum(psum, axis=-1, keepdims=True)
    o_lat = (acc / l).astype(BF16)
    mw = wuv_ref.shape[1]
    r = _dot(o_lat, wuv_ref[...])
    rh = lax.broadcasted_iota(jnp.int32, (nrow, mw), 0) >> int(math.log2(sq))
    ch = lax.broadcasted_iota(jnp.int32, (nrow, mw), 1) >> int(math.log2(V_HEAD))
    r = jnp.where(rh == ch, r, 0.0)
    out = r[0:sq, :]
    for h in range(1, MLA_HEADS):
        out = out + r[h * sq:(h + 1) * sq, :]
    o_ref[...] = _rms(out, nb_ref[...])


def _attn_sample_call(page_table, qs, nckv, nkpe, cache_ckv, cache_kpe_t, layer, lw_, kvr):
    B, sq, qw = qs.shape
    n_pages = page_table.shape[1]
    page = cache_ckv.shape[2]
    n_pp = math.gcd(PAGES_PER_STEP, n_pages)
    n_steps = n_pages // n_pp
    mw = lw_['w_uv_cat'].shape[1]
    nrow = MLA_HEADS * sq
    assert sq == SUBLANES and sq <= page and n_pages % SCORE_PAGES == 0 and n_pages % VALUE_PAGES == 0
    hbm = pl.BlockSpec(memory_space=pl.ANY)
    in_specs = [pl.BlockSpec((None, sq, qw), lambda b, pt: (b, 0, 0)),
                pl.BlockSpec((None, sq, kvr), lambda b, pt: (b, 0, 0)),
                pl.BlockSpec((None, sq, QK_ROPE), lambda b, pt: (b, 0, 0)),
                hbm, hbm,
                pl.BlockSpec(lw_['w_uv_cat'].shape, lambda b, pt: (0, 0)),
                pl.BlockSpec((1, mw), lambda b, pt: (0, 0))]
    grid_spec = pltpu.PrefetchScalarGridSpec(
        num_scalar_prefetch=1, grid=(B,), in_specs=in_specs,
        out_specs=pl.BlockSpec((None, sq, mw), lambda b, pt: (b, 0, 0)),
        scratch_shapes=[pltpu.VMEM((nrow, 2 * LANES), F32), pltpu.VMEM((nrow, n_pages * page), F32),
                        pltpu.VMEM((n_pages * page, kvr), BF16),
                        pltpu.VMEM((2, n_pages, page, kvr), F32), pltpu.VMEM((2, n_pages, QK_ROPE, page), F32),
                        pltpu.SemaphoreType.DMA((2, 2))])
    return pl.pallas_call(
        functools.partial(_attn_sample_kernel, layer=layer, n_pp=n_pp, n_steps=n_steps, sq=sq, kvr=kvr, page=page),
        grid_spec=grid_spec, out_shape=jax.ShapeDtypeStruct((B, sq, mw), F32),
        compiler_params=_cparams("arbitrary"), name="attn_sample",
    )(page_table.reshape(-1), qs, nckv, nkpe, cache_ckv, cache_kpe_t, lw_['w_uv_cat'], lw_['n_b'])


def _post_kernel(a_ref, b_ref, c_ref, x_ref, wo_ref, g2_ref, wr_ref, wrl_ref, br_ref, x1_ref, hn_ref, rt_ref, cnt_ref,
                 cnt_scr, *, widths):
    wa, wb, wc = widths

    @pl.when(pl.program_id(0) == 0)
    def _():
        cnt_scr[...] = jnp.zeros(cnt_scr.shape, F32)

    mix = _dot(a_ref[...], wo_ref[:wa, :])
    mix = mix + _dot(b_ref[...], wo_ref[wa:wa + wb, :])
    mix = mix + _dot(c_ref[...], wo_ref[wa + wb:, :])
    x1 = x_ref[...] + mix
    x1_ref[...] = x1
    hn = _rms(x1, g2_ref[...])
    for sub in range(hn_ref.shape[1]):
        hn_ref[:, sub, :] = hn[:, sub * LANES:(sub + 1) * LANES]
    hn_hi = hn.astype(BF16)
    hn_lo = (hn - hn_hi.astype(F32)).astype(BF16)
    logits = (_dot(hn_hi, wr_ref[...]) + _dot(hn_hi, wrl_ref[...]) + _dot(hn_lo, wr_ref[...])) + br_ref[...]

    lane = lax.broadcasted_iota(jnp.int32, logits.shape, 1)
    lane_f = lane.astype(F32)
    big = float(LANES)
    is_g = lane < N_GROUPS
    gl = jnp.where(is_g, logits, NEG_BIG)
    gmax = jnp.max(gl, axis=-1, keepdims=True)
    gsel = jnp.min(jnp.where(gl == gmax, lane_f, big), axis=-1, keepdims=True)
    gprob = 1.0 / jnp.sum(jnp.where(is_g, jnp.exp(gl - gmax), 0.0), axis=-1, keepdims=True)
    lo = N_GROUPS + gsel * EXPERTS_PER_GROUP
    el = jnp.where((lane_f >= lo) & (lane_f < lo + EXPERTS_PER_GROUP), logits, NEG_BIG)
    v1 = jnp.max(el, axis=-1, keepdims=True)
    i1 = jnp.min(jnp.where(el == v1, lane_f, big), axis=-1, keepdims=True)
    el2 = jnp.where(lane_f == i1, NEG_BIG, el)
    v2 = jnp.max(el2, axis=-1, keepdims=True)
    i2 = jnp.min(jnp.where(el2 == v2, lane_f, big), axis=-1, keepdims=True)
    e21 = jnp.exp(v2 - v1)
    w1 = gprob / (1.0 + e21)
    w2 = w1 * e21
    e1 = i1 - N_GROUPS
    e2 = i2 - N_GROUPS

    oh1 = (lane_f == e1).astype(F32)
    oh2 = (lane_f == e2).astype(F32)
    both = oh1 + oh2
    tm = both.shape[0]
    tri = (lax.broadcasted_iota(jnp.int32, (tm, tm), 0) >= lax.broadcasted_iota(jnp.int32, (tm, tm), 1))
    incl = _dot(tri.astype(BF16), both.astype(BF16))
    base = cnt_scr[0:1, :] + incl - both
    r1 = jnp.sum(oh1 * base, axis=-1, keepdims=True)
    r2 = jnp.sum(oh2 * base, axis=-1, keepdims=True)
    cnt_scr[...] = cnt_scr[...] + incl[tm - 1:tm, :]
    cnt_ref[...] = cnt_scr[...]

    rt = jnp.where(lane == 0, e1, 0.0)
    rt = jnp.where(lane == 1, e2, rt)
    rt = jnp.where(lane == 2, w1, rt)
    rt = jnp.where(lane == 3, w2, rt)
    rt = jnp.where(lane == 4, r1, rt)
    rt = jnp.where(lane == 5, r2, rt)
    rt_ref[...] = rt


def _post_call(a, b, c, x, lw_):
    T, D = x.shape
    tm = TOKEN_TILE
    row = lambda i: (i, 0)
    full = lambda i: (0, 0)
    widths = (a.shape[1], b.shape[1], c.shape[1])
    ws = [lw_['w_out'], lw_['g2'], lw_['w_r'], lw_['w_r_lo'], lw_['b_r']]
    out_shape = (jax.ShapeDtypeStruct((T, D), F32), jax.ShapeDtypeStruct((T, D // LANES, LANES), F32),
                 jax.ShapeDtypeStruct((T, LANES), F32), jax.ShapeDtypeStruct((SUBLANES, LANES), F32))
    return pl.pallas_call(
        functools.partial(_post_kernel, widths=widths),
        grid=(T // tm,),
        in_specs=[pl.BlockSpec((tm, w), row) for w in widths] + [pl.BlockSpec((tm, D), row)]
        + [pl.BlockSpec(w.shape, full) for w in ws],
        out_specs=tuple(pl.BlockSpec((tm,) + s.shape[1:], (lambda i: (i, 0, 0)) if len(s.shape) == 3 else row)
                        for s in out_shape[:3])
        + (pl.BlockSpec((SUBLANES, LANES), full),),
        out_shape=out_shape, scratch_shapes=[pltpu.VMEM((SUBLANES, LANES), F32)],
        compiler_params=_cparams("arbitrary"), name="post_proj",
    )(a, b, c, x, *ws)


def _dest_kernel(rt_ref, ps_ref, d_ref):
    rt = rt_ref[...]
    lane = lax.broadcasted_iota(jnp.int32, rt.shape, 1)
    lane_f = lane.astype(F32)
    pick = lambda k: jnp.sum(jnp.where(lane == k, rt, 0.0), axis=-1, keepdims=True)
    ps = ps_ref[...]
    d1 = jnp.sum(jnp.where(lane_f == pick(0), ps, 0.0), axis=-1, keepdims=True) + pick(2 * TOP_K)
    d2 = jnp.sum(jnp.where(lane_f == pick(1), ps, 0.0), axis=-1, keepdims=True) + pick(2 * TOP_K + 1)
    d_ref[...] = jnp.where(lane == 0, d1, jnp.where(lane == 1, d2, 0.0))


def _dest_call(route, pstart_row):
    T = route.shape[0]
    tm = TOKEN_TILE
    return pl.pallas_call(
        _dest_kernel, grid=(T // tm,),
        in_specs=[pl.BlockSpec((tm, LANES), lambda i: (i, 0)), pl.BlockSpec((1, LANES), lambda i: (0, 0))],
        out_specs=pl.BlockSpec((tm, LANES), lambda i: (i, 0)),
        out_shape=jax.ShapeDtypeStruct((T, LANES), F32), compiler_params=_cparams("parallel"), name="moe_dest",
    )(route, pstart_row)


def _moe_kernel(te_ref, nt_ref, tok_ref, hn_hbm, w_ref, wg_ref, wu_ref, wd_ref, y_ref, xbuf, sem):
    i = pl.program_id(0)
    tm = xbuf.shape[1]
    nt = nt_ref[0]

    def start_rows(t):
        slot = lax.rem(t, 2)

        def body(r, carry):
            row = tok_ref[t * tm + r]
            pltpu.make_async_copy(hn_hbm.at[row], xbuf.at[slot, r], sem.at[slot]).start()
            return carry

        lax.fori_loop(0, tm, body, 0, unroll=8)

    @pl.when(i == 0)
    def _():
        start_rows(i)

    @pl.when(i + 1 < nt)
    def _():
        start_rows(i + 1)

    @pl.when(i < nt)
    def _():
        slot = lax.rem(i, 2)
        pltpu.make_async_copy(hn_hbm.at[pl.ds(0, tm)], xbuf.at[slot], sem.at[slot]).wait()
        x = jnp.concatenate([xbuf[slot, :, sub, :] for sub in range(xbuf.shape[2])], axis=1).astype(BF16)
        hg = _dot(x, wg_ref[...].astype(BF16))
        hu = _dot(x, wu_ref[...].astype(BF16))
        act = hg * jax.nn.sigmoid(hg) * hu * w_ref[...]
        y_ref[...] = _dot(act.astype(BF16), wd_ref[...].astype(BF16))

    @pl.when(i >= nt)
    def _():
        y_ref[...] = jnp.zeros(y_ref.shape, F32)


def _moe_call(hn, row_tok, row_w, tile_e, n_used, w_gate, w_up, w_down, layer):
    D = w_gate.shape[-2]
    tm = MOE_TILE
    F = w_gate.shape[-1]
    n_tiles = tile_e.shape[0]
    grid_spec = pltpu.PrefetchScalarGridSpec(
        num_scalar_prefetch=3, grid=(n_tiles,),
        in_specs=[pl.BlockSpec(memory_space=pl.ANY),
                  pl.BlockSpec((tm, 1), lambda i, te, nt, tok: (i, 0)),
                  pl.BlockSpec((None, None, D, F), lambda i, te, nt, tok: (layer, te[i], 0, 0)),
                  pl.BlockSpec((None, None, D, F), lambda i, te, nt, tok: (layer, te[i], 0, 0)),
                  pl.BlockSpec((None, None, F, D), lambda i, te, nt, tok: (layer, te[i], 0, 0))],
        out_specs=pl.BlockSpec((tm, D), lambda i, te, nt, tok: (i, 0)),
        scratch_shapes=[pltpu.VMEM((2, tm) + hn.shape[1:], F32), pltpu.SemaphoreType.DMA((2,))])
    return pl.pallas_call(
        _moe_kernel, grid_spec=grid_spec, out_shape=jax.ShapeDtypeStruct((n_tiles * tm, D), F32),
        compiler_params=_cparams("arbitrary"), name="moe_experts",
    )(tile_e, n_used, row_tok, hn, row_w, w_gate, w_up, w_down)


def _route_meta(route, counts_f, tm, n_tiles):
    T = route.shape[0]
    n_assign = TOP_K * T
    e = route[:, :TOP_K].astype(jnp.int32).reshape(-1)
    w = route[:, TOP_K:2 * TOP_K].reshape(-1)
    order = jnp.argsort(e, stable=True).astype(jnp.int32)
    counts = counts_f.astype(jnp.int32)
    zero = jnp.zeros((1,), jnp.int32)
    start = jnp.concatenate([zero, jnp.cumsum(counts)])
    pstart = jnp.concatenate([zero, jnp.cumsum(((counts + tm - 1) // tm) * tm)])
    n_used = (pstart[N_EXPERTS] // tm).reshape(1)
    tile_lo = jnp.arange(n_tiles, dtype=jnp.int32) * tm
    tile_e = jnp.sum((pstart[None, 1:] <= tile_lo[:, None]).astype(jnp.int32), axis=1)
    tile_e = jnp.minimum(tile_e, N_EXPERTS - 1)
    oh = (tile_e[:, None] == jnp.arange(N_EXPERTS, dtype=jnp.int32)[None, :]).astype(jnp.int32)
    pick = lambda tab: jnp.sum(oh * tab[None, :N_EXPERTS], axis=1)
    k = (tile_lo - pick(pstart))[:, None] + jnp.arange(tm, dtype=jnp.int32)[None, :]
    valid = (k < pick(counts)[:, None]).reshape(-1)
    idx = jnp.clip(pick(start)[:, None] + k, 0, n_assign - 1).reshape(-1)
    src = order[idx]
    row_tok = jnp.where(valid, src // TOP_K, 0)
    row_w = jnp.where(valid, w[src], 0.0)
    ps_row = jnp.pad(pstart[:N_EXPERTS].astype(F32), (0, LANES - N_EXPERTS)).reshape(1, LANES)
    pos = _dest_call(route, ps_row)[:, :TOP_K].astype(jnp.int32)
    return row_tok, row_w.reshape(-1, 1), pos, tile_e, n_used


def _final_kernel(x_ref, g_ref, o_ref):
    o_ref[...] = _rms(x_ref[...], g_ref[...])


def _final_call(x, g):
    T, D = x.shape
    tm = TOKEN_TILE
    return pl.pallas_call(
        _final_kernel, grid=(T // tm,),
        in_specs=[pl.BlockSpec((tm, D), lambda i: (i, 0)), pl.BlockSpec((1, D), lambda i: (0, 0))],
        out_specs=pl.BlockSpec((tm, D), lambda i: (i, 0)),
        out_shape=jax.ShapeDtypeStruct((T, D), F32), compiler_params=_cparams("parallel"), name="final_norm",
    )(x, g)


def _block_diag(w):
    n, c, d = w.shape
    return jnp.einsum('ncd,nm->ncmd', w, jnp.eye(n, dtype=w.dtype)).reshape(n * c, n * d)


def _half_swap(w):
    half = w.shape[-1] // 2
    return jnp.concatenate([w[..., half:], w[..., :half]], axis=-1)


def _layer_weights(l, p, dims):
    lw, qr, kvr, gk_w, gv_w = dims
    D = p['w_in'].shape[1]
    row = lambda a: a.reshape(1, -1).astype(F32)
    w_in = p['w_in'][l]
    sizes = (lw, lw, qr, kvr, QK_ROPE, gk_w, gk_w, gv_w, p['gla_wa2'].shape[1], gv_w)
    x_lru, y_lru, c_q, c_kv, k_pe, g_q, g_k, g_v, g_a, g_o = jnp.split(w_in, list(np.cumsum(sizes)[:-1]), axis=1)
    tail_pad = LANES - 2 * QK_ROPE - g_a.shape[1]
    tail = jnp.concatenate([k_pe, _half_swap(k_pe), g_a, jnp.zeros((D, tail_pad), F32)], axis=1)
    w_in_p = jnp.concatenate([x_lru, y_lru, c_q, c_kv, g_q, g_k, g_v, g_o, tail], axis=1).astype(BF16)

    w_uq = p['w_uq'][l].reshape(qr, MLA_HEADS, QK_NOPE + QK_ROPE)
    nope = w_uq[:, :, :QK_NOPE].reshape(qr, MLA_HEADS * QK_NOPE)
    pe = w_uq[:, :, QK_NOPE:]
    widen = lambda a: jnp.pad(a, ((0, 0), (0, 0), (0, LANES - QK_ROPE))).reshape(qr, MLA_HEADS * LANES)
    w_q = jnp.concatenate([nope, widen(pe), widen(_half_swap(pe))], axis=1).astype(BF16)
    w_abs = _block_diag(jnp.transpose(p['w_uk'][l], (1, 2, 0))).astype(BF16)
    wa2 = jnp.zeros((LANES, gk_w), F32).at[2 * QK_ROPE:2 * QK_ROPE + g_a.shape[1]].set(p['gla_wa2'][l]).astype(BF16)

    w_uv = p['w_uv'][l]
    mw = MLA_HEADS * V_HEAD
    w_uv_cat = w_uv.reshape(kvr, mw)
    w_uv_rows = _block_diag(jnp.transpose(w_uv, (1, 0, 2)))
    n_a, n_b, n_c = jnp.split(p['out_norm_g'][l], [lw, lw + mw])
    n_r = N_GROUPS + N_EXPERTS
    w_r = jnp.concatenate([p['router_wg'][l], p['router_we'][l], jnp.zeros((D, LANES - n_r), F32)], axis=1)
    b_r = jnp.concatenate([p['router_bg'][l], p['router_be'][l], jnp.zeros((LANES - n_r,), F32)])
    return dict(
        g1=row(p['norm1_g'][l]), w_in=w_in_p, q_g=row(p['q_norm_g'][l]), w_q=w_q, w_abs=w_abs,
        kv_g=row(p['kv_norm_g'][l]), wa2=wa2, ba=row(p['gla_ba'][l]),
        conv_w=p['conv_w'][l], conv_b=row(p['conv_b'][l]),
        lru_wa=_block_diag(p['lru_wa'][l]).astype(BF16), lru_ba=row(p['lru_ba'][l]),
        lru_wi=_block_diag(p['lru_wi'][l]).astype(BF16), lru_bi=row(p['lru_bi'][l]),
        lru_lam=row(p['lru_lambda'][l]), n_a=row(n_a), n_b=row(n_b), n_c=row(n_c),
        gla_g=row(jnp.tile(p['gla_norm_g'][l], GLA_HEADS)),
        w_uv_cat=w_uv_cat.astype(BF16), w_uv_rows_t=w_uv_rows.T.astype(BF16),
        w_out=p['w_out'][l].astype(BF16), g2=row(p['norm2_g'][l]), w_r=w_r.astype(BF16),
        w_r_lo=(w_r - w_r.astype(BF16).astype(F32)).astype(BF16), b_r=row(b_r),
    )


def _rope_tables(positions):
    half = QK_ROPE // 2
    inv = ROPE_THETA ** (-np.arange(half, dtype=np.float64) / half)
    ang = np.asarray(positions, np.float64)[:, None] * inv
    zeros = np.zeros((ang.shape[0], LANES - QK_ROPE))
    cos = np.concatenate([np.cos(ang), np.cos(ang), zeros], axis=1)
    sin = np.concatenate([-np.sin(ang), np.sin(ang), zeros], axis=1)
    return jnp.asarray(cos, F32), jnp.asarray(sin, F32)


def _gla_state_to_rows(s):
    B = s.shape[0]
    eye = jnp.eye(GLA_HEADS, dtype=s.dtype)
    return jnp.einsum('bhde,hg->bhegd', s, eye).reshape(B, GLA_HEADS * GLA_DV, GLA_HEADS * GLA_DK)


def _gla_rows_to_state(st):
    B = st.shape[0]
    s5 = st.reshape(B, GLA_HEADS, GLA_DV, GLA_HEADS, GLA_DK)
    blocks = jnp.stack([s5[:, h, :, h, :] for h in range(GLA_HEADS)], axis=1)
    return jnp.swapaxes(blocks, -1, -2)


def kernel(x_prompt, x_sample, cache_ckv, cache_kpe, page_table, state_conv, state_lru, state_gla, norm1_g, w_in, conv_w, conv_b, lru_wa, lru_ba, lru_wi, lru_bi, lru_lambda, q_norm_g, w_uq, kv_norm_g, w_uk, w_uv, gla_wa2, gla_ba, gla_norm_g, out_norm_g, w_out, norm2_g, router_wg, router_bg, router_we, router_be, w_gate, w_up, w_down, final_norm_g):
    p = dict(norm1_g=norm1_g, w_in=w_in, conv_w=conv_w, conv_b=conv_b, lru_wa=lru_wa, lru_ba=lru_ba,
             lru_wi=lru_wi, lru_bi=lru_bi, lru_lambda=lru_lambda, q_norm_g=q_norm_g, w_uq=w_uq,
             kv_norm_g=kv_norm_g, w_uk=w_uk, w_uv=w_uv, gla_wa2=gla_wa2, gla_ba=gla_ba,
             gla_norm_g=gla_norm_g, out_norm_g=out_norm_g, w_out=w_out, norm2_g=norm2_g,
             router_wg=router_wg, router_bg=router_bg, router_we=router_we, router_be=router_be)
    Bp, Sp, D = x_prompt.shape
    Bs, Ss, _ = x_sample.shape
    depth = w_in.shape[0]
    lw = state_lru.shape[-1]
    qr = q_norm_g.shape[-1]
    kvr = kv_norm_g.shape[-1]
    gk_w = GLA_HEADS * GLA_DK
    gv_w = GLA_HEADS * GLA_DV
    dims = (lw, qr, kvr, gk_w, gv_w)
    assert kvr == LANES and gk_w == LANES
    Tp, Ts = Bp * Sp, Bs * Ss
    T = Tp + Ts
    past = page_table.shape[1] * cache_ckv.shape[2]

    cos_t, sin_t = _rope_tables(np.concatenate([np.tile(np.arange(Sp), Bp), np.tile(past + np.arange(Ss), Bs)]))
    cache_kpe_t = jnp.swapaxes(cache_kpe, 2, 3)
    n_tiles = (TOP_K * T + N_EXPERTS * (MOE_TILE - 1)) // MOE_TILE + 1

    x = jnp.concatenate([x_prompt.reshape(Tp, D), x_sample.reshape(Ts, D)], axis=0)
    outs = {k: [] for k in ('ckv_p', 'kpe_p', 'ckv_s', 'kpe_s', 'conv_p', 'conv_s', 'lru_p', 'lru_s', 'gla_p', 'gla_s')}
    for l in range(depth):
        lw_ = _layer_weights(l, p, dims)
        zl, qcat, kcat, kcat_t, ckv_n, kpe_n, g = _pre_call(x, lw_, cos_t, sin_t, dims)

        a_p, conv_p, lru_p = _lru_call(zl, 0, Bp, Sp, jnp.zeros((Bp, CONV_WIDTH - 1, lw), F32),
                                       jnp.zeros((Bp, lw), F32), lw_, BF16)
        a_s, lru_s = _lru_seg_call(zl, Tp, Bs, Ss, state_conv[l], state_lru[l], lw_)
        conv_s = zl[Tp:, :lw].reshape(Bs, Ss, lw)[:, Ss - (CONV_WIDTH - 1):]

        c_p, gla_p = _gla_call(g, 0, Bp, Sp, jnp.zeros((Bp, gv_w, gk_w), F32), lw_, BF16, gk_w, gv_w)
        c_s, gla_s = _gla_call(g, Tp, Bs, Ss, _gla_state_to_rows(state_gla[l]), lw_, BF16, gk_w, gv_w)

        b_p = _attn_prompt_call(qcat, kcat, kcat_t, Bp, Sp, lw_, kvr)
        b_s = _attn_sample_call(page_table, qcat[Tp:].reshape(Bs, Ss, -1), ckv_n[Tp:].reshape(Bs, Ss, kvr),
                                kpe_n[Tp:].reshape(Bs, Ss, QK_ROPE), cache_ckv, cache_kpe_t, l, lw_, kvr)

        a = jnp.concatenate([a_p, a_s], axis=0)
        b = jnp.concatenate([b_p, b_s.reshape(Ts, -1).astype(BF16)], axis=0)
        c = jnp.concatenate([c_p, c_s], axis=0)
        x1, hn, route, cnt = _post_call(a, b, c, x, lw_)

        row_tok, row_w, pos, tile_e, n_used = _route_meta(route, cnt[0, :N_EXPERTS], MOE_TILE, n_tiles)
        ys = _moe_call(hn, row_tok, row_w, tile_e, n_used, w_gate, w_up, w_down, l)
        x = x1 + ys[pos[:, 0]] + ys[pos[:, 1]]

        outs['ckv_p'].append(ckv_n[:Tp].reshape(Bp, Sp, kvr))
        outs['kpe_p'].append(kpe_n[:Tp].reshape(Bp, Sp, QK_ROPE))
        outs['ckv_s'].append(ckv_n[Tp:].reshape(Bs, Ss, kvr))
        outs['kpe_s'].append(kpe_n[Tp:].reshape(Bs, Ss, QK_ROPE))
        outs['conv_p'].append(conv_p)
        outs['conv_s'].append(conv_s)
        outs['lru_p'].append(lru_p.reshape(Bp, lw))
        outs['lru_s'].append(lru_s.reshape(Bs, lw))
        outs['gla_p'].append(_gla_rows_to_state(gla_p))
        outs['gla_s'].append(_gla_rows_to_state(gla_s))

    y = _final_call(x, final_norm_g.reshape(1, D))
    st = {k: jnp.stack(v) for k, v in outs.items()}
    return (y[:Tp].reshape(Bp, Sp, D), y[Tp:].reshape(Bs, Ss, D),
            st['ckv_p'], st['kpe_p'], st['ckv_s'], st['kpe_s'], st['conv_p'], st['conv_s'],
            st['lru_p'], st['lru_s'], st['gla_p'], st['gla_s'])
```

```python
import functools
import math

import numpy as np
import jax
import jax.numpy as jnp
from jax import lax
from jax.experimental import pallas as pl
from jax.experimental.pallas import tpu as pltpu

F32 = jnp.float32
BF16 = jnp.bfloat16

LRU_BLOCKS = 4
CONV_WIDTH = 4
LRU_C = 8.0
MLA_HEADS = 8
QK_NOPE = 64
QK_ROPE = 32
V_HEAD = 64
ROPE_THETA = 10000.0
GLA_HEADS = 4
GLA_DK = 32
GLA_DV = 64
GLA_TAU = 16.0
GLA_CHUNK = 16
N_GROUPS = 4
EXPERTS_PER_GROUP = 8
N_EXPERTS = N_GROUPS * EXPERTS_PER_GROUP
TOP_K = 2
EPS = 1e-6

LANES = 128
SUBLANES = 8
BF16_ROWS = 16
VMEM_LIMIT_BYTES = 56 * 1024 * 1024

TOKEN_TILE = 512
SEQ_TILE = 256
MOE_TILE = 256
SCORE_PAGES = 4
VALUE_PAGES = 2
NEG_BIG = -1e30


def _cparams(*sem):
    return pltpu.CompilerParams(dimension_semantics=sem, vmem_limit_bytes=VMEM_LIMIT_BYTES)


def _rms(x, g):
    return x * lax.rsqrt(jnp.mean(x * x, axis=-1, keepdims=True) + EPS) * g


def _dot(a, b):
    return jnp.dot(a, b, preferred_element_type=F32)


def _dot_nt(a, b):
    return lax.dot_general(a, b, (((1,), (1,)), ((), ())), preferred_element_type=F32)


def _softplus(x):
    return jnp.maximum(x, 0.0) + jnp.log1p(jnp.exp(-jnp.abs(x)))


def _token_tile(refs, n_first):
    if n_first is None:
        return refs[0][...]
    return jnp.where(pl.program_id(0) < n_first, refs[0][...], refs[1][...])


def _token_specs(x, tm):
    if not isinstance(x, tuple):
        return [x], [pl.BlockSpec((tm, x.shape[1]), lambda i: (i, 0))], None, x.shape[0]
    a, b = x
    assert a.shape[0] % tm == 0 and b.shape[0] % tm == 0
    n_first = a.shape[0] // tm
    specs = [pl.BlockSpec((tm, a.shape[1]), lambda i: (jnp.minimum(i, n_first - 1), 0)),
             pl.BlockSpec((tm, b.shape[1]), lambda i: (jnp.maximum(i - n_first, 0), 0))]
    return [a, b], specs, n_first, a.shape[0] + b.shape[0]


def _shift_rows(x, d, fill):
    row = lax.broadcasted_iota(jnp.int32, x.shape, 0)
    return jnp.where(row >= d, pltpu.roll(x, d, 0), fill)


def _pre_kernel(*refs, dims, n_first):
    n_x = 1 if n_first is None else 2
    (g1_ref, win_ref, qg_ref, wq_ref, wabs_ref, kvg_ref, wa2_ref, ba_ref, cos_ref, sin_ref,
     zl_ref, q_ref, k_ref, kt_ref, ckv_ref, kpe_ref, g_ref) = refs[n_x:]
    lw, qr, kvr, gk_w, gv_w = dims
    xn = _rms(_token_tile(refs[:n_x], n_first), g1_ref[...])
    z = _dot(xn.astype(BF16), win_ref[...])
    o = 2 * lw
    zl_ref[...] = z[:, :o]
    cq = z[:, o:o + qr]
    o += qr
    ckv = z[:, o:o + kvr]
    o += kvr
    gqk = z[:, o:o + 2 * gk_w]
    o += 2 * gk_w
    gvo = z[:, o:o + 2 * gv_w]
    o += 2 * gv_w
    tail = z[:, o:o + LANES]
    cos = cos_ref[...]
    sin = sin_ref[...]
    scale = (QK_NOPE + QK_ROPE) ** -0.5 * math.log2(math.e)

    ckv_n = _rms(ckv, kvg_ref[...])
    kpe = tail * cos + pltpu.roll(tail, LANES - QK_ROPE, 1) * sin
    ckv_ref[...] = ckv_n
    kpe_ref[...] = kpe[:, :QK_ROPE]
    k_ref[:, :kvr] = ckv_n.astype(BF16)
    k_ref[:, kvr:] = kpe.astype(BF16)
    ones = jnp.ones((kt_ref.shape[0] - kvr, ckv_n.shape[0]), F32)
    kt_ref[...] = jnp.concatenate([ckv_n.T, ones], axis=0).astype(BF16)

    cqn = _rms(cq, qg_ref[...]).astype(BF16)
    qall = _dot(cqn, wq_ref[...])
    n_nope = MLA_HEADS * QK_NOPE
    q_abs = _dot(qall[:, :n_nope].astype(BF16), wabs_ref[...]) * scale
    for h in range(MLA_HEADS):
        pe = qall[:, n_nope + h * LANES:n_nope + (h + 1) * LANES]
        sw = qall[:, n_nope + (MLA_HEADS + h) * LANES:n_nope + (MLA_HEADS + h + 1) * LANES]
        q_ref[:, 2 * h * LANES:(2 * h + 1) * LANES] = q_abs[:, h * kvr:(h + 1) * kvr].astype(BF16)
        q_ref[:, (2 * h + 1) * LANES:(2 * h + 2) * LANES] = ((pe * cos + sw * sin) * scale).astype(BF16)

    la_pre = _dot(tail.astype(BF16), wa2_ref[...]) + ba_ref[...]
    la = -_softplus(-la_pre) * (1.0 / GLA_TAU)
    g_ref[:, :gk_w] = gqk[:, :gk_w] * (GLA_DK ** -0.5)
    g_ref[:, gk_w:2 * gk_w] = gqk[:, gk_w:]
    g_ref[:, 2 * gk_w:3 * gk_w] = la
    g_ref[:, 3 * gk_w:] = gvo


def _pre_call(x, lw_, cos_t, sin_t, dims):
    lw, qr, kvr, gk_w, gv_w = dims
    tm = TOKEN_TILE
    xs, x_specs, n_first, T = _token_specs(x, tm)
    assert T % tm == 0
    row = lambda i: (i, 0)
    full = lambda i: (0, 0)
    wspec = lambda a: pl.BlockSpec(a.shape, full)
    ws = [lw_['g1'], lw_['w_in'], lw_['q_g'], lw_['w_q'], lw_['w_abs'], lw_['kv_g'], lw_['wa2'], lw_['ba']]
    ins = xs + ws
    in_specs = x_specs + [wspec(a) for a in ws]
    in_specs += [pl.BlockSpec((tm, LANES), row), pl.BlockSpec((tm, LANES), row)]
    out_shape = (
        jax.ShapeDtypeStruct((T, 2 * lw), F32),
        jax.ShapeDtypeStruct((T, 2 * LANES * MLA_HEADS), BF16),
        jax.ShapeDtypeStruct((T, 2 * LANES), BF16),
        jax.ShapeDtypeStruct((T // tm, kvr + BF16_ROWS, tm), BF16),
        jax.ShapeDtypeStruct((T, kvr), F32),
        jax.ShapeDtypeStruct((T, QK_ROPE), F32),
        jax.ShapeDtypeStruct((T, 3 * gk_w + 2 * gv_w), F32),
    )
    out_specs = tuple(pl.BlockSpec((None, s.shape[1], tm), lambda i: (i, 0, 0)) if len(s.shape) == 3
                      else pl.BlockSpec((tm, s.shape[1]), row) for s in out_shape)
    return pl.pallas_call(
        functools.partial(_pre_kernel, dims=dims, n_first=n_first),
        grid=(T // tm,), in_specs=in_specs, out_specs=out_specs, out_shape=out_shape,
        compiler_params=_cparams("parallel"), name="pre_proj",
    )(*ins, cos_t, sin_t)


def _lru_kernel(zl_ref, cbuf_ref, h0_ref, cw_ref, cb_ref, wa_ref, ba_ref, wi_ref, bi_ref, lam_ref, na_ref,
                a_ref, conv_ref, hout_ref, xbuf, hcar, *, ts, lw):
    i = pl.program_id(1)
    last = pl.num_programs(1) - 1
    pad = SUBLANES
    nbuf = CONV_WIDTH - 1

    @pl.when(i == 0)
    def _():
        xbuf[0:pad, :] = jnp.zeros((pad, lw), F32)
        xbuf[pad - nbuf:pad, :] = cbuf_ref[...]
        hcar[...] = h0_ref[...]

    x = zl_ref[:, :lw]
    y = zl_ref[:, lw:]
    xbuf[pad:pad + ts, :] = x
    xc = cb_ref[...] + cw_ref[nbuf:nbuf + 1, :] * x
    for k in range(nbuf):
        xc = xc + cw_ref[k:k + 1, :] * xbuf[pad - nbuf + k:pad - nbuf + k + ts, :]

    @pl.when(i == last)
    def _():
        conv_ref[...] = xbuf[pad + ts - nbuf:pad + ts, :]

    xbuf[0:pad, :] = xbuf[ts:ts + pad, :]

    xb = xc.astype(BF16)
    r = jax.nn.sigmoid(_dot(xb, wa_ref[...]) + ba_ref[...])
    gi = jax.nn.sigmoid(_dot(xb, wi_ref[...]) + bi_ref[...])
    log_a = (-LRU_C) * r * _softplus(-lam_ref[...])
    a = jnp.exp(log_a)
    th = jnp.tanh(log_a)
    u = jnp.sqrt(-2.0 * th / (1.0 - th)) * (gi * xc)

    d = 1
    while d < ts:
        u = a * _shift_rows(u, d, 0.0) + u
        a = a * _shift_rows(a, d, 1.0)
        d *= 2
    h = a * hcar[...] + u
    hcar[...] = h[ts - 1:ts, :]

    @pl.when(i == last)
    def _():
        hout_ref[...] = h[ts - 1:ts, :]

    out_a = h * jax.nn.gelu(y)
    a_ref[...] = _rms(out_a, na_ref[...]).astype(a_ref.dtype)


def _lru_call(zl, row_off, B, S, cbuf, h0, lw_, out_dtype):
    lw = h0.shape[-1]
    ts = min(S, SEQ_TILE)
    assert S % ts == 0 and row_off % ts == 0 and S >= CONV_WIDTH - 1
    n = S // ts
    off = row_off // ts
    full = lambda b, i: (0, 0)
    ws = [lw_['conv_w'], lw_['conv_b'], lw_['lru_wa'], lw_['lru_ba'], lw_['lru_wi'], lw_['lru_bi'],
          lw_['lru_lam'], lw_['n_a']]
    in_specs = [
        pl.BlockSpec((ts, 2 * lw), lambda b, i: (off + b * n + i, 0)),
        pl.BlockSpec((None, CONV_WIDTH - 1, lw), lambda b, i: (b, 0, 0)),
        pl.BlockSpec((None, 1, lw), lambda b, i: (b, 0, 0)),
    ] + [pl.BlockSpec(a.shape, full) for a in ws]
    out_shape = (
        jax.ShapeDtypeStruct((B * S, lw), out_dtype),
        jax.ShapeDtypeStruct((B, CONV_WIDTH - 1, lw), F32),
        jax.ShapeDtypeStruct((B, 1, lw), F32),
    )
    out_specs = (
        pl.BlockSpec((ts, lw), lambda b, i: (b * n + i, 0)),
        pl.BlockSpec((None, CONV_WIDTH - 1, lw), lambda b, i: (b, 0, 0)),
        pl.BlockSpec((None, 1, lw), lambda b, i: (b, 0, 0)),
    )
    return pl.pallas_call(
        functools.partial(_lru_kernel, ts=ts, lw=lw),
        grid=(B, n), in_specs=in_specs, out_specs=out_specs, out_shape=out_shape,
        scratch_shapes=[pltpu.VMEM((ts + SUBLANES, lw), F32), pltpu.VMEM((1, lw), F32)],
        compiler_params=_cparams("parallel", "arbitrary"), name="rg_lru",
    )(zl, cbuf, h0.reshape(B, 1, lw), *ws)


def _lru_seg_kernel(zl_ref, hist_ref, h0_ref, cw_ref, cb_ref, wa_ref, ba_ref, wi_ref, bi_ref, lam_ref, na_ref,
                    a_ref, h_ref, *, seg, lw):
    nbuf = CONV_WIDTH - 1
    x = zl_ref[:, :lw]
    y = zl_ref[:, lw:]
    n = x.shape[0]
    pos = lax.broadcasted_iota(jnp.int32, (n, 1), 0) & (seg - 1)
    hist = hist_ref[...]
    xc = cb_ref[...] + cw_ref[nbuf:nbuf + 1, :] * x
    for k in range(nbuf):
        j = nbuf - k
        prev = jnp.where(pos >= j, pltpu.roll(x, j, 0), pltpu.roll(hist, (j - seg) % n, 0))
        xc = xc + cw_ref[k:k + 1, :] * prev
    xb = xc.astype(BF16)
    r = jax.nn.sigmoid(_dot(xb, wa_ref[...]) + ba_ref[...])
    gi = jax.nn.sigmoid(_dot(xb, wi_ref[...]) + bi_ref[...])
    log_a = (-LRU_C) * r * _softplus(-lam_ref[...])
    a = jnp.exp(log_a)
    th = jnp.tanh(log_a)
    u = jnp.sqrt(-2.0 * th / (1.0 - th)) * (gi * xc)
    d = 1
    while d < seg:
        u = a * jnp.where(pos >= d, pltpu.roll(u, d, 0), 0.0) + u
        a = a * jnp.where(pos >= d, pltpu.roll(a, d, 0), 1.0)
        d *= 2
    h = a * h0_ref[...] + u
    h_ref[...] = h
    a_ref[...] = _rms(h * jax.nn.gelu(y), na_ref[...]).astype(a_ref.dtype)


def _lru_seg_call(zl, row_off, B, S, cbuf, h0, lw_):
    lw = h0.shape[-1]
    n = B * S
    tm = min(n, TOKEN_TILE)
    assert S & (S - 1) == 0 and S >= CONV_WIDTH - 1 and tm % S == 0 and n % tm == 0 and row_off % tm == 0
    off = row_off // tm
    hist = jnp.pad(cbuf, ((0, 0), (S - (CONV_WIDTH - 1), 0), (0, 0))).reshape(n, lw)
    h0_rows = jnp.repeat(h0, S, axis=0)
    full = lambda i: (0, 0)
    row = lambda i: (i, 0)
    ws = [lw_['conv_w'], lw_['conv_b'], lw_['lru_wa'], lw_['lru_ba'], lw_['lru_wi'], lw_['lru_bi'],
          lw_['lru_lam'], lw_['n_a']]
    a, h = pl.pallas_call(
        functools.partial(_lru_seg_kernel, seg=S, lw=lw),
        grid=(n // tm,),
        in_specs=[pl.BlockSpec((tm, 2 * lw), lambda i: (off + i, 0)), pl.BlockSpec((tm, lw), row),
                  pl.BlockSpec((tm, lw), row)] + [pl.BlockSpec(w.shape, full) for w in ws],
        out_specs=(pl.BlockSpec((tm, lw), row), pl.BlockSpec((tm, lw), row)),
        out_shape=(jax.ShapeDtypeStruct((n, lw), BF16), jax.ShapeDtypeStruct((n, lw), F32)),
        compiler_params=_cparams("parallel"), name="rg_lru_seg",
    )(zl, hist, h0_rows, *ws)
    return a, h.reshape(B, S, lw)[:, S - 1]


def _gla_kernel(g_ref, s0_ref, gn_ref, nc_ref, c_ref, sout_ref, st, *, ts, nseq, gk_w, gv_w):
    i = pl.program_id(1)
    last = pl.num_programs(1) - 1
    C = GLA_CHUNK
    rows = min(ts, C)
    n_chunks = max(ts // C, 1)
    log2c = int(math.log2(C))

    @pl.when(i == 0)
    def _():
        st[...] = s0_ref[...]

    dk_sh, dv_sh = int(math.log2(GLA_DK)), int(math.log2(GLA_DV))
    hd = lax.broadcasted_iota(jnp.int32, (gk_w, gv_w), 0) >> dk_sh
    he = lax.broadcasted_iota(jnp.int32, (gk_w, gv_w), 1) >> dv_sh
    same = (hd == he).astype(BF16)
    he_t = lax.broadcasted_iota(jnp.int32, (gv_w, gk_w), 0) >> dv_sh
    hd_t = lax.broadcasted_iota(jnp.int32, (gv_w, gk_w), 1) >> dk_sh
    same_t = (he_t == hd_t).astype(F32)
    sel_t = lax.broadcasted_iota(jnp.int32, (C, C * C), 0)
    sel_r = lax.broadcasted_iota(jnp.int32, (C, C * C), 1) >> log2c
    sel = (sel_t == sel_r).astype(BF16)
    srow = lax.broadcasted_iota(jnp.int32, (C, 1), 0)

    seq_out = []
    for sq in range(nseq):
        blk = g_ref[sq * ts:(sq + 1) * ts, :]
        if rows < C:
            blk = jnp.concatenate([blk, jnp.zeros((C - rows, blk.shape[1]), F32)], axis=0)
        q = blk[:, :gk_w]
        k = blk[:, gk_w:2 * gk_w]
        la = blk[:, 2 * gk_w:3 * gk_w]
        v = blk[:, 3 * gk_w:3 * gk_w + gv_w]
        nrow = n_chunks * C
        pos = lax.broadcasted_iota(jnp.int32, (nrow, 1), 0) & (C - 1)
        cum = la
        d = 1
        while d < C:
            cum = cum + jnp.where(pos >= d, pltpu.roll(cum, d, 0), 0.0)
            d *= 2
        tot = jnp.where(pos == C - 1, cum, 0.0)
        d = 1
        while d < C:
            tot = tot + jnp.where(pos < C - d, pltpu.roll(tot, nrow - d, 0), 0.0)
            d *= 2
        qe = (q * jnp.exp(cum)).astype(BF16)
        kdec = (k * jnp.exp(tot - cum)).astype(BF16)
        dec = jnp.exp(tot)
        vb = v.astype(BF16)

        o_intra, upd = [], []
        for c in range(n_chunks):
            sl = slice(c * C, (c + 1) * C)
            cum_c, q_c, k_c, v_c = cum[sl], q[sl], k[sl], v[sl]
            pieces = []
            for t in range(C):
                diff = jnp.where(srow <= t, cum_c[t:t + 1, :] - cum_c, NEG_BIG)
                pieces.append(q_c[t:t + 1, :] * k_c * jnp.exp(diff))
            w = jnp.concatenate(pieces, axis=0)
            att = _dot(w.astype(BF16), same)
            xv = att * jnp.concatenate([v_c] * C, axis=0)
            o_intra.append(_dot(sel, xv.astype(BF16)))
            upd.append(lax.dot_general(vb[sl], kdec[sl], (((0,), (0,)), ((), ())),
                                       preferred_element_type=F32) * same_t)

        s_t = st[sq]
        outs = []
        for c in range(n_chunks):
            sl = slice(c * C, (c + 1) * C)
            outs.append(o_intra[c] + _dot_nt(qe[sl], s_t.astype(BF16)))
            s_t = s_t * dec[c * C:c * C + 1, :] + upd[c]
        st[sq] = s_t
        seq_out.append(jnp.concatenate(outs, axis=0)[:ts] if n_chunks > 1 else outs[0][:ts])

    o = jnp.concatenate(seq_out, axis=0) if nseq > 1 else seq_out[0]

    @pl.when(i == last)
    def _():
        sout_ref[...] = st[...]

    go = g_ref[:, 3 * gk_w + gv_w:]
    e64 = (lax.broadcasted_iota(jnp.int32, (gv_w, gv_w), 0) >> dv_sh
           == lax.broadcasted_iota(jnp.int32, (gv_w, gv_w), 1) >> dv_sh).astype(BF16)
    osq = o * o
    osq_hi = osq.astype(BF16)
    osq_lo = (osq - osq_hi.astype(F32)).astype(BF16)
    ms = (_dot(osq_hi, e64) + _dot(osq_lo, e64)) * (1.0 / GLA_DV)
    out_c = o * lax.rsqrt(ms + EPS) * gn_ref[...] * (go * jax.nn.sigmoid(go))
    c_ref[...] = _rms(out_c, nc_ref[...]).astype(c_ref.dtype)


def _gla_call(g, row_off, B, S, s0t, lw_, out_dtype, gk_w, gv_w):
    ts = min(S, SEQ_TILE)
    nseq = math.gcd(B, max(SEQ_TILE // (2 * ts), 1)) if ts == S else 1
    rows = nseq * ts
    assert S % ts == 0 and row_off % rows == 0 and (ts % GLA_CHUNK == 0 or ts < GLA_CHUNK)
    n = S // ts
    off = row_off // rows
    gw = g.shape[1]
    full = lambda b, i: (0, 0)
    out_shape = (jax.ShapeDtypeStruct((B * S, gv_w), out_dtype),
                 jax.ShapeDtypeStruct((B, gv_w, gk_w), F32))
    return pl.pallas_call(
        functools.partial(_gla_kernel, ts=ts, nseq=nseq, gk_w=gk_w, gv_w=gv_w),
        grid=(B // nseq, n),
        in_specs=[pl.BlockSpec((rows, gw), lambda b, i: (off + b * n + i, 0)),
                  pl.BlockSpec((nseq, gv_w, gk_w), lambda b, i: (b, 0, 0)),
                  pl.BlockSpec((1, gv_w), full), pl.BlockSpec((1, gv_w), full)],
        out_specs=(pl.BlockSpec((rows, gv_w), lambda b, i: (b * n + i, 0)),
                   pl.BlockSpec((nseq, gv_w, gk_w), lambda b, i: (b, 0, 0))),
        out_shape=out_shape,
        scratch_shapes=[pltpu.VMEM((nseq, gv_w, gk_w), F32)],
        compiler_params=_cparams("parallel", "arbitrary"), name="gla",
    )(g, s0t, lw_['gla_g'], lw_['n_c'])


def _tree(op, xs):
    xs = list(xs)
    while len(xs) > 1:
        xs = [op(xs[i], xs[i + 1]) if i + 1 < len(xs) else xs[i] for i in range(0, len(xs), 2)]
    return xs[0]


def _attn_prompt_kernel(q_ref, k_ref, kt_ref, wuvt_ref, nb_ref, o_ref, m_scr, acc_scr, *, tq, kvr):
    i = pl.program_id(1)
    kw = 2 * LANES
    m_scr[...] = jnp.full(m_scr.shape, NEG_BIG, F32)
    acc_scr[...] = jnp.zeros(acc_scr.shape, F32)

    def block(j, masked):
        kb = k_ref[pl.ds(pl.multiple_of(j * tq, tq), tq), :]
        kbt = kt_ref[j]
        if masked:
            key = lax.broadcasted_iota(jnp.int32, (tq, tq), 0)
            qry = lax.broadcasted_iota(jnp.int32, (tq, tq), 1)
            keep = key <= qry
        qk = lambda h: _dot_nt(kb, q_ref[:, h * kw:(h + 1) * kw])
        st_next = qk(0)
        for h in range(MLA_HEADS):
            st = st_next
            if h + 1 < MLA_HEADS:
                st_next = qk(h + 1)
            if masked:
                st = jnp.where(keep, st, NEG_BIG)
            m_old = m_scr[h]
            m_new = jnp.maximum(m_old, jnp.max(st, axis=0, keepdims=True))
            alpha = jnp.exp2(m_old - m_new)
            pt = jnp.exp2(st - m_new).astype(BF16)
            acc_scr[h] = alpha * acc_scr[h] + _dot(kbt, pt)
            m_scr[h] = m_new

    def body(j, carry):
        block(j, False)
        return carry

    lax.fori_loop(0, i, body, 0)
    block(i, True)

    out_t = jnp.zeros((wuvt_ref.shape[0], tq), F32)
    for h in range(MLA_HEADS):
        acc = acc_scr[h]
        o_t = (acc[:kvr, :] / acc[kvr:kvr + 1, :]).astype(BF16)
        out_t = out_t + _dot(wuvt_ref[:, h * kvr:(h + 1) * kvr], o_t)
    ms = jnp.mean(out_t * out_t, axis=0, keepdims=True)
    out_t = out_t * lax.rsqrt(ms + EPS) * nb_ref[...]
    o_ref[...] = out_t.T.astype(o_ref.dtype)


def _attn_prompt_call(qcat, kcat, kcat_t, B, S, lw_, kvr):
    tq = kcat_t.shape[2]
    assert S % tq == 0
    n = S // tq
    mw = lw_['w_uv_rows_t'].shape[0]
    full = lambda b, i: (0, 0)
    return pl.pallas_call(
        functools.partial(_attn_prompt_kernel, tq=tq, kvr=kvr),
        grid=(B, n),
        in_specs=[pl.BlockSpec((tq, qcat.shape[1]), lambda b, i: (b * n + i, 0)),
                  pl.BlockSpec((S, kcat.shape[1]), lambda b, i: (b, 0)),
                  pl.BlockSpec((n, kcat_t.shape[1], tq), lambda b, i: (b, 0, 0)),
                  pl.BlockSpec(lw_['w_uv_rows_t'].shape, full), pl.BlockSpec((mw, 1), full)],
        out_specs=pl.BlockSpec((tq, mw), lambda b, i: (b * n + i, 0)),
        out_shape=jax.ShapeDtypeStruct((B * S, mw), BF16),
        scratch_shapes=[pltpu.VMEM((MLA_HEADS, 1, tq), F32),
                        pltpu.VMEM((MLA_HEADS, kcat_t.shape[1], tq), F32)],
        compiler_params=_cparams("parallel", "arbitrary"), name="attn_prompt",
    )(qcat, kcat, kcat_t, lw_['w_uv_rows_t'], lw_['n_b'].reshape(mw, 1))


def _attn_sample_kernel(pt_ref, q_ref, nckv_ref, nkpe_ref, ckv_hbm, kpe_hbm, wuv_ref, nb_ref, o_ref,
                        q_scr, s_scr, v_scr, ckv_buf, kpe_buf, sem, *, layer, n_pages, sq, kvr, page):
    b = pl.program_id(0)
    nrow = MLA_HEADS * sq

    def page_copies(bb):
        slot = lax.rem(bb, 2)
        out = []
        for j in range(n_pages):
            pg = pt_ref[bb * n_pages + j]
            out.append(pltpu.make_async_copy(ckv_hbm.at[layer, pg], ckv_buf.at[slot, j], sem.at[0, slot]))
            out.append(pltpu.make_async_copy(kpe_hbm.at[layer, pg], kpe_buf.at[slot, j], sem.at[1, slot]))
        return out

    @pl.when(b == 0)
    def _():
        for c in page_copies(b):
            c.start()

    @pl.when(b + 1 < pl.num_programs(0))
    def _():
        for c in page_copies(b + 1):
            c.start()

    for h in range(MLA_HEADS):
        q_scr[h * sq:(h + 1) * sq, :] = q_ref[:, 2 * h * LANES:(2 * h + 2) * LANES].astype(F32)
    qa = q_scr[:, :kvr].astype(BF16)
    qp = q_scr[:, kvr:kvr + QK_ROPE].astype(BF16)

    for c in page_copies(b):
        c.wait()
    slot = lax.rem(b, 2)
    gp = SCORE_PAGES
    for j in range(0, n_pages, gp):
        ck = ckv_buf[slot, j:j + gp].reshape(gp * page, kvr).astype(BF16)
        kp = jnp.concatenate([kpe_buf[slot, j + t] for t in range(gp)], axis=1).astype(BF16)
        v_scr[j * page:(j + gp) * page, :] = ck
        s_scr[:, j * page:(j + gp) * page] = _dot_nt(qa, ck) + _dot(qp, kp)

    zpad = lambda a: jnp.concatenate([a, jnp.zeros((page - sq, a.shape[1]), F32)], axis=0)
    ck_new = zpad(nckv_ref[...]).astype(BF16)
    kp_new = zpad(nkpe_ref[...]).astype(BF16)
    tok = lax.broadcasted_iota(jnp.int32, (nrow, page), 0) & (sq - 1)
    key = lax.broadcasted_iota(jnp.int32, (nrow, page), 1)
    s_new = jnp.where(key <= tok, _dot_nt(qa, ck_new) + _dot_nt(qp, kp_new), NEG_BIG)
    tile = lambda i: s_scr[:, i * page:(i + 1) * page]
    m = jnp.max(_tree(jnp.maximum, [tile(i) for i in range(n_pages)] + [s_new]), axis=-1, keepdims=True)
    p_new = jnp.exp2(s_new - m)
    acc = _dot(p_new.astype(BF16), ck_new)
    psum = p_new
    vp = VALUE_PAGES
    for i in range(0, n_pages, vp):
        p = jnp.exp2(s_scr[:, i * page:(i + vp) * page] - m)
        psum = psum + _tree(jnp.add, [p[:, t * page:(t + 1) * page] for t in range(vp)])
        acc = acc + _dot(p.astype(BF16), v_scr[i * page:(i + vp) * page, :])
    l = jnp.sum(psum, axis=-1, keepdims=True)
    o_lat = (acc / l).astype(BF16)
    mw = wuv_ref.shape[1]
    r = _dot(o_lat, wuv_ref[...])
    rh = lax.broadcasted_iota(jnp.int32, (nrow, mw), 0) >> int(math.log2(sq))
    ch = lax.broadcasted_iota(jnp.int32, (nrow, mw), 1) >> int(math.log2(V_HEAD))
    r = jnp.where(rh == ch, r, 0.0)
    out = r[0:sq, :]
    for h in range(1, MLA_HEADS):
        out = out + r[h * sq:(h + 1) * sq, :]
    o_ref[...] = _rms(out, nb_ref[...])


def _attn_sample_call(page_table, qs, nckv, nkpe, cache_ckv, cache_kpe_t, layer, lw_, kvr):
    B, sq, qw = qs.shape
    n_pages = page_table.shape[1]
    page = cache_ckv.shape[2]
    mw = lw_['w_uv_cat'].shape[1]
    nrow = MLA_HEADS * sq
    assert sq == SUBLANES and sq <= page and n_pages % SCORE_PAGES == 0 and n_pages % VALUE_PAGES == 0
    hbm = pl.BlockSpec(memory_space=pl.ANY)
    in_specs = [pl.BlockSpec((None, sq, qw), lambda b, pt: (b, 0, 0)),
                pl.BlockSpec((None, sq, kvr), lambda b, pt: (b, 0, 0)),
                pl.BlockSpec((None, sq, QK_ROPE), lambda b, pt: (b, 0, 0)),
                hbm, hbm,
                pl.BlockSpec(lw_['w_uv_cat'].shape, lambda b, pt: (0, 0)),
                pl.BlockSpec((1, mw), lambda b, pt: (0, 0))]
    grid_spec = pltpu.PrefetchScalarGridSpec(
        num_scalar_prefetch=1, grid=(B,), in_specs=in_specs,
        out_specs=pl.BlockSpec((None, sq, mw), lambda b, pt: (b, 0, 0)),
        scratch_shapes=[pltpu.VMEM((nrow, 2 * LANES), F32), pltpu.VMEM((nrow, n_pages * page), F32),
                        pltpu.VMEM((n_pages * page, kvr), BF16),
                        pltpu.VMEM((2, n_pages, page, kvr), F32), pltpu.VMEM((2, n_pages, QK_ROPE, page), F32),
                        pltpu.SemaphoreType.DMA((2, 2))])
    return pl.pallas_call(
        functools.partial(_attn_sample_kernel, layer=layer, n_pages=n_pages, sq=sq, kvr=kvr, page=page),
        grid_spec=grid_spec, out_shape=jax.ShapeDtypeStruct((B, sq, mw), F32),
        compiler_params=_cparams("arbitrary"), name="attn_sample",
    )(page_table.reshape(-1), qs, nckv, nkpe, cache_ckv, cache_kpe_t, lw_['w_uv_cat'], lw_['n_b'])


def _post_kernel(a_ref, b_ref, c_ref, *refs, widths, n_first):
    n_x = 1 if n_first is None else 2
    wo_ref, g2_ref, wr_ref, br_ref, x1_ref, hn_ref, rt_ref, cnt_ref, cnt_scr = refs[n_x:]
    wa, wb, wc = widths

    @pl.when(pl.program_id(0) == 0)
    def _():
        cnt_scr[...] = jnp.zeros(cnt_scr.shape, F32)

    mix = _dot(a_ref[...], wo_ref[:wa, :])
    mix = mix + _dot(b_ref[...], wo_ref[wa:wa + wb, :])
    mix = mix + _dot(c_ref[...], wo_ref[wa + wb:, :])
    x1 = _token_tile(refs[:n_x], n_first) + mix
    x1_ref[...] = x1
    hn = _rms(x1, g2_ref[...])
    for sub in range(hn_ref.shape[1]):
        hn_ref[:, sub, :] = hn[:, sub * LANES:(sub + 1) * LANES]
    hn_hi = hn.astype(BF16)
    hn_lo = (hn - hn_hi.astype(F32)).astype(BF16)
    hh = _dot(hn_hi, wr_ref[...])
    logits = (hh[:, :LANES] + hh[:, LANES:] + _dot(hn_lo, wr_ref[:, :LANES])) + br_ref[...]

    lane = lax.broadcasted_iota(jnp.int32, logits.shape, 1)
    lane_f = lane.astype(F32)
    big = float(LANES)
    is_g = lane < N_GROUPS
    gl = jnp.where(is_g, logits, NEG_BIG)
    gmax = jnp.max(gl, axis=-1, keepdims=True)
    gsel = jnp.min(jnp.where(gl == gmax, lane_f, big), axis=-1, keepdims=True)
    gprob = 1.0 / jnp.sum(jnp.where(is_g, jnp.exp(gl - gmax), 0.0), axis=-1, keepdims=True)
    lo = N_GROUPS + gsel * EXPERTS_PER_GROUP
    el = jnp.where((lane_f >= lo) & (lane_f < lo + EXPERTS_PER_GROUP), logits, NEG_BIG)
    v1 = jnp.max(el, axis=-1, keepdims=True)
    i1 = jnp.min(jnp.where(el == v1, lane_f, big), axis=-1, keepdims=True)
    el2 = jnp.where(lane_f == i1, NEG_BIG, el)
    v2 = jnp.max(el2, axis=-1, keepdims=True)
    i2 = jnp.min(jnp.where(el2 == v2, lane_f, big), axis=-1, keepdims=True)
    e21 = jnp.exp(v2 - v1)
    w1 = gprob / (1.0 + e21)
    w2 = w1 * e21
    e1 = i1 - N_GROUPS
    e2 = i2 - N_GROUPS

    oh1 = (lane_f == e1).astype(F32)
    oh2 = (lane_f == e2).astype(F32)
    both = oh1 + oh2
    tm = both.shape[0]
    tri = (lax.broadcasted_iota(jnp.int32, (tm, tm), 0) >= lax.broadcasted_iota(jnp.int32, (tm, tm), 1))
    incl = _dot(tri.astype(BF16), both.astype(BF16))
    base = cnt_scr[0:1, :] + incl - both
    r1 = jnp.sum(oh1 * base, axis=-1, keepdims=True)
    r2 = jnp.sum(oh2 * base, axis=-1, keepdims=True)
    cnt_scr[...] = cnt_scr[...] + incl[tm - 1:tm, :]
    cnt_ref[...] = cnt_scr[...]

    rt = jnp.where(lane == 0, e1, 0.0)
    rt = jnp.where(lane == 1, e2, rt)
    rt = jnp.where(lane == 2, w1, rt)
    rt = jnp.where(lane == 3, w2, rt)
    rt = jnp.where(lane == 4, r1, rt)
    rt = jnp.where(lane == 5, r2, rt)
    rt_ref[...] = rt


def _post_call(a, b, c, x, lw_):
    tm = TOKEN_TILE
    xs, x_specs, n_first, T = _token_specs(x, tm)
    D = xs[0].shape[1]
    row = lambda i: (i, 0)
    full = lambda i: (0, 0)
    widths = (a.shape[1], b.shape[1], c.shape[1])
    ws = [lw_['w_out'], lw_['g2'], lw_['w_r'], lw_['b_r']]
    out_shape = (jax.ShapeDtypeStruct((T, D), F32), jax.ShapeDtypeStruct((T, D // LANES, LANES), F32),
                 jax.ShapeDtypeStruct((T, LANES), F32), jax.ShapeDtypeStruct((SUBLANES, LANES), F32))
    return pl.pallas_call(
        functools.partial(_post_kernel, widths=widths, n_first=n_first),
        grid=(T // tm,),
        in_specs=[pl.BlockSpec((tm, w), row) for w in widths] + x_specs
        + [pl.BlockSpec(w.shape, full) for w in ws],
        out_specs=tuple(pl.BlockSpec((tm,) + s.shape[1:], (lambda i: (i, 0, 0)) if len(s.shape) == 3 else row)
                        for s in out_shape[:3])
        + (pl.BlockSpec((SUBLANES, LANES), full),),
        out_shape=out_shape, scratch_shapes=[pltpu.VMEM((SUBLANES, LANES), F32)],
        compiler_params=_cparams("arbitrary"), name="post_proj",
    )(a, b, c, *xs, *ws)


def _dest_kernel(rt_ref, ps_ref, d_ref):
    rt = rt_ref[...]
    lane = lax.broadcasted_iota(jnp.int32, rt.shape, 1)
    lane_f = lane.astype(F32)
    pick = lambda k: jnp.sum(jnp.where(lane == k, rt, 0.0), axis=-1, keepdims=True)
    ps = ps_ref[...]
    d1 = jnp.sum(jnp.where(lane_f == pick(0), ps, 0.0), axis=-1, keepdims=True) + pick(2 * TOP_K)
    d2 = jnp.sum(jnp.where(lane_f == pick(1), ps, 0.0), axis=-1, keepdims=True) + pick(2 * TOP_K + 1)
    d_ref[...] = jnp.where(lane == 0, d1, jnp.where(lane == 1, d2, 0.0))


def _dest_call(route, pstart_row):
    T = route.shape[0]
    tm = TOKEN_TILE
    return pl.pallas_call(
        _dest_kernel, grid=(T // tm,),
        in_specs=[pl.BlockSpec((tm, LANES), lambda i: (i, 0)), pl.BlockSpec((1, LANES), lambda i: (0, 0))],
        out_specs=pl.BlockSpec((tm, LANES), lambda i: (i, 0)),
        out_shape=jax.ShapeDtypeStruct((T, LANES), F32), compiler_params=_cparams("parallel"), name="moe_dest",
    )(route, pstart_row)


def _moe_kernel(te_ref, nt_ref, cnt_ref, tok_ref, hn_hbm, w_ref, wg_ref, wu_ref, wd_ref, y_ref, xbuf, sem):
    i = pl.program_id(0)
    tm = xbuf.shape[1]
    nt = nt_ref[0]

    def start_rows(t):
        slot = lax.rem(t, 2)

        def body(r, carry):
            row = tok_ref[t * tm + r]
            pltpu.make_async_copy(hn_hbm.at[row], xbuf.at[slot, r], sem.at[slot]).start()
            return carry

        lax.fori_loop(0, cnt_ref[t], body, 0)

    @pl.when(i == 0)
    def _():
        xbuf[...] = jnp.zeros(xbuf.shape, F32)
        start_rows(i)

    @pl.when(i + 1 < nt)
    def _():
        start_rows(i + 1)

    @pl.when(i < nt)
    def _():
        slot = lax.rem(i, 2)
        def wait_row(r, carry):
            pltpu.make_async_copy(hn_hbm.at[0], xbuf.at[slot, 0], sem.at[slot]).wait()
            return carry

        lax.fori_loop(0, cnt_ref[i], wait_row, 0)
        x = jnp.concatenate([xbuf[slot, :, sub, :] for sub in range(xbuf.shape[2])], axis=1).astype(BF16)
        hg = _dot(x, wg_ref[...].astype(BF16))
        hu = _dot(x, wu_ref[...].astype(BF16))
        act = hg * jax.nn.sigmoid(hg) * hu * w_ref[...]
        y_ref[...] = _dot(act.astype(BF16), wd_ref[...].astype(BF16))

    @pl.when(i >= nt)
    def _():
        y_ref[...] = jnp.zeros(y_ref.shape, F32)


def _moe_call(hn, row_tok, row_w, tile_e, n_used, tile_cnt, w_gate, w_up, w_down, layer):
    D = w_gate.shape[-2]
    tm = MOE_TILE
    F = w_gate.shape[-1]
    n_tiles = tile_e.shape[0]
    grid_spec = pltpu.PrefetchScalarGridSpec(
        num_scalar_prefetch=4, grid=(n_tiles,),
        in_specs=[pl.BlockSpec(memory_space=pl.ANY),
                  pl.BlockSpec((tm, 1), lambda i, te, nt, cnt, tok: (i, 0)),
                  pl.BlockSpec((None, None, D, F), lambda i, te, nt, cnt, tok: (layer, te[i], 0, 0)),
                  pl.BlockSpec((None, None, D, F), lambda i, te, nt, cnt, tok: (layer, te[i], 0, 0)),
                  pl.BlockSpec((None, None, F, D), lambda i, te, nt, cnt, tok: (layer, te[i], 0, 0))],
        out_specs=pl.BlockSpec((tm, D), lambda i, te, nt, cnt, tok: (i, 0)),
        scratch_shapes=[pltpu.VMEM((2, tm) + hn.shape[1:], F32), pltpu.SemaphoreType.DMA((2,))])
    return pl.pallas_call(
        _moe_kernel, grid_spec=grid_spec, out_shape=jax.ShapeDtypeStruct((n_tiles * tm, D), F32),
        compiler_params=_cparams("arbitrary"), name="moe_experts",
    )(tile_e, n_used, tile_cnt, row_tok, hn, row_w, w_gate, w_up, w_down)


def _route_meta(route, counts_f, tm, n_tiles):
    T = route.shape[0]
    n_assign = TOP_K * T
    e = route[:, :TOP_K].astype(jnp.int32).reshape(-1)
    w = route[:, TOP_K:2 * TOP_K].reshape(-1)
    order = jnp.argsort(e, stable=True).astype(jnp.int32)
    counts = counts_f.astype(jnp.int32)
    zero = jnp.zeros((1,), jnp.int32)
    start = jnp.concatenate([zero, jnp.cumsum(counts)])
    pstart = jnp.concatenate([zero, jnp.cumsum(((counts + tm - 1) // tm) * tm)])
    n_used = (pstart[N_EXPERTS] // tm).reshape(1)
    tile_lo = jnp.arange(n_tiles, dtype=jnp.int32) * tm
    tile_e = jnp.sum((pstart[None, 1:] <= tile_lo[:, None]).astype(jnp.int32), axis=1)
    tile_e = jnp.minimum(tile_e, N_EXPERTS - 1)
    oh = (tile_e[:, None] == jnp.arange(N_EXPERTS, dtype=jnp.int32)[None, :]).astype(jnp.int32)
    pick = lambda tab: jnp.sum(oh * tab[None, :N_EXPERTS], axis=1)
    k0 = tile_lo - pick(pstart)
    tile_cnt = jnp.clip(pick(counts) - k0, 0, tm)
    k = k0[:, None] + jnp.arange(tm, dtype=jnp.int32)[None, :]
    valid = (k < pick(counts)[:, None]).reshape(-1)
    idx = jnp.clip(pick(start)[:, None] + k, 0, n_assign - 1).reshape(-1)
    src = order[idx]
    row_tok = jnp.where(valid, src // TOP_K, 0)
    row_w = jnp.where(valid, w[src], 0.0)
    ps_row = jnp.pad(pstart[:N_EXPERTS].astype(F32), (0, LANES - N_EXPERTS)).reshape(1, LANES)
    pos = _dest_call(route, ps_row)[:, :TOP_K].astype(jnp.int32)
    return row_tok, row_w.reshape(-1, 1), pos, tile_e, n_used, tile_cnt


def _final_kernel(x_ref, g_ref, o_ref):
    o_ref[...] = _rms(x_ref[...], g_ref[...])


def _final_call(x, g, row_off, n_rows):
    D = x.shape[1]
    tm = TOKEN_TILE
    assert row_off % tm == 0 and n_rows % tm == 0
    off = row_off // tm
    return pl.pallas_call(
        _final_kernel, grid=(n_rows // tm,),
        in_specs=[pl.BlockSpec((tm, D), lambda i: (off + i, 0)), pl.BlockSpec((1, D), lambda i: (0, 0))],
        out_specs=pl.BlockSpec((tm, D), lambda i: (i, 0)),
        out_shape=jax.ShapeDtypeStruct((n_rows, D), F32), compiler_params=_cparams("parallel"), name="final_norm",
    )(x, g)


def _block_diag(w):
    n, c, d = w.shape
    return jnp.einsum('ncd,nm->ncmd', w, jnp.eye(n, dtype=w.dtype)).reshape(n * c, n * d)


def _half_swap(w):
    half = w.shape[-1] // 2
    return jnp.concatenate([w[..., half:], w[..., :half]], axis=-1)


def _layer_weights(l, p, dims):
    lw, qr, kvr, gk_w, gv_w = dims
    D = p['w_in'].shape[1]
    row = lambda a: a.reshape(1, -1).astype(F32)
    w_in = p['w_in'][l]
    sizes = (lw, lw, qr, kvr, QK_ROPE, gk_w, gk_w, gv_w, p['gla_wa2'].shape[1], gv_w)
    x_lru, y_lru, c_q, c_kv, k_pe, g_q, g_k, g_v, g_a, g_o = jnp.split(w_in, list(np.cumsum(sizes)[:-1]), axis=1)
    tail_pad = LANES - 2 * QK_ROPE - g_a.shape[1]
    tail = jnp.concatenate([k_pe, _half_swap(k_pe), g_a, jnp.zeros((D, tail_pad), F32)], axis=1)
    w_in_p = jnp.concatenate([x_lru, y_lru, c_q, c_kv, g_q, g_k, g_v, g_o, tail], axis=1).astype(BF16)

    w_uq = p['w_uq'][l].reshape(qr, MLA_HEADS, QK_NOPE + QK_ROPE)
    nope = w_uq[:, :, :QK_NOPE].reshape(qr, MLA_HEADS * QK_NOPE)
    pe = w_uq[:, :, QK_NOPE:]
    widen = lambda a: jnp.pad(a, ((0, 0), (0, 0), (0, LANES - QK_ROPE))).reshape(qr, MLA_HEADS * LANES)
    w_q = jnp.concatenate([nope, widen(pe), widen(_half_swap(pe))], axis=1).astype(BF16)
    w_abs = _block_diag(jnp.transpose(p['w_uk'][l], (1, 2, 0))).astype(BF16)
    wa2 = jnp.zeros((LANES, gk_w), F32).at[2 * QK_ROPE:2 * QK_ROPE + g_a.shape[1]].set(p['gla_wa2'][l]).astype(BF16)

    w_uv = p['w_uv'][l]
    mw = MLA_HEADS * V_HEAD
    w_uv_cat = w_uv.reshape(kvr, mw)
    w_uv_rows = _block_diag(jnp.transpose(w_uv, (1, 0, 2)))
    n_a, n_b, n_c = jnp.split(p['out_norm_g'][l], [lw, lw + mw])
    n_r = N_GROUPS + N_EXPERTS
    w_r = jnp.concatenate([p['router_wg'][l], p['router_we'][l], jnp.zeros((D, LANES - n_r), F32)], axis=1)
    b_r = jnp.concatenate([p['router_bg'][l], p['router_be'][l], jnp.zeros((LANES - n_r,), F32)])
    return dict(
        g1=row(p['norm1_g'][l]), w_in=w_in_p, q_g=row(p['q_norm_g'][l]), w_q=w_q, w_abs=w_abs,
        kv_g=row(p['kv_norm_g'][l]), wa2=wa2, ba=row(p['gla_ba'][l]),
        conv_w=p['conv_w'][l], conv_b=row(p['conv_b'][l]),
        lru_wa=_block_diag(p['lru_wa'][l]).astype(BF16), lru_ba=row(p['lru_ba'][l]),
        lru_wi=_block_diag(p['lru_wi'][l]).astype(BF16), lru_bi=row(p['lru_bi'][l]),
        lru_lam=row(p['lru_lambda'][l]), n_a=row(n_a), n_b=row(n_b), n_c=row(n_c),
        gla_g=row(jnp.tile(p['gla_norm_g'][l], GLA_HEADS)),
        w_uv_cat=w_uv_cat.astype(BF16), w_uv_rows_t=w_uv_rows.T.astype(BF16),
        w_out=p['w_out'][l].astype(BF16), g2=row(p['norm2_g'][l]),
        w_r=jnp.concatenate([w_r.astype(BF16), (w_r - w_r.astype(BF16).astype(F32)).astype(BF16)], axis=1),
        b_r=row(b_r),
    )


def _rope_tables(positions):
    half = QK_ROPE // 2
    inv = ROPE_THETA ** (-np.arange(half, dtype=np.float64) / half)
    ang = np.asarray(positions, np.float64)[:, None] * inv
    zeros = np.zeros((ang.shape[0], LANES - QK_ROPE))
    cos = np.concatenate([np.cos(ang), np.cos(ang), zeros], axis=1)
    sin = np.concatenate([-np.sin(ang), np.sin(ang), zeros], axis=1)
    return jnp.asarray(cos, F32), jnp.asarray(sin, F32)


def _gla_state_to_rows(s):
    B = s.shape[0]
    eye = jnp.eye(GLA_HEADS, dtype=s.dtype)
    return jnp.einsum('bhde,hg->bhegd', s, eye).reshape(B, GLA_HEADS * GLA_DV, GLA_HEADS * GLA_DK)


def _gla_rows_to_state(st):
    B = st.shape[0]
    s5 = st.reshape(B, GLA_HEADS, GLA_DV, GLA_HEADS, GLA_DK)
    blocks = jnp.stack([s5[:, h, :, h, :] for h in range(GLA_HEADS)], axis=1)
    return jnp.swapaxes(blocks, -1, -2)


def kernel(x_prompt, x_sample, cache_ckv, cache_kpe, page_table, state_conv, state_lru, state_gla, norm1_g, w_in, conv_w, conv_b, lru_wa, lru_ba, lru_wi, lru_bi, lru_lambda, q_norm_g, w_uq, kv_norm_g, w_uk, w_uv, gla_wa2, gla_ba, gla_norm_g, out_norm_g, w_out, norm2_g, router_wg, router_bg, router_we, router_be, w_gate, w_up, w_down, final_norm_g):
    p = dict(norm1_g=norm1_g, w_in=w_in, conv_w=conv_w, conv_b=conv_b, lru_wa=lru_wa, lru_ba=lru_ba,
             lru_wi=lru_wi, lru_bi=lru_bi, lru_lambda=lru_lambda, q_norm_g=q_norm_g, w_uq=w_uq,
             kv_norm_g=kv_norm_g, w_uk=w_uk, w_uv=w_uv, gla_wa2=gla_wa2, gla_ba=gla_ba,
             gla_norm_g=gla_norm_g, out_norm_g=out_norm_g, w_out=w_out, norm2_g=norm2_g,
             router_wg=router_wg, router_bg=router_bg, router_we=router_we, router_be=router_be)
    Bp, Sp, D = x_prompt.shape
    Bs, Ss, _ = x_sample.shape
    depth = w_in.shape[0]
    lw = state_lru.shape[-1]
    qr = q_norm_g.shape[-1]
    kvr = kv_norm_g.shape[-1]
    gk_w = GLA_HEADS * GLA_DK
    gv_w = GLA_HEADS * GLA_DV
    dims = (lw, qr, kvr, gk_w, gv_w)
    assert kvr == LANES and gk_w == LANES
    Tp, Ts = Bp * Sp, Bs * Ss
    T = Tp + Ts
    past = page_table.shape[1] * cache_ckv.shape[2]

    cos_t, sin_t = _rope_tables(np.concatenate([np.tile(np.arange(Sp), Bp), np.tile(past + np.arange(Ss), Bs)]))
    cache_kpe_t = jnp.swapaxes(cache_kpe, 2, 3)
    n_tiles = (TOP_K * T + N_EXPERTS * (MOE_TILE - 1)) // MOE_TILE + 1

    x = (x_prompt.reshape(Tp, D), x_sample.reshape(Ts, D))
    outs = {k: [] for k in ('ckv_p', 'kpe_p', 'ckv_s', 'kpe_s', 'conv_p', 'conv_s', 'lru_p', 'lru_s', 'gla_p', 'gla_s')}
    for l in range(depth):
        lw_ = _layer_weights(l, p, dims)
        zl, qcat, kcat, kcat_t, ckv_n, kpe_n, g = _pre_call(x, lw_, cos_t, sin_t, dims)

        a_p, conv_p, lru_p = _lru_call(zl, 0, Bp, Sp, jnp.zeros((Bp, CONV_WIDTH - 1, lw), F32),
                                       jnp.zeros((Bp, lw), F32), lw_, BF16)
        a_s, lru_s = _lru_seg_call(zl, Tp, Bs, Ss, state_conv[l], state_lru[l], lw_)
        conv_s = zl[Tp:, :lw].reshape(Bs, Ss, lw)[:, Ss - (CONV_WIDTH - 1):]

        c_p, gla_p = _gla_call(g, 0, Bp, Sp, jnp.zeros((Bp, gv_w, gk_w), F32), lw_, BF16, gk_w, gv_w)
        c_s, gla_s = _gla_call(g, Tp, Bs, Ss, _gla_state_to_rows(state_gla[l]), lw_, BF16, gk_w, gv_w)

        b_p = _attn_prompt_call(qcat, kcat, kcat_t, Bp, Sp, lw_, kvr)
        b_s = _attn_sample_call(page_table, qcat[Tp:].reshape(Bs, Ss, -1), ckv_n[Tp:].reshape(Bs, Ss, kvr),
                                kpe_n[Tp:].reshape(Bs, Ss, QK_ROPE), cache_ckv, cache_kpe_t, l, lw_, kvr)

        a = jnp.concatenate([a_p, a_s], axis=0)
        b = jnp.concatenate([b_p, b_s.reshape(Ts, -1).astype(BF16)], axis=0)
        c = jnp.concatenate([c_p, c_s], axis=0)
        x1, hn, route, cnt = _post_call(a, b, c, x, lw_)

        row_tok, row_w, pos, tile_e, n_used, tile_cnt = _route_meta(route, cnt[0, :N_EXPERTS], MOE_TILE, n_tiles)
        ys = _moe_call(hn, row_tok, row_w, tile_e, n_used, tile_cnt, w_gate, w_up, w_down, l)
        x = x1 + ys[pos[:, 0]] + ys[pos[:, 1]]

        outs['ckv_p'].append(ckv_n[:Tp].reshape(Bp, Sp, kvr))
        outs['kpe_p'].append(kpe_n[:Tp].reshape(Bp, Sp, QK_ROPE))
        outs['ckv_s'].append(ckv_n[Tp:].reshape(Bs, Ss, kvr))
        outs['kpe_s'].append(kpe_n[Tp:].reshape(Bs, Ss, QK_ROPE))
        outs['conv_p'].append(conv_p)
        outs['conv_s'].append(conv_s)
        outs['lru_p'].append(lru_p.reshape(Bp, lw))
        outs['lru_s'].append(lru_s.reshape(Bs, lw))
        outs['gla_p'].append(_gla_rows_to_state(gla_p))
        outs['gla_s'].append(_gla_rows_to_state(gla_s))

    g_fin = final_norm_g.reshape(1, D)
    y_p, y_s = _final_call(x, g_fin, 0, Tp), _final_call(x, g_fin, Tp, Ts)
    st = {k: jnp.stack(v) for k, v in outs.items()}
    return (y_p.reshape(Bp, Sp, D), y_s.reshape(Bs, Ss, D),
            st['ckv_p'], st['kpe_p'], st['ckv_s'], st['kpe_s'], st['conv_p'], st['conv_s'],
            st['lru_p'], st['lru_s'], st['gla_p'], st['gla_s'])
```

```python
import functools
import math

import numpy as np
import jax
import jax.numpy as jnp
from jax import lax
from jax.experimental import pallas as pl
from jax.experimental.pallas import tpu as pltpu

F32 = jnp.float32
BF16 = jnp.bfloat16

LRU_BLOCKS = 4
CONV_WIDTH = 4
LRU_C = 8.0
MLA_HEADS = 8
QK_NOPE = 64
QK_ROPE = 32
V_HEAD = 64
ROPE_THETA = 10000.0
GLA_HEADS = 4
GLA_DK = 32
GLA_DV = 64
GLA_TAU = 16.0
GLA_CHUNK = 16
N_GROUPS = 4
EXPERTS_PER_GROUP = 8
N_EXPERTS = N_GROUPS * EXPERTS_PER_GROUP
TOP_K = 2
EPS = 1e-6

LANES = 128
SUBLANES = 8
BF16_ROWS = 16
VMEM_LIMIT_BYTES = 56 * 1024 * 1024

TOKEN_TILE = 512
SEQ_TILE = 256
MOE_TILE = 256
SCORE_PAGES = 4
VALUE_PAGES = 2
NEG_BIG = -1e30


def _cparams(*sem):
    return pltpu.CompilerParams(dimension_semantics=sem, vmem_limit_bytes=VMEM_LIMIT_BYTES)


def _rms(x, g):
    return x * lax.rsqrt(jnp.mean(x * x, axis=-1, keepdims=True) + EPS) * g


def _dot(a, b):
    return jnp.dot(a, b, preferred_element_type=F32)


def _dot_nt(a, b):
    return lax.dot_general(a, b, (((1,), (1,)), ((), ())), preferred_element_type=F32)


def _softplus(x):
    return jnp.maximum(x, 0.0) + jnp.log1p(jnp.exp(-jnp.abs(x)))


def _token_tile(refs, n_first):
    if n_first is None:
        return refs[0][...]
    return jnp.where(pl.program_id(0) < n_first, refs[0][...], refs[1][...])


def _token_specs(x, tm):
    if not isinstance(x, tuple):
        return [x], [pl.BlockSpec((tm, x.shape[1]), lambda i: (i, 0))], None, x.shape[0]
    a, b = x
    assert a.shape[0] % tm == 0 and b.shape[0] % tm == 0
    n_first = a.shape[0] // tm
    specs = [pl.BlockSpec((tm, a.shape[1]), lambda i: (jnp.minimum(i, n_first - 1), 0)),
             pl.BlockSpec((tm, b.shape[1]), lambda i: (jnp.maximum(i - n_first, 0), 0))]
    return [a, b], specs, n_first, a.shape[0] + b.shape[0]


def _shift_rows(x, d, fill):
    row = lax.broadcasted_iota(jnp.int32, x.shape, 0)
    return jnp.where(row >= d, pltpu.roll(x, d, 0), fill)


def _pre_kernel(*refs, dims, n_first):
    n_x = 1 if n_first is None else 2
    (g1_ref, win_ref, qg_ref, wq_ref, wabs_ref, kvg_ref, wa2_ref, ba_ref, cos_ref, sin_ref,
     zl_ref, q_ref, k_ref, kt_ref, ckv_ref, kpe_ref, g_ref) = refs[n_x:]
    lw, qr, kvr, gk_w, gv_w = dims
    xn = _rms(_token_tile(refs[:n_x], n_first), g1_ref[...])
    z = _dot(xn.astype(BF16), win_ref[...])
    o = 2 * lw
    zl_ref[...] = z[:, :o]
    cq = z[:, o:o + qr]
    o += qr
    ckv = z[:, o:o + kvr]
    o += kvr
    gqk = z[:, o:o + 2 * gk_w]
    o += 2 * gk_w
    gvo = z[:, o:o + 2 * gv_w]
    o += 2 * gv_w
    tail = z[:, o:o + LANES]
    cos = cos_ref[...]
    sin = sin_ref[...]
    scale = (QK_NOPE + QK_ROPE) ** -0.5 * math.log2(math.e)

    ckv_n = _rms(ckv, kvg_ref[...])
    kpe = tail * cos + pltpu.roll(tail, LANES - QK_ROPE, 1) * sin
    ckv_ref[...] = ckv_n
    kpe_ref[...] = kpe[:, :QK_ROPE]
    k_ref[:, :kvr] = ckv_n.astype(BF16)
    k_ref[:, kvr:] = kpe.astype(BF16)
    ones = jnp.ones((kt_ref.shape[0] - kvr, ckv_n.shape[0]), F32)
    kt_ref[...] = jnp.concatenate([ckv_n.T, ones], axis=0).astype(BF16)

    cqn = _rms(cq, qg_ref[...]).astype(BF16)
    qall = _dot(cqn, wq_ref[...])
    n_nope = MLA_HEADS * QK_NOPE
    q_abs = _dot(qall[:, :n_nope].astype(BF16), wabs_ref[...]) * scale
    for h in range(MLA_HEADS):
        pe = qall[:, n_nope + h * LANES:n_nope + (h + 1) * LANES]
        sw = qall[:, n_nope + (MLA_HEADS + h) * LANES:n_nope + (MLA_HEADS + h + 1) * LANES]
        q_ref[:, 2 * h * LANES:(2 * h + 1) * LANES] = q_abs[:, h * kvr:(h + 1) * kvr].astype(BF16)
        q_ref[:, (2 * h + 1) * LANES:(2 * h + 2) * LANES] = ((pe * cos + sw * sin) * scale).astype(BF16)

    la_pre = _dot(tail.astype(BF16), wa2_ref[...]) + ba_ref[...]
    la = -_softplus(-la_pre) * (1.0 / GLA_TAU)
    g_ref[:, :gk_w] = gqk[:, :gk_w] * (GLA_DK ** -0.5)
    g_ref[:, gk_w:2 * gk_w] = gqk[:, gk_w:]
    g_ref[:, 2 * gk_w:3 * gk_w] = la
    g_ref[:, 3 * gk_w:] = gvo


def _pre_call(x, lw_, cos_t, sin_t, dims):
    lw, qr, kvr, gk_w, gv_w = dims
    tm = TOKEN_TILE
    xs, x_specs, n_first, T = _token_specs(x, tm)
    assert T % tm == 0
    row = lambda i: (i, 0)
    full = lambda i: (0, 0)
    wspec = lambda a: pl.BlockSpec(a.shape, full)
    ws = [lw_['g1'], lw_['w_in'], lw_['q_g'], lw_['w_q'], lw_['w_abs'], lw_['kv_g'], lw_['wa2'], lw_['ba']]
    ins = xs + ws
    in_specs = x_specs + [wspec(a) for a in ws]
    in_specs += [pl.BlockSpec((tm, LANES), row), pl.BlockSpec((tm, LANES), row)]
    out_shape = (
        jax.ShapeDtypeStruct((T, 2 * lw), F32),
        jax.ShapeDtypeStruct((T, 2 * LANES * MLA_HEADS), BF16),
        jax.ShapeDtypeStruct((T, 2 * LANES), BF16),
        jax.ShapeDtypeStruct((T // tm, kvr + BF16_ROWS, tm), BF16),
        jax.ShapeDtypeStruct((T, kvr), F32),
        jax.ShapeDtypeStruct((T, QK_ROPE), F32),
        jax.ShapeDtypeStruct((T, 3 * gk_w + 2 * gv_w), F32),
    )
    out_specs = tuple(pl.BlockSpec((None, s.shape[1], tm), lambda i: (i, 0, 0)) if len(s.shape) == 3
                      else pl.BlockSpec((tm, s.shape[1]), row) for s in out_shape)
    return pl.pallas_call(
        functools.partial(_pre_kernel, dims=dims, n_first=n_first),
        grid=(T // tm,), in_specs=in_specs, out_specs=out_specs, out_shape=out_shape,
        compiler_params=_cparams("parallel"), name="pre_proj",
    )(*ins, cos_t, sin_t)


def _lru_kernel(zl_ref, cbuf_ref, h0_ref, cw_ref, cb_ref, wa_ref, ba_ref, wi_ref, bi_ref, lam_ref, na_ref,
                a_ref, conv_ref, hout_ref, xbuf, hcar, *, ts, lw):
    i = pl.program_id(1)
    last = pl.num_programs(1) - 1
    pad = SUBLANES
    nbuf = CONV_WIDTH - 1

    @pl.when(i == 0)
    def _():
        xbuf[0:pad, :] = jnp.zeros((pad, lw), F32)
        xbuf[pad - nbuf:pad, :] = cbuf_ref[...]
        hcar[...] = h0_ref[...]

    x = zl_ref[:, :lw]
    y = zl_ref[:, lw:]
    xbuf[pad:pad + ts, :] = x
    xc = cb_ref[...] + cw_ref[nbuf:nbuf + 1, :] * x
    for k in range(nbuf):
        xc = xc + cw_ref[k:k + 1, :] * xbuf[pad - nbuf + k:pad - nbuf + k + ts, :]

    @pl.when(i == last)
    def _():
        conv_ref[...] = xbuf[pad + ts - nbuf:pad + ts, :]

    xbuf[0:pad, :] = xbuf[ts:ts + pad, :]

    xb = xc.astype(BF16)
    r = jax.nn.sigmoid(_dot(xb, wa_ref[...]) + ba_ref[...])
    gi = jax.nn.sigmoid(_dot(xb, wi_ref[...]) + bi_ref[...])
    log_a = (-LRU_C) * r * _softplus(-lam_ref[...])
    a = jnp.exp(log_a)
    th = jnp.tanh(log_a)
    u = jnp.sqrt(-2.0 * th / (1.0 - th)) * (gi * xc)

    d = 1
    while d < ts:
        u = a * _shift_rows(u, d, 0.0) + u
        a = a * _shift_rows(a, d, 1.0)
        d *= 2
    h = a * hcar[...] + u
    hcar[...] = h[ts - 1:ts, :]

    @pl.when(i == last)
    def _():
        hout_ref[...] = h[ts - 1:ts, :]

    out_a = h * jax.nn.gelu(y)
    a_ref[...] = _rms(out_a, na_ref[...]).astype(a_ref.dtype)


def _lru_call(zl, row_off, B, S, cbuf, h0, lw_, out_dtype):
    lw = h0.shape[-1]
    ts = min(S, SEQ_TILE)
    assert S % ts == 0 and row_off % ts == 0 and S >= CONV_WIDTH - 1
    n = S // ts
    off = row_off // ts
    full = lambda b, i: (0, 0)
    ws = [lw_['conv_w'], lw_['conv_b'], lw_['lru_wa'], lw_['lru_ba'], lw_['lru_wi'], lw_['lru_bi'],
          lw_['lru_lam'], lw_['n_a']]
    in_specs = [
        pl.BlockSpec((ts, 2 * lw), lambda b, i: (off + b * n + i, 0)),
        pl.BlockSpec((None, CONV_WIDTH - 1, lw), lambda b, i: (b, 0, 0)),
        pl.BlockSpec((None, 1, lw), lambda b, i: (b, 0, 0)),
    ] + [pl.BlockSpec(a.shape, full) for a in ws]
    out_shape = (
        jax.ShapeDtypeStruct((B * S, lw), out_dtype),
        jax.ShapeDtypeStruct((B, CONV_WIDTH - 1, lw), F32),
        jax.ShapeDtypeStruct((B, 1, lw), F32),
    )
    out_specs = (
        pl.BlockSpec((ts, lw), lambda b, i: (b * n + i, 0)),
        pl.BlockSpec((None, CONV_WIDTH - 1, lw), lambda b, i: (b, 0, 0)),
        pl.BlockSpec((None, 1, lw), lambda b, i: (b, 0, 0)),
    )
    return pl.pallas_call(
        functools.partial(_lru_kernel, ts=ts, lw=lw),
        grid=(B, n), in_specs=in_specs, out_specs=out_specs, out_shape=out_shape,
        scratch_shapes=[pltpu.VMEM((ts + SUBLANES, lw), F32), pltpu.VMEM((1, lw), F32)],
        compiler_params=_cparams("parallel", "arbitrary"), name="rg_lru",
    )(zl, cbuf, h0.reshape(B, 1, lw), *ws)


def _lru_seg_kernel(zl_ref, hist_ref, h0_ref, cw_ref, cb_ref, wa_ref, ba_ref, wi_ref, bi_ref, lam_ref, na_ref,
                    a_ref, h_ref, *, seg, lw):
    nbuf = CONV_WIDTH - 1
    x = zl_ref[:, :lw]
    y = zl_ref[:, lw:]
    n = x.shape[0]
    pos = lax.broadcasted_iota(jnp.int32, (n, 1), 0) & (seg - 1)
    hist = hist_ref[...]
    xc = cb_ref[...] + cw_ref[nbuf:nbuf + 1, :] * x
    for k in range(nbuf):
        j = nbuf - k
        prev = jnp.where(pos >= j, pltpu.roll(x, j, 0), pltpu.roll(hist, (j - seg) % n, 0))
        xc = xc + cw_ref[k:k + 1, :] * prev
    xb = xc.astype(BF16)
    r = jax.nn.sigmoid(_dot(xb, wa_ref[...]) + ba_ref[...])
    gi = jax.nn.sigmoid(_dot(xb, wi_ref[...]) + bi_ref[...])
    log_a = (-LRU_C) * r * _softplus(-lam_ref[...])
    a = jnp.exp(log_a)
    th = jnp.tanh(log_a)
    u = jnp.sqrt(-2.0 * th / (1.0 - th)) * (gi * xc)
    d = 1
    while d < seg:
        u = a * jnp.where(pos >= d, pltpu.roll(u, d, 0), 0.0) + u
        a = a * jnp.where(pos >= d, pltpu.roll(a, d, 0), 1.0)
        d *= 2
    h = a * h0_ref[...] + u
    h_ref[...] = h
    a_ref[...] = _rms(h * jax.nn.gelu(y), na_ref[...]).astype(a_ref.dtype)


def _lru_seg_call(zl, row_off, B, S, cbuf, h0, lw_):
    lw = h0.shape[-1]
    n = B * S
    tm = min(n, TOKEN_TILE)
    assert S & (S - 1) == 0 and S >= CONV_WIDTH - 1 and tm % S == 0 and n % tm == 0 and row_off % tm == 0
    off = row_off // tm
    hist = jnp.pad(cbuf, ((0, 0), (S - (CONV_WIDTH - 1), 0), (0, 0))).reshape(n, lw)
    h0_rows = jnp.repeat(h0, S, axis=0)
    full = lambda i: (0, 0)
    row = lambda i: (i, 0)
    ws = [lw_['conv_w'], lw_['conv_b'], lw_['lru_wa'], lw_['lru_ba'], lw_['lru_wi'], lw_['lru_bi'],
          lw_['lru_lam'], lw_['n_a']]
    a, h = pl.pallas_call(
        functools.partial(_lru_seg_kernel, seg=S, lw=lw),
        grid=(n // tm,),
        in_specs=[pl.BlockSpec((tm, 2 * lw), lambda i: (off + i, 0)), pl.BlockSpec((tm, lw), row),
                  pl.BlockSpec((tm, lw), row)] + [pl.BlockSpec(w.shape, full) for w in ws],
        out_specs=(pl.BlockSpec((tm, lw), row), pl.BlockSpec((tm, lw), row)),
        out_shape=(jax.ShapeDtypeStruct((n, lw), BF16), jax.ShapeDtypeStruct((n, lw), F32)),
        compiler_params=_cparams("parallel"), name="rg_lru_seg",
    )(zl, hist, h0_rows, *ws)
    return a, h.reshape(B, S, lw)[:, S - 1]


def _gla_kernel(g_ref, s0_ref, gn_ref, nc_ref, c_ref, sout_ref, st, *, ts, nseq, gk_w, gv_w):
    i = pl.program_id(1)
    last = pl.num_programs(1) - 1
    C = GLA_CHUNK
    rows = min(ts, C)
    n_chunks = max(ts // C, 1)
    log2c = int(math.log2(C))

    @pl.when(i == 0)
    def _():
        st[...] = s0_ref[...]

    dk_sh, dv_sh = int(math.log2(GLA_DK)), int(math.log2(GLA_DV))
    hd = lax.broadcasted_iota(jnp.int32, (gk_w, gv_w), 0) >> dk_sh
    he = lax.broadcasted_iota(jnp.int32, (gk_w, gv_w), 1) >> dv_sh
    same = (hd == he).astype(BF16)
    he_t = lax.broadcasted_iota(jnp.int32, (gv_w, gk_w), 0) >> dv_sh
    hd_t = lax.broadcasted_iota(jnp.int32, (gv_w, gk_w), 1) >> dk_sh
    same_t = (he_t == hd_t).astype(F32)
    sel_t = lax.broadcasted_iota(jnp.int32, (C, C * C), 0)
    sel_r = lax.broadcasted_iota(jnp.int32, (C, C * C), 1) >> log2c
    sel = (sel_t == sel_r).astype(BF16)
    srow = lax.broadcasted_iota(jnp.int32, (C, 1), 0)

    seq_out = []
    for sq in range(nseq):
        blk = g_ref[sq * ts:(sq + 1) * ts, :]
        if rows < C:
            blk = jnp.concatenate([blk, jnp.zeros((C - rows, blk.shape[1]), F32)], axis=0)
        q = blk[:, :gk_w]
        k = blk[:, gk_w:2 * gk_w]
        la = blk[:, 2 * gk_w:3 * gk_w]
        v = blk[:, 3 * gk_w:3 * gk_w + gv_w]
        nrow = n_chunks * C
        pos = lax.broadcasted_iota(jnp.int32, (nrow, 1), 0) & (C - 1)
        cum = la
        d = 1
        while d < C:
            cum = cum + jnp.where(pos >= d, pltpu.roll(cum, d, 0), 0.0)
            d *= 2
        tot = jnp.where(pos == C - 1, cum, 0.0)
        d = 1
        while d < C:
            tot = tot + jnp.where(pos < C - d, pltpu.roll(tot, nrow - d, 0), 0.0)
            d *= 2
        qe = (q * jnp.exp(cum)).astype(BF16)
        kdec = (k * jnp.exp(tot - cum)).astype(BF16)
        dec = jnp.exp(tot)
        vb = v.astype(BF16)

        o_intra, upd = [], []
        for c in range(n_chunks):
            sl = slice(c * C, (c + 1) * C)
            cum_c, q_c, k_c, v_c = cum[sl], q[sl], k[sl], v[sl]
            pieces = []
            for t in range(C):
                diff = jnp.where(srow <= t, cum_c[t:t + 1, :] - cum_c, NEG_BIG)
                pieces.append(q_c[t:t + 1, :] * k_c * jnp.exp(diff))
            w = jnp.concatenate(pieces, axis=0)
            att = _dot(w.astype(BF16), same)
            xv = att * jnp.concatenate([v_c] * C, axis=0)
            o_intra.append(_dot(sel, xv.astype(BF16)))
            upd.append(lax.dot_general(vb[sl], kdec[sl], (((0,), (0,)), ((), ())),
                                       preferred_element_type=F32) * same_t)

        s_t = st[sq]
        outs = []
        for c in range(n_chunks):
            sl = slice(c * C, (c + 1) * C)
            outs.append(o_intra[c] + _dot_nt(qe[sl], s_t.astype(BF16)))
            s_t = s_t * dec[c * C:c * C + 1, :] + upd[c]
        st[sq] = s_t
        seq_out.append(jnp.concatenate(outs, axis=0)[:ts] if n_chunks > 1 else outs[0][:ts])

    o = jnp.concatenate(seq_out, axis=0) if nseq > 1 else seq_out[0]

    @pl.when(i == last)
    def _():
        sout_ref[...] = st[...]

    go = g_ref[:, 3 * gk_w + gv_w:]
    e64 = (lax.broadcasted_iota(jnp.int32, (gv_w, gv_w), 0) >> dv_sh
           == lax.broadcasted_iota(jnp.int32, (gv_w, gv_w), 1) >> dv_sh).astype(BF16)
    osq = o * o
    osq_hi = osq.astype(BF16)
    osq_lo = (osq - osq_hi.astype(F32)).astype(BF16)
    ms = (_dot(osq_hi, e64) + _dot(osq_lo, e64)) * (1.0 / GLA_DV)
    out_c = o * lax.rsqrt(ms + EPS) * gn_ref[...] * (go * jax.nn.sigmoid(go))
    c_ref[...] = _rms(out_c, nc_ref[...]).astype(c_ref.dtype)


def _gla_call(g, row_off, B, S, s0t, lw_, out_dtype, gk_w, gv_w):
    ts = min(S, SEQ_TILE)
    nseq = math.gcd(B, max(SEQ_TILE // (2 * ts), 1)) if ts == S else 1
    rows = nseq * ts
    assert S % ts == 0 and row_off % rows == 0 and (ts % GLA_CHUNK == 0 or ts < GLA_CHUNK)
    n = S // ts
    off = row_off // rows
    gw = g.shape[1]
    full = lambda b, i: (0, 0)
    out_shape = (jax.ShapeDtypeStruct((B * S, gv_w), out_dtype),
                 jax.ShapeDtypeStruct((B, gv_w, gk_w), F32))
    return pl.pallas_call(
        functools.partial(_gla_kernel, ts=ts, nseq=nseq, gk_w=gk_w, gv_w=gv_w),
        grid=(B // nseq, n),
        in_specs=[pl.BlockSpec((rows, gw), lambda b, i: (off + b * n + i, 0)),
                  pl.BlockSpec((nseq, gv_w, gk_w), lambda b, i: (b, 0, 0)),
                  pl.BlockSpec((1, gv_w), full), pl.BlockSpec((1, gv_w), full)],
        out_specs=(pl.BlockSpec((rows, gv_w), lambda b, i: (b * n + i, 0)),
                   pl.BlockSpec((nseq, gv_w, gk_w), lambda b, i: (b, 0, 0))),
        out_shape=out_shape,
        scratch_shapes=[pltpu.VMEM((nseq, gv_w, gk_w), F32)],
        compiler_params=_cparams("parallel", "arbitrary"), name="gla",
    )(g, s0t, lw_['gla_g'], lw_['n_c'])


def _tree(op, xs):
    xs = list(xs)
    while len(xs) > 1:
        xs = [op(xs[i], xs[i + 1]) if i + 1 < len(xs) else xs[i] for i in range(0, len(xs), 2)]
    return xs[0]


def _attn_prompt_kernel(q_ref, k_ref, kt_ref, wuvt_ref, nb_ref, o_ref, m_scr, acc_scr, *, tq, kvr):
    i = pl.program_id(1)
    kw = 2 * LANES
    m_scr[...] = jnp.full(m_scr.shape, NEG_BIG, F32)
    acc_scr[...] = jnp.zeros(acc_scr.shape, F32)

    def block(j, masked):
        kb = k_ref[pl.ds(pl.multiple_of(j * tq, tq), tq), :]
        kbt = kt_ref[j]
        if masked:
            key = lax.broadcasted_iota(jnp.int32, (tq, tq), 0)
            qry = lax.broadcasted_iota(jnp.int32, (tq, tq), 1)
            keep = key <= qry
        qk = lambda h: _dot_nt(kb, q_ref[:, h * kw:(h + 1) * kw])
        st_next = qk(0)
        for h in range(MLA_HEADS):
            st = st_next
            if h + 1 < MLA_HEADS:
                st_next = qk(h + 1)
            if masked:
                st = jnp.where(keep, st, NEG_BIG)
            m_old = m_scr[h]
            m_new = jnp.maximum(m_old, jnp.max(st, axis=0, keepdims=True))
            alpha = jnp.exp2(m_old - m_new)
            pt = jnp.exp2(st - m_new).astype(BF16)
            acc_scr[h] = alpha * acc_scr[h] + _dot(kbt, pt)
            m_scr[h] = m_new

    def body(j, carry):
        block(j, False)
        return carry

    lax.fori_loop(0, i, body, 0)
    block(i, True)

    out_t = jnp.zeros((wuvt_ref.shape[0], tq), F32)
    for h in range(MLA_HEADS):
        acc = acc_scr[h]
        o_t = (acc[:kvr, :] / acc[kvr:kvr + 1, :]).astype(BF16)
        out_t = out_t + _dot(wuvt_ref[:, h * kvr:(h + 1) * kvr], o_t)
    ms = jnp.mean(out_t * out_t, axis=0, keepdims=True)
    out_t = out_t * lax.rsqrt(ms + EPS) * nb_ref[...]
    o_ref[...] = out_t.T.astype(o_ref.dtype)


def _attn_prompt_call(qcat, kcat, kcat_t, B, S, lw_, kvr):
    tq = kcat_t.shape[2]
    assert S % tq == 0
    n = S // tq
    mw = lw_['w_uv_rows_t'].shape[0]
    full = lambda b, i: (0, 0)
    return pl.pallas_call(
        functools.partial(_attn_prompt_kernel, tq=tq, kvr=kvr),
        grid=(B, n),
        in_specs=[pl.BlockSpec((tq, qcat.shape[1]), lambda b, i: (b * n + i, 0)),
                  pl.BlockSpec((S, kcat.shape[1]), lambda b, i: (b, 0)),
                  pl.BlockSpec((n, kcat_t.shape[1], tq), lambda b, i: (b, 0, 0)),
                  pl.BlockSpec(lw_['w_uv_rows_t'].shape, full), pl.BlockSpec((mw, 1), full)],
        out_specs=pl.BlockSpec((tq, mw), lambda b, i: (b * n + i, 0)),
        out_shape=jax.ShapeDtypeStruct((B * S, mw), BF16),
        scratch_shapes=[pltpu.VMEM((MLA_HEADS, 1, tq), F32),
                        pltpu.VMEM((MLA_HEADS, kcat_t.shape[1], tq), F32)],
        compiler_params=_cparams("parallel", "arbitrary"), name="attn_prompt",
    )(qcat, kcat, kcat_t, lw_['w_uv_rows_t'], lw_['n_b'].reshape(mw, 1))


def _attn_sample_kernel(pt_ref, q_ref, nckv_ref, nkpe_ref, ckv_hbm, kpe_hbm, wuv_ref, nb_ref, o_ref,
                        q_scr, s_scr, v_scr, ckv_buf, kpe_buf, sem, *, layer, n_pages, sq, kvr, page):
    b = pl.program_id(0)
    nrow = MLA_HEADS * sq

    def page_copies(bb):
        slot = lax.rem(bb, 2)
        out = []
        for j in range(n_pages):
            pg = pt_ref[bb * n_pages + j]
            out.append(pltpu.make_async_copy(ckv_hbm.at[layer, pg], ckv_buf.at[slot, j], sem.at[0, slot]))
            out.append(pltpu.make_async_copy(kpe_hbm.at[layer, pg], kpe_buf.at[slot, j], sem.at[1, slot]))
        return out

    @pl.when(b == 0)
    def _():
        for c in page_copies(b):
            c.start()

    @pl.when(b + 1 < pl.num_programs(0))
    def _():
        for c in page_copies(b + 1):
            c.start()

    for h in range(MLA_HEADS):
        q_scr[h * sq:(h + 1) * sq, :] = q_ref[:, 2 * h * LANES:(2 * h + 2) * LANES].astype(F32)
    qa = q_scr[:, :kvr].astype(BF16)
    qp = q_scr[:, kvr:kvr + QK_ROPE].astype(BF16)

    for c in page_copies(b):
        c.wait()
    slot = lax.rem(b, 2)
    gp = SCORE_PAGES
    for j in range(0, n_pages, gp):
        ck = ckv_buf[slot, j:j + gp].reshape(gp * page, kvr).astype(BF16)
        kp = jnp.concatenate([kpe_buf[slot, j + t] for t in range(gp)], axis=1).astype(BF16)
        v_scr[j * page:(j + gp) * page, :] = ck
        s_scr[:, j * page:(j + gp) * page] = _dot_nt(qa, ck) + _dot(qp, kp)

    zpad = lambda a: jnp.concatenate([a, jnp.zeros((page - sq, a.shape[1]), F32)], axis=0)
    ck_new = zpad(nckv_ref[...]).astype(BF16)
    kp_new = zpad(nkpe_ref[...]).astype(BF16)
    tok = lax.broadcasted_iota(jnp.int32, (nrow, page), 0) & (sq - 1)
    key = lax.broadcasted_iota(jnp.int32, (nrow, page), 1)
    s_new = jnp.where(key <= tok, _dot_nt(qa, ck_new) + _dot_nt(qp, kp_new), NEG_BIG)
    tile = lambda i: s_scr[:, i * page:(i + 1) * page]
    m = jnp.max(_tree(jnp.maximum, [tile(i) for i in range(n_pages)] + [s_new]), axis=-1, keepdims=True)
    p_new = jnp.exp2(s_new - m)
    acc = _dot(p_new.astype(BF16), ck_new)
    psum = p_new
    vp = VALUE_PAGES
    for i in range(0, n_pages, vp):
        p = jnp.exp2(s_scr[:, i * page:(i + vp) * page] - m)
        psum = psum + _tree(jnp.add, [p[:, t * page:(t + 1) * page] for t in range(vp)])
        acc = acc + _dot(p.astype(BF16), v_scr[i * page:(i + vp) * page, :])
    l = jnp.sum(psum, axis=-1, keepdims=True)
    o_lat = (acc / l).astype(BF16)
    mw = wuv_ref.shape[1]
    r = _dot(o_lat, wuv_ref[...])
    rh = lax.broadcasted_iota(jnp.int32, (nrow, mw), 0) >> int(math.log2(sq))
    ch = lax.broadcasted_iota(jnp.int32, (nrow, mw), 1) >> int(math.log2(V_HEAD))
    r = jnp.where(rh == ch, r, 0.0)
    out = r[0:sq, :]
    for h in range(1, MLA_HEADS):
        out = out + r[h * sq:(h + 1) * sq, :]
    o_ref[...] = _rms(out, nb_ref[...])


def _attn_sample_call(page_table, qs, nckv, nkpe, cache_ckv, cache_kpe_t, layer, lw_, kvr):
    B, sq, qw = qs.shape
    n_pages = page_table.shape[1]
    page = cache_ckv.shape[2]
    mw = lw_['w_uv_cat'].shape[1]
    nrow = MLA_HEADS * sq
    assert sq == SUBLANES and sq <= page and n_pages % SCORE_PAGES == 0 and n_pages % VALUE_PAGES == 0
    hbm = pl.BlockSpec(memory_space=pl.ANY)
    in_specs = [pl.BlockSpec((None, sq, qw), lambda b, pt: (b, 0, 0)),
                pl.BlockSpec((None, sq, kvr), lambda b, pt: (b, 0, 0)),
                pl.BlockSpec((None, sq, QK_ROPE), lambda b, pt: (b, 0, 0)),
                hbm, hbm,
                pl.BlockSpec(lw_['w_uv_cat'].shape, lambda b, pt: (0, 0)),
                pl.BlockSpec((1, mw), lambda b, pt: (0, 0))]
    grid_spec = pltpu.PrefetchScalarGridSpec(
        num_scalar_prefetch=1, grid=(B,), in_specs=in_specs,
        out_specs=pl.BlockSpec((None, sq, mw), lambda b, pt: (b, 0, 0)),
        scratch_shapes=[pltpu.VMEM((nrow, 2 * LANES), F32), pltpu.VMEM((nrow, n_pages * page), F32),
                        pltpu.VMEM((n_pages * page, kvr), BF16),
                        pltpu.VMEM((2, n_pages, page, kvr), F32), pltpu.VMEM((2, n_pages, QK_ROPE, page), F32),
                        pltpu.SemaphoreType.DMA((2, 2))])
    return pl.pallas_call(
        functools.partial(_attn_sample_kernel, layer=layer, n_pages=n_pages, sq=sq, kvr=kvr, page=page),
        grid_spec=grid_spec, out_shape=jax.ShapeDtypeStruct((B, sq, mw), F32),
        compiler_params=_cparams("arbitrary"), name="attn_sample",
    )(page_table.reshape(-1), qs, nckv, nkpe, cache_ckv, cache_kpe_t, lw_['w_uv_cat'], lw_['n_b'])


def _post_kernel(a_ref, b_ref, c_ref, *refs, widths, n_first):
    n_x = 1 if n_first is None else 2
    wo_ref, g2_ref, wr_ref, br_ref, x1_ref, hn_ref, rt_ref, cnt_ref, cnt_scr = refs[n_x:]
    wa, wb, wc = widths

    @pl.when(pl.program_id(0) == 0)
    def _():
        cnt_scr[...] = jnp.zeros(cnt_scr.shape, F32)

    mix = _dot(a_ref[...], wo_ref[:wa, :])
    mix = mix + _dot(b_ref[...], wo_ref[wa:wa + wb, :])
    mix = mix + _dot(c_ref[...], wo_ref[wa + wb:, :])
    x1 = _token_tile(refs[:n_x], n_first) + mix
    x1_ref[...] = x1
    hn = _rms(x1, g2_ref[...])
    for sub in range(hn_ref.shape[1]):
        hn_ref[:, sub, :] = hn[:, sub * LANES:(sub + 1) * LANES]
    hn_hi = hn.astype(BF16)
    hn_lo = (hn - hn_hi.astype(F32)).astype(BF16)
    hh = _dot(hn_hi, wr_ref[...])
    logits = (hh[:, :LANES] + hh[:, LANES:] + _dot(hn_lo, wr_ref[:, :LANES])) + br_ref[...]

    lane = lax.broadcasted_iota(jnp.int32, logits.shape, 1)
    lane_f = lane.astype(F32)
    big = float(LANES)
    is_g = lane < N_GROUPS
    gl = jnp.where(is_g, logits, NEG_BIG)
    gmax = jnp.max(gl, axis=-1, keepdims=True)
    gsel = jnp.min(jnp.where(gl == gmax, lane_f, big), axis=-1, keepdims=True)
    gprob = 1.0 / jnp.sum(jnp.where(is_g, jnp.exp(gl - gmax), 0.0), axis=-1, keepdims=True)
    lo = N_GROUPS + gsel * EXPERTS_PER_GROUP
    el = jnp.where((lane_f >= lo) & (lane_f < lo + EXPERTS_PER_GROUP), logits, NEG_BIG)
    v1 = jnp.max(el, axis=-1, keepdims=True)
    i1 = jnp.min(jnp.where(el == v1, lane_f, big), axis=-1, keepdims=True)
    el2 = jnp.where(lane_f == i1, NEG_BIG, el)
    v2 = jnp.max(el2, axis=-1, keepdims=True)
    i2 = jnp.min(jnp.where(el2 == v2, lane_f, big), axis=-1, keepdims=True)
    e21 = jnp.exp(v2 - v1)
    w1 = gprob / (1.0 + e21)
    w2 = w1 * e21
    e1 = i1 - N_GROUPS
    e2 = i2 - N_GROUPS

    oh1 = (lane_f == e1).astype(F32)
    oh2 = (lane_f == e2).astype(F32)
    both = oh1 + oh2
    tm = both.shape[0]
    tri = (lax.broadcasted_iota(jnp.int32, (tm, tm), 0) >= lax.broadcasted_iota(jnp.int32, (tm, tm), 1))
    incl = _dot(tri.astype(BF16), both.astype(BF16))
    base = cnt_scr[0:1, :] + incl - both
    r1 = jnp.sum(oh1 * base, axis=-1, keepdims=True)
    r2 = jnp.sum(oh2 * base, axis=-1, keepdims=True)
    cnt_scr[...] = cnt_scr[...] + incl[tm - 1:tm, :]
    cnt_ref[...] = cnt_scr[...]

    rt = jnp.where(lane == 0, e1, 0.0)
    rt = jnp.where(lane == 1, e2, rt)
    rt = jnp.where(lane == 2, w1, rt)
    rt = jnp.where(lane == 3, w2, rt)
    rt = jnp.where(lane == 4, r1, rt)
    rt = jnp.where(lane == 5, r2, rt)
    rt_ref[...] = rt


def _post_call(a, b, c, x, lw_):
    tm = TOKEN_TILE
    xs, x_specs, n_first, T = _token_specs(x, tm)
    D = xs[0].shape[1]
    row = lambda i: (i, 0)
    full = lambda i: (0, 0)
    widths = (a.shape[1], b.shape[1], c.shape[1])
    ws = [lw_['w_out'], lw_['g2'], lw_['w_r'], lw_['b_r']]
    out_shape = (jax.ShapeDtypeStruct((T, D), F32), jax.ShapeDtypeStruct((T, D // LANES, LANES), F32),
                 jax.ShapeDtypeStruct((T, LANES), F32), jax.ShapeDtypeStruct((SUBLANES, LANES), F32))
    return pl.pallas_call(
        functools.partial(_post_kernel, widths=widths, n_first=n_first),
        grid=(T // tm,),
        in_specs=[pl.BlockSpec((tm, w), row) for w in widths] + x_specs
        + [pl.BlockSpec(w.shape, full) for w in ws],
        out_specs=tuple(pl.BlockSpec((tm,) + s.shape[1:], (lambda i: (i, 0, 0)) if len(s.shape) == 3 else row)
                        for s in out_shape[:3])
        + (pl.BlockSpec((SUBLANES, LANES), full),),
        out_shape=out_shape, scratch_shapes=[pltpu.VMEM((SUBLANES, LANES), F32)],
        compiler_params=_cparams("arbitrary"), name="post_proj",
    )(a, b, c, *xs, *ws)


def _dest_kernel(rt_ref, ps_ref, d_ref):
    rt = rt_ref[...]
    lane = lax.broadcasted_iota(jnp.int32, rt.shape, 1)
    lane_f = lane.astype(F32)
    pick = lambda k: jnp.sum(jnp.where(lane == k, rt, 0.0), axis=-1, keepdims=True)
    ps = ps_ref[...]
    d1 = jnp.sum(jnp.where(lane_f == pick(0), ps, 0.0), axis=-1, keepdims=True) + pick(2 * TOP_K)
    d2 = jnp.sum(jnp.where(lane_f == pick(1), ps, 0.0), axis=-1, keepdims=True) + pick(2 * TOP_K + 1)
    d_ref[...] = jnp.where(lane == 0, d1, jnp.where(lane == 1, d2, 0.0))


def _dest_call(route, pstart_row):
    T = route.shape[0]
    tm = TOKEN_TILE
    return pl.pallas_call(
        _dest_kernel, grid=(T // tm,),
        in_specs=[pl.BlockSpec((tm, LANES), lambda i: (i, 0)), pl.BlockSpec((1, LANES), lambda i: (0, 0))],
        out_specs=pl.BlockSpec((tm, LANES), lambda i: (i, 0)),
        out_shape=jax.ShapeDtypeStruct((T, LANES), F32), compiler_params=_cparams("parallel"), name="moe_dest",
    )(route, pstart_row)


def _moe_kernel(te_ref, nt_ref, tok_ref, hn_hbm, w_ref, wg_ref, wu_ref, wd_ref, y_ref, xbuf, sem):
    i = pl.program_id(0)
    tm = xbuf.shape[1]
    nt = nt_ref[0]

    def start_rows(t):
        slot = lax.rem(t, 2)

        def body(r, carry):
            for k in range(2):
                rk = r + k * (tm // 2)
                row = tok_ref[t * tm + rk]
                pltpu.make_async_copy(hn_hbm.at[row], xbuf.at[slot, rk], sem.at[slot, k]).start()
            return carry

        lax.fori_loop(0, tm // 2, body, 0, unroll=4)

    @pl.when(i == 0)
    def _():
        start_rows(i)

    @pl.when(i + 1 < nt)
    def _():
        start_rows(i + 1)

    @pl.when(i < nt)
    def _():
        slot = lax.rem(i, 2)
        for k in range(2):
            half = pl.ds(k * (tm // 2), tm // 2)
            pltpu.make_async_copy(hn_hbm.at[half], xbuf.at[slot, half], sem.at[slot, k]).wait()
        x = jnp.concatenate([xbuf[slot, :, sub, :] for sub in range(xbuf.shape[2])], axis=1).astype(BF16)
        hg = _dot(x, wg_ref[...].astype(BF16))
        hu = _dot(x, wu_ref[...].astype(BF16))
        act = hg * jax.nn.sigmoid(hg) * hu * w_ref[...]
        y_ref[...] = _dot(act.astype(BF16), wd_ref[...].astype(BF16))

    @pl.when(i >= nt)
    def _():
        y_ref[...] = jnp.zeros(y_ref.shape, F32)


def _moe_call(hn, row_tok, row_w, tile_e, n_used, w_gate, w_up, w_down, layer):
    D = w_gate.shape[-2]
    tm = MOE_TILE
    F = w_gate.shape[-1]
    n_tiles = tile_e.shape[0]
    grid_spec = pltpu.PrefetchScalarGridSpec(
        num_scalar_prefetch=3, grid=(n_tiles,),
        in_specs=[pl.BlockSpec(memory_space=pl.ANY),
                  pl.BlockSpec((tm, 1), lambda i, te, nt, tok: (i, 0)),
                  pl.BlockSpec((None, None, D, F), lambda i, te, nt, tok: (layer, te[i], 0, 0)),
                  pl.BlockSpec((None, None, D, F), lambda i, te, nt, tok: (layer, te[i], 0, 0)),
                  pl.BlockSpec((None, None, F, D), lambda i, te, nt, tok: (layer, te[i], 0, 0))],
        out_specs=pl.BlockSpec((tm, D), lambda i, te, nt, tok: (i, 0)),
        scratch_shapes=[pltpu.VMEM((2, tm) + hn.shape[1:], F32), pltpu.SemaphoreType.DMA((2, 2))])
    return pl.pallas_call(
        _moe_kernel, grid_spec=grid_spec, out_shape=jax.ShapeDtypeStruct((n_tiles * tm, D), F32),
        compiler_params=_cparams("arbitrary"), name="moe_experts",
    )(tile_e, n_used, row_tok, hn, row_w, w_gate, w_up, w_down)


def _route_meta(route, counts_f, tm, n_tiles):
    T = route.shape[0]
    n_assign = TOP_K * T
    e = route[:, :TOP_K].astype(jnp.int32).reshape(-1)
    w = route[:, TOP_K:2 * TOP_K].reshape(-1)
    order = jnp.argsort(e, stable=True).astype(jnp.int32)
    counts = counts_f.astype(jnp.int32)
    zero = jnp.zeros((1,), jnp.int32)
    start = jnp.concatenate([zero, jnp.cumsum(counts)])
    pstart = jnp.concatenate([zero, jnp.cumsum(((counts + tm - 1) // tm) * tm)])
    n_used = (pstart[N_EXPERTS] // tm).reshape(1)
    tile_lo = jnp.arange(n_tiles, dtype=jnp.int32) * tm
    tile_e = jnp.sum((pstart[None, 1:] <= tile_lo[:, None]).astype(jnp.int32), axis=1)
    tile_e = jnp.minimum(tile_e, N_EXPERTS - 1)
    oh = (tile_e[:, None] == jnp.arange(N_EXPERTS, dtype=jnp.int32)[None, :]).astype(jnp.int32)
    pick = lambda tab: jnp.sum(oh * tab[None, :N_EXPERTS], axis=1)
    k = (tile_lo - pick(pstart))[:, None] + jnp.arange(tm, dtype=jnp.int32)[None, :]
    valid = (k < pick(counts)[:, None]).reshape(-1)
    idx = jnp.clip(pick(start)[:, None] + k, 0, n_assign - 1).reshape(-1)
    src = order[idx]
    row_tok = jnp.where(valid, src // TOP_K, 0)
    row_w = jnp.where(valid, w[src], 0.0)
    ps_row = jnp.pad(pstart[:N_EXPERTS].astype(F32), (0, LANES - N_EXPERTS)).reshape(1, LANES)
    pos = _dest_call(route, ps_row)[:, :TOP_K].astype(jnp.int32)
    return row_tok, row_w.reshape(-1, 1), pos, tile_e, n_used


def _final_kernel(x_ref, g_ref, o_ref):
    o_ref[...] = _rms(x_ref[...], g_ref[...])


def _final_call(x, g, row_off, n_rows):
    D = x.shape[1]
    tm = TOKEN_TILE
    assert row_off % tm == 0 and n_rows % tm == 0
    off = row_off // tm
    return pl.pallas_call(
        _final_kernel, grid=(n_rows // tm,),
        in_specs=[pl.BlockSpec((tm, D), lambda i: (off + i, 0)), pl.BlockSpec((1, D), lambda i: (0, 0))],
        out_specs=pl.BlockSpec((tm, D), lambda i: (i, 0)),
        out_shape=jax.ShapeDtypeStruct((n_rows, D), F32), compiler_params=_cparams("parallel"), name="final_norm",
    )(x, g)


def _block_diag(w):
    n, c, d = w.shape
    return jnp.einsum('ncd,nm->ncmd', w, jnp.eye(n, dtype=w.dtype)).reshape(n * c, n * d)


def _half_swap(w):
    half = w.shape[-1] // 2
    return jnp.concatenate([w[..., half:], w[..., :half]], axis=-1)


def _layer_weights(l, p, dims):
    lw, qr, kvr, gk_w, gv_w = dims
    D = p['w_in'].shape[1]
    row = lambda a: a.reshape(1, -1).astype(F32)
    w_in = p['w_in'][l]
    sizes = (lw, lw, qr, kvr, QK_ROPE, gk_w, gk_w, gv_w, p['gla_wa2'].shape[1], gv_w)
    x_lru, y_lru, c_q, c_kv, k_pe, g_q, g_k, g_v, g_a, g_o = jnp.split(w_in, list(np.cumsum(sizes)[:-1]), axis=1)
    tail_pad = LANES - 2 * QK_ROPE - g_a.shape[1]
    tail = jnp.concatenate([k_pe, _half_swap(k_pe), g_a, jnp.zeros((D, tail_pad), F32)], axis=1)
    w_in_p = jnp.concatenate([x_lru, y_lru, c_q, c_kv, g_q, g_k, g_v, g_o, tail], axis=1).astype(BF16)

    w_uq = p['w_uq'][l].reshape(qr, MLA_HEADS, QK_NOPE + QK_ROPE)
    nope = w_uq[:, :, :QK_NOPE].reshape(qr, MLA_HEADS * QK_NOPE)
    pe = w_uq[:, :, QK_NOPE:]
    widen = lambda a: jnp.pad(a, ((0, 0), (0, 0), (0, LANES - QK_ROPE))).reshape(qr, MLA_HEADS * LANES)
    w_q = jnp.concatenate([nope, widen(pe), widen(_half_swap(pe))], axis=1).astype(BF16)
    w_abs = _block_diag(jnp.transpose(p['w_uk'][l], (1, 2, 0))).astype(BF16)
    wa2 = jnp.zeros((LANES, gk_w), F32).at[2 * QK_ROPE:2 * QK_ROPE + g_a.shape[1]].set(p['gla_wa2'][l]).astype(BF16)

    w_uv = p['w_uv'][l]
    mw = MLA_HEADS * V_HEAD
    w_uv_cat = w_uv.reshape(kvr, mw)
    w_uv_rows = _block_diag(jnp.transpose(w_uv, (1, 0, 2)))
    n_a, n_b, n_c = jnp.split(p['out_norm_g'][l], [lw, lw + mw])
    n_r = N_GROUPS + N_EXPERTS
    w_r = jnp.concatenate([p['router_wg'][l], p['router_we'][l], jnp.zeros((D, LANES - n_r), F32)], axis=1)
    b_r = jnp.concatenate([p['router_bg'][l], p['router_be'][l], jnp.zeros((LANES - n_r,), F32)])
    return dict(
        g1=row(p['norm1_g'][l]), w_in=w_in_p, q_g=row(p['q_norm_g'][l]), w_q=w_q, w_abs=w_abs,
        kv_g=row(p['kv_norm_g'][l]), wa2=wa2, ba=row(p['gla_ba'][l]),
        conv_w=p['conv_w'][l], conv_b=row(p['conv_b'][l]),
        lru_wa=_block_diag(p['lru_wa'][l]).astype(BF16), lru_ba=row(p['lru_ba'][l]),
        lru_wi=_block_diag(p['lru_wi'][l]).astype(BF16), lru_bi=row(p['lru_bi'][l]),
        lru_lam=row(p['lru_lambda'][l]), n_a=row(n_a), n_b=row(n_b), n_c=row(n_c),
        gla_g=row(jnp.tile(p['gla_norm_g'][l], GLA_HEADS)),
        w_uv_cat=w_uv_cat.astype(BF16), w_uv_rows_t=w_uv_rows.T.astype(BF16),
        w_out=p['w_out'][l].astype(BF16), g2=row(p['norm2_g'][l]),
        w_r=jnp.concatenate([w_r.astype(BF16), (w_r - w_r.astype(BF16).astype(F32)).astype(BF16)], axis=1),
        b_r=row(b_r),
    )


def _rope_tables(positions):
    half = QK_ROPE // 2
    inv = ROPE_THETA ** (-np.arange(half, dtype=np.float64) / half)
    ang = np.asarray(positions, np.float64)[:, None] * inv
    zeros = np.zeros((ang.shape[0], LANES - QK_ROPE))
    cos = np.concatenate([np.cos(ang), np.cos(ang), zeros], axis=1)
    sin = np.concatenate([-np.sin(ang), np.sin(ang), zeros], axis=1)
    return jnp.asarray(cos, F32), jnp.asarray(sin, F32)


def _gla_state_to_rows(s):
    B = s.shape[0]
    eye = jnp.eye(GLA_HEADS, dtype=s.dtype)
    return jnp.einsum('bhde,hg->bhegd', s, eye).reshape(B, GLA_HEADS * GLA_DV, GLA_HEADS * GLA_DK)


def _gla_rows_to_state(st):
    B = st.shape[0]
    s5 = st.reshape(B, GLA_HEADS, GLA_DV, GLA_HEADS, GLA_DK)
    blocks = jnp.stack([s5[:, h, :, h, :] for h in range(GLA_HEADS)], axis=1)
    return jnp.swapaxes(blocks, -1, -2)


def kernel(x_prompt, x_sample, cache_ckv, cache_kpe, page_table, state_conv, state_lru, state_gla, norm1_g, w_in, conv_w, conv_b, lru_wa, lru_ba, lru_wi, lru_bi, lru_lambda, q_norm_g, w_uq, kv_norm_g, w_uk, w_uv, gla_wa2, gla_ba, gla_norm_g, out_norm_g, w_out, norm2_g, router_wg, router_bg, router_we, router_be, w_gate, w_up, w_down, final_norm_g):
    p = dict(norm1_g=norm1_g, w_in=w_in, conv_w=conv_w, conv_b=conv_b, lru_wa=lru_wa, lru_ba=lru_ba,
             lru_wi=lru_wi, lru_bi=lru_bi, lru_lambda=lru_lambda, q_norm_g=q_norm_g, w_uq=w_uq,
             kv_norm_g=kv_norm_g, w_uk=w_uk, w_uv=w_uv, gla_wa2=gla_wa2, gla_ba=gla_ba,
             gla_norm_g=gla_norm_g, out_norm_g=out_norm_g, w_out=w_out, norm2_g=norm2_g,
             router_wg=router_wg, router_bg=router_bg, router_we=router_we, router_be=router_be)
    Bp, Sp, D = x_prompt.shape
    Bs, Ss, _ = x_sample.shape
    depth = w_in.shape[0]
    lw = state_lru.shape[-1]
    qr = q_norm_g.shape[-1]
    kvr = kv_norm_g.shape[-1]
    gk_w = GLA_HEADS * GLA_DK
    gv_w = GLA_HEADS * GLA_DV
    dims = (lw, qr, kvr, gk_w, gv_w)
    assert kvr == LANES and gk_w == LANES
    Tp, Ts = Bp * Sp, Bs * Ss
    T = Tp + Ts
    past = page_table.shape[1] * cache_ckv.shape[2]

    cos_t, sin_t = _rope_tables(np.concatenate([np.tile(np.arange(Sp), Bp), np.tile(past + np.arange(Ss), Bs)]))
    cache_kpe_t = jnp.swapaxes(cache_kpe, 2, 3)
    n_tiles = (TOP_K * T + N_EXPERTS * (MOE_TILE - 1)) // MOE_TILE + 1

    x = (x_prompt.reshape(Tp, D), x_sample.reshape(Ts, D))
    outs = {k: [] for k in ('ckv_p', 'kpe_p', 'ckv_s', 'kpe_s', 'conv_p', 'conv_s', 'lru_p', 'lru_s', 'gla_p', 'gla_s')}
    for l in range(depth):
        lw_ = _layer_weights(l, p, dims)
        zl, qcat, kcat, kcat_t, ckv_n, kpe_n, g = _pre_call(x, lw_, cos_t, sin_t, dims)

        a_p, conv_p, lru_p = _lru_call(zl, 0, Bp, Sp, jnp.zeros((Bp, CONV_WIDTH - 1, lw), F32),
                                       jnp.zeros((Bp, lw), F32), lw_, BF16)
        a_s, lru_s = _lru_seg_call(zl, Tp, Bs, Ss, state_conv[l], state_lru[l], lw_)
        conv_s = zl[Tp:, :lw].reshape(Bs, Ss, lw)[:, Ss - (CONV_WIDTH - 1):]

        c_p, gla_p = _gla_call(g, 0, Bp, Sp, jnp.zeros((Bp, gv_w, gk_w), F32), lw_, BF16, gk_w, gv_w)
        c_s, gla_s = _gla_call(g, Tp, Bs, Ss, _gla_state_to_rows(state_gla[l]), lw_, BF16, gk_w, gv_w)

        b_p = _attn_prompt_call(qcat, kcat, kcat_t, Bp, Sp, lw_, kvr)
        b_s = _attn_sample_call(page_table, qcat[Tp:].reshape(Bs, Ss, -1), ckv_n[Tp:].reshape(Bs, Ss, kvr),
                                kpe_n[Tp:].reshape(Bs, Ss, QK_ROPE), cache_ckv, cache_kpe_t, l, lw_, kvr)

        a = jnp.concatenate([a_p, a_s], axis=0)
        b = jnp.concatenate([b_p, b_s.reshape(Ts, -1).astype(BF16)], axis=0)
        c = jnp.concatenate([c_p, c_s], axis=0)
        x1, hn, route, cnt = _post_call(a, b, c, x, lw_)

        row_tok, row_w, pos, tile_e, n_used = _route_meta(route, cnt[0, :N_EXPERTS], MOE_TILE, n_tiles)
        ys = _moe_call(hn, row_tok, row_w, tile_e, n_used, w_gate, w_up, w_down, l)
        x = x1 + ys[pos[:, 0]] + ys[pos[:, 1]]

        outs['ckv_p'].append(ckv_n[:Tp].reshape(Bp, Sp, kvr))
        outs['kpe_p'].append(kpe_n[:Tp].reshape(Bp, Sp, QK_ROPE))
        outs['ckv_s'].append(ckv_n[Tp:].reshape(Bs, Ss, kvr))
        outs['kpe_s'].append(kpe_n[Tp:].reshape(Bs, Ss, QK_ROPE))
        outs['conv_p'].append(conv_p)
        outs['conv_s'].append(conv_s)
        outs['lru_p'].append(lru_p.reshape(Bp, lw))
        outs['lru_s'].append(lru_s.reshape(Bs, lw))
        outs['gla_p'].append(_gla_rows_to_state(gla_p))
        outs['gla_s'].append(_gla_rows_to_state(gla_s))

    g_fin = final_norm_g.reshape(1, D)
    y_p, y_s = _final_call(x, g_fin, 0, Tp), _final_call(x, g_fin, Tp, Ts)
    st = {k: jnp.stack(v) for k, v in outs.items()}
    return (y_p.reshape(Bp, Sp, D), y_s.reshape(Bs, Ss, D),
            st['ckv_p'], st['kpe_p'], st['ckv_s'], st['kpe_s'], st['conv_p'], st['conv_s'],
            st['lru_p'], st['lru_s'], st['gla_p'], st['gla_s'])
```

```python
import functools
import math

import numpy as np
import jax
import jax.numpy as jnp
from jax import lax
from jax.experimental import pallas as pl
from jax.experimental.pallas import tpu as pltpu

F32 = jnp.float32
BF16 = jnp.bfloat16

LRU_BLOCKS = 4
CONV_WIDTH = 4
LRU_C = 8.0
MLA_HEADS = 8
QK_NOPE = 64
QK_ROPE = 32
V_HEAD = 64
ROPE_THETA = 10000.0
GLA_HEADS = 4
GLA_DK = 32
GLA_DV = 64
GLA_TAU = 16.0
GLA_CHUNK = 16
N_GROUPS = 4
EXPERTS_PER_GROUP = 8
N_EXPERTS = N_GROUPS * EXPERTS_PER_GROUP
TOP_K = 2
EPS = 1e-6

LANES = 128
SUBLANES = 8
BF16_ROWS = 16
VMEM_LIMIT_BYTES = 56 * 1024 * 1024

TOKEN_TILE = 512
SEQ_TILE = 256
MOE_TILE = 256
MOE_ROW_GROUP = 8
SCORE_PAGES = 4
VALUE_PAGES = 2
NEG_BIG = -1e30


def _cparams(*sem):
    return pltpu.CompilerParams(dimension_semantics=sem, vmem_limit_bytes=VMEM_LIMIT_BYTES)


def _rms(x, g):
    return x * lax.rsqrt(jnp.mean(x * x, axis=-1, keepdims=True) + EPS) * g


def _dot(a, b):
    return jnp.dot(a, b, preferred_element_type=F32)


def _dot_nt(a, b):
    return lax.dot_general(a, b, (((1,), (1,)), ((), ())), preferred_element_type=F32)


def _softplus(x):
    return jnp.maximum(x, 0.0) + jnp.log1p(jnp.exp(-jnp.abs(x)))


def _token_tile(refs, n_first):
    if n_first is None:
        return refs[0][...]
    return jnp.where(pl.program_id(0) < n_first, refs[0][...], refs[1][...])


def _token_specs(x, tm):
    if not isinstance(x, tuple):
        return [x], [pl.BlockSpec((tm, x.shape[1]), lambda i: (i, 0))], None, x.shape[0]
    a, b = x
    assert a.shape[0] % tm == 0 and b.shape[0] % tm == 0
    n_first = a.shape[0] // tm
    specs = [pl.BlockSpec((tm, a.shape[1]), lambda i: (jnp.minimum(i, n_first - 1), 0)),
             pl.BlockSpec((tm, b.shape[1]), lambda i: (jnp.maximum(i - n_first, 0), 0))]
    return [a, b], specs, n_first, a.shape[0] + b.shape[0]


def _shift_rows(x, d, fill):
    row = lax.broadcasted_iota(jnp.int32, x.shape, 0)
    return jnp.where(row >= d, pltpu.roll(x, d, 0), fill)


def _pre_kernel(*refs, dims, n_first):
    n_x = 1 if n_first is None else 2
    (g1_ref, win_ref, qg_ref, wq_ref, wabs_ref, kvg_ref, wa2_ref, ba_ref, cos_ref, sin_ref,
     zl_ref, q_ref, k_ref, kt_ref, ckv_ref, kpe_ref, g_ref) = refs[n_x:]
    lw, qr, kvr, gk_w, gv_w = dims
    xn = _rms(_token_tile(refs[:n_x], n_first), g1_ref[...])
    z = _dot(xn.astype(BF16), win_ref[...])
    o = 2 * lw
    zl_ref[...] = z[:, :o]
    cq = z[:, o:o + qr]
    o += qr
    ckv = z[:, o:o + kvr]
    o += kvr
    gqk = z[:, o:o + 2 * gk_w]
    o += 2 * gk_w
    gvo = z[:, o:o + 2 * gv_w]
    o += 2 * gv_w
    tail = z[:, o:o + LANES]
    cos = cos_ref[...]
    sin = sin_ref[...]
    scale = (QK_NOPE + QK_ROPE) ** -0.5 * math.log2(math.e)

    ckv_n = _rms(ckv, kvg_ref[...])
    kpe = tail * cos + pltpu.roll(tail, LANES - QK_ROPE, 1) * sin
    ckv_ref[...] = ckv_n
    kpe_ref[...] = kpe[:, :QK_ROPE]
    k_ref[:, :kvr] = ckv_n.astype(BF16)
    k_ref[:, kvr:] = kpe.astype(BF16)
    ones = jnp.ones((kt_ref.shape[0] - kvr, ckv_n.shape[0]), F32)
    kt_ref[...] = jnp.concatenate([ckv_n.T, ones], axis=0).astype(BF16)

    cqn = _rms(cq, qg_ref[...]).astype(BF16)
    qall = _dot(cqn, wq_ref[...])
    n_nope = MLA_HEADS * QK_NOPE
    q_abs = _dot(qall[:, :n_nope].astype(BF16), wabs_ref[...]) * scale
    for h in range(MLA_HEADS):
        pe = qall[:, n_nope + h * LANES:n_nope + (h + 1) * LANES]
        sw = qall[:, n_nope + (MLA_HEADS + h) * LANES:n_nope + (MLA_HEADS + h + 1) * LANES]
        q_ref[:, 2 * h * LANES:(2 * h + 1) * LANES] = q_abs[:, h * kvr:(h + 1) * kvr].astype(BF16)
        q_ref[:, (2 * h + 1) * LANES:(2 * h + 2) * LANES] = ((pe * cos + sw * sin) * scale).astype(BF16)

    la_pre = _dot(tail.astype(BF16), wa2_ref[...]) + ba_ref[...]
    la = -_softplus(-la_pre) * (1.0 / GLA_TAU)
    g_ref[:, :gk_w] = gqk[:, :gk_w] * (GLA_DK ** -0.5)
    g_ref[:, gk_w:2 * gk_w] = gqk[:, gk_w:]
    g_ref[:, 2 * gk_w:3 * gk_w] = la
    g_ref[:, 3 * gk_w:] = gvo


def _pre_call(x, lw_, cos_t, sin_t, dims):
    lw, qr, kvr, gk_w, gv_w = dims
    tm = TOKEN_TILE
    xs, x_specs, n_first, T = _token_specs(x, tm)
    assert T % tm == 0
    row = lambda i: (i, 0)
    full = lambda i: (0, 0)
    wspec = lambda a: pl.BlockSpec(a.shape, full)
    ws = [lw_['g1'], lw_['w_in'], lw_['q_g'], lw_['w_q'], lw_['w_abs'], lw_['kv_g'], lw_['wa2'], lw_['ba']]
    ins = xs + ws
    in_specs = x_specs + [wspec(a) for a in ws]
    in_specs += [pl.BlockSpec((tm, LANES), row), pl.BlockSpec((tm, LANES), row)]
    out_shape = (
        jax.ShapeDtypeStruct((T, 2 * lw), F32),
        jax.ShapeDtypeStruct((T, 2 * LANES * MLA_HEADS), BF16),
        jax.ShapeDtypeStruct((T, 2 * LANES), BF16),
        jax.ShapeDtypeStruct((T // tm, kvr + BF16_ROWS, tm), BF16),
        jax.ShapeDtypeStruct((T, kvr), F32),
        jax.ShapeDtypeStruct((T, QK_ROPE), F32),
        jax.ShapeDtypeStruct((T, 3 * gk_w + 2 * gv_w), F32),
    )
    out_specs = tuple(pl.BlockSpec((None, s.shape[1], tm), lambda i: (i, 0, 0)) if len(s.shape) == 3
                      else pl.BlockSpec((tm, s.shape[1]), row) for s in out_shape)
    return pl.pallas_call(
        functools.partial(_pre_kernel, dims=dims, n_first=n_first),
        grid=(T // tm,), in_specs=in_specs, out_specs=out_specs, out_shape=out_shape,
        compiler_params=_cparams("parallel"), name="pre_proj",
    )(*ins, cos_t, sin_t)


def _lru_kernel(zl_ref, cbuf_ref, h0_ref, cw_ref, cb_ref, wa_ref, ba_ref, wi_ref, bi_ref, lam_ref, na_ref,
                a_ref, conv_ref, hout_ref, xbuf, hcar, *, ts, lw):
    i = pl.program_id(1)
    last = pl.num_programs(1) - 1
    pad = SUBLANES
    nbuf = CONV_WIDTH - 1

    @pl.when(i == 0)
    def _():
        xbuf[0:pad, :] = jnp.zeros((pad, lw), F32)
        xbuf[pad - nbuf:pad, :] = cbuf_ref[...]
        hcar[...] = h0_ref[...]

    x = zl_ref[:, :lw]
    y = zl_ref[:, lw:]
    xbuf[pad:pad + ts, :] = x
    xc = cb_ref[...] + cw_ref[nbuf:nbuf + 1, :] * x
    for k in range(nbuf):
        xc = xc + cw_ref[k:k + 1, :] * xbuf[pad - nbuf + k:pad - nbuf + k + ts, :]

    @pl.when(i == last)
    def _():
        conv_ref[...] = xbuf[pad + ts - nbuf:pad + ts, :]

    xbuf[0:pad, :] = xbuf[ts:ts + pad, :]

    xb = xc.astype(BF16)
    r = jax.nn.sigmoid(_dot(xb, wa_ref[...]) + ba_ref[...])
    gi = jax.nn.sigmoid(_dot(xb, wi_ref[...]) + bi_ref[...])
    log_a = (-LRU_C) * r * _softplus(-lam_ref[...])
    a = jnp.exp(log_a)
    th = jnp.tanh(log_a)
    u = jnp.sqrt(-2.0 * th / (1.0 - th)) * (gi * xc)

    d = 1
    while d < ts:
        u = a * _shift_rows(u, d, 0.0) + u
        a = a * _shift_rows(a, d, 1.0)
        d *= 2
    h = a * hcar[...] + u
    hcar[...] = h[ts - 1:ts, :]

    @pl.when(i == last)
    def _():
        hout_ref[...] = h[ts - 1:ts, :]

    out_a = h * jax.nn.gelu(y)
    a_ref[...] = _rms(out_a, na_ref[...]).astype(a_ref.dtype)


def _lru_call(zl, row_off, B, S, cbuf, h0, lw_, out_dtype):
    lw = h0.shape[-1]
    ts = min(S, SEQ_TILE)
    assert S % ts == 0 and row_off % ts == 0 and S >= CONV_WIDTH - 1
    n = S // ts
    off = row_off // ts
    full = lambda b, i: (0, 0)
    ws = [lw_['conv_w'], lw_['conv_b'], lw_['lru_wa'], lw_['lru_ba'], lw_['lru_wi'], lw_['lru_bi'],
          lw_['lru_lam'], lw_['n_a']]
    in_specs = [
        pl.BlockSpec((ts, 2 * lw), lambda b, i: (off + b * n + i, 0)),
        pl.BlockSpec((None, CONV_WIDTH - 1, lw), lambda b, i: (b, 0, 0)),
        pl.BlockSpec((None, 1, lw), lambda b, i: (b, 0, 0)),
    ] + [pl.BlockSpec(a.shape, full) for a in ws]
    out_shape = (
        jax.ShapeDtypeStruct((B * S, lw), out_dtype),
        jax.ShapeDtypeStruct((B, CONV_WIDTH - 1, lw), F32),
        jax.ShapeDtypeStruct((B, 1, lw), F32),
    )
    out_specs = (
        pl.BlockSpec((ts, lw), lambda b, i: (b * n + i, 0)),
        pl.BlockSpec((None, CONV_WIDTH - 1, lw), lambda b, i: (b, 0, 0)),
        pl.BlockSpec((None, 1, lw), lambda b, i: (b, 0, 0)),
    )
    return pl.pallas_call(
        functools.partial(_lru_kernel, ts=ts, lw=lw),
        grid=(B, n), in_specs=in_specs, out_specs=out_specs, out_shape=out_shape,
        scratch_shapes=[pltpu.VMEM((ts + SUBLANES, lw), F32), pltpu.VMEM((1, lw), F32)],
        compiler_params=_cparams("parallel", "arbitrary"), name="rg_lru",
    )(zl, cbuf, h0.reshape(B, 1, lw), *ws)


def _lru_seg_kernel(zl_ref, hist_ref, h0_ref, cw_ref, cb_ref, wa_ref, ba_ref, wi_ref, bi_ref, lam_ref, na_ref,
                    a_ref, h_ref, *, seg, lw):
    nbuf = CONV_WIDTH - 1
    x = zl_ref[:, :lw]
    y = zl_ref[:, lw:]
    n = x.shape[0]
    pos = lax.broadcasted_iota(jnp.int32, (n, 1), 0) & (seg - 1)
    hist = hist_ref[...]
    xc = cb_ref[...] + cw_ref[nbuf:nbuf + 1, :] * x
    for k in range(nbuf):
        j = nbuf - k
        prev = jnp.where(pos >= j, pltpu.roll(x, j, 0), pltpu.roll(hist, (j - seg) % n, 0))
        xc = xc + cw_ref[k:k + 1, :] * prev
    xb = xc.astype(BF16)
    r = jax.nn.sigmoid(_dot(xb, wa_ref[...]) + ba_ref[...])
    gi = jax.nn.sigmoid(_dot(xb, wi_ref[...]) + bi_ref[...])
    log_a = (-LRU_C) * r * _softplus(-lam_ref[...])
    a = jnp.exp(log_a)
    th = jnp.tanh(log_a)
    u = jnp.sqrt(-2.0 * th / (1.0 - th)) * (gi * xc)
    d = 1
    while d < seg:
        u = a * jnp.where(pos >= d, pltpu.roll(u, d, 0), 0.0) + u
        a = a * jnp.where(pos >= d, pltpu.roll(a, d, 0), 1.0)
        d *= 2
    h = a * h0_ref[...] + u
    h_ref[...] = h
    a_ref[...] = _rms(h * jax.nn.gelu(y), na_ref[...]).astype(a_ref.dtype)


def _lru_seg_call(zl, row_off, B, S, cbuf, h0, lw_):
    lw = h0.shape[-1]
    n = B * S
    tm = min(n, TOKEN_TILE)
    assert S & (S - 1) == 0 and S >= CONV_WIDTH - 1 and tm % S == 0 and n % tm == 0 and row_off % tm == 0
    off = row_off // tm
    hist = jnp.pad(cbuf, ((0, 0), (S - (CONV_WIDTH - 1), 0), (0, 0))).reshape(n, lw)
    h0_rows = jnp.repeat(h0, S, axis=0)
    full = lambda i: (0, 0)
    row = lambda i: (i, 0)
    ws = [lw_['conv_w'], lw_['conv_b'], lw_['lru_wa'], lw_['lru_ba'], lw_['lru_wi'], lw_['lru_bi'],
          lw_['lru_lam'], lw_['n_a']]
    a, h = pl.pallas_call(
        functools.partial(_lru_seg_kernel, seg=S, lw=lw),
        grid=(n // tm,),
        in_specs=[pl.BlockSpec((tm, 2 * lw), lambda i: (off + i, 0)), pl.BlockSpec((tm, lw), row),
                  pl.BlockSpec((tm, lw), row)] + [pl.BlockSpec(w.shape, full) for w in ws],
        out_specs=(pl.BlockSpec((tm, lw), row), pl.BlockSpec((tm, lw), row)),
        out_shape=(jax.ShapeDtypeStruct((n, lw), BF16), jax.ShapeDtypeStruct((n, lw), F32)),
        compiler_params=_cparams("parallel"), name="rg_lru_seg",
    )(zl, hist, h0_rows, *ws)
    return a, h.reshape(B, S, lw)[:, S - 1]


def _gla_kernel(g_ref, s0_ref, gn_ref, nc_ref, c_ref, sout_ref, st, *, ts, nseq, gk_w, gv_w):
    i = pl.program_id(1)
    last = pl.num_programs(1) - 1
    C = GLA_CHUNK
    rows = min(ts, C)
    n_chunks = max(ts // C, 1)
    log2c = int(math.log2(C))

    @pl.when(i == 0)
    def _():
        st[...] = s0_ref[...]

    dk_sh, dv_sh = int(math.log2(GLA_DK)), int(math.log2(GLA_DV))
    hd = lax.broadcasted_iota(jnp.int32, (gk_w, gv_w), 0) >> dk_sh
    he = lax.broadcasted_iota(jnp.int32, (gk_w, gv_w), 1) >> dv_sh
    same = (hd == he).astype(BF16)
    he_t = lax.broadcasted_iota(jnp.int32, (gv_w, gk_w), 0) >> dv_sh
    hd_t = lax.broadcasted_iota(jnp.int32, (gv_w, gk_w), 1) >> dk_sh
    same_t = (he_t == hd_t).astype(F32)
    sel_t = lax.broadcasted_iota(jnp.int32, (C, C * C), 0)
    sel_r = lax.broadcasted_iota(jnp.int32, (C, C * C), 1) >> log2c
    sel = (sel_t == sel_r).astype(BF16)
    srow = lax.broadcasted_iota(jnp.int32, (C, 1), 0)

    seq_out = []
    for sq in range(nseq):
        blk = g_ref[sq * ts:(sq + 1) * ts, :]
        if rows < C:
            blk = jnp.concatenate([blk, jnp.zeros((C - rows, blk.shape[1]), F32)], axis=0)
        q = blk[:, :gk_w]
        k = blk[:, gk_w:2 * gk_w]
        la = blk[:, 2 * gk_w:3 * gk_w]
        v = blk[:, 3 * gk_w:3 * gk_w + gv_w]
        nrow = n_chunks * C
        pos = lax.broadcasted_iota(jnp.int32, (nrow, 1), 0) & (C - 1)
        cum = la
        d = 1
        while d < C:
            cum = cum + jnp.where(pos >= d, pltpu.roll(cum, d, 0), 0.0)
            d *= 2
        tot = jnp.where(pos == C - 1, cum, 0.0)
        d = 1
        while d < C:
            tot = tot + jnp.where(pos < C - d, pltpu.roll(tot, nrow - d, 0), 0.0)
            d *= 2
        qe = (q * jnp.exp(cum)).astype(BF16)
        kdec = (k * jnp.exp(tot - cum)).astype(BF16)
        dec = jnp.exp(tot)
        vb = v.astype(BF16)

        o_intra, upd = [], []
        for c in range(n_chunks):
            sl = slice(c * C, (c + 1) * C)
            cum_c, q_c, k_c, v_c = cum[sl], q[sl], k[sl], v[sl]
            pieces = []
            for t in range(C):
                diff = jnp.where(srow <= t, cum_c[t:t + 1, :] - cum_c, NEG_BIG)
                pieces.append(q_c[t:t + 1, :] * k_c * jnp.exp(diff))
            w = jnp.concatenate(pieces, axis=0)
            att = _dot(w.astype(BF16), same)
            xv = att * jnp.concatenate([v_c] * C, axis=0)
            o_intra.append(_dot(sel, xv.astype(BF16)))
            upd.append(lax.dot_general(vb[sl], kdec[sl], (((0,), (0,)), ((), ())),
                                       preferred_element_type=F32) * same_t)

        s_t = st[sq]
        outs = []
        for c in range(n_chunks):
            sl = slice(c * C, (c + 1) * C)
            outs.append(o_intra[c] + _dot_nt(qe[sl], s_t.astype(BF16)))
            s_t = s_t * dec[c * C:c * C + 1, :] + upd[c]
        st[sq] = s_t
        seq_out.append(jnp.concatenate(outs, axis=0)[:ts] if n_chunks > 1 else outs[0][:ts])

    o = jnp.concatenate(seq_out, axis=0) if nseq > 1 else seq_out[0]

    @pl.when(i == last)
    def _():
        sout_ref[...] = st[...]

    go = g_ref[:, 3 * gk_w + gv_w:]
    e64 = (lax.broadcasted_iota(jnp.int32, (gv_w, gv_w), 0) >> dv_sh
           == lax.broadcasted_iota(jnp.int32, (gv_w, gv_w), 1) >> dv_sh).astype(BF16)
    osq = o * o
    osq_hi = osq.astype(BF16)
    osq_lo = (osq - osq_hi.astype(F32)).astype(BF16)
    ms = (_dot(osq_hi, e64) + _dot(osq_lo, e64)) * (1.0 / GLA_DV)
    out_c = o * lax.rsqrt(ms + EPS) * gn_ref[...] * (go * jax.nn.sigmoid(go))
    c_ref[...] = _rms(out_c, nc_ref[...]).astype(c_ref.dtype)


def _gla_call(g, row_off, B, S, s0t, lw_, out_dtype, gk_w, gv_w):
    ts = min(S, SEQ_TILE)
    nseq = math.gcd(B, max(SEQ_TILE // (2 * ts), 1)) if ts == S else 1
    rows = nseq * ts
    assert S % ts == 0 and row_off % rows == 0 and (ts % GLA_CHUNK == 0 or ts < GLA_CHUNK)
    n = S // ts
    off = row_off // rows
    gw = g.shape[1]
    full = lambda b, i: (0, 0)
    out_shape = (jax.ShapeDtypeStruct((B * S, gv_w), out_dtype),
                 jax.ShapeDtypeStruct((B, gv_w, gk_w), F32))
    return pl.pallas_call(
        functools.partial(_gla_kernel, ts=ts, nseq=nseq, gk_w=gk_w, gv_w=gv_w),
        grid=(B // nseq, n),
        in_specs=[pl.BlockSpec((rows, gw), lambda b, i: (off + b * n + i, 0)),
                  pl.BlockSpec((nseq, gv_w, gk_w), lambda b, i: (b, 0, 0)),
                  pl.BlockSpec((1, gv_w), full), pl.BlockSpec((1, gv_w), full)],
        out_specs=(pl.BlockSpec((rows, gv_w), lambda b, i: (b * n + i, 0)),
                   pl.BlockSpec((nseq, gv_w, gk_w), lambda b, i: (b, 0, 0))),
        out_shape=out_shape,
        scratch_shapes=[pltpu.VMEM((nseq, gv_w, gk_w), F32)],
        compiler_params=_cparams("parallel", "arbitrary"), name="gla",
    )(g, s0t, lw_['gla_g'], lw_['n_c'])


def _tree(op, xs):
    xs = list(xs)
    while len(xs) > 1:
        xs = [op(xs[i], xs[i + 1]) if i + 1 < len(xs) else xs[i] for i in range(0, len(xs), 2)]
    return xs[0]


def _attn_prompt_kernel(q_ref, k_ref, kt_ref, wuvt_ref, nb_ref, o_ref, m_scr, acc_scr, *, tq, kvr):
    i = pl.program_id(1)
    kw = 2 * LANES
    m_scr[...] = jnp.full(m_scr.shape, NEG_BIG, F32)
    acc_scr[...] = jnp.zeros(acc_scr.shape, F32)

    def block(j, masked):
        kb = k_ref[pl.ds(pl.multiple_of(j * tq, tq), tq), :]
        kbt = kt_ref[j]
        if masked:
            key = lax.broadcasted_iota(jnp.int32, (tq, tq), 0)
            qry = lax.broadcasted_iota(jnp.int32, (tq, tq), 1)
            keep = key <= qry
        qk = lambda h: _dot_nt(kb, q_ref[:, h * kw:(h + 1) * kw])
        st_next = qk(0)
        for h in range(MLA_HEADS):
            st = st_next
            if h + 1 < MLA_HEADS:
                st_next = qk(h + 1)
            if masked:
                st = jnp.where(keep, st, NEG_BIG)
            m_old = m_scr[h]
            m_new = jnp.maximum(m_old, jnp.max(st, axis=0, keepdims=True))
            alpha = jnp.exp2(m_old - m_new)
            pt = jnp.exp2(st - m_new).astype(BF16)
            acc_scr[h] = alpha * acc_scr[h] + _dot(kbt, pt)
            m_scr[h] = m_new

    def body(j, carry):
        block(j, False)
        return carry

    lax.fori_loop(0, i, body, 0)
    block(i, True)

    out_t = jnp.zeros((wuvt_ref.shape[0], tq), F32)
    for h in range(MLA_HEADS):
        acc = acc_scr[h]
        o_t = (acc[:kvr, :] / acc[kvr:kvr + 1, :]).astype(BF16)
        out_t = out_t + _dot(wuvt_ref[:, h * kvr:(h + 1) * kvr], o_t)
    ms = jnp.mean(out_t * out_t, axis=0, keepdims=True)
    out_t = out_t * lax.rsqrt(ms + EPS) * nb_ref[...]
    o_ref[...] = out_t.T.astype(o_ref.dtype)


def _attn_prompt_call(qcat, kcat, kcat_t, B, S, lw_, kvr):
    tq = kcat_t.shape[2]
    assert S % tq == 0
    n = S // tq
    mw = lw_['w_uv_rows_t'].shape[0]
    full = lambda b, i: (0, 0)
    return pl.pallas_call(
        functools.partial(_attn_prompt_kernel, tq=tq, kvr=kvr),
        grid=(B, n),
        in_specs=[pl.BlockSpec((tq, qcat.shape[1]), lambda b, i: (b * n + i, 0)),
                  pl.BlockSpec((S, kcat.shape[1]), lambda b, i: (b, 0)),
                  pl.BlockSpec((n, kcat_t.shape[1], tq), lambda b, i: (b, 0, 0)),
                  pl.BlockSpec(lw_['w_uv_rows_t'].shape, full), pl.BlockSpec((mw, 1), full)],
        out_specs=pl.BlockSpec((tq, mw), lambda b, i: (b * n + i, 0)),
        out_shape=jax.ShapeDtypeStruct((B * S, mw), BF16),
        scratch_shapes=[pltpu.VMEM((MLA_HEADS, 1, tq), F32),
                        pltpu.VMEM((MLA_HEADS, kcat_t.shape[1], tq), F32)],
        compiler_params=_cparams("parallel", "arbitrary"), name="attn_prompt",
    )(qcat, kcat, kcat_t, lw_['w_uv_rows_t'], lw_['n_b'].reshape(mw, 1))


def _attn_sample_kernel(pt_ref, q_ref, nckv_ref, nkpe_ref, ckv_hbm, kpe_hbm, wuv_ref, nb_ref, o_ref,
                        q_scr, s_scr, v_scr, ckv_buf, kpe_buf, sem, *, layer, n_pages, sq, kvr, page):
    b = pl.program_id(0)
    nrow = MLA_HEADS * sq

    def page_copies(bb):
        slot = lax.rem(bb, 2)
        out = []
        for j in range(n_pages):
            pg = pt_ref[bb * n_pages + j]
            out.append(pltpu.make_async_copy(ckv_hbm.at[layer, pg], ckv_buf.at[slot, j], sem.at[0, slot]))
            out.append(pltpu.make_async_copy(kpe_hbm.at[layer, pg], kpe_buf.at[slot, j], sem.at[1, slot]))
        return out

    @pl.when(b == 0)
    def _():
        for c in page_copies(b):
            c.start()

    @pl.when(b + 1 < pl.num_programs(0))
    def _():
        for c in page_copies(b + 1):
            c.start()

    for h in range(MLA_HEADS):
        q_scr[h * sq:(h + 1) * sq, :] = q_ref[:, 2 * h * LANES:(2 * h + 2) * LANES].astype(F32)
    qa = q_scr[:, :kvr].astype(BF16)
    qp = q_scr[:, kvr:kvr + QK_ROPE].astype(BF16)

    for c in page_copies(b):
        c.wait()
    slot = lax.rem(b, 2)
    gp = SCORE_PAGES
    for j in range(0, n_pages, gp):
        ck = ckv_buf[slot, j:j + gp].reshape(gp * page, kvr).astype(BF16)
        kp = jnp.concatenate([kpe_buf[slot, j + t] for t in range(gp)], axis=1).astype(BF16)
        v_scr[j * page:(j + gp) * page, :] = ck
        s_scr[:, j * page:(j + gp) * page] = _dot_nt(qa, ck) + _dot(qp, kp)

    zpad = lambda a: jnp.concatenate([a, jnp.zeros((page - sq, a.shape[1]), F32)], axis=0)
    ck_new = zpad(nckv_ref[...]).astype(BF16)
    kp_new = zpad(nkpe_ref[...]).astype(BF16)
    tok = lax.broadcasted_iota(jnp.int32, (nrow, page), 0) & (sq - 1)
    key = lax.broadcasted_iota(jnp.int32, (nrow, page), 1)
    s_new = jnp.where(key <= tok, _dot_nt(qa, ck_new) + _dot_nt(qp, kp_new), NEG_BIG)
    tile = lambda i: s_scr[:, i * page:(i + 1) * page]
    m = jnp.max(_tree(jnp.maximum, [tile(i) for i in range(n_pages)] + [s_new]), axis=-1, keepdims=True)
    p_new = jnp.exp2(s_new - m)
    acc = _dot(p_new.astype(BF16), ck_new)
    psum = p_new
    vp = VALUE_PAGES
    for i in range(0, n_pages, vp):
        p = jnp.exp2(s_scr[:, i * page:(i + vp) * page] - m)
        psum = psum + _tree(jnp.add, [p[:, t * page:(t + 1) * page] for t in range(vp)])
        acc = acc + _dot(p.astype(BF16), v_scr[i * page:(i + vp) * page, :])
    l = jnp.sum(psum, axis=-1, keepdims=True)
    o_lat = (acc / l).astype(BF16)
    mw = wuv_ref.shape[1]
    r = _dot(o_lat, wuv_ref[...])
    rh = lax.broadcasted_iota(jnp.int32, (nrow, mw), 0) >> int(math.log2(sq))
    ch = lax.broadcasted_iota(jnp.int32, (nrow, mw), 1) >> int(math.log2(V_HEAD))
    r = jnp.where(rh == ch, r, 0.0)
    out = r[0:sq, :]
    for h in range(1, MLA_HEADS):
        out = out + r[h * sq:(h + 1) * sq, :]
    o_ref[...] = _rms(out, nb_ref[...])


def _attn_sample_call(page_table, qs, nckv, nkpe, cache_ckv, cache_kpe_t, layer, lw_, kvr):
    B, sq, qw = qs.shape
    n_pages = page_table.shape[1]
    page = cache_ckv.shape[2]
    mw = lw_['w_uv_cat'].shape[1]
    nrow = MLA_HEADS * sq
    assert sq == SUBLANES and sq <= page and n_pages % SCORE_PAGES == 0 and n_pages % VALUE_PAGES == 0
    hbm = pl.BlockSpec(memory_space=pl.ANY)
    in_specs = [pl.BlockSpec((None, sq, qw), lambda b, pt: (b, 0, 0)),
                pl.BlockSpec((None, sq, kvr), lambda b, pt: (b, 0, 0)),
                pl.BlockSpec((None, sq, QK_ROPE), lambda b, pt: (b, 0, 0)),
                hbm, hbm,
                pl.BlockSpec(lw_['w_uv_cat'].shape, lambda b, pt: (0, 0)),
                pl.BlockSpec((1, mw), lambda b, pt: (0, 0))]
    grid_spec = pltpu.PrefetchScalarGridSpec(
        num_scalar_prefetch=1, grid=(B,), in_specs=in_specs,
        out_specs=pl.BlockSpec((None, sq, mw), lambda b, pt: (b, 0, 0)),
        scratch_shapes=[pltpu.VMEM((nrow, 2 * LANES), F32), pltpu.VMEM((nrow, n_pages * page), F32),
                        pltpu.VMEM((n_pages * page, kvr), BF16),
                        pltpu.VMEM((2, n_pages, page, kvr), F32), pltpu.VMEM((2, n_pages, QK_ROPE, page), F32),
                        pltpu.SemaphoreType.DMA((2, 2))])
    return pl.pallas_call(
        functools.partial(_attn_sample_kernel, layer=layer, n_pages=n_pages, sq=sq, kvr=kvr, page=page),
        grid_spec=grid_spec, out_shape=jax.ShapeDtypeStruct((B, sq, mw), F32),
        compiler_params=_cparams("arbitrary"), name="attn_sample",
    )(page_table.reshape(-1), qs, nckv, nkpe, cache_ckv, cache_kpe_t, lw_['w_uv_cat'], lw_['n_b'])


def _post_kernel(a_ref, b_ref, c_ref, *refs, widths, n_first):
    n_x = 1 if n_first is None else 2
    wo_ref, g2_ref, wr_ref, br_ref, x1_ref, hn_ref, rt_ref, cnt_ref, cnt_scr = refs[n_x:]
    wa, wb, wc = widths

    @pl.when(pl.program_id(0) == 0)
    def _():
        cnt_scr[...] = jnp.zeros(cnt_scr.shape, F32)

    mix = _dot(a_ref[...], wo_ref[:wa, :])
    mix = mix + _dot(b_ref[...], wo_ref[wa:wa + wb, :])
    mix = mix + _dot(c_ref[...], wo_ref[wa + wb:, :])
    x1 = _token_tile(refs[:n_x], n_first) + mix
    x1_ref[...] = x1
    hn = _rms(x1, g2_ref[...])
    for sub in range(hn_ref.shape[1]):
        hn_ref[:, sub, :] = hn[:, sub * LANES:(sub + 1) * LANES]
    hn_hi = hn.astype(BF16)
    hn_lo = (hn - hn_hi.astype(F32)).astype(BF16)
    hh = _dot(hn_hi, wr_ref[...])
    logits = (hh[:, :LANES] + hh[:, LANES:] + _dot(hn_lo, wr_ref[:, :LANES])) + br_ref[...]

    lane = lax.broadcasted_iota(jnp.int32, logits.shape, 1)
    lane_f = lane.astype(F32)
    big = float(LANES)
    is_g = lane < N_GROUPS
    gl = jnp.where(is_g, logits, NEG_BIG)
    gmax = jnp.max(gl, axis=-1, keepdims=True)
    gsel = jnp.min(jnp.where(gl == gmax, lane_f, big), axis=-1, keepdims=True)
    gprob = 1.0 / jnp.sum(jnp.where(is_g, jnp.exp(gl - gmax), 0.0), axis=-1, keepdims=True)
    lo = N_GROUPS + gsel * EXPERTS_PER_GROUP
    el = jnp.where((lane_f >= lo) & (lane_f < lo + EXPERTS_PER_GROUP), logits, NEG_BIG)
    v1 = jnp.max(el, axis=-1, keepdims=True)
    i1 = jnp.min(jnp.where(el == v1, lane_f, big), axis=-1, keepdims=True)
    el2 = jnp.where(lane_f == i1, NEG_BIG, el)
    v2 = jnp.max(el2, axis=-1, keepdims=True)
    i2 = jnp.min(jnp.where(el2 == v2, lane_f, big), axis=-1, keepdims=True)
    e21 = jnp.exp(v2 - v1)
    w1 = gprob / (1.0 + e21)
    w2 = w1 * e21
    e1 = i1 - N_GROUPS
    e2 = i2 - N_GROUPS

    oh1 = (lane_f == e1).astype(F32)
    oh2 = (lane_f == e2).astype(F32)
    both = oh1 + oh2
    tm = both.shape[0]
    tri = (lax.broadcasted_iota(jnp.int32, (tm, tm), 0) >= lax.broadcasted_iota(jnp.int32, (tm, tm), 1))
    incl = _dot(tri.astype(BF16), both.astype(BF16))
    base = cnt_scr[0:1, :] + incl - both
    r1 = jnp.sum(oh1 * base, axis=-1, keepdims=True)
    r2 = jnp.sum(oh2 * base, axis=-1, keepdims=True)
    cnt_scr[...] = cnt_scr[...] + incl[tm - 1:tm, :]
    cnt_ref[...] = cnt_scr[...]

    rt = jnp.where(lane == 0, e1, 0.0)
    rt = jnp.where(lane == 1, e2, rt)
    rt = jnp.where(lane == 2, w1, rt)
    rt = jnp.where(lane == 3, w2, rt)
    rt = jnp.where(lane == 4, r1, rt)
    rt = jnp.where(lane == 5, r2, rt)
    rt_ref[...] = rt


def _post_call(a, b, c, x, lw_):
    tm = TOKEN_TILE
    xs, x_specs, n_first, T = _token_specs(x, tm)
    D = xs[0].shape[1]
    row = lambda i: (i, 0)
    full = lambda i: (0, 0)
    widths = (a.shape[1], b.shape[1], c.shape[1])
    ws = [lw_['w_out'], lw_['g2'], lw_['w_r'], lw_['b_r']]
    out_shape = (jax.ShapeDtypeStruct((T, D), F32), jax.ShapeDtypeStruct((T, D // LANES, LANES), F32),
                 jax.ShapeDtypeStruct((T, LANES), F32), jax.ShapeDtypeStruct((SUBLANES, LANES), F32))
    return pl.pallas_call(
        functools.partial(_post_kernel, widths=widths, n_first=n_first),
        grid=(T // tm,),
        in_specs=[pl.BlockSpec((tm, w), row) for w in widths] + x_specs
        + [pl.BlockSpec(w.shape, full) for w in ws],
        out_specs=tuple(pl.BlockSpec((tm,) + s.shape[1:], (lambda i: (i, 0, 0)) if len(s.shape) == 3 else row)
                        for s in out_shape[:3])
        + (pl.BlockSpec((SUBLANES, LANES), full),),
        out_shape=out_shape, scratch_shapes=[pltpu.VMEM((SUBLANES, LANES), F32)],
        compiler_params=_cparams("arbitrary"), name="post_proj",
    )(a, b, c, *xs, *ws)


def _dest_kernel(rt_ref, ps_ref, d_ref):
    rt = rt_ref[...]
    lane = lax.broadcasted_iota(jnp.int32, rt.shape, 1)
    lane_f = lane.astype(F32)
    pick = lambda k: jnp.sum(jnp.where(lane == k, rt, 0.0), axis=-1, keepdims=True)
    ps = ps_ref[...]
    d1 = jnp.sum(jnp.where(lane_f == pick(0), ps, 0.0), axis=-1, keepdims=True) + pick(2 * TOP_K)
    d2 = jnp.sum(jnp.where(lane_f == pick(1), ps, 0.0), axis=-1, keepdims=True) + pick(2 * TOP_K + 1)
    d_ref[...] = jnp.where(lane == 0, d1, jnp.where(lane == 1, d2, 0.0))


def _dest_call(route, pstart_row):
    T = route.shape[0]
    tm = TOKEN_TILE
    return pl.pallas_call(
        _dest_kernel, grid=(T // tm,),
        in_specs=[pl.BlockSpec((tm, LANES), lambda i: (i, 0)), pl.BlockSpec((1, LANES), lambda i: (0, 0))],
        out_specs=pl.BlockSpec((tm, LANES), lambda i: (i, 0)),
        out_shape=jax.ShapeDtypeStruct((T, LANES), F32), compiler_params=_cparams("parallel"), name="moe_dest",
    )(route, pstart_row)


def _moe_kernel(te_ref, nt_ref, cnt_ref, tok_ref, hn_hbm, w_ref, wg_ref, wu_ref, wd_ref, y_ref, xbuf, sem):
    i = pl.program_id(0)
    tm = xbuf.shape[1]
    nt = nt_ref[0]
    n_groups = tm // MOE_ROW_GROUP

    def start_rows(t):
        slot = lax.rem(t, 2)

        def body(g, carry):
            @pl.when(g * MOE_ROW_GROUP < cnt_ref[t])
            def _():
                for k in range(MOE_ROW_GROUP):
                    r = g * MOE_ROW_GROUP + k
                    pltpu.make_async_copy(hn_hbm.at[tok_ref[t * tm + r]], xbuf.at[slot, r], sem.at[slot]).start()
            return carry

        lax.fori_loop(0, n_groups, body, 0)

    @pl.when(i == 0)
    def _():
        xbuf[...] = jnp.zeros(xbuf.shape, F32)
        start_rows(i)

    @pl.when(i + 1 < nt)
    def _():
        start_rows(i + 1)

    @pl.when(i < nt)
    def _():
        slot = lax.rem(i, 2)

        def wait_group(g, carry):
            @pl.when(g * MOE_ROW_GROUP < cnt_ref[i])
            def _():
                grp = pl.ds(0, MOE_ROW_GROUP)
                pltpu.make_async_copy(hn_hbm.at[grp], xbuf.at[slot, grp], sem.at[slot]).wait()
            return carry

        lax.fori_loop(0, n_groups, wait_group, 0)
        x = jnp.concatenate([xbuf[slot, :, sub, :] for sub in range(xbuf.shape[2])], axis=1).astype(BF16)
        hg = _dot(x, wg_ref[...].astype(BF16))
        hu = _dot(x, wu_ref[...].astype(BF16))
        act = hg * jax.nn.sigmoid(hg) * hu * w_ref[...]
        y_ref[...] = _dot(act.astype(BF16), wd_ref[...].astype(BF16))

    @pl.when(i >= nt)
    def _():
        y_ref[...] = jnp.zeros(y_ref.shape, F32)


def _moe_call(hn, row_tok, row_w, tile_e, n_used, tile_cnt, w_gate, w_up, w_down, layer):
    D = w_gate.shape[-2]
    tm = MOE_TILE
    F = w_gate.shape[-1]
    n_tiles = tile_e.shape[0]
    grid_spec = pltpu.PrefetchScalarGridSpec(
        num_scalar_prefetch=4, grid=(n_tiles,),
        in_specs=[pl.BlockSpec(memory_space=pl.ANY),
                  pl.BlockSpec((tm, 1), lambda i, te, nt, cnt, tok: (i, 0)),
                  pl.BlockSpec((None, None, D, F), lambda i, te, nt, cnt, tok: (layer, te[i], 0, 0)),
                  pl.BlockSpec((None, None, D, F), lambda i, te, nt, cnt, tok: (layer, te[i], 0, 0)),
                  pl.BlockSpec((None, None, F, D), lambda i, te, nt, cnt, tok: (layer, te[i], 0, 0))],
        out_specs=pl.BlockSpec((tm, D), lambda i, te, nt, cnt, tok: (i, 0)),
        scratch_shapes=[pltpu.VMEM((2, tm) + hn.shape[1:], F32), pltpu.SemaphoreType.DMA((2,))])
    return pl.pallas_call(
        _moe_kernel, grid_spec=grid_spec, out_shape=jax.ShapeDtypeStruct((n_tiles * tm, D), F32),
        compiler_params=_cparams("arbitrary"), name="moe_experts",
    )(tile_e, n_used, tile_cnt, row_tok, hn, row_w, w_gate, w_up, w_down)


def _route_meta(route, counts_f, tm, n_tiles):
    T = route.shape[0]
    n_assign = TOP_K * T
    e = route[:, :TOP_K].astype(jnp.int32).reshape(-1)
    w = route[:, TOP_K:2 * TOP_K].reshape(-1)
    order = jnp.argsort(e, stable=True).astype(jnp.int32)
    counts = counts_f.astype(jnp.int32)
    zero = jnp.zeros((1,), jnp.int32)
    start = jnp.concatenate([zero, jnp.cumsum(counts)])
    pstart = jnp.concatenate([zero, jnp.cumsum(((counts + tm - 1) // tm) * tm)])
    n_used = (pstart[N_EXPERTS] // tm).reshape(1)
    tile_lo = jnp.arange(n_tiles, dtype=jnp.int32) * tm
    tile_e = jnp.sum((pstart[None, 1:] <= tile_lo[:, None]).astype(jnp.int32), axis=1)
    tile_e = jnp.minimum(tile_e, N_EXPERTS - 1)
    oh = (tile_e[:, None] == jnp.arange(N_EXPERTS, dtype=jnp.int32)[None, :]).astype(jnp.int32)
    pick = lambda tab: jnp.sum(oh * tab[None, :N_EXPERTS], axis=1)
    k0 = tile_lo - pick(pstart)
    tile_cnt = jnp.clip(pick(counts) - k0, 0, tm)
    k = k0[:, None] + jnp.arange(tm, dtype=jnp.int32)[None, :]
    valid = (k < pick(counts)[:, None]).reshape(-1)
    idx = jnp.clip(pick(start)[:, None] + k, 0, n_assign - 1).reshape(-1)
    src = order[idx]
    row_tok = jnp.where(valid, src // TOP_K, 0)
    row_w = jnp.where(valid, w[src], 0.0)
    ps_row = jnp.pad(pstart[:N_EXPERTS].astype(F32), (0, LANES - N_EXPERTS)).reshape(1, LANES)
    pos = _dest_call(route, ps_row)[:, :TOP_K].astype(jnp.int32)
    return row_tok, row_w.reshape(-1, 1), pos, tile_e, n_used, tile_cnt


def _final_kernel(x_ref, g_ref, o_ref):
    o_ref[...] = _rms(x_ref[...], g_ref[...])


def _final_call(x, g, row_off, n_rows):
    D = x.shape[1]
    tm = TOKEN_TILE
    assert row_off % tm == 0 and n_rows % tm == 0
    off = row_off // tm
    return pl.pallas_call(
        _final_kernel, grid=(n_rows // tm,),
        in_specs=[pl.BlockSpec((tm, D), lambda i: (off + i, 0)), pl.BlockSpec((1, D), lambda i: (0, 0))],
        out_specs=pl.BlockSpec((tm, D), lambda i: (i, 0)),
        out_shape=jax.ShapeDtypeStruct((n_rows, D), F32), compiler_params=_cparams("parallel"), name="final_norm",
    )(x, g)


def _block_diag(w):
    n, c, d = w.shape
    return jnp.einsum('ncd,nm->ncmd', w, jnp.eye(n, dtype=w.dtype)).reshape(n * c, n * d)


def _half_swap(w):
    half = w.shape[-1] // 2
    return jnp.concatenate([w[..., half:], w[..., :half]], axis=-1)


def _layer_weights(l, p, dims):
    lw, qr, kvr, gk_w, gv_w = dims
    D = p['w_in'].shape[1]
    row = lambda a: a.reshape(1, -1).astype(F32)
    w_in = p['w_in'][l]
    sizes = (lw, lw, qr, kvr, QK_ROPE, gk_w, gk_w, gv_w, p['gla_wa2'].shape[1], gv_w)
    x_lru, y_lru, c_q, c_kv, k_pe, g_q, g_k, g_v, g_a, g_o = jnp.split(w_in, list(np.cumsum(sizes)[:-1]), axis=1)
    tail_pad = LANES - 2 * QK_ROPE - g_a.shape[1]
    tail = jnp.concatenate([k_pe, _half_swap(k_pe), g_a, jnp.zeros((D, tail_pad), F32)], axis=1)
    w_in_p = jnp.concatenate([x_lru, y_lru, c_q, c_kv, g_q, g_k, g_v, g_o, tail], axis=1).astype(BF16)

    w_uq = p['w_uq'][l].reshape(qr, MLA_HEADS, QK_NOPE + QK_ROPE)
    nope = w_uq[:, :, :QK_NOPE].reshape(qr, MLA_HEADS * QK_NOPE)
    pe = w_uq[:, :, QK_NOPE:]
    widen = lambda a: jnp.pad(a, ((0, 0), (0, 0), (0, LANES - QK_ROPE))).reshape(qr, MLA_HEADS * LANES)
    w_q = jnp.concatenate([nope, widen(pe), widen(_half_swap(pe))], axis=1).astype(BF16)
    w_abs = _block_diag(jnp.transpose(p['w_uk'][l], (1, 2, 0))).astype(BF16)
    wa2 = jnp.zeros((LANES, gk_w), F32).at[2 * QK_ROPE:2 * QK_ROPE + g_a.shape[1]].set(p['gla_wa2'][l]).astype(BF16)

    w_uv = p['w_uv'][l]
    mw = MLA_HEADS * V_HEAD
    w_uv_cat = w_uv.reshape(kvr, mw)
    w_uv_rows = _block_diag(jnp.transpose(w_uv, (1, 0, 2)))
    n_a, n_b, n_c = jnp.split(p['out_norm_g'][l], [lw, lw + mw])
    n_r = N_GROUPS + N_EXPERTS
    w_r = jnp.concatenate([p['router_wg'][l], p['router_we'][l], jnp.zeros((D, LANES - n_r), F32)], axis=1)
    b_r = jnp.concatenate([p['router_bg'][l], p['router_be'][l], jnp.zeros((LANES - n_r,), F32)])
    return dict(
        g1=row(p['norm1_g'][l]), w_in=w_in_p, q_g=row(p['q_norm_g'][l]), w_q=w_q, w_abs=w_abs,
        kv_g=row(p['kv_norm_g'][l]), wa2=wa2, ba=row(p['gla_ba'][l]),
        conv_w=p['conv_w'][l], conv_b=row(p['conv_b'][l]),
        lru_wa=_block_diag(p['lru_wa'][l]).astype(BF16), lru_ba=row(p['lru_ba'][l]),
        lru_wi=_block_diag(p['lru_wi'][l]).astype(BF16), lru_bi=row(p['lru_bi'][l]),
        lru_lam=row(p['lru_lambda'][l]), n_a=row(n_a), n_b=row(n_b), n_c=row(n_c),
        gla_g=row(jnp.tile(p['gla_norm_g'][l], GLA_HEADS)),
        w_uv_cat=w_uv_cat.astype(BF16), w_uv_rows_t=w_uv_rows.T.astype(BF16),
        w_out=p['w_out'][l].astype(BF16), g2=row(p['norm2_g'][l]),
        w_r=jnp.concatenate([w_r.astype(BF16), (w_r - w_r.astype(BF16).astype(F32)).astype(BF16)], axis=1),
        b_r=row(b_r),
    )


def _rope_tables(positions):
    half = QK_ROPE // 2
    inv = ROPE_THETA ** (-np.arange(half, dtype=np.float64) / half)
    ang = np.asarray(positions, np.float64)[:, None] * inv
    zeros = np.zeros((ang.shape[0], LANES - QK_ROPE))
    cos = np.concatenate([np.cos(ang), np.cos(ang), zeros], axis=1)
    sin = np.concatenate([-np.sin(ang), np.sin(ang), zeros], axis=1)
    return jnp.asarray(cos, F32), jnp.asarray(sin, F32)


def _gla_state_to_rows(s):
    B = s.shape[0]
    eye = jnp.eye(GLA_HEADS, dtype=s.dtype)
    return jnp.einsum('bhde,hg->bhegd', s, eye).reshape(B, GLA_HEADS * GLA_DV, GLA_HEADS * GLA_DK)


def _gla_rows_to_state(st):
    B = st.shape[0]
    s5 = st.reshape(B, GLA_HEADS, GLA_DV, GLA_HEADS, GLA_DK)
    blocks = jnp.stack([s5[:, h, :, h, :] for h in range(GLA_HEADS)], axis=1)
    return jnp.swapaxes(blocks, -1, -2)


def kernel(x_prompt, x_sample, cache_ckv, cache_kpe, page_table, state_conv, state_lru, state_gla, norm1_g, w_in, conv_w, conv_b, lru_wa, lru_ba, lru_wi, lru_bi, lru_lambda, q_norm_g, w_uq, kv_norm_g, w_uk, w_uv, gla_wa2, gla_ba, gla_norm_g, out_norm_g, w_out, norm2_g, router_wg, router_bg, router_we, router_be, w_gate, w_up, w_down, final_norm_g):
    p = dict(norm1_g=norm1_g, w_in=w_in, conv_w=conv_w, conv_b=conv_b, lru_wa=lru_wa, lru_ba=lru_ba,
             lru_wi=lru_wi, lru_bi=lru_bi, lru_lambda=lru_lambda, q_norm_g=q_norm_g, w_uq=w_uq,
             kv_norm_g=kv_norm_g, w_uk=w_uk, w_uv=w_uv, gla_wa2=gla_wa2, gla_ba=gla_ba,
             gla_norm_g=gla_norm_g, out_norm_g=out_norm_g, w_out=w_out, norm2_g=norm2_g,
             router_wg=router_wg, router_bg=router_bg, router_we=router_we, router_be=router_be)
    Bp, Sp, D = x_prompt.shape
    Bs, Ss, _ = x_sample.shape
    depth = w_in.shape[0]
    lw = state_lru.shape[-1]
    qr = q_norm_g.shape[-1]
    kvr = kv_norm_g.shape[-1]
    gk_w = GLA_HEADS * GLA_DK
    gv_w = GLA_HEADS * GLA_DV
    dims = (lw, qr, kvr, gk_w, gv_w)
    assert kvr == LANES and gk_w == LANES
    Tp, Ts = Bp * Sp, Bs * Ss
    T = Tp + Ts
    past = page_table.shape[1] * cache_ckv.shape[2]

    cos_t, sin_t = _rope_tables(np.concatenate([np.tile(np.arange(Sp), Bp), np.tile(past + np.arange(Ss), Bs)]))
    cache_kpe_t = jnp.swapaxes(cache_kpe, 2, 3)
    n_tiles = (TOP_K * T + N_EXPERTS * (MOE_TILE - 1)) // MOE_TILE + 1

    x = (x_prompt.reshape(Tp, D), x_sample.reshape(Ts, D))
    outs = {k: [] for k in ('ckv_p', 'kpe_p', 'ckv_s', 'kpe_s', 'conv_p', 'conv_s', 'lru_p', 'lru_s', 'gla_p', 'gla_s')}
    for l in range(depth):
        lw_ = _layer_weights(l, p, dims)
        zl, qcat, kcat, kcat_t, ckv_n, kpe_n, g = _pre_call(x, lw_, cos_t, sin_t, dims)

        a_p, conv_p, lru_p = _lru_call(zl, 0, Bp, Sp, jnp.zeros((Bp, CONV_WIDTH - 1, lw), F32),
                                       jnp.zeros((Bp, lw), F32), lw_, BF16)
        a_s, lru_s = _lru_seg_call(zl, Tp, Bs, Ss, state_conv[l], state_lru[l], lw_)
        conv_s = zl[Tp:, :lw].reshape(Bs, Ss, lw)[:, Ss - (CONV_WIDTH - 1):]

        c_p, gla_p = _gla_call(g, 0, Bp, Sp, jnp.zeros((Bp, gv_w, gk_w), F32), lw_, BF16, gk_w, gv_w)
        c_s, gla_s = _gla_call(g, Tp, Bs, Ss, _gla_state_to_rows(state_gla[l]), lw_, BF16, gk_w, gv_w)

        b_p = _attn_prompt_call(qcat, kcat, kcat_t, Bp, Sp, lw_, kvr)
        b_s = _attn_sample_call(page_table, qcat[Tp:].reshape(Bs, Ss, -1), ckv_n[Tp:].reshape(Bs, Ss, kvr),
                                kpe_n[Tp:].reshape(Bs, Ss, QK_ROPE), cache_ckv, cache_kpe_t, l, lw_, kvr)

        a = jnp.concatenate([a_p, a_s], axis=0)
        b = jnp.concatenate([b_p, b_s.reshape(Ts, -1).astype(BF16)], axis=0)
        c = jnp.concatenate([c_p, c_s], axis=0)
        x1, hn, route, cnt = _post_call(a, b, c, x, lw_)

        row_tok, row_w, pos, tile_e, n_used, tile_cnt = _route_meta(route, cnt[0, :N_EXPERTS], MOE_TILE, n_tiles)
        ys = _moe_call(hn, row_tok, row_w, tile_e, n_used, tile_cnt, w_gate, w_up, w_down, l)
        x = x1 + ys[pos[:, 0]] + ys[pos[:, 1]]

        outs['ckv_p'].append(ckv_n[:Tp].reshape(Bp, Sp, kvr))
        outs['kpe_p'].append(kpe_n[:Tp].reshape(Bp, Sp, QK_ROPE))
        outs['ckv_s'].append(ckv_n[Tp:].reshape(Bs, Ss, kvr))
        outs['kpe_s'].append(kpe_n[Tp:].reshape(Bs, Ss, QK_ROPE))
        outs['conv_p'].append(conv_p)
        outs['conv_s'].append(conv_s)
        outs['lru_p'].append(lru_p.reshape(Bp, lw))
        outs['lru_s'].append(lru_s.reshape(Bs, lw))
        outs['gla_p'].append(_gla_rows_to_state(gla_p))
        outs['gla_s'].append(_gla_rows_to_state(gla_s))

    g_fin = final_norm_g.reshape(1, D)
    y_p, y_s = _final_call(x, g_fin, 0, Tp), _final_call(x, g_fin, Tp, Ts)
    st = {k: jnp.stack(v) for k, v in outs.items()}
    return (y_p.reshape(Bp, Sp, D), y_s.reshape(Bs, Ss, D),
            st['ckv_p'], st['kpe_p'], st['ckv_s'], st['kpe_s'], st['conv_p'], st['conv_s'],
            st['lru_p'], st['lru_s'], st['gla_p'], st['gla_s'])
```

```python
import functools
import math

import numpy as np
import jax
import jax.numpy as jnp
from jax import lax
from jax.experimental import pallas as pl
from jax.experimental.pallas import tpu as pltpu

F32 = jnp.float32
BF16 = jnp.bfloat16

LRU_BLOCKS = 4
CONV_WIDTH = 4
LRU_C = 8.0
MLA_HEADS = 8
QK_NOPE = 64
QK_ROPE = 32
V_HEAD = 64
ROPE_THETA = 10000.0
GLA_HEADS = 4
GLA_DK = 32
GLA_DV = 64
GLA_TAU = 16.0
GLA_CHUNK = 16
N_GROUPS = 4
EXPERTS_PER_GROUP = 8
N_EXPERTS = N_GROUPS * EXPERTS_PER_GROUP
TOP_K = 2
EPS = 1e-6

LANES = 128
SUBLANES = 8
BF16_ROWS = 16
VMEM_LIMIT_BYTES = 56 * 1024 * 1024

TOKEN_TILE = 512
SEQ_TILE = 256
MOE_TILE = 256
MOE_ROW_GROUP = 8
SCORE_PAGES = 4
VALUE_PAGES = 2
NEG_BIG = -1e30


def _cparams(*sem):
    return pltpu.CompilerParams(dimension_semantics=sem, vmem_limit_bytes=VMEM_LIMIT_BYTES)


def _rms(x, g):
    return x * lax.rsqrt(jnp.mean(x * x, axis=-1, keepdims=True) + EPS) * g


def _dot(a, b):
    return jnp.dot(a, b, preferred_element_type=F32)


def _dot_nt(a, b):
    return lax.dot_general(a, b, (((1,), (1,)), ((), ())), preferred_element_type=F32)


def _softplus(x):
    return jnp.maximum(x, 0.0) + jnp.log1p(jnp.exp(-jnp.abs(x)))


def _token_tile(refs, n_first):
    if n_first is None:
        return refs[0][...]
    return jnp.where(pl.program_id(0) < n_first, refs[0][...], refs[1][...])


def _token_specs(x, tm):
    if not isinstance(x, tuple):
        return [x], [pl.BlockSpec((tm, x.shape[1]), lambda i: (i, 0))], None, x.shape[0]
    a, b = x
    assert a.shape[0] % tm == 0 and b.shape[0] % tm == 0
    n_first = a.shape[0] // tm
    specs = [pl.BlockSpec((tm, a.shape[1]), lambda i: (jnp.minimum(i, n_first - 1), 0)),
             pl.BlockSpec((tm, b.shape[1]), lambda i: (jnp.maximum(i - n_first, 0), 0))]
    return [a, b], specs, n_first, a.shape[0] + b.shape[0]


def _shift_rows(x, d, fill):
    row = lax.broadcasted_iota(jnp.int32, x.shape, 0)
    return jnp.where(row >= d, pltpu.roll(x, d, 0), fill)


def _pre_kernel(*refs, dims, n_first):
    n_x = 1 if n_first is None else 2
    (g1_ref, win_ref, qg_ref, wq_ref, wabs_ref, kvg_ref, wa2_ref, ba_ref, cos_ref, sin_ref,
     zl_ref, q_ref, k_ref, kt_ref, ckv_ref, kpe_ref, g_ref) = refs[n_x:]
    lw, qr, kvr, gk_w, gv_w = dims
    xn = _rms(_token_tile(refs[:n_x], n_first), g1_ref[...])
    z = _dot(xn.astype(BF16), win_ref[...])
    o = 2 * lw
    zl_ref[...] = z[:, :o]
    cq = z[:, o:o + qr]
    o += qr
    ckv = z[:, o:o + kvr]
    o += kvr
    gqk = z[:, o:o + 2 * gk_w]
    o += 2 * gk_w
    gvo = z[:, o:o + 2 * gv_w]
    o += 2 * gv_w
    tail = z[:, o:o + LANES]
    cos = cos_ref[...]
    sin = sin_ref[...]
    scale = (QK_NOPE + QK_ROPE) ** -0.5 * math.log2(math.e)

    ckv_n = _rms(ckv, kvg_ref[...])
    kpe = tail * cos + pltpu.roll(tail, LANES - QK_ROPE, 1) * sin
    ckv_ref[...] = ckv_n
    kpe_ref[...] = kpe[:, :QK_ROPE]
    k_ref[:, :kvr] = ckv_n.astype(BF16)
    k_ref[:, kvr:] = kpe.astype(BF16)
    ones = jnp.ones((kt_ref.shape[0] - kvr, ckv_n.shape[0]), F32)
    kt_ref[...] = jnp.concatenate([ckv_n.T, ones], axis=0).astype(BF16)

    cqn = _rms(cq, qg_ref[...]).astype(BF16)
    qall = _dot(cqn, wq_ref[...])
    n_nope = MLA_HEADS * QK_NOPE
    q_abs = _dot(qall[:, :n_nope].astype(BF16), wabs_ref[...]) * scale
    for h in range(MLA_HEADS):
        pe = qall[:, n_nope + h * LANES:n_nope + (h + 1) * LANES]
        sw = qall[:, n_nope + (MLA_HEADS + h) * LANES:n_nope + (MLA_HEADS + h + 1) * LANES]
        q_ref[:, 2 * h * LANES:(2 * h + 1) * LANES] = q_abs[:, h * kvr:(h + 1) * kvr].astype(BF16)
        q_ref[:, (2 * h + 1) * LANES:(2 * h + 2) * LANES] = ((pe * cos + sw * sin) * scale).astype(BF16)

    la_pre = _dot(tail.astype(BF16), wa2_ref[...]) + ba_ref[...]
    la = -_softplus(-la_pre) * (1.0 / GLA_TAU)
    g_ref[:, :gk_w] = gqk[:, :gk_w] * (GLA_DK ** -0.5)
    g_ref[:, gk_w:2 * gk_w] = gqk[:, gk_w:]
    g_ref[:, 2 * gk_w:3 * gk_w] = la
    g_ref[:, 3 * gk_w:] = gvo


def _pre_call(x, lw_, cos_t, sin_t, dims):
    lw, qr, kvr, gk_w, gv_w = dims
    tm = TOKEN_TILE
    xs, x_specs, n_first, T = _token_specs(x, tm)
    assert T % tm == 0
    row = lambda i: (i, 0)
    full = lambda i: (0, 0)
    wspec = lambda a: pl.BlockSpec(a.shape, full)
    ws = [lw_['g1'], lw_['w_in'], lw_['q_g'], lw_['w_q'], lw_['w_abs'], lw_['kv_g'], lw_['wa2'], lw_['ba']]
    ins = xs + ws
    in_specs = x_specs + [wspec(a) for a in ws]
    in_specs += [pl.BlockSpec((tm, LANES), row), pl.BlockSpec((tm, LANES), row)]
    out_shape = (
        jax.ShapeDtypeStruct((T, 2 * lw), F32),
        jax.ShapeDtypeStruct((T, 2 * LANES * MLA_HEADS), BF16),
        jax.ShapeDtypeStruct((T, 2 * LANES), BF16),
        jax.ShapeDtypeStruct((T // tm, kvr + BF16_ROWS, tm), BF16),
        jax.ShapeDtypeStruct((T, kvr), F32),
        jax.ShapeDtypeStruct((T, QK_ROPE), F32),
        jax.ShapeDtypeStruct((T, 3 * gk_w + 2 * gv_w), F32),
    )
    out_specs = tuple(pl.BlockSpec((None, s.shape[1], tm), lambda i: (i, 0, 0)) if len(s.shape) == 3
                      else pl.BlockSpec((tm, s.shape[1]), row) for s in out_shape)
    return pl.pallas_call(
        functools.partial(_pre_kernel, dims=dims, n_first=n_first),
        grid=(T // tm,), in_specs=in_specs, out_specs=out_specs, out_shape=out_shape,
        compiler_params=_cparams("parallel"), name="pre_proj",
    )(*ins, cos_t, sin_t)


def _lru_kernel(zl_ref, cbuf_ref, h0_ref, cw_ref, cb_ref, wa_ref, ba_ref, wi_ref, bi_ref, lam_ref, na_ref,
                a_ref, conv_ref, hout_ref, xbuf, hcar, *, ts, lw):
    i = pl.program_id(1)
    last = pl.num_programs(1) - 1
    pad = SUBLANES
    nbuf = CONV_WIDTH - 1

    @pl.when(i == 0)
    def _():
        xbuf[0:pad, :] = jnp.zeros((pad, lw), F32)
        xbuf[pad - nbuf:pad, :] = cbuf_ref[...]
        hcar[...] = h0_ref[...]

    x = zl_ref[:, :lw]
    y = zl_ref[:, lw:]
    xbuf[pad:pad + ts, :] = x
    xc = cb_ref[...] + cw_ref[nbuf:nbuf + 1, :] * x
    for k in range(nbuf):
        xc = xc + cw_ref[k:k + 1, :] * xbuf[pad - nbuf + k:pad - nbuf + k + ts, :]

    @pl.when(i == last)
    def _():
        conv_ref[...] = xbuf[pad + ts - nbuf:pad + ts, :]

    xbuf[0:pad, :] = xbuf[ts:ts + pad, :]

    xb = xc.astype(BF16)
    r = jax.nn.sigmoid(_dot(xb, wa_ref[...]) + ba_ref[...])
    gi = jax.nn.sigmoid(_dot(xb, wi_ref[...]) + bi_ref[...])
    log_a = (-LRU_C) * r * _softplus(-lam_ref[...])
    a = jnp.exp(log_a)
    th = jnp.tanh(log_a)
    u = jnp.sqrt(-2.0 * th / (1.0 - th)) * (gi * xc)

    d = 1
    while d < ts:
        u = a * _shift_rows(u, d, 0.0) + u
        a = a * _shift_rows(a, d, 1.0)
        d *= 2
    h = a * hcar[...] + u
    hcar[...] = h[ts - 1:ts, :]

    @pl.when(i == last)
    def _():
        hout_ref[...] = h[ts - 1:ts, :]

    out_a = h * jax.nn.gelu(y)
    a_ref[...] = _rms(out_a, na_ref[...]).astype(a_ref.dtype)


def _lru_call(zl, row_off, B, S, cbuf, h0, lw_, out_dtype):
    lw = h0.shape[-1]
    ts = min(S, SEQ_TILE)
    assert S % ts == 0 and row_off % ts == 0 and S >= CONV_WIDTH - 1
    n = S // ts
    off = row_off // ts
    full = lambda b, i: (0, 0)
    ws = [lw_['conv_w'], lw_['conv_b'], lw_['lru_wa'], lw_['lru_ba'], lw_['lru_wi'], lw_['lru_bi'],
          lw_['lru_lam'], lw_['n_a']]
    in_specs = [
        pl.BlockSpec((ts, 2 * lw), lambda b, i: (off + b * n + i, 0)),
        pl.BlockSpec((None, CONV_WIDTH - 1, lw), lambda b, i: (b, 0, 0)),
        pl.BlockSpec((None, 1, lw), lambda b, i: (b, 0, 0)),
    ] + [pl.BlockSpec(a.shape, full) for a in ws]
    out_shape = (
        jax.ShapeDtypeStruct((B * S, lw), out_dtype),
        jax.ShapeDtypeStruct((B, CONV_WIDTH - 1, lw), F32),
        jax.ShapeDtypeStruct((B, 1, lw), F32),
    )
    out_specs = (
        pl.BlockSpec((ts, lw), lambda b, i: (b * n + i, 0)),
        pl.BlockSpec((None, CONV_WIDTH - 1, lw), lambda b, i: (b, 0, 0)),
        pl.BlockSpec((None, 1, lw), lambda b, i: (b, 0, 0)),
    )
    return pl.pallas_call(
        functools.partial(_lru_kernel, ts=ts, lw=lw),
        grid=(B, n), in_specs=in_specs, out_specs=out_specs, out_shape=out_shape,
        scratch_shapes=[pltpu.VMEM((ts + SUBLANES, lw), F32), pltpu.VMEM((1, lw), F32)],
        compiler_params=_cparams("parallel", "arbitrary"), name="rg_lru",
    )(zl, cbuf, h0.reshape(B, 1, lw), *ws)


def _lru_seg_kernel(zl_ref, hist_ref, h0_ref, cw_ref, cb_ref, wa_ref, ba_ref, wi_ref, bi_ref, lam_ref, na_ref,
                    a_ref, h_ref, *, seg, lw):
    nbuf = CONV_WIDTH - 1
    x = zl_ref[:, :lw]
    y = zl_ref[:, lw:]
    n = x.shape[0]
    pos = lax.broadcasted_iota(jnp.int32, (n, 1), 0) & (seg - 1)
    hist = hist_ref[...]
    xc = cb_ref[...] + cw_ref[nbuf:nbuf + 1, :] * x
    for k in range(nbuf):
        j = nbuf - k
        prev = jnp.where(pos >= j, pltpu.roll(x, j, 0), pltpu.roll(hist, (j - seg) % n, 0))
        xc = xc + cw_ref[k:k + 1, :] * prev
    xb = xc.astype(BF16)
    r = jax.nn.sigmoid(_dot(xb, wa_ref[...]) + ba_ref[...])
    gi = jax.nn.sigmoid(_dot(xb, wi_ref[...]) + bi_ref[...])
    log_a = (-LRU_C) * r * _softplus(-lam_ref[...])
    a = jnp.exp(log_a)
    th = jnp.tanh(log_a)
    u = jnp.sqrt(-2.0 * th / (1.0 - th)) * (gi * xc)
    d = 1
    while d < seg:
        u = a * jnp.where(pos >= d, pltpu.roll(u, d, 0), 0.0) + u
        a = a * jnp.where(pos >= d, pltpu.roll(a, d, 0), 1.0)
        d *= 2
    h = a * h0_ref[...] + u
    h_ref[...] = h
    a_ref[...] = _rms(h * jax.nn.gelu(y), na_ref[...]).astype(a_ref.dtype)


def _lru_seg_call(zl, row_off, B, S, cbuf, h0, lw_):
    lw = h0.shape[-1]
    n = B * S
    tm = min(n, TOKEN_TILE)
    assert S & (S - 1) == 0 and S >= CONV_WIDTH - 1 and tm % S == 0 and n % tm == 0 and row_off % tm == 0
    off = row_off // tm
    hist = jnp.pad(cbuf, ((0, 0), (S - (CONV_WIDTH - 1), 0), (0, 0))).reshape(n, lw)
    h0_rows = jnp.repeat(h0, S, axis=0)
    full = lambda i: (0, 0)
    row = lambda i: (i, 0)
    ws = [lw_['conv_w'], lw_['conv_b'], lw_['lru_wa'], lw_['lru_ba'], lw_['lru_wi'], lw_['lru_bi'],
          lw_['lru_lam'], lw_['n_a']]
    a, h = pl.pallas_call(
        functools.partial(_lru_seg_kernel, seg=S, lw=lw),
        grid=(n // tm,),
        in_specs=[pl.BlockSpec((tm, 2 * lw), lambda i: (off + i, 0)), pl.BlockSpec((tm, lw), row),
                  pl.BlockSpec((tm, lw), row)] + [pl.BlockSpec(w.shape, full) for w in ws],
        out_specs=(pl.BlockSpec((tm, lw), row), pl.BlockSpec((tm, lw), row)),
        out_shape=(jax.ShapeDtypeStruct((n, lw), BF16), jax.ShapeDtypeStruct((n, lw), F32)),
        compiler_params=_cparams("parallel"), name="rg_lru_seg",
    )(zl, hist, h0_rows, *ws)
    return a, h.reshape(B, S, lw)[:, S - 1]


def _gla_kernel(g_ref, s0_ref, gn_ref, nc_ref, c_ref, sout_ref, st, *, ts, nseq, gk_w, gv_w):
    i = pl.program_id(1)
    last = pl.num_programs(1) - 1
    C = GLA_CHUNK
    rows = min(ts, C)
    n_chunks = max(ts // C, 1)
    log2c = int(math.log2(C))

    @pl.when(i == 0)
    def _():
        st[...] = s0_ref[...]

    dk_sh, dv_sh = int(math.log2(GLA_DK)), int(math.log2(GLA_DV))
    hd = lax.broadcasted_iota(jnp.int32, (gk_w, gv_w), 0) >> dk_sh
    he = lax.broadcasted_iota(jnp.int32, (gk_w, gv_w), 1) >> dv_sh
    same = (hd == he).astype(BF16)
    he_t = lax.broadcasted_iota(jnp.int32, (gv_w, gk_w), 0) >> dv_sh
    hd_t = lax.broadcasted_iota(jnp.int32, (gv_w, gk_w), 1) >> dk_sh
    same_t = (he_t == hd_t).astype(F32)
    sel_t = lax.broadcasted_iota(jnp.int32, (C, C * C), 0)
    sel_r = lax.broadcasted_iota(jnp.int32, (C, C * C), 1) >> log2c
    sel = (sel_t == sel_r).astype(BF16)
    srow = lax.broadcasted_iota(jnp.int32, (C, 1), 0)

    seq_out = []
    for sq in range(nseq):
        blk = g_ref[sq * ts:(sq + 1) * ts, :]
        if rows < C:
            blk = jnp.concatenate([blk, jnp.zeros((C - rows, blk.shape[1]), F32)], axis=0)
        q = blk[:, :gk_w]
        k = blk[:, gk_w:2 * gk_w]
        la = blk[:, 2 * gk_w:3 * gk_w]
        v = blk[:, 3 * gk_w:3 * gk_w + gv_w]
        nrow = n_chunks * C
        pos = lax.broadcasted_iota(jnp.int32, (nrow, 1), 0) & (C - 1)
        cum = la
        d = 1
        while d < C:
            cum = cum + jnp.where(pos >= d, pltpu.roll(cum, d, 0), 0.0)
            d *= 2
        tot = jnp.where(pos == C - 1, cum, 0.0)
        d = 1
        while d < C:
            tot = tot + jnp.where(pos < C - d, pltpu.roll(tot, nrow - d, 0), 0.0)
            d *= 2
        qe = (q * jnp.exp(cum)).astype(BF16)
        kdec = (k * jnp.exp(tot - cum)).astype(BF16)
        dec = jnp.exp(tot)
        vb = v.astype(BF16)

        o_intra, upd = [], []
        for c in range(n_chunks):
            sl = slice(c * C, (c + 1) * C)
            cum_c, q_c, k_c, v_c = cum[sl], q[sl], k[sl], v[sl]
            pieces = []
            for t in range(C):
                diff = jnp.where(srow <= t, cum_c[t:t + 1, :] - cum_c, NEG_BIG)
                pieces.append(q_c[t:t + 1, :] * k_c * jnp.exp(diff))
            w = jnp.concatenate(pieces, axis=0)
            att = _dot(w.astype(BF16), same)
            xv = att * jnp.concatenate([v_c] * C, axis=0)
            o_intra.append(_dot(sel, xv.astype(BF16)))
            upd.append(lax.dot_general(vb[sl], kdec[sl], (((0,), (0,)), ((), ())),
                                       preferred_element_type=F32) * same_t)

        s_t = st[sq]
        outs = []
        for c in range(n_chunks):
            sl = slice(c * C, (c + 1) * C)
            outs.append(o_intra[c] + _dot_nt(qe[sl], s_t.astype(BF16)))
            s_t = s_t * dec[c * C:c * C + 1, :] + upd[c]
        st[sq] = s_t
        seq_out.append(jnp.concatenate(outs, axis=0)[:ts] if n_chunks > 1 else outs[0][:ts])

    o = jnp.concatenate(seq_out, axis=0) if nseq > 1 else seq_out[0]

    @pl.when(i == last)
    def _():
        sout_ref[...] = st[...]

    go = g_ref[:, 3 * gk_w + gv_w:]
    e64 = (lax.broadcasted_iota(jnp.int32, (gv_w, gv_w), 0) >> dv_sh
           == lax.broadcasted_iota(jnp.int32, (gv_w, gv_w), 1) >> dv_sh).astype(BF16)
    osq = o * o
    osq_hi = osq.astype(BF16)
    osq_lo = (osq - osq_hi.astype(F32)).astype(BF16)
    ms = (_dot(osq_hi, e64) + _dot(osq_lo, e64)) * (1.0 / GLA_DV)
    out_c = o * lax.rsqrt(ms + EPS) * gn_ref[...] * (go * jax.nn.sigmoid(go))
    c_ref[...] = _rms(out_c, nc_ref[...]).astype(c_ref.dtype)


def _gla_call(g, row_off, B, S, s0t, lw_, out_dtype, gk_w, gv_w):
    ts = min(S, SEQ_TILE)
    nseq = math.gcd(B, max(SEQ_TILE // (2 * ts), 1)) if ts == S else 1
    rows = nseq * ts
    assert S % ts == 0 and row_off % rows == 0 and (ts % GLA_CHUNK == 0 or ts < GLA_CHUNK)
    n = S // ts
    off = row_off // rows
    gw = g.shape[1]
    full = lambda b, i: (0, 0)
    out_shape = (jax.ShapeDtypeStruct((B * S, gv_w), out_dtype),
                 jax.ShapeDtypeStruct((B, gv_w, gk_w), F32))
    return pl.pallas_call(
        functools.partial(_gla_kernel, ts=ts, nseq=nseq, gk_w=gk_w, gv_w=gv_w),
        grid=(B // nseq, n),
        in_specs=[pl.BlockSpec((rows, gw), lambda b, i: (off + b * n + i, 0)),
                  pl.BlockSpec((nseq, gv_w, gk_w), lambda b, i: (b, 0, 0)),
                  pl.BlockSpec((1, gv_w), full), pl.BlockSpec((1, gv_w), full)],
        out_specs=(pl.BlockSpec((rows, gv_w), lambda b, i: (b * n + i, 0)),
                   pl.BlockSpec((nseq, gv_w, gk_w), lambda b, i: (b, 0, 0))),
        out_shape=out_shape,
        scratch_shapes=[pltpu.VMEM((nseq, gv_w, gk_w), F32)],
        compiler_params=_cparams("parallel", "arbitrary"), name="gla",
    )(g, s0t, lw_['gla_g'], lw_['n_c'])


def _tree(op, xs):
    xs = list(xs)
    while len(xs) > 1:
        xs = [op(xs[i], xs[i + 1]) if i + 1 < len(xs) else xs[i] for i in range(0, len(xs), 2)]
    return xs[0]


def _attn_prompt_kernel(q_ref, k_ref, kt_ref, wuvt_ref, nb_ref, o_ref, m_scr, acc_scr, *, tq, kvr):
    i = pl.program_id(1)
    kw = 2 * LANES
    m_scr[...] = jnp.full(m_scr.shape, NEG_BIG, F32)
    acc_scr[...] = jnp.zeros(acc_scr.shape, F32)

    def block(j, masked):
        kb = k_ref[pl.ds(pl.multiple_of(j * tq, tq), tq), :]
        kbt = kt_ref[j]
        if masked:
            key = lax.broadcasted_iota(jnp.int32, (tq, tq), 0)
            qry = lax.broadcasted_iota(jnp.int32, (tq, tq), 1)
            keep = key <= qry
        qk = lambda h: _dot_nt(kb, q_ref[:, h * kw:(h + 1) * kw])
        st_next = qk(0)
        for h in range(MLA_HEADS):
            st = st_next
            if h + 1 < MLA_HEADS:
                st_next = qk(h + 1)
            if masked:
                st = jnp.where(keep, st, NEG_BIG)
            m_old = m_scr[h]
            m_new = jnp.maximum(m_old, jnp.max(st, axis=0, keepdims=True))
            alpha = jnp.exp2(m_old - m_new)
            pt = jnp.exp2(st - m_new).astype(BF16)
            acc_scr[h] = alpha * acc_scr[h] + _dot(kbt, pt)
            m_scr[h] = m_new

    def body(j, carry):
        block(j, False)
        return carry

    lax.fori_loop(0, i, body, 0)
    block(i, True)

    out_t = jnp.zeros((wuvt_ref.shape[0], tq), F32)
    for h in range(MLA_HEADS):
        acc = acc_scr[h]
        o_t = (acc[:kvr, :] / acc[kvr:kvr + 1, :]).astype(BF16)
        out_t = out_t + _dot(wuvt_ref[:, h * kvr:(h + 1) * kvr], o_t)
    ms = jnp.mean(out_t * out_t, axis=0, keepdims=True)
    out_t = out_t * lax.rsqrt(ms + EPS) * nb_ref[...]
    o_ref[...] = out_t.T.astype(o_ref.dtype)


def _attn_prompt_call(qcat, kcat, kcat_t, B, S, lw_, kvr):
    tq = kcat_t.shape[2]
    assert S % tq == 0
    n = S // tq
    mw = lw_['w_uv_rows_t'].shape[0]
    full = lambda b, i: (0, 0)
    return pl.pallas_call(
        functools.partial(_attn_prompt_kernel, tq=tq, kvr=kvr),
        grid=(B, n),
        in_specs=[pl.BlockSpec((tq, qcat.shape[1]), lambda b, i: (b * n + i, 0)),
                  pl.BlockSpec((S, kcat.shape[1]), lambda b, i: (b, 0)),
                  pl.BlockSpec((n, kcat_t.shape[1], tq), lambda b, i: (b, 0, 0)),
                  pl.BlockSpec(lw_['w_uv_rows_t'].shape, full), pl.BlockSpec((mw, 1), full)],
        out_specs=pl.BlockSpec((tq, mw), lambda b, i: (b * n + i, 0)),
        out_shape=jax.ShapeDtypeStruct((B * S, mw), BF16),
        scratch_shapes=[pltpu.VMEM((MLA_HEADS, 1, tq), F32),
                        pltpu.VMEM((MLA_HEADS, kcat_t.shape[1], tq), F32)],
        compiler_params=_cparams("parallel", "arbitrary"), name="attn_prompt",
    )(qcat, kcat, kcat_t, lw_['w_uv_rows_t'], lw_['n_b'].reshape(mw, 1))


def _attn_sample_kernel(pt_ref, q_ref, nckv_ref, nkpe_ref, ckv_hbm, kpe_hbm, wuv_ref, nb_ref, o_ref,
                        q_scr, s_scr, v_scr, ckv_buf, kpe_buf, sem, *, layer, n_pages, sq, kvr, page):
    b = pl.program_id(0)
    nrow = MLA_HEADS * sq

    def page_copies(bb, lo=0, hi=n_pages):
        slot = lax.rem(bb, 2)
        out = []
        for j in range(lo, hi):
            pg = pt_ref[bb * n_pages + j]
            out.append(pltpu.make_async_copy(ckv_hbm.at[layer, pg], ckv_buf.at[slot, j], sem.at[0, slot]))
            out.append(pltpu.make_async_copy(kpe_hbm.at[layer, pg], kpe_buf.at[slot, j], sem.at[1, slot]))
        return out

    def prefetch_next(lo, hi):
        @pl.when(b + 1 < pl.num_programs(0))
        def _():
            for c in page_copies(b + 1, lo, hi):
                c.start()

    @pl.when(b == 0)
    def _():
        for c in page_copies(b):
            c.start()

    prefetch_next(0, n_pages // 2)

    for h in range(MLA_HEADS):
        q_scr[h * sq:(h + 1) * sq, :] = q_ref[:, 2 * h * LANES:(2 * h + 2) * LANES].astype(F32)
    qa = q_scr[:, :kvr].astype(BF16)
    qp = q_scr[:, kvr:kvr + QK_ROPE].astype(BF16)

    for c in page_copies(b):
        c.wait()
    slot = lax.rem(b, 2)
    gp = SCORE_PAGES
    for j in range(0, n_pages, gp):
        ck = ckv_buf[slot, j:j + gp].reshape(gp * page, kvr).astype(BF16)
        kp = jnp.concatenate([kpe_buf[slot, j + t] for t in range(gp)], axis=1).astype(BF16)
        v_scr[j * page:(j + gp) * page, :] = ck
        s_scr[:, j * page:(j + gp) * page] = _dot_nt(qa, ck) + _dot(qp, kp)
    prefetch_next(n_pages // 2, n_pages)

    zpad = lambda a: jnp.concatenate([a, jnp.zeros((page - sq, a.shape[1]), F32)], axis=0)
    ck_new = zpad(nckv_ref[...]).astype(BF16)
    kp_new = zpad(nkpe_ref[...]).astype(BF16)
    tok = lax.broadcasted_iota(jnp.int32, (nrow, page), 0) & (sq - 1)
    key = lax.broadcasted_iota(jnp.int32, (nrow, page), 1)
    s_new = jnp.where(key <= tok, _dot_nt(qa, ck_new) + _dot_nt(qp, kp_new), NEG_BIG)
    tile = lambda i: s_scr[:, i * page:(i + 1) * page]
    m = jnp.max(_tree(jnp.maximum, [tile(i) for i in range(n_pages)] + [s_new]), axis=-1, keepdims=True)
    p_new = jnp.exp2(s_new - m)
    acc = _dot(p_new.astype(BF16), ck_new)
    psum = p_new
    vp = VALUE_PAGES
    for i in range(0, n_pages, vp):
        p = jnp.exp2(s_scr[:, i * page:(i + vp) * page] - m)
        psum = psum + _tree(jnp.add, [p[:, t * page:(t + 1) * page] for t in range(vp)])
        acc = acc + _dot(p.astype(BF16), v_scr[i * page:(i + vp) * page, :])
    l = jnp.sum(psum, axis=-1, keepdims=True)
    o_lat = (acc / l).astype(BF16)
    mw = wuv_ref.shape[1]
    r = _dot(o_lat, wuv_ref[...])
    rh = lax.broadcasted_iota(jnp.int32, (nrow, mw), 0) >> int(math.log2(sq))
    ch = lax.broadcasted_iota(jnp.int32, (nrow, mw), 1) >> int(math.log2(V_HEAD))
    r = jnp.where(rh == ch, r, 0.0)
    out = r[0:sq, :]
    for h in range(1, MLA_HEADS):
        out = out + r[h * sq:(h + 1) * sq, :]
    o_ref[...] = _rms(out, nb_ref[...])


def _attn_sample_call(page_table, qs, nckv, nkpe, cache_ckv, cache_kpe_t, layer, lw_, kvr):
    B, sq, qw = qs.shape
    n_pages = page_table.shape[1]
    page = cache_ckv.shape[2]
    mw = lw_['w_uv_cat'].shape[1]
    nrow = MLA_HEADS * sq
    assert sq == SUBLANES and sq <= page and n_pages % SCORE_PAGES == 0 and n_pages % VALUE_PAGES == 0
    hbm = pl.BlockSpec(memory_space=pl.ANY)
    in_specs = [pl.BlockSpec((None, sq, qw), lambda b, pt: (b, 0, 0)),
                pl.BlockSpec((None, sq, kvr), lambda b, pt: (b, 0, 0)),
                pl.BlockSpec((None, sq, QK_ROPE), lambda b, pt: (b, 0, 0)),
                hbm, hbm,
                pl.BlockSpec(lw_['w_uv_cat'].shape, lambda b, pt: (0, 0)),
                pl.BlockSpec((1, mw), lambda b, pt: (0, 0))]
    grid_spec = pltpu.PrefetchScalarGridSpec(
        num_scalar_prefetch=1, grid=(B,), in_specs=in_specs,
        out_specs=pl.BlockSpec((None, sq, mw), lambda b, pt: (b, 0, 0)),
        scratch_shapes=[pltpu.VMEM((nrow, 2 * LANES), F32), pltpu.VMEM((nrow, n_pages * page), F32),
                        pltpu.VMEM((n_pages * page, kvr), BF16),
                        pltpu.VMEM((2, n_pages, page, kvr), F32), pltpu.VMEM((2, n_pages, QK_ROPE, page), F32),
                        pltpu.SemaphoreType.DMA((2, 2))])
    return pl.pallas_call(
        functools.partial(_attn_sample_kernel, layer=layer, n_pages=n_pages, sq=sq, kvr=kvr, page=page),
        grid_spec=grid_spec, out_shape=jax.ShapeDtypeStruct((B, sq, mw), F32),
        compiler_params=_cparams("arbitrary"), name="attn_sample",
    )(page_table.reshape(-1), qs, nckv, nkpe, cache_ckv, cache_kpe_t, lw_['w_uv_cat'], lw_['n_b'])


def _post_kernel(a_ref, b_ref, c_ref, *refs, widths, n_first):
    n_x = 1 if n_first is None else 2
    wo_ref, g2_ref, wr_ref, br_ref, x1_ref, hn_ref, rt_ref, cnt_ref, cnt_scr = refs[n_x:]
    wa, wb, wc = widths

    @pl.when(pl.program_id(0) == 0)
    def _():
        cnt_scr[...] = jnp.zeros(cnt_scr.shape, F32)

    mix = _dot(a_ref[...], wo_ref[:wa, :])
    mix = mix + _dot(b_ref[...], wo_ref[wa:wa + wb, :])
    mix = mix + _dot(c_ref[...], wo_ref[wa + wb:, :])
    x1 = _token_tile(refs[:n_x], n_first) + mix
    x1_ref[...] = x1
    hn = _rms(x1, g2_ref[...])
    for sub in range(hn_ref.shape[1]):
        hn_ref[:, sub, :] = hn[:, sub * LANES:(sub + 1) * LANES]
    hn_hi = hn.astype(BF16)
    hn_lo = (hn - hn_hi.astype(F32)).astype(BF16)
    hh = _dot(hn_hi, wr_ref[...])
    logits = (hh[:, :LANES] + hh[:, LANES:] + _dot(hn_lo, wr_ref[:, :LANES])) + br_ref[...]

    lane = lax.broadcasted_iota(jnp.int32, logits.shape, 1)
    lane_f = lane.astype(F32)
    big = float(LANES)
    is_g = lane < N_GROUPS
    gl = jnp.where(is_g, logits, NEG_BIG)
    gmax = jnp.max(gl, axis=-1, keepdims=True)
    gsel = jnp.min(jnp.where(gl == gmax, lane_f, big), axis=-1, keepdims=True)
    gprob = 1.0 / jnp.sum(jnp.where(is_g, jnp.exp(gl - gmax), 0.0), axis=-1, keepdims=True)
    lo = N_GROUPS + gsel * EXPERTS_PER_GROUP
    el = jnp.where((lane_f >= lo) & (lane_f < lo + EXPERTS_PER_GROUP), logits, NEG_BIG)
    v1 = jnp.max(el, axis=-1, keepdims=True)
    i1 = jnp.min(jnp.where(el == v1, lane_f, big), axis=-1, keepdims=True)
    el2 = jnp.where(lane_f == i1, NEG_BIG, el)
    v2 = jnp.max(el2, axis=-1, keepdims=True)
    i2 = jnp.min(jnp.where(el2 == v2, lane_f, big), axis=-1, keepdims=True)
    e21 = jnp.exp(v2 - v1)
    w1 = gprob / (1.0 + e21)
    w2 = w1 * e21
    e1 = i1 - N_GROUPS
    e2 = i2 - N_GROUPS

    oh1 = (lane_f == e1).astype(F32)
    oh2 = (lane_f == e2).astype(F32)
    both = oh1 + oh2
    tm = both.shape[0]
    tri = (lax.broadcasted_iota(jnp.int32, (tm, tm), 0) >= lax.broadcasted_iota(jnp.int32, (tm, tm), 1))
    incl = _dot(tri.astype(BF16), both.astype(BF16))
    base = cnt_scr[0:1, :] + incl - both
    r1 = jnp.sum(oh1 * base, axis=-1, keepdims=True)
    r2 = jnp.sum(oh2 * base, axis=-1, keepdims=True)
    cnt_scr[...] = cnt_scr[...] + incl[tm - 1:tm, :]
    cnt_ref[...] = cnt_scr[...]

    rt = jnp.where(lane == 0, e1, 0.0)
    rt = jnp.where(lane == 1, e2, rt)
    rt = jnp.where(lane == 2, w1, rt)
    rt = jnp.where(lane == 3, w2, rt)
    rt = jnp.where(lane == 4, r1, rt)
    rt = jnp.where(lane == 5, r2, rt)
    rt_ref[...] = rt


def _post_call(a, b, c, x, lw_):
    tm = TOKEN_TILE
    xs, x_specs, n_first, T = _token_specs(x, tm)
    D = xs[0].shape[1]
    row = lambda i: (i, 0)
    full = lambda i: (0, 0)
    widths = (a.shape[1], b.shape[1], c.shape[1])
    ws = [lw_['w_out'], lw_['g2'], lw_['w_r'], lw_['b_r']]
    out_shape = (jax.ShapeDtypeStruct((T, D), F32), jax.ShapeDtypeStruct((T, D // LANES, LANES), F32),
                 jax.ShapeDtypeStruct((T, LANES), F32), jax.ShapeDtypeStruct((SUBLANES, LANES), F32))
    return pl.pallas_call(
        functools.partial(_post_kernel, widths=widths, n_first=n_first),
        grid=(T // tm,),
        in_specs=[pl.BlockSpec((tm, w), row) for w in widths] + x_specs
        + [pl.BlockSpec(w.shape, full) for w in ws],
        out_specs=tuple(pl.BlockSpec((tm,) + s.shape[1:], (lambda i: (i, 0, 0)) if len(s.shape) == 3 else row)
                        for s in out_shape[:3])
        + (pl.BlockSpec((SUBLANES, LANES), full),),
        out_shape=out_shape, scratch_shapes=[pltpu.VMEM((SUBLANES, LANES), F32)],
        compiler_params=_cparams("arbitrary"), name="post_proj",
    )(a, b, c, *xs, *ws)


def _dest_kernel(rt_ref, ps_ref, d_ref):
    rt = rt_ref[...]
    lane = lax.broadcasted_iota(jnp.int32, rt.shape, 1)
    lane_f = lane.astype(F32)
    pick = lambda k: jnp.sum(jnp.where(lane == k, rt, 0.0), axis=-1, keepdims=True)
    ps = ps_ref[...]
    d1 = jnp.sum(jnp.where(lane_f == pick(0), ps, 0.0), axis=-1, keepdims=True) + pick(2 * TOP_K)
    d2 = jnp.sum(jnp.where(lane_f == pick(1), ps, 0.0), axis=-1, keepdims=True) + pick(2 * TOP_K + 1)
    d_ref[...] = jnp.where(lane == 0, d1, jnp.where(lane == 1, d2, 0.0))


def _dest_call(route, pstart_row):
    T = route.shape[0]
    tm = TOKEN_TILE
    return pl.pallas_call(
        _dest_kernel, grid=(T // tm,),
        in_specs=[pl.BlockSpec((tm, LANES), lambda i: (i, 0)), pl.BlockSpec((1, LANES), lambda i: (0, 0))],
        out_specs=pl.BlockSpec((tm, LANES), lambda i: (i, 0)),
        out_shape=jax.ShapeDtypeStruct((T, LANES), F32), compiler_params=_cparams("parallel"), name="moe_dest",
    )(route, pstart_row)


def _moe_kernel(te_ref, nt_ref, cnt_ref, tok_ref, hn_hbm, w_ref, wg_ref, wu_ref, wd_ref, y_ref, xbuf, sem):
    i = pl.program_id(0)
    tm = xbuf.shape[1]
    nt = nt_ref[0]
    n_groups = tm // MOE_ROW_GROUP

    def start_rows(t):
        slot = lax.rem(t, 2)

        def body(g, carry):
            @pl.when(g * MOE_ROW_GROUP < cnt_ref[t])
            def _():
                for k in range(MOE_ROW_GROUP):
                    r = g * MOE_ROW_GROUP + k
                    pltpu.make_async_copy(hn_hbm.at[tok_ref[t * tm + r]], xbuf.at[slot, r], sem.at[slot]).start()
            return carry

        lax.fori_loop(0, n_groups, body, 0)

    @pl.when(i == 0)
    def _():
        xbuf[...] = jnp.zeros(xbuf.shape, F32)
        start_rows(i)

    @pl.when(i + 1 < nt)
    def _():
        start_rows(i + 1)

    @pl.when(i < nt)
    def _():
        slot = lax.rem(i, 2)

        def wait_group(g, carry):
            @pl.when(g * MOE_ROW_GROUP < cnt_ref[i])
            def _():
                grp = pl.ds(0, MOE_ROW_GROUP)
                pltpu.make_async_copy(hn_hbm.at[grp], xbuf.at[slot, grp], sem.at[slot]).wait()
            return carry

        lax.fori_loop(0, n_groups, wait_group, 0)
        x = jnp.concatenate([xbuf[slot, :, sub, :] for sub in range(xbuf.shape[2])], axis=1).astype(BF16)
        hg = _dot(x, wg_ref[...].astype(BF16))
        hu = _dot(x, wu_ref[...].astype(BF16))
        act = hg * jax.nn.sigmoid(hg) * hu * w_ref[...]
        y_ref[...] = _dot(act.astype(BF16), wd_ref[...].astype(BF16))

    @pl.when(i >= nt)
    def _():
        y_ref[...] = jnp.zeros(y_ref.shape, F32)


def _moe_call(hn, row_tok, row_w, tile_e, n_used, tile_cnt, w_gate, w_up, w_down, layer):
    D = w_gate.shape[-2]
    tm = MOE_TILE
    F = w_gate.shape[-1]
    n_tiles = tile_e.shape[0]
    grid_spec = pltpu.PrefetchScalarGridSpec(
        num_scalar_prefetch=4, grid=(n_tiles,),
        in_specs=[pl.BlockSpec(memory_space=pl.ANY),
                  pl.BlockSpec((tm, 1), lambda i, te, nt, cnt, tok: (i, 0)),
                  pl.BlockSpec((None, None, D, F), lambda i, te, nt, cnt, tok: (layer, te[i], 0, 0)),
                  pl.BlockSpec((None, None, D, F), lambda i, te, nt, cnt, tok: (layer, te[i], 0, 0)),
                  pl.BlockSpec((None, None, F, D), lambda i, te, nt, cnt, tok: (layer, te[i], 0, 0))],
        out_specs=pl.BlockSpec((tm, D), lambda i, te, nt, cnt, tok: (i, 0)),
        scratch_shapes=[pltpu.VMEM((2, tm) + hn.shape[1:], F32), pltpu.SemaphoreType.DMA((2,))])
    return pl.pallas_call(
        _moe_kernel, grid_spec=grid_spec, out_shape=jax.ShapeDtypeStruct((n_tiles * tm, D), F32),
        compiler_params=_cparams("arbitrary"), name="moe_experts",
    )(tile_e, n_used, tile_cnt, row_tok, hn, row_w, w_gate, w_up, w_down)


def _route_meta(route, counts_f, tm, n_tiles):
    T = route.shape[0]
    n_assign = TOP_K * T
    e = route[:, :TOP_K].astype(jnp.int32).reshape(-1)
    w = route[:, TOP_K:2 * TOP_K].reshape(-1)
    order = jnp.argsort(e, stable=True).astype(jnp.int32)
    counts = counts_f.astype(jnp.int32)
    zero = jnp.zeros((1,), jnp.int32)
    start = jnp.concatenate([zero, jnp.cumsum(counts)])
    pstart = jnp.concatenate([zero, jnp.cumsum(((counts + tm - 1) // tm) * tm)])
    n_used = (pstart[N_EXPERTS] // tm).reshape(1)
    tile_lo = jnp.arange(n_tiles, dtype=jnp.int32) * tm
    tile_e = jnp.sum((pstart[None, 1:] <= tile_lo[:, None]).astype(jnp.int32), axis=1)
    tile_e = jnp.minimum(tile_e, N_EXPERTS - 1)
    oh = (tile_e[:, None] == jnp.arange(N_EXPERTS, dtype=jnp.int32)[None, :]).astype(jnp.int32)
    pick = lambda tab: jnp.sum(oh * tab[None, :N_EXPERTS], axis=1)
    k0 = tile_lo - pick(pstart)
    tile_cnt = jnp.clip(pick(counts) - k0, 0, tm)
    k = k0[:, None] + jnp.arange(tm, dtype=jnp.int32)[None, :]
    valid = (k < pick(counts)[:, None]).reshape(-1)
    idx = jnp.clip(pick(start)[:, None] + k, 0, n_assign - 1).reshape(-1)
    src = order[idx]
    row_tok = jnp.where(valid, src // TOP_K, 0)
    row_w = jnp.where(valid, w[src], 0.0)
    ps_row = jnp.pad(pstart[:N_EXPERTS].astype(F32), (0, LANES - N_EXPERTS)).reshape(1, LANES)
    pos = _dest_call(route, ps_row)[:, :TOP_K].astype(jnp.int32)
    return row_tok, row_w.reshape(-1, 1), pos, tile_e, n_used, tile_cnt


def _final_kernel(x_ref, g_ref, o_ref):
    o_ref[...] = _rms(x_ref[...], g_ref[...])


def _final_call(x, g, row_off, n_rows):
    D = x.shape[1]
    tm = TOKEN_TILE
    assert row_off % tm == 0 and n_rows % tm == 0
    off = row_off // tm
    return pl.pallas_call(
        _final_kernel, grid=(n_rows // tm,),
        in_specs=[pl.BlockSpec((tm, D), lambda i: (off + i, 0)), pl.BlockSpec((1, D), lambda i: (0, 0))],
        out_specs=pl.BlockSpec((tm, D), lambda i: (i, 0)),
        out_shape=jax.ShapeDtypeStruct((n_rows, D), F32), compiler_params=_cparams("parallel"), name="final_norm",
    )(x, g)


def _block_diag(w):
    n, c, d = w.shape
    return jnp.einsum('ncd,nm->ncmd', w, jnp.eye(n, dtype=w.dtype)).reshape(n * c, n * d)


def _half_swap(w):
    half = w.shape[-1] // 2
    return jnp.concatenate([w[..., half:], w[..., :half]], axis=-1)


def _layer_weights(l, p, dims):
    lw, qr, kvr, gk_w, gv_w = dims
    D = p['w_in'].shape[1]
    row = lambda a: a.reshape(1, -1).astype(F32)
    w_in = p['w_in'][l]
    sizes = (lw, lw, qr, kvr, QK_ROPE, gk_w, gk_w, gv_w, p['gla_wa2'].shape[1], gv_w)
    x_lru, y_lru, c_q, c_kv, k_pe, g_q, g_k, g_v, g_a, g_o = jnp.split(w_in, list(np.cumsum(sizes)[:-1]), axis=1)
    tail_pad = LANES - 2 * QK_ROPE - g_a.shape[1]
    tail = jnp.concatenate([k_pe, _half_swap(k_pe), g_a, jnp.zeros((D, tail_pad), F32)], axis=1)
    w_in_p = jnp.concatenate([x_lru, y_lru, c_q, c_kv, g_q, g_k, g_v, g_o, tail], axis=1).astype(BF16)

    w_uq = p['w_uq'][l].reshape(qr, MLA_HEADS, QK_NOPE + QK_ROPE)
    nope = w_uq[:, :, :QK_NOPE].reshape(qr, MLA_HEADS * QK_NOPE)
    pe = w_uq[:, :, QK_NOPE:]
    widen = lambda a: jnp.pad(a, ((0, 0), (0, 0), (0, LANES - QK_ROPE))).reshape(qr, MLA_HEADS * LANES)
    w_q = jnp.concatenate([nope, widen(pe), widen(_half_swap(pe))], axis=1).astype(BF16)
    w_abs = _block_diag(jnp.transpose(p['w_uk'][l], (1, 2, 0))).astype(BF16)
    wa2 = jnp.zeros((LANES, gk_w), F32).at[2 * QK_ROPE:2 * QK_ROPE + g_a.shape[1]].set(p['gla_wa2'][l]).astype(BF16)

    w_uv = p['w_uv'][l]
    mw = MLA_HEADS * V_HEAD
    w_uv_cat = w_uv.reshape(kvr, mw)
    w_uv_rows = _block_diag(jnp.transpose(w_uv, (1, 0, 2)))
    n_a, n_b, n_c = jnp.split(p['out_norm_g'][l], [lw, lw + mw])
    n_r = N_GROUPS + N_EXPERTS
    w_r = jnp.concatenate([p['router_wg'][l], p['router_we'][l], jnp.zeros((D, LANES - n_r), F32)], axis=1)
    b_r = jnp.concatenate([p['router_bg'][l], p['router_be'][l], jnp.zeros((LANES - n_r,), F32)])
    return dict(
        g1=row(p['norm1_g'][l]), w_in=w_in_p, q_g=row(p['q_norm_g'][l]), w_q=w_q, w_abs=w_abs,
        kv_g=row(p['kv_norm_g'][l]), wa2=wa2, ba=row(p['gla_ba'][l]),
        conv_w=p['conv_w'][l], conv_b=row(p['conv_b'][l]),
        lru_wa=_block_diag(p['lru_wa'][l]).astype(BF16), lru_ba=row(p['lru_ba'][l]),
        lru_wi=_block_diag(p['lru_wi'][l]).astype(BF16), lru_bi=row(p['lru_bi'][l]),
        lru_lam=row(p['lru_lambda'][l]), n_a=row(n_a), n_b=row(n_b), n_c=row(n_c),
        gla_g=row(jnp.tile(p['gla_norm_g'][l], GLA_HEADS)),
        w_uv_cat=w_uv_cat.astype(BF16), w_uv_rows_t=w_uv_rows.T.astype(BF16),
        w_out=p['w_out'][l].astype(BF16), g2=row(p['norm2_g'][l]),
        w_r=jnp.concatenate([w_r.astype(BF16), (w_r - w_r.astype(BF16).astype(F32)).astype(BF16)], axis=1),
        b_r=row(b_r),
    )


def _rope_tables(positions):
    half = QK_ROPE // 2
    inv = ROPE_THETA ** (-np.arange(half, dtype=np.float64) / half)
    ang = np.asarray(positions, np.float64)[:, None] * inv
    zeros = np.zeros((ang.shape[0], LANES - QK_ROPE))
    cos = np.concatenate([np.cos(ang), np.cos(ang), zeros], axis=1)
    sin = np.concatenate([-np.sin(ang), np.sin(ang), zeros], axis=1)
    return jnp.asarray(cos, F32), jnp.asarray(sin, F32)


def _gla_state_to_rows(s):
    B = s.shape[0]
    eye = jnp.eye(GLA_HEADS, dtype=s.dtype)
    return jnp.einsum('bhde,hg->bhegd', s, eye).reshape(B, GLA_HEADS * GLA_DV, GLA_HEADS * GLA_DK)


def _gla_rows_to_state(st):
    B = st.shape[0]
    s5 = st.reshape(B, GLA_HEADS, GLA_DV, GLA_HEADS, GLA_DK)
    blocks = jnp.stack([s5[:, h, :, h, :] for h in range(GLA_HEADS)], axis=1)
    return jnp.swapaxes(blocks, -1, -2)


def kernel(x_prompt, x_sample, cache_ckv, cache_kpe, page_table, state_conv, state_lru, state_gla, norm1_g, w_in, conv_w, conv_b, lru_wa, lru_ba, lru_wi, lru_bi, lru_lambda, q_norm_g, w_uq, kv_norm_g, w_uk, w_uv, gla_wa2, gla_ba, gla_norm_g, out_norm_g, w_out, norm2_g, router_wg, router_bg, router_we, router_be, w_gate, w_up, w_down, final_norm_g):
    p = dict(norm1_g=norm1_g, w_in=w_in, conv_w=conv_w, conv_b=conv_b, lru_wa=lru_wa, lru_ba=lru_ba,
             lru_wi=lru_wi, lru_bi=lru_bi, lru_lambda=lru_lambda, q_norm_g=q_norm_g, w_uq=w_uq,
             kv_norm_g=kv_norm_g, w_uk=w_uk, w_uv=w_uv, gla_wa2=gla_wa2, gla_ba=gla_ba,
             gla_norm_g=gla_norm_g, out_norm_g=out_norm_g, w_out=w_out, norm2_g=norm2_g,
             router_wg=router_wg, router_bg=router_bg, router_we=router_we, router_be=router_be)
    Bp, Sp, D = x_prompt.shape
    Bs, Ss, _ = x_sample.shape
    depth = w_in.shape[0]
    lw = state_lru.shape[-1]
    qr = q_norm_g.shape[-1]
    kvr = kv_norm_g.shape[-1]
    gk_w = GLA_HEADS * GLA_DK
    gv_w = GLA_HEADS * GLA_DV
    dims = (lw, qr, kvr, gk_w, gv_w)
    assert kvr == LANES and gk_w == LANES
    Tp, Ts = Bp * Sp, Bs * Ss
    T = Tp + Ts
    past = page_table.shape[1] * cache_ckv.shape[2]

    cos_t, sin_t = _rope_tables(np.concatenate([np.tile(np.arange(Sp), Bp), np.tile(past + np.arange(Ss), Bs)]))
    cache_kpe_t = jnp.swapaxes(cache_kpe, 2, 3)
    n_tiles = (TOP_K * T + N_EXPERTS * (MOE_TILE - 1)) // MOE_TILE + 1

    x = (x_prompt.reshape(Tp, D), x_sample.reshape(Ts, D))
    outs = {k: [] for k in ('ckv_p', 'kpe_p', 'ckv_s', 'kpe_s', 'conv_p', 'conv_s', 'lru_p', 'lru_s', 'gla_p', 'gla_s')}
    for l in range(depth):
        lw_ = _layer_weights(l, p, dims)
        zl, qcat, kcat, kcat_t, ckv_n, kpe_n, g = _pre_call(x, lw_, cos_t, sin_t, dims)

        a_p, conv_p, lru_p = _lru_call(zl, 0, Bp, Sp, jnp.zeros((Bp, CONV_WIDTH - 1, lw), F32),
                                       jnp.zeros((Bp, lw), F32), lw_, BF16)
        a_s, lru_s = _lru_seg_call(zl, Tp, Bs, Ss, state_conv[l], state_lru[l], lw_)
        conv_s = zl[Tp:, :lw].reshape(Bs, Ss, lw)[:, Ss - (CONV_WIDTH - 1):]

        c_p, gla_p = _gla_call(g, 0, Bp, Sp, jnp.zeros((Bp, gv_w, gk_w), F32), lw_, BF16, gk_w, gv_w)
        c_s, gla_s = _gla_call(g, Tp, Bs, Ss, _gla_state_to_rows(state_gla[l]), lw_, BF16, gk_w, gv_w)

        b_p = _attn_prompt_call(qcat, kcat, kcat_t, Bp, Sp, lw_, kvr)
        b_s = _attn_sample_call(page_table, qcat[Tp:].reshape(Bs, Ss, -1), ckv_n[Tp:].reshape(Bs, Ss, kvr),
                                kpe_n[Tp:].reshape(Bs, Ss, QK_ROPE), cache_ckv, cache_kpe_t, l, lw_, kvr)

        a = jnp.concatenate([a_p, a_s], axis=0)
        b = jnp.concatenate([b_p, b_s.reshape(Ts, -1).astype(BF16)], axis=0)
        c = jnp.concatenate([c_p, c_s], axis=0)
        x1, hn, route, cnt = _post_call(a, b, c, x, lw_)

        row_tok, row_w, pos, tile_e, n_used, tile_cnt = _route_meta(route, cnt[0, :N_EXPERTS], MOE_TILE, n_tiles)
        ys = _moe_call(hn, row_tok, row_w, tile_e, n_used, tile_cnt, w_gate, w_up, w_down, l)
        x = x1 + ys[pos[:, 0]] + ys[pos[:, 1]]

        outs['ckv_p'].append(ckv_n[:Tp].reshape(Bp, Sp, kvr))
        outs['kpe_p'].append(kpe_n[:Tp].reshape(Bp, Sp, QK_ROPE))
        outs['ckv_s'].append(ckv_n[Tp:].reshape(Bs, Ss, kvr))
        outs['kpe_s'].append(kpe_n[Tp:].reshape(Bs, Ss, QK_ROPE))
        outs['conv_p'].append(conv_p)
        outs['conv_s'].append(conv_s)
        outs['lru_p'].append(lru_p.reshape(Bp, lw))
        outs['lru_s'].append(lru_s.reshape(Bs, lw))
        outs['gla_p'].append(_gla_rows_to_state(gla_p))
        outs['gla_s'].append(_gla_rows_to_state(gla_s))

    g_fin = final_norm_g.reshape(1, D)
    y_p, y_s = _final_call(x, g_fin, 0, Tp), _final_call(x, g_fin, Tp, Ts)
    st = {k: jnp.stack(v) for k, v in outs.items()}
    return (y_p.reshape(Bp, Sp, D), y_s.reshape(Bs, Ss, D),
            st['ckv_p'], st['kpe_p'], st['ckv_s'], st['kpe_s'], st['conv_p'], st['conv_s'],
            st['lru_p'], st['lru_s'], st['gla_p'], st['gla_s'])
```

```python
import functools
import math

import numpy as np
import jax
import jax.numpy as jnp
from jax import lax
from jax.experimental import pallas as pl
from jax.experimental.pallas import tpu as pltpu

F32 = jnp.float32
BF16 = jnp.bfloat16

LRU_BLOCKS = 4
CONV_WIDTH = 4
LRU_C = 8.0
MLA_HEADS = 8
QK_NOPE = 64
QK_ROPE = 32
V_HEAD = 64
ROPE_THETA = 10000.0
GLA_HEADS = 4
GLA_DK = 32
GLA_DV = 64
GLA_TAU = 16.0
GLA_CHUNK = 16
N_GROUPS = 4
EXPERTS_PER_GROUP = 8
N_EXPERTS = N_GROUPS * EXPERTS_PER_GROUP
TOP_K = 2
EPS = 1e-6

LANES = 128
SUBLANES = 8
BF16_ROWS = 16
VMEM_LIMIT_BYTES = 56 * 1024 * 1024

TOKEN_TILE = 512
SEQ_TILE = 256
MOE_TILE = 256
MOE_ROW_GROUP = 8
QK_LOOKAHEAD = 3
SCORE_PAGES = 4
VALUE_PAGES = 2
NEG_BIG = -1e30


def _cparams(*sem):
    return pltpu.CompilerParams(dimension_semantics=sem, vmem_limit_bytes=VMEM_LIMIT_BYTES)


def _rms(x, g):
    return x * lax.rsqrt(jnp.mean(x * x, axis=-1, keepdims=True) + EPS) * g


def _dot(a, b):
    return jnp.dot(a, b, preferred_element_type=F32)


def _dot_nt(a, b):
    return lax.dot_general(a, b, (((1,), (1,)), ((), ())), preferred_element_type=F32)


def _softplus(x):
    return jnp.maximum(x, 0.0) + jnp.log1p(jnp.exp(-jnp.abs(x)))


def _token_tile(refs, n_first):
    if n_first is None:
        return refs[0][...]
    return jnp.where(pl.program_id(0) < n_first, refs[0][...], refs[1][...])


def _token_specs(x, tm):
    if not isinstance(x, tuple):
        return [x], [pl.BlockSpec((tm, x.shape[1]), lambda i: (i, 0))], None, x.shape[0]
    a, b = x
    assert a.shape[0] % tm == 0 and b.shape[0] % tm == 0
    n_first = a.shape[0] // tm
    specs = [pl.BlockSpec((tm, a.shape[1]), lambda i: (jnp.minimum(i, n_first - 1), 0)),
             pl.BlockSpec((tm, b.shape[1]), lambda i: (jnp.maximum(i - n_first, 0), 0))]
    return [a, b], specs, n_first, a.shape[0] + b.shape[0]


def _shift_rows(x, d, fill):
    row = lax.broadcasted_iota(jnp.int32, x.shape, 0)
    return jnp.where(row >= d, pltpu.roll(x, d, 0), fill)


def _pre_kernel(*refs, dims, n_first):
    n_x = 1 if n_first is None else 2
    (g1_ref, win_ref, qg_ref, wq_ref, wabs_ref, kvg_ref, wa2_ref, ba_ref, cos_ref, sin_ref,
     zl_ref, q_ref, k_ref, kt_ref, ckv_ref, kpe_ref, g_ref) = refs[n_x:]
    lw, qr, kvr, gk_w, gv_w = dims
    xn = _rms(_token_tile(refs[:n_x], n_first), g1_ref[...])
    z = _dot(xn.astype(BF16), win_ref[...])
    o = 2 * lw
    zl_ref[...] = z[:, :o]
    cq = z[:, o:o + qr]
    o += qr
    ckv = z[:, o:o + kvr]
    o += kvr
    gqk = z[:, o:o + 2 * gk_w]
    o += 2 * gk_w
    gvo = z[:, o:o + 2 * gv_w]
    o += 2 * gv_w
    tail = z[:, o:o + LANES]
    cos = cos_ref[...]
    sin = sin_ref[...]
    scale = (QK_NOPE + QK_ROPE) ** -0.5 * math.log2(math.e)

    ckv_n = _rms(ckv, kvg_ref[...])
    kpe = tail * cos + pltpu.roll(tail, LANES - QK_ROPE, 1) * sin
    ckv_ref[...] = ckv_n
    kpe_ref[...] = kpe[:, :QK_ROPE]
    k_ref[:, :kvr] = ckv_n.astype(BF16)
    k_ref[:, kvr:] = kpe.astype(BF16)
    ones = jnp.ones((kt_ref.shape[0] - kvr, ckv_n.shape[0]), F32)
    kt_ref[...] = jnp.concatenate([ckv_n.T, ones], axis=0).astype(BF16)

    cqn = _rms(cq, qg_ref[...]).astype(BF16)
    qall = _dot(cqn, wq_ref[...])
    n_nope = MLA_HEADS * QK_NOPE
    q_abs = _dot(qall[:, :n_nope].astype(BF16), wabs_ref[...]) * scale
    for h in range(MLA_HEADS):
        pe = qall[:, n_nope + h * LANES:n_nope + (h + 1) * LANES]
        sw = qall[:, n_nope + (MLA_HEADS + h) * LANES:n_nope + (MLA_HEADS + h + 1) * LANES]
        q_ref[:, 2 * h * LANES:(2 * h + 1) * LANES] = q_abs[:, h * kvr:(h + 1) * kvr].astype(BF16)
        q_ref[:, (2 * h + 1) * LANES:(2 * h + 2) * LANES] = ((pe * cos + sw * sin) * scale).astype(BF16)

    la_pre = _dot(tail.astype(BF16), wa2_ref[...]) + ba_ref[...]
    la = -_softplus(-la_pre) * (1.0 / GLA_TAU)
    g_ref[:, :gk_w] = gqk[:, :gk_w] * (GLA_DK ** -0.5)
    g_ref[:, gk_w:2 * gk_w] = gqk[:, gk_w:]
    g_ref[:, 2 * gk_w:3 * gk_w] = la
    g_ref[:, 3 * gk_w:] = gvo


def _pre_call(x, lw_, cos_t, sin_t, dims):
    lw, qr, kvr, gk_w, gv_w = dims
    tm = TOKEN_TILE
    xs, x_specs, n_first, T = _token_specs(x, tm)
    assert T % tm == 0
    row = lambda i: (i, 0)
    full = lambda i: (0, 0)
    wspec = lambda a: pl.BlockSpec(a.shape, full)
    ws = [lw_['g1'], lw_['w_in'], lw_['q_g'], lw_['w_q'], lw_['w_abs'], lw_['kv_g'], lw_['wa2'], lw_['ba']]
    ins = xs + ws
    in_specs = x_specs + [wspec(a) for a in ws]
    in_specs += [pl.BlockSpec((tm, LANES), row), pl.BlockSpec((tm, LANES), row)]
    out_shape = (
        jax.ShapeDtypeStruct((T, 2 * lw), F32),
        jax.ShapeDtypeStruct((T, 2 * LANES * MLA_HEADS), BF16),
        jax.ShapeDtypeStruct((T, 2 * LANES), BF16),
        jax.ShapeDtypeStruct((T // tm, kvr + BF16_ROWS, tm), BF16),
        jax.ShapeDtypeStruct((T, kvr), F32),
        jax.ShapeDtypeStruct((T, QK_ROPE), F32),
        jax.ShapeDtypeStruct((T, 3 * gk_w + 2 * gv_w), F32),
    )
    out_specs = tuple(pl.BlockSpec((None, s.shape[1], tm), lambda i: (i, 0, 0)) if len(s.shape) == 3
                      else pl.BlockSpec((tm, s.shape[1]), row) for s in out_shape)
    return pl.pallas_call(
        functools.partial(_pre_kernel, dims=dims, n_first=n_first),
        grid=(T // tm,), in_specs=in_specs, out_specs=out_specs, out_shape=out_shape,
        compiler_params=_cparams("parallel"), name="pre_proj",
    )(*ins, cos_t, sin_t)


def _lru_kernel(zl_ref, cbuf_ref, h0_ref, cw_ref, cb_ref, wa_ref, ba_ref, wi_ref, bi_ref, lam_ref, na_ref,
                a_ref, conv_ref, hout_ref, xbuf, hcar, *, ts, lw):
    i = pl.program_id(1)
    last = pl.num_programs(1) - 1
    pad = SUBLANES
    nbuf = CONV_WIDTH - 1

    @pl.when(i == 0)
    def _():
        xbuf[0:pad, :] = jnp.zeros((pad, lw), F32)
        xbuf[pad - nbuf:pad, :] = cbuf_ref[...]
        hcar[...] = h0_ref[...]

    x = zl_ref[:, :lw]
    y = zl_ref[:, lw:]
    xbuf[pad:pad + ts, :] = x
    xc = cb_ref[...] + cw_ref[nbuf:nbuf + 1, :] * x
    for k in range(nbuf):
        xc = xc + cw_ref[k:k + 1, :] * xbuf[pad - nbuf + k:pad - nbuf + k + ts, :]

    @pl.when(i == last)
    def _():
        conv_ref[...] = xbuf[pad + ts - nbuf:pad + ts, :]

    xbuf[0:pad, :] = xbuf[ts:ts + pad, :]

    xb = xc.astype(BF16)
    r = jax.nn.sigmoid(_dot(xb, wa_ref[...]) + ba_ref[...])
    gi = jax.nn.sigmoid(_dot(xb, wi_ref[...]) + bi_ref[...])
    log_a = (-LRU_C) * r * _softplus(-lam_ref[...])
    a = jnp.exp(log_a)
    th = jnp.tanh(log_a)
    u = jnp.sqrt(-2.0 * th / (1.0 - th)) * (gi * xc)

    d = 1
    while d < ts:
        u = a * _shift_rows(u, d, 0.0) + u
        a = a * _shift_rows(a, d, 1.0)
        d *= 2
    h = a * hcar[...] + u
    hcar[...] = h[ts - 1:ts, :]

    @pl.when(i == last)
    def _():
        hout_ref[...] = h[ts - 1:ts, :]

    out_a = h * jax.nn.gelu(y)
    a_ref[...] = _rms(out_a, na_ref[...]).astype(a_ref.dtype)


def _lru_call(zl, row_off, B, S, cbuf, h0, lw_, out_dtype):
    lw = h0.shape[-1]
    ts = min(S, SEQ_TILE)
    assert S % ts == 0 and row_off % ts == 0 and S >= CONV_WIDTH - 1
    n = S // ts
    off = row_off // ts
    full = lambda b, i: (0, 0)
    ws = [lw_['conv_w'], lw_['conv_b'], lw_['lru_wa'], lw_['lru_ba'], lw_['lru_wi'], lw_['lru_bi'],
          lw_['lru_lam'], lw_['n_a']]
    in_specs = [
        pl.BlockSpec((ts, 2 * lw), lambda b, i: (off + b * n + i, 0)),
        pl.BlockSpec((None, CONV_WIDTH - 1, lw), lambda b, i: (b, 0, 0)),
        pl.BlockSpec((None, 1, lw), lambda b, i: (b, 0, 0)),
    ] + [pl.BlockSpec(a.shape, full) for a in ws]
    out_shape = (
        jax.ShapeDtypeStruct((B * S, lw), out_dtype),
        jax.ShapeDtypeStruct((B, CONV_WIDTH - 1, lw), F32),
        jax.ShapeDtypeStruct((B, 1, lw), F32),
    )
    out_specs = (
        pl.BlockSpec((ts, lw), lambda b, i: (b * n + i, 0)),
        pl.BlockSpec((None, CONV_WIDTH - 1, lw), lambda b, i: (b, 0, 0)),
        pl.BlockSpec((None, 1, lw), lambda b, i: (b, 0, 0)),
    )
    return pl.pallas_call(
        functools.partial(_lru_kernel, ts=ts, lw=lw),
        grid=(B, n), in_specs=in_specs, out_specs=out_specs, out_shape=out_shape,
        scratch_shapes=[pltpu.VMEM((ts + SUBLANES, lw), F32), pltpu.VMEM((1, lw), F32)],
        compiler_params=_cparams("parallel", "arbitrary"), name="rg_lru",
    )(zl, cbuf, h0.reshape(B, 1, lw), *ws)


def _lru_seg_kernel(zl_ref, hist_ref, h0_ref, cw_ref, cb_ref, wa_ref, ba_ref, wi_ref, bi_ref, lam_ref, na_ref,
                    a_ref, h_ref, *, seg, lw):
    nbuf = CONV_WIDTH - 1
    x = zl_ref[:, :lw]
    y = zl_ref[:, lw:]
    n = x.shape[0]
    pos = lax.broadcasted_iota(jnp.int32, (n, 1), 0) & (seg - 1)
    hist = hist_ref[...]
    xc = cb_ref[...] + cw_ref[nbuf:nbuf + 1, :] * x
    for k in range(nbuf):
        j = nbuf - k
        prev = jnp.where(pos >= j, pltpu.roll(x, j, 0), pltpu.roll(hist, (j - seg) % n, 0))
        xc = xc + cw_ref[k:k + 1, :] * prev
    xb = xc.astype(BF16)
    r = jax.nn.sigmoid(_dot(xb, wa_ref[...]) + ba_ref[...])
    gi = jax.nn.sigmoid(_dot(xb, wi_ref[...]) + bi_ref[...])
    log_a = (-LRU_C) * r * _softplus(-lam_ref[...])
    a = jnp.exp(log_a)
    th = jnp.tanh(log_a)
    u = jnp.sqrt(-2.0 * th / (1.0 - th)) * (gi * xc)
    d = 1
    while d < seg:
        u = a * jnp.where(pos >= d, pltpu.roll(u, d, 0), 0.0) + u
        a = a * jnp.where(pos >= d, pltpu.roll(a, d, 0), 1.0)
        d *= 2
    h = a * h0_ref[...] + u
    h_ref[...] = h
    a_ref[...] = _rms(h * jax.nn.gelu(y), na_ref[...]).astype(a_ref.dtype)


def _lru_seg_call(zl, row_off, B, S, cbuf, h0, lw_):
    lw = h0.shape[-1]
    n = B * S
    tm = min(n, TOKEN_TILE)
    assert S & (S - 1) == 0 and S >= CONV_WIDTH - 1 and tm % S == 0 and n % tm == 0 and row_off % tm == 0
    off = row_off // tm
    hist = jnp.pad(cbuf, ((0, 0), (S - (CONV_WIDTH - 1), 0), (0, 0))).reshape(n, lw)
    h0_rows = jnp.repeat(h0, S, axis=0)
    full = lambda i: (0, 0)
    row = lambda i: (i, 0)
    ws = [lw_['conv_w'], lw_['conv_b'], lw_['lru_wa'], lw_['lru_ba'], lw_['lru_wi'], lw_['lru_bi'],
          lw_['lru_lam'], lw_['n_a']]
    a, h = pl.pallas_call(
        functools.partial(_lru_seg_kernel, seg=S, lw=lw),
        grid=(n // tm,),
        in_specs=[pl.BlockSpec((tm, 2 * lw), lambda i: (off + i, 0)), pl.BlockSpec((tm, lw), row),
                  pl.BlockSpec((tm, lw), row)] + [pl.BlockSpec(w.shape, full) for w in ws],
        out_specs=(pl.BlockSpec((tm, lw), row), pl.BlockSpec((tm, lw), row)),
        out_shape=(jax.ShapeDtypeStruct((n, lw), BF16), jax.ShapeDtypeStruct((n, lw), F32)),
        compiler_params=_cparams("parallel"), name="rg_lru_seg",
    )(zl, hist, h0_rows, *ws)
    return a, h.reshape(B, S, lw)[:, S - 1]


def _gla_kernel(g_ref, s0_ref, gn_ref, nc_ref, c_ref, sout_ref, st, *, ts, nseq, gk_w, gv_w):
    i = pl.program_id(1)
    last = pl.num_programs(1) - 1
    C = GLA_CHUNK
    rows = min(ts, C)
    n_chunks = max(ts // C, 1)
    log2c = int(math.log2(C))

    @pl.when(i == 0)
    def _():
        st[...] = s0_ref[...]

    dk_sh, dv_sh = int(math.log2(GLA_DK)), int(math.log2(GLA_DV))
    hd = lax.broadcasted_iota(jnp.int32, (gk_w, gv_w), 0) >> dk_sh
    he = lax.broadcasted_iota(jnp.int32, (gk_w, gv_w), 1) >> dv_sh
    same = (hd == he).astype(BF16)
    he_t = lax.broadcasted_iota(jnp.int32, (gv_w, gk_w), 0) >> dv_sh
    hd_t = lax.broadcasted_iota(jnp.int32, (gv_w, gk_w), 1) >> dk_sh
    same_t = (he_t == hd_t).astype(F32)
    sel_t = lax.broadcasted_iota(jnp.int32, (C, C * C), 0)
    sel_r = lax.broadcasted_iota(jnp.int32, (C, C * C), 1) >> log2c
    sel = (sel_t == sel_r).astype(BF16)
    srow = lax.broadcasted_iota(jnp.int32, (C, 1), 0)

    seq_out = []
    for sq in range(nseq):
        blk = g_ref[sq * ts:(sq + 1) * ts, :]
        if rows < C:
            blk = jnp.concatenate([blk, jnp.zeros((C - rows, blk.shape[1]), F32)], axis=0)
        q = blk[:, :gk_w]
        k = blk[:, gk_w:2 * gk_w]
        la = blk[:, 2 * gk_w:3 * gk_w]
        v = blk[:, 3 * gk_w:3 * gk_w + gv_w]
        nrow = n_chunks * C
        pos = lax.broadcasted_iota(jnp.int32, (nrow, 1), 0) & (C - 1)
        cum = la
        d = 1
        while d < C:
            cum = cum + jnp.where(pos >= d, pltpu.roll(cum, d, 0), 0.0)
            d *= 2
        tot = jnp.where(pos == C - 1, cum, 0.0)
        d = 1
        while d < C:
            tot = tot + jnp.where(pos < C - d, pltpu.roll(tot, nrow - d, 0), 0.0)
            d *= 2
        qe = (q * jnp.exp(cum)).astype(BF16)
        kdec = (k * jnp.exp(tot - cum)).astype(BF16)
        dec = jnp.exp(tot)
        vb = v.astype(BF16)

        o_intra, upd = [], []
        for c in range(n_chunks):
            sl = slice(c * C, (c + 1) * C)
            cum_c, q_c, k_c, v_c = cum[sl], q[sl], k[sl], v[sl]
            pieces = []
            for t in range(C):
                diff = jnp.where(srow <= t, cum_c[t:t + 1, :] - cum_c, NEG_BIG)
                pieces.append(q_c[t:t + 1, :] * k_c * jnp.exp(diff))
            w = jnp.concatenate(pieces, axis=0)
            att = _dot(w.astype(BF16), same)
            xv = att * jnp.concatenate([v_c] * C, axis=0)
            o_intra.append(_dot(sel, xv.astype(BF16)))
            upd.append(lax.dot_general(vb[sl], kdec[sl], (((0,), (0,)), ((), ())),
                                       preferred_element_type=F32) * same_t)

        s_t = st[sq]
        outs = []
        for c in range(n_chunks):
            sl = slice(c * C, (c + 1) * C)
            outs.append(o_intra[c] + _dot_nt(qe[sl], s_t.astype(BF16)))
            s_t = s_t * dec[c * C:c * C + 1, :] + upd[c]
        st[sq] = s_t
        seq_out.append(jnp.concatenate(outs, axis=0)[:ts] if n_chunks > 1 else outs[0][:ts])

    o = jnp.concatenate(seq_out, axis=0) if nseq > 1 else seq_out[0]

    @pl.when(i == last)
    def _():
        sout_ref[...] = st[...]

    go = g_ref[:, 3 * gk_w + gv_w:]
    e64 = (lax.broadcasted_iota(jnp.int32, (gv_w, gv_w), 0) >> dv_sh
           == lax.broadcasted_iota(jnp.int32, (gv_w, gv_w), 1) >> dv_sh).astype(BF16)
    osq = o * o
    osq_hi = osq.astype(BF16)
    osq_lo = (osq - osq_hi.astype(F32)).astype(BF16)
    ms = (_dot(osq_hi, e64) + _dot(osq_lo, e64)) * (1.0 / GLA_DV)
    out_c = o * lax.rsqrt(ms + EPS) * gn_ref[...] * (go * jax.nn.sigmoid(go))
    c_ref[...] = _rms(out_c, nc_ref[...]).astype(c_ref.dtype)


def _gla_call(g, row_off, B, S, s0t, lw_, out_dtype, gk_w, gv_w):
    ts = min(S, SEQ_TILE)
    nseq = math.gcd(B, max(SEQ_TILE // (2 * ts), 1)) if ts == S else 1
    rows = nseq * ts
    assert S % ts == 0 and row_off % rows == 0 and (ts % GLA_CHUNK == 0 or ts < GLA_CHUNK)
    n = S // ts
    off = row_off // rows
    gw = g.shape[1]
    full = lambda b, i: (0, 0)
    out_shape = (jax.ShapeDtypeStruct((B * S, gv_w), out_dtype),
                 jax.ShapeDtypeStruct((B, gv_w, gk_w), F32))
    return pl.pallas_call(
        functools.partial(_gla_kernel, ts=ts, nseq=nseq, gk_w=gk_w, gv_w=gv_w),
        grid=(B // nseq, n),
        in_specs=[pl.BlockSpec((rows, gw), lambda b, i: (off + b * n + i, 0)),
                  pl.BlockSpec((nseq, gv_w, gk_w), lambda b, i: (b, 0, 0)),
                  pl.BlockSpec((1, gv_w), full), pl.BlockSpec((1, gv_w), full)],
        out_specs=(pl.BlockSpec((rows, gv_w), lambda b, i: (b * n + i, 0)),
                   pl.BlockSpec((nseq, gv_w, gk_w), lambda b, i: (b, 0, 0))),
        out_shape=out_shape,
        scratch_shapes=[pltpu.VMEM((nseq, gv_w, gk_w), F32)],
        compiler_params=_cparams("parallel", "arbitrary"), name="gla",
    )(g, s0t, lw_['gla_g'], lw_['n_c'])


def _tree(op, xs):
    xs = list(xs)
    while len(xs) > 1:
        xs = [op(xs[i], xs[i + 1]) if i + 1 < len(xs) else xs[i] for i in range(0, len(xs), 2)]
    return xs[0]


def _attn_prompt_kernel(q_ref, k_ref, kt_ref, wuvt_ref, nb_ref, o_ref, m_scr, acc_scr, *, tq, kvr):
    i = pl.program_id(1)
    kw = 2 * LANES
    m_scr[...] = jnp.full(m_scr.shape, NEG_BIG, F32)
    acc_scr[...] = jnp.zeros(acc_scr.shape, F32)

    def block(j, masked):
        kb = k_ref[pl.ds(pl.multiple_of(j * tq, tq), tq), :]
        kbt = kt_ref[j]
        if masked:
            key = lax.broadcasted_iota(jnp.int32, (tq, tq), 0)
            qry = lax.broadcasted_iota(jnp.int32, (tq, tq), 1)
            keep = key <= qry
        qk = lambda h: _dot_nt(kb, q_ref[:, h * kw:(h + 1) * kw])
        ahead = [qk(h) for h in range(QK_LOOKAHEAD)]
        for h in range(MLA_HEADS):
            st = ahead.pop(0)
            if h + QK_LOOKAHEAD < MLA_HEADS:
                ahead.append(qk(h + QK_LOOKAHEAD))
            if masked:
                st = jnp.where(keep, st, NEG_BIG)
            m_old = m_scr[h]
            m_new = jnp.maximum(m_old, jnp.max(st, axis=0, keepdims=True))
            alpha = jnp.exp2(m_old - m_new)
            pt = jnp.exp2(st - m_new).astype(BF16)
            acc_scr[h] = alpha * acc_scr[h] + _dot(kbt, pt)
            m_scr[h] = m_new

    def body(j, carry):
        block(j, False)
        return carry

    lax.fori_loop(0, i, body, 0)
    block(i, True)

    out_t = jnp.zeros((wuvt_ref.shape[0], tq), F32)
    for h in range(MLA_HEADS):
        acc = acc_scr[h]
        o_t = (acc[:kvr, :] / acc[kvr:kvr + 1, :]).astype(BF16)
        out_t = out_t + _dot(wuvt_ref[:, h * kvr:(h + 1) * kvr], o_t)
    ms = jnp.mean(out_t * out_t, axis=0, keepdims=True)
    out_t = out_t * lax.rsqrt(ms + EPS) * nb_ref[...]
    o_ref[...] = out_t.T.astype(o_ref.dtype)


def _attn_prompt_call(qcat, kcat, kcat_t, B, S, lw_, kvr):
    tq = kcat_t.shape[2]
    assert S % tq == 0
    n = S // tq
    mw = lw_['w_uv_rows_t'].shape[0]
    full = lambda b, i: (0, 0)
    return pl.pallas_call(
        functools.partial(_attn_prompt_kernel, tq=tq, kvr=kvr),
        grid=(B, n),
        in_specs=[pl.BlockSpec((tq, qcat.shape[1]), lambda b, i: (b * n + i, 0)),
                  pl.BlockSpec((S, kcat.shape[1]), lambda b, i: (b, 0)),
                  pl.BlockSpec((n, kcat_t.shape[1], tq), lambda b, i: (b, 0, 0)),
                  pl.BlockSpec(lw_['w_uv_rows_t'].shape, full), pl.BlockSpec((mw, 1), full)],
        out_specs=pl.BlockSpec((tq, mw), lambda b, i: (b * n + i, 0)),
        out_shape=jax.ShapeDtypeStruct((B * S, mw), BF16),
        scratch_shapes=[pltpu.VMEM((MLA_HEADS, 1, tq), F32),
                        pltpu.VMEM((MLA_HEADS, kcat_t.shape[1], tq), F32)],
        compiler_params=_cparams("parallel", "arbitrary"), name="attn_prompt",
    )(qcat, kcat, kcat_t, lw_['w_uv_rows_t'], lw_['n_b'].reshape(mw, 1))


def _attn_sample_kernel(pt_ref, q_ref, nckv_ref, nkpe_ref, ckv_hbm, kpe_hbm, wuv_ref, nb_ref, o_ref,
                        q_scr, s_scr, v_scr, ckv_buf, kpe_buf, sem, *, layer, n_pages, sq, kvr, page):
    b = pl.program_id(0)
    nrow = MLA_HEADS * sq

    def page_copies(bb):
        slot = lax.rem(bb, 2)
        out = []
        for j in range(n_pages):
            pg = pt_ref[bb * n_pages + j]
            out.append(pltpu.make_async_copy(ckv_hbm.at[layer, pg], ckv_buf.at[slot, j], sem.at[0, slot]))
            out.append(pltpu.make_async_copy(kpe_hbm.at[layer, pg], kpe_buf.at[slot, j], sem.at[1, slot]))
        return out

    @pl.when(b == 0)
    def _():
        for c in page_copies(b):
            c.start()

    @pl.when(b + 1 < pl.num_programs(0))
    def _():
        for c in page_copies(b + 1):
            c.start()

    for h in range(MLA_HEADS):
        q_scr[h * sq:(h + 1) * sq, :] = q_ref[:, 2 * h * LANES:(2 * h + 2) * LANES].astype(F32)
    qa = q_scr[:, :kvr].astype(BF16)
    qp = q_scr[:, kvr:kvr + QK_ROPE].astype(BF16)

    for c in page_copies(b):
        c.wait()
    slot = lax.rem(b, 2)
    gp = SCORE_PAGES
    for j in range(0, n_pages, gp):
        ck = ckv_buf[slot, j:j + gp].reshape(gp * page, kvr).astype(BF16)
        kp = jnp.concatenate([kpe_buf[slot, j + t] for t in range(gp)], axis=1).astype(BF16)
        v_scr[j * page:(j + gp) * page, :] = ck
        s_scr[:, j * page:(j + gp) * page] = _dot_nt(qa, ck) + _dot(qp, kp)

    zpad = lambda a: jnp.concatenate([a, jnp.zeros((page - sq, a.shape[1]), F32)], axis=0)
    ck_new = zpad(nckv_ref[...]).astype(BF16)
    kp_new = zpad(nkpe_ref[...]).astype(BF16)
    tok = lax.broadcasted_iota(jnp.int32, (nrow, page), 0) & (sq - 1)
    key = lax.broadcasted_iota(jnp.int32, (nrow, page), 1)
    s_new = jnp.where(key <= tok, _dot_nt(qa, ck_new) + _dot_nt(qp, kp_new), NEG_BIG)
    tile = lambda i: s_scr[:, i * page:(i + 1) * page]
    m = jnp.max(_tree(jnp.maximum, [tile(i) for i in range(n_pages)] + [s_new]), axis=-1, keepdims=True)
    p_new = jnp.exp2(s_new - m)
    acc = _dot(p_new.astype(BF16), ck_new)
    psum = p_new
    vp = VALUE_PAGES
    for i in range(0, n_pages, vp):
        p = jnp.exp2(s_scr[:, i * page:(i + vp) * page] - m)
        psum = psum + _tree(jnp.add, [p[:, t * page:(t + 1) * page] for t in range(vp)])
        acc = acc + _dot(p.astype(BF16), v_scr[i * page:(i + vp) * page, :])
    l = jnp.sum(psum, axis=-1, keepdims=True)
    o_lat = (acc / l).astype(BF16)
    mw = wuv_ref.shape[1]
    r = _dot(o_lat, wuv_ref[...])
    rh = lax.broadcasted_iota(jnp.int32, (nrow, mw), 0) >> int(math.log2(sq))
    ch = lax.broadcasted_iota(jnp.int32, (nrow, mw), 1) >> int(math.log2(V_HEAD))
    r = jnp.where(rh == ch, r, 0.0)
    out = r[0:sq, :]
    for h in range(1, MLA_HEADS):
        out = out + r[h * sq:(h + 1) * sq, :]
    o_ref[...] = _rms(out, nb_ref[...])


def _attn_sample_call(page_table, qs, nckv, nkpe, cache_ckv, cache_kpe_t, layer, lw_, kvr):
    B, sq, qw = qs.shape
    n_pages = page_table.shape[1]
    page = cache_ckv.shape[2]
    mw = lw_['w_uv_cat'].shape[1]
    nrow = MLA_HEADS * sq
    assert sq == SUBLANES and sq <= page and n_pages % SCORE_PAGES == 0 and n_pages % VALUE_PAGES == 0
    hbm = pl.BlockSpec(memory_space=pl.ANY)
    in_specs = [pl.BlockSpec((None, sq, qw), lambda b, pt: (b, 0, 0)),
                pl.BlockSpec((None, sq, kvr), lambda b, pt: (b, 0, 0)),
                pl.BlockSpec((None, sq, QK_ROPE), lambda b, pt: (b, 0, 0)),
                hbm, hbm,
                pl.BlockSpec(lw_['w_uv_cat'].shape, lambda b, pt: (0, 0)),
                pl.BlockSpec((1, mw), lambda b, pt: (0, 0))]
    grid_spec = pltpu.PrefetchScalarGridSpec(
        num_scalar_prefetch=1, grid=(B,), in_specs=in_specs,
        out_specs=pl.BlockSpec((None, sq, mw), lambda b, pt: (b, 0, 0)),
        scratch_shapes=[pltpu.VMEM((nrow, 2 * LANES), F32), pltpu.VMEM((nrow, n_pages * page), F32),
                        pltpu.VMEM((n_pages * page, kvr), BF16),
                        pltpu.VMEM((2, n_pages, page, kvr), F32), pltpu.VMEM((2, n_pages, QK_ROPE, page), F32),
                        pltpu.SemaphoreType.DMA((2, 2))])
    return pl.pallas_call(
        functools.partial(_attn_sample_kernel, layer=layer, n_pages=n_pages, sq=sq, kvr=kvr, page=page),
        grid_spec=grid_spec, out_shape=jax.ShapeDtypeStruct((B, sq, mw), F32),
        compiler_params=_cparams("arbitrary"), name="attn_sample",
    )(page_table.reshape(-1), qs, nckv, nkpe, cache_ckv, cache_kpe_t, lw_['w_uv_cat'], lw_['n_b'])


def _post_kernel(a_ref, b_ref, c_ref, *refs, widths, n_first):
    n_x = 1 if n_first is None else 2
    wo_ref, g2_ref, wr_ref, br_ref, x1_ref, hn_ref, rt_ref, cnt_ref, cnt_scr = refs[n_x:]
    wa, wb, wc = widths

    @pl.when(pl.program_id(0) == 0)
    def _():
        cnt_scr[...] = jnp.zeros(cnt_scr.shape, F32)

    mix = _dot(a_ref[...], wo_ref[:wa, :])
    mix = mix + _dot(b_ref[...], wo_ref[wa:wa + wb, :])
    mix = mix + _dot(c_ref[...], wo_ref[wa + wb:, :])
    x1 = _token_tile(refs[:n_x], n_first) + mix
    x1_ref[...] = x1
    hn = _rms(x1, g2_ref[...])
    for sub in range(hn_ref.shape[1]):
        hn_ref[:, sub, :] = hn[:, sub * LANES:(sub + 1) * LANES]
    hn_hi = hn.astype(BF16)
    hn_lo = (hn - hn_hi.astype(F32)).astype(BF16)
    hh = _dot(hn_hi, wr_ref[...])
    logits = (hh[:, :LANES] + hh[:, LANES:] + _dot(hn_lo, wr_ref[:, :LANES])) + br_ref[...]

    lane = lax.broadcasted_iota(jnp.int32, logits.shape, 1)
    lane_f = lane.astype(F32)
    big = float(LANES)
    is_g = lane < N_GROUPS
    gl = jnp.where(is_g, logits, NEG_BIG)
    gmax = jnp.max(gl, axis=-1, keepdims=True)
    gsel = jnp.min(jnp.where(gl == gmax, lane_f, big), axis=-1, keepdims=True)
    gprob = 1.0 / jnp.sum(jnp.where(is_g, jnp.exp(gl - gmax), 0.0), axis=-1, keepdims=True)
    lo = N_GROUPS + gsel * EXPERTS_PER_GROUP
    el = jnp.where((lane_f >= lo) & (lane_f < lo + EXPERTS_PER_GROUP), logits, NEG_BIG)
    v1 = jnp.max(el, axis=-1, keepdims=True)
    i1 = jnp.min(jnp.where(el == v1, lane_f, big), axis=-1, keepdims=True)
    el2 = jnp.where(lane_f == i1, NEG_BIG, el)
    v2 = jnp.max(el2, axis=-1, keepdims=True)
    i2 = jnp.min(jnp.where(el2 == v2, lane_f, big), axis=-1, keepdims=True)
    e21 = jnp.exp(v2 - v1)
    w1 = gprob / (1.0 + e21)
    w2 = w1 * e21
    e1 = i1 - N_GROUPS
    e2 = i2 - N_GROUPS

    oh1 = (lane_f == e1).astype(F32)
    oh2 = (lane_f == e2).astype(F32)
    both = oh1 + oh2
    tm = both.shape[0]
    tri = (lax.broadcasted_iota(jnp.int32, (tm, tm), 0) >= lax.broadcasted_iota(jnp.int32, (tm, tm), 1))
    incl = _dot(tri.astype(BF16), both.astype(BF16))
    base = cnt_scr[0:1, :] + incl - both
    r1 = jnp.sum(oh1 * base, axis=-1, keepdims=True)
    r2 = jnp.sum(oh2 * base, axis=-1, keepdims=True)
    cnt_scr[...] = cnt_scr[...] + incl[tm - 1:tm, :]
    cnt_ref[...] = cnt_scr[...]

    rt = jnp.where(lane == 0, e1, 0.0)
    rt = jnp.where(lane == 1, e2, rt)
    rt = jnp.where(lane == 2, w1, rt)
    rt = jnp.where(lane == 3, w2, rt)
    rt = jnp.where(lane == 4, r1, rt)
    rt = jnp.where(lane == 5, r2, rt)
    rt_ref[...] = rt


def _post_call(a, b, c, x, lw_):
    tm = TOKEN_TILE
    xs, x_specs, n_first, T = _token_specs(x, tm)
    D = xs[0].shape[1]
    row = lambda i: (i, 0)
    full = lambda i: (0, 0)
    widths = (a.shape[1], b.shape[1], c.shape[1])
    ws = [lw_['w_out'], lw_['g2'], lw_['w_r'], lw_['b_r']]
    out_shape = (jax.ShapeDtypeStruct((T, D), F32), jax.ShapeDtypeStruct((T, D // LANES, LANES), F32),
                 jax.ShapeDtypeStruct((T, LANES), F32), jax.ShapeDtypeStruct((SUBLANES, LANES), F32))
    return pl.pallas_call(
        functools.partial(_post_kernel, widths=widths, n_first=n_first),
        grid=(T // tm,),
        in_specs=[pl.BlockSpec((tm, w), row) for w in widths] + x_specs
        + [pl.BlockSpec(w.shape, full) for w in ws],
        out_specs=tuple(pl.BlockSpec((tm,) + s.shape[1:], (lambda i: (i, 0, 0)) if len(s.shape) == 3 else row)
                        for s in out_shape[:3])
        + (pl.BlockSpec((SUBLANES, LANES), full),),
        out_shape=out_shape, scratch_shapes=[pltpu.VMEM((SUBLANES, LANES), F32)],
        compiler_params=_cparams("arbitrary"), name="post_proj",
    )(a, b, c, *xs, *ws)


def _dest_kernel(rt_ref, ps_ref, d_ref):
    rt = rt_ref[...]
    lane = lax.broadcasted_iota(jnp.int32, rt.shape, 1)
    lane_f = lane.astype(F32)
    pick = lambda k: jnp.sum(jnp.where(lane == k, rt, 0.0), axis=-1, keepdims=True)
    ps = ps_ref[...]
    d1 = jnp.sum(jnp.where(lane_f == pick(0), ps, 0.0), axis=-1, keepdims=True) + pick(2 * TOP_K)
    d2 = jnp.sum(jnp.where(lane_f == pick(1), ps, 0.0), axis=-1, keepdims=True) + pick(2 * TOP_K + 1)
    d_ref[...] = jnp.where(lane == 0, d1, jnp.where(lane == 1, d2, 0.0))


def _dest_call(route, pstart_row):
    T = route.shape[0]
    tm = TOKEN_TILE
    return pl.pallas_call(
        _dest_kernel, grid=(T // tm,),
        in_specs=[pl.BlockSpec((tm, LANES), lambda i: (i, 0)), pl.BlockSpec((1, LANES), lambda i: (0, 0))],
        out_specs=pl.BlockSpec((tm, LANES), lambda i: (i, 0)),
        out_shape=jax.ShapeDtypeStruct((T, LANES), F32), compiler_params=_cparams("parallel"), name="moe_dest",
    )(route, pstart_row)


def _moe_kernel(te_ref, nt_ref, cnt_ref, tok_ref, hn_hbm, w_ref, wg_ref, wu_ref, wd_ref, y_ref, xbuf, sem):
    i = pl.program_id(0)
    tm = xbuf.shape[1]
    nt = nt_ref[0]
    n_groups = tm // MOE_ROW_GROUP

    def start_rows(t):
        slot = lax.rem(t, 2)

        def body(g, carry):
            @pl.when(g * MOE_ROW_GROUP < cnt_ref[t])
            def _():
                for k in range(MOE_ROW_GROUP):
                    r = g * MOE_ROW_GROUP + k
                    pltpu.make_async_copy(hn_hbm.at[tok_ref[t * tm + r]], xbuf.at[slot, r], sem.at[slot]).start()
            return carry

        lax.fori_loop(0, n_groups, body, 0)

    @pl.when(i == 0)
    def _():
        xbuf[...] = jnp.zeros(xbuf.shape, F32)
        start_rows(i)

    @pl.when(i + 1 < nt)
    def _():
        start_rows(i + 1)

    @pl.when(i < nt)
    def _():
        slot = lax.rem(i, 2)

        def wait_group(g, carry):
            @pl.when(g * MOE_ROW_GROUP < cnt_ref[i])
            def _():
                grp = pl.ds(0, MOE_ROW_GROUP)
                pltpu.make_async_copy(hn_hbm.at[grp], xbuf.at[slot, grp], sem.at[slot]).wait()
            return carry

        lax.fori_loop(0, n_groups, wait_group, 0)
        x = jnp.concatenate([xbuf[slot, :, sub, :] for sub in range(xbuf.shape[2])], axis=1).astype(BF16)
        hg = _dot(x, wg_ref[...].astype(BF16))
        hu = _dot(x, wu_ref[...].astype(BF16))
        act = hg * jax.nn.sigmoid(hg) * hu * w_ref[...]
        y_ref[...] = _dot(act.astype(BF16), wd_ref[...].astype(BF16))

    @pl.when(i >= nt)
    def _():
        y_ref[...] = jnp.zeros(y_ref.shape, F32)


def _moe_call(hn, row_tok, row_w, tile_e, n_used, tile_cnt, w_gate, w_up, w_down, layer):
    D = w_gate.shape[-2]
    tm = MOE_TILE
    F = w_gate.shape[-1]
    n_tiles = tile_e.shape[0]
    grid_spec = pltpu.PrefetchScalarGridSpec(
        num_scalar_prefetch=4, grid=(n_tiles,),
        in_specs=[pl.BlockSpec(memory_space=pl.ANY),
                  pl.BlockSpec((tm, 1), lambda i, te, nt, cnt, tok: (i, 0)),
                  pl.BlockSpec((None, None, D, F), lambda i, te, nt, cnt, tok: (layer, te[i], 0, 0)),
                  pl.BlockSpec((None, None, D, F), lambda i, te, nt, cnt, tok: (layer, te[i], 0, 0)),
                  pl.BlockSpec((None, None, F, D), lambda i, te, nt, cnt, tok: (layer, te[i], 0, 0))],
        out_specs=pl.BlockSpec((tm, D), lambda i, te, nt, cnt, tok: (i, 0)),
        scratch_shapes=[pltpu.VMEM((2, tm) + hn.shape[1:], F32), pltpu.SemaphoreType.DMA((2,))])
    return pl.pallas_call(
        _moe_kernel, grid_spec=grid_spec, out_shape=jax.ShapeDtypeStruct((n_tiles * tm, D), F32),
        compiler_params=_cparams("arbitrary"), name="moe_experts",
    )(tile_e, n_used, tile_cnt, row_tok, hn, row_w, w_gate, w_up, w_down)


def _route_meta(route, counts_f, tm, n_tiles):
    T = route.shape[0]
    n_assign = TOP_K * T
    e = route[:, :TOP_K].astype(jnp.int32).reshape(-1)
    w = route[:, TOP_K:2 * TOP_K].reshape(-1)
    order = jnp.argsort(e, stable=True).astype(jnp.int32)
    counts = counts_f.astype(jnp.int32)
    zero = jnp.zeros((1,), jnp.int32)
    start = jnp.concatenate([zero, jnp.cumsum(counts)])
    pstart = jnp.concatenate([zero, jnp.cumsum(((counts + tm - 1) // tm) * tm)])
    n_used = (pstart[N_EXPERTS] // tm).reshape(1)
    tile_lo = jnp.arange(n_tiles, dtype=jnp.int32) * tm
    tile_e = jnp.sum((pstart[None, 1:] <= tile_lo[:, None]).astype(jnp.int32), axis=1)
    tile_e = jnp.minimum(tile_e, N_EXPERTS - 1)
    oh = (tile_e[:, None] == jnp.arange(N_EXPERTS, dtype=jnp.int32)[None, :]).astype(jnp.int32)
    pick = lambda tab: jnp.sum(oh * tab[None, :N_EXPERTS], axis=1)
    k0 = tile_lo - pick(pstart)
    tile_cnt = jnp.clip(pick(counts) - k0, 0, tm)
    k = k0[:, None] + jnp.arange(tm, dtype=jnp.int32)[None, :]
    valid = (k < pick(counts)[:, None]).reshape(-1)
    idx = jnp.clip(pick(start)[:, None] + k, 0, n_assign - 1).reshape(-1)
    src = order[idx]
    row_tok = jnp.where(valid, src // TOP_K, 0)
    row_w = jnp.where(valid, w[src], 0.0)
    ps_row = jnp.pad(pstart[:N_EXPERTS].astype(F32), (0, LANES - N_EXPERTS)).reshape(1, LANES)
    pos = _dest_call(route, ps_row)[:, :TOP_K].astype(jnp.int32)
    return row_tok, row_w.reshape(-1, 1), pos, tile_e, n_used, tile_cnt


def _final_kernel(x_ref, g_ref, o_ref):
    o_ref[...] = _rms(x_ref[...], g_ref[...])


def _final_call(x, g, row_off, n_rows):
    D = x.shape[1]
    tm = TOKEN_TILE
    assert row_off % tm == 0 and n_rows % tm == 0
    off = row_off // tm
    return pl.pallas_call(
        _final_kernel, grid=(n_rows // tm,),
        in_specs=[pl.BlockSpec((tm, D), lambda i: (off + i, 0)), pl.BlockSpec((1, D), lambda i: (0, 0))],
        out_specs=pl.BlockSpec((tm, D), lambda i: (i, 0)),
        out_shape=jax.ShapeDtypeStruct((n_rows, D), F32), compiler_params=_cparams("parallel"), name="final_norm",
    )(x, g)


def _block_diag(w):
    n, c, d = w.shape
    return jnp.einsum('ncd,nm->ncmd', w, jnp.eye(n, dtype=w.dtype)).reshape(n * c, n * d)


def _half_swap(w):
    half = w.shape[-1] // 2
    return jnp.concatenate([w[..., half:], w[..., :half]], axis=-1)


def _layer_weights(l, p, dims):
    lw, qr, kvr, gk_w, gv_w = dims
    D = p['w_in'].shape[1]
    row = lambda a: a.reshape(1, -1).astype(F32)
    w_in = p['w_in'][l]
    sizes = (lw, lw, qr, kvr, QK_ROPE, gk_w, gk_w, gv_w, p['gla_wa2'].shape[1], gv_w)
    x_lru, y_lru, c_q, c_kv, k_pe, g_q, g_k, g_v, g_a, g_o = jnp.split(w_in, list(np.cumsum(sizes)[:-1]), axis=1)
    tail_pad = LANES - 2 * QK_ROPE - g_a.shape[1]
    tail = jnp.concatenate([k_pe, _half_swap(k_pe), g_a, jnp.zeros((D, tail_pad), F32)], axis=1)
    w_in_p = jnp.concatenate([x_lru, y_lru, c_q, c_kv, g_q, g_k, g_v, g_o, tail], axis=1).astype(BF16)

    w_uq = p['w_uq'][l].reshape(qr, MLA_HEADS, QK_NOPE + QK_ROPE)
    nope = w_uq[:, :, :QK_NOPE].reshape(qr, MLA_HEADS * QK_NOPE)
    pe = w_uq[:, :, QK_NOPE:]
    widen = lambda a: jnp.pad(a, ((0, 0), (0, 0), (0, LANES - QK_ROPE))).reshape(qr, MLA_HEADS * LANES)
    w_q = jnp.concatenate([nope, widen(pe), widen(_half_swap(pe))], axis=1).astype(BF16)
    w_abs = _block_diag(jnp.transpose(p['w_uk'][l], (1, 2, 0))).astype(BF16)
    wa2 = jnp.zeros((LANES, gk_w), F32).at[2 * QK_ROPE:2 * QK_ROPE + g_a.shape[1]].set(p['gla_wa2'][l]).astype(BF16)

    w_uv = p['w_uv'][l]
    mw = MLA_HEADS * V_HEAD
    w_uv_cat = w_uv.reshape(kvr, mw)
    w_uv_rows = _block_diag(jnp.transpose(w_uv, (1, 0, 2)))
    n_a, n_b, n_c = jnp.split(p['out_norm_g'][l], [lw, lw + mw])
    n_r = N_GROUPS + N_EXPERTS
    w_r = jnp.concatenate([p['router_wg'][l], p['router_we'][l], jnp.zeros((D, LANES - n_r), F32)], axis=1)
    b_r = jnp.concatenate([p['router_bg'][l], p['router_be'][l], jnp.zeros((LANES - n_r,), F32)])
    return dict(
        g1=row(p['norm1_g'][l]), w_in=w_in_p, q_g=row(p['q_norm_g'][l]), w_q=w_q, w_abs=w_abs,
        kv_g=row(p['kv_norm_g'][l]), wa2=wa2, ba=row(p['gla_ba'][l]),
        conv_w=p['conv_w'][l], conv_b=row(p['conv_b'][l]),
        lru_wa=_block_diag(p['lru_wa'][l]).astype(BF16), lru_ba=row(p['lru_ba'][l]),
        lru_wi=_block_diag(p['lru_wi'][l]).astype(BF16), lru_bi=row(p['lru_bi'][l]),
        lru_lam=row(p['lru_lambda'][l]), n_a=row(n_a), n_b=row(n_b), n_c=row(n_c),
        gla_g=row(jnp.tile(p['gla_norm_g'][l], GLA_HEADS)),
        w_uv_cat=w_uv_cat.astype(BF16), w_uv_rows_t=w_uv_rows.T.astype(BF16),
        w_out=p['w_out'][l].astype(BF16), g2=row(p['norm2_g'][l]),
        w_r=jnp.concatenate([w_r.astype(BF16), (w_r - w_r.astype(BF16).astype(F32)).astype(BF16)], axis=1),
        b_r=row(b_r),
    )


def _rope_tables(positions):
    half = QK_ROPE // 2
    inv = ROPE_THETA ** (-np.arange(half, dtype=np.float64) / half)
    ang = np.asarray(positions, np.float64)[:, None] * inv
    zeros = np.zeros((ang.shape[0], LANES - QK_ROPE))
    cos = np.concatenate([np.cos(ang), np.cos(ang), zeros], axis=1)
    sin = np.concatenate([-np.sin(ang), np.sin(ang), zeros], axis=1)
    return jnp.asarray(cos, F32), jnp.asarray(sin, F32)


def _gla_state_to_rows(s):
    B = s.shape[0]
    eye = jnp.eye(GLA_HEADS, dtype=s.dtype)
    return jnp.einsum('bhde,hg->bhegd', s, eye).reshape(B, GLA_HEADS * GLA_DV, GLA_HEADS * GLA_DK)


def _gla_rows_to_state(st):
    B = st.shape[0]
    s5 = st.reshape(B, GLA_HEADS, GLA_DV, GLA_HEADS, GLA_DK)
    blocks = jnp.stack([s5[:, h, :, h, :] for h in range(GLA_HEADS)], axis=1)
    return jnp.swapaxes(blocks, -1, -2)


def kernel(x_prompt, x_sample, cache_ckv, cache_kpe, page_table, state_conv, state_lru, state_gla, norm1_g, w_in, conv_w, conv_b, lru_wa, lru_ba, lru_wi, lru_bi, lru_lambda, q_norm_g, w_uq, kv_norm_g, w_uk, w_uv, gla_wa2, gla_ba, gla_norm_g, out_norm_g, w_out, norm2_g, router_wg, router_bg, router_we, router_be, w_gate, w_up, w_down, final_norm_g):
    p = dict(norm1_g=norm1_g, w_in=w_in, conv_w=conv_w, conv_b=conv_b, lru_wa=lru_wa, lru_ba=lru_ba,
             lru_wi=lru_wi, lru_bi=lru_bi, lru_lambda=lru_lambda, q_norm_g=q_norm_g, w_uq=w_uq,
             kv_norm_g=kv_norm_g, w_uk=w_uk, w_uv=w_uv, gla_wa2=gla_wa2, gla_ba=gla_ba,
             gla_norm_g=gla_norm_g, out_norm_g=out_norm_g, w_out=w_out, norm2_g=norm2_g,
             router_wg=router_wg, router_bg=router_bg, router_we=router_we, router_be=router_be)
    Bp, Sp, D = x_prompt.shape
    Bs, Ss, _ = x_sample.shape
    depth = w_in.shape[0]
    lw = state_lru.shape[-1]
    qr = q_norm_g.shape[-1]
    kvr = kv_norm_g.shape[-1]
    gk_w = GLA_HEADS * GLA_DK
    gv_w = GLA_HEADS * GLA_DV
    dims = (lw, qr, kvr, gk_w, gv_w)
    assert kvr == LANES and gk_w == LANES
    Tp, Ts = Bp * Sp, Bs * Ss
    T = Tp + Ts
    past = page_table.shape[1] * cache_ckv.shape[2]

    cos_t, sin_t = _rope_tables(np.concatenate([np.tile(np.arange(Sp), Bp), np.tile(past + np.arange(Ss), Bs)]))
    cache_kpe_t = jnp.swapaxes(cache_kpe, 2, 3)
    n_tiles = (TOP_K * T + N_EXPERTS * (MOE_TILE - 1)) // MOE_TILE + 1

    x = (x_prompt.reshape(Tp, D), x_sample.reshape(Ts, D))
    outs = {k: [] for k in ('ckv_p', 'kpe_p', 'ckv_s', 'kpe_s', 'conv_p', 'conv_s', 'lru_p', 'lru_s', 'gla_p', 'gla_s')}
    for l in range(depth):
        lw_ = _layer_weights(l, p, dims)
        zl, qcat, kcat, kcat_t, ckv_n, kpe_n, g = _pre_call(x, lw_, cos_t, sin_t, dims)

        a_p, conv_p, lru_p = _lru_call(zl, 0, Bp, Sp, jnp.zeros((Bp, CONV_WIDTH - 1, lw), F32),
                                       jnp.zeros((Bp, lw), F32), lw_, BF16)
        a_s, lru_s = _lru_seg_call(zl, Tp, Bs, Ss, state_conv[l], state_lru[l], lw_)
        conv_s = zl[Tp:, :lw].reshape(Bs, Ss, lw)[:, Ss - (CONV_WIDTH - 1):]

        c_p, gla_p = _gla_call(g, 0, Bp, Sp, jnp.zeros((Bp, gv_w, gk_w), F32), lw_, BF16, gk_w, gv_w)
        c_s, gla_s = _gla_call(g, Tp, Bs, Ss, _gla_state_to_rows(state_gla[l]), lw_, BF16, gk_w, gv_w)

        b_p = _attn_prompt_call(qcat, kcat, kcat_t, Bp, Sp, lw_, kvr)
        b_s = _attn_sample_call(page_table, qcat[Tp:].reshape(Bs, Ss, -1), ckv_n[Tp:].reshape(Bs, Ss, kvr),
                                kpe_n[Tp:].reshape(Bs, Ss, QK_ROPE), cache_ckv, cache_kpe_t, l, lw_, kvr)

        a = jnp.concatenate([a_p, a_s], axis=0)
        b = jnp.concatenate([b_p, b_s.reshape(Ts, -1).astype(BF16)], axis=0)
        c = jnp.concatenate([c_p, c_s], axis=0)
        x1, hn, route, cnt = _post_call(a, b, c, x, lw_)

        row_tok, row_w, pos, tile_e, n_used, tile_cnt = _route_meta(route, cnt[0, :N_EXPERTS], MOE_TILE, n_tiles)
        ys = _moe_call(hn, row_tok, row_w, tile_e, n_used, tile_cnt, w_gate, w_up, w_down, l)
        x = x1 + ys[pos[:, 0]] + ys[pos[:, 1]]

        outs['ckv_p'].append(ckv_n[:Tp].reshape(Bp, Sp, kvr))
        outs['kpe_p'].append(kpe_n[:Tp].reshape(Bp, Sp, QK_ROPE))
        outs['ckv_s'].append(ckv_n[Tp:].reshape(Bs, Ss, kvr))
        outs['kpe_s'].append(kpe_n[Tp:].reshape(Bs, Ss, QK_ROPE))
        outs['conv_p'].append(conv_p)
        outs['conv_s'].append(conv_s)
        outs['lru_p'].append(lru_p.reshape(Bp, lw))
        outs['lru_s'].append(lru_s.reshape(Bs, lw))
        outs['gla_p'].append(_gla_rows_to_state(gla_p))
        outs['gla_s'].append(_gla_rows_to_state(gla_s))

    g_fin = final_norm_g.reshape(1, D)
    y_p, y_s = _final_call(x, g_fin, 0, Tp), _final_call(x, g_fin, Tp, Ts)
    st = {k: jnp.stack(v) for k, v in outs.items()}
    return (y_p.reshape(Bp, Sp, D), y_s.reshape(Bs, Ss, D),
            st['ckv_p'], st['kpe_p'], st['ckv_s'], st['kpe_s'], st['conv_p'], st['conv_s'],
            st['lru_p'], st['lru_s'], st['gla_p'], st['gla_s'])
```

```python
import functools
import math

import numpy as np
import jax
import jax.numpy as jnp
from jax import lax
from jax.experimental import pallas as pl
from jax.experimental.pallas import tpu as pltpu

F32 = jnp.float32
BF16 = jnp.bfloat16

LRU_BLOCKS = 4
CONV_WIDTH = 4
LRU_C = 8.0
MLA_HEADS = 8
QK_NOPE = 64
QK_ROPE = 32
V_HEAD = 64
ROPE_THETA = 10000.0
GLA_HEADS = 4
GLA_DK = 32
GLA_DV = 64
GLA_TAU = 16.0
GLA_CHUNK = 16
N_GROUPS = 4
EXPERTS_PER_GROUP = 8
N_EXPERTS = N_GROUPS * EXPERTS_PER_GROUP
TOP_K = 2
EPS = 1e-6

LANES = 128
SUBLANES = 8
BF16_ROWS = 16
VMEM_LIMIT_BYTES = 56 * 1024 * 1024

TOKEN_TILE = 512
SEQ_TILE = 256
MOE_TILE = 256
MOE_ROW_GROUP = 8
QK_LOOKAHEAD = 3
SCORE_PAGES = 4
VALUE_PAGES = 2
NEG_BIG = -1e30


def _cparams(*sem):
    return pltpu.CompilerParams(dimension_semantics=sem, vmem_limit_bytes=VMEM_LIMIT_BYTES)


def _rms(x, g):
    return x * lax.rsqrt(jnp.mean(x * x, axis=-1, keepdims=True) + EPS) * g


def _dot(a, b):
    return jnp.dot(a, b, preferred_element_type=F32)


def _dot_nt(a, b):
    return lax.dot_general(a, b, (((1,), (1,)), ((), ())), preferred_element_type=F32)


def _softplus(x):
    return jnp.maximum(x, 0.0) + jnp.log1p(jnp.exp(-jnp.abs(x)))


def _token_tile(refs, n_first):
    if n_first is None:
        return refs[0][...]
    return jnp.where(pl.program_id(0) < n_first, refs[0][...], refs[1][...])


def _token_specs(x, tm):
    if not isinstance(x, tuple):
        return [x], [pl.BlockSpec((tm, x.shape[1]), lambda i: (i, 0))], None, x.shape[0]
    a, b = x
    assert a.shape[0] % tm == 0 and b.shape[0] % tm == 0
    n_first = a.shape[0] // tm
    specs = [pl.BlockSpec((tm, a.shape[1]), lambda i: (jnp.minimum(i, n_first - 1), 0)),
             pl.BlockSpec((tm, b.shape[1]), lambda i: (jnp.maximum(i - n_first, 0), 0))]
    return [a, b], specs, n_first, a.shape[0] + b.shape[0]


def _shift_rows(x, d, fill):
    row = lax.broadcasted_iota(jnp.int32, x.shape, 0)
    return jnp.where(row >= d, pltpu.roll(x, d, 0), fill)


def _pre_kernel(*refs, dims, n_first):
    n_x = 1 if n_first is None else 2
    (g1_ref, win_ref, qg_ref, wq_ref, wabs_ref, kvg_ref, wa2_ref, ba_ref, cos_ref, sin_ref,
     zl_ref, q_ref, k_ref, kt_ref, ckv_ref, kpe_ref, g_ref) = refs[n_x:]
    lw, qr, kvr, gk_w, gv_w = dims
    xn = _rms(_token_tile(refs[:n_x], n_first), g1_ref[...])
    z = _dot(xn.astype(BF16), win_ref[...])
    o = 2 * lw
    zl_ref[...] = z[:, :o]
    cq = z[:, o:o + qr]
    o += qr
    ckv = z[:, o:o + kvr]
    o += kvr
    gqk = z[:, o:o + 2 * gk_w]
    o += 2 * gk_w
    gvo = z[:, o:o + 2 * gv_w]
    o += 2 * gv_w
    tail = z[:, o:o + LANES]
    cos = cos_ref[...]
    sin = sin_ref[...]
    scale = (QK_NOPE + QK_ROPE) ** -0.5 * math.log2(math.e)

    ckv_n = _rms(ckv, kvg_ref[...])
    kpe = tail * cos + pltpu.roll(tail, LANES - QK_ROPE, 1) * sin
    ckv_ref[...] = ckv_n
    kpe_ref[...] = kpe[:, :QK_ROPE]
    k_ref[:, :kvr] = ckv_n.astype(BF16)
    k_ref[:, kvr:] = kpe.astype(BF16)
    ones = jnp.ones((kt_ref.shape[0] - kvr, ckv_n.shape[0]), F32)
    kt_ref[...] = jnp.concatenate([ckv_n.T, ones], axis=0).astype(BF16)

    cqn = _rms(cq, qg_ref[...]).astype(BF16)
    qall = _dot(cqn, wq_ref[...])
    n_nope = MLA_HEADS * QK_NOPE
    q_abs = _dot(qall[:, :n_nope].astype(BF16), wabs_ref[...]) * scale
    for h in range(MLA_HEADS):
        pe = qall[:, n_nope + h * LANES:n_nope + (h + 1) * LANES]
        sw = qall[:, n_nope + (MLA_HEADS + h) * LANES:n_nope + (MLA_HEADS + h + 1) * LANES]
        q_ref[:, 2 * h * LANES:(2 * h + 1) * LANES] = q_abs[:, h * kvr:(h + 1) * kvr].astype(BF16)
        q_ref[:, (2 * h + 1) * LANES:(2 * h + 2) * LANES] = ((pe * cos + sw * sin) * scale).astype(BF16)

    la_pre = _dot(tail.astype(BF16), wa2_ref[...]) + ba_ref[...]
    la = -_softplus(-la_pre) * (1.0 / GLA_TAU)
    g_ref[:, :gk_w] = gqk[:, :gk_w] * (GLA_DK ** -0.5)
    g_ref[:, gk_w:2 * gk_w] = gqk[:, gk_w:]
    g_ref[:, 2 * gk_w:3 * gk_w] = la
    g_ref[:, 3 * gk_w:] = gvo


def _pre_call(x, lw_, cos_t, sin_t, dims):
    lw, qr, kvr, gk_w, gv_w = dims
    tm = TOKEN_TILE
    xs, x_specs, n_first, T = _token_specs(x, tm)
    assert T % tm == 0
    row = lambda i: (i, 0)
    full = lambda i: (0, 0)
    wspec = lambda a: pl.BlockSpec(a.shape, full)
    ws = [lw_['g1'], lw_['w_in'], lw_['q_g'], lw_['w_q'], lw_['w_abs'], lw_['kv_g'], lw_['wa2'], lw_['ba']]
    ins = xs + ws
    in_specs = x_specs + [wspec(a) for a in ws]
    in_specs += [pl.BlockSpec((tm, LANES), row), pl.BlockSpec((tm, LANES), row)]
    out_shape = (
        jax.ShapeDtypeStruct((T, 2 * lw), F32),
        jax.ShapeDtypeStruct((T, 2 * LANES * MLA_HEADS), BF16),
        jax.ShapeDtypeStruct((T, 2 * LANES), BF16),
        jax.ShapeDtypeStruct((T // tm, kvr + BF16_ROWS, tm), BF16),
        jax.ShapeDtypeStruct((T, kvr), F32),
        jax.ShapeDtypeStruct((T, QK_ROPE), F32),
        jax.ShapeDtypeStruct((T, 3 * gk_w + 2 * gv_w), F32),
    )
    out_specs = tuple(pl.BlockSpec((None, s.shape[1], tm), lambda i: (i, 0, 0)) if len(s.shape) == 3
                      else pl.BlockSpec((tm, s.shape[1]), row) for s in out_shape)
    return pl.pallas_call(
        functools.partial(_pre_kernel, dims=dims, n_first=n_first),
        grid=(T // tm,), in_specs=in_specs, out_specs=out_specs, out_shape=out_shape,
        compiler_params=_cparams("parallel"), name="pre_proj",
    )(*ins, cos_t, sin_t)


def _lru_kernel(zl_ref, cbuf_ref, h0_ref, cw_ref, cb_ref, wa_ref, ba_ref, wi_ref, bi_ref, lam_ref, na_ref,
                a_ref, conv_ref, hout_ref, xbuf, hcar, *, ts, lw):
    i = pl.program_id(1)
    last = pl.num_programs(1) - 1
    pad = SUBLANES
    nbuf = CONV_WIDTH - 1

    @pl.when(i == 0)
    def _():
        xbuf[0:pad, :] = jnp.zeros((pad, lw), F32)
        xbuf[pad - nbuf:pad, :] = cbuf_ref[...]
        hcar[...] = h0_ref[...]

    x = zl_ref[:, :lw]
    y = zl_ref[:, lw:]
    xbuf[pad:pad + ts, :] = x
    xc = cb_ref[...] + cw_ref[nbuf:nbuf + 1, :] * x
    for k in range(nbuf):
        xc = xc + cw_ref[k:k + 1, :] * xbuf[pad - nbuf + k:pad - nbuf + k + ts, :]

    @pl.when(i == last)
    def _():
        conv_ref[...] = xbuf[pad + ts - nbuf:pad + ts, :]

    xbuf[0:pad, :] = xbuf[ts:ts + pad, :]

    xb = xc.astype(BF16)
    r = jax.nn.sigmoid(_dot(xb, wa_ref[...]) + ba_ref[...])
    gi = jax.nn.sigmoid(_dot(xb, wi_ref[...]) + bi_ref[...])
    log_a = (-LRU_C) * r * _softplus(-lam_ref[...])
    a = jnp.exp(log_a)
    th = jnp.tanh(log_a)
    u = jnp.sqrt(-2.0 * th / (1.0 - th)) * (gi * xc)

    d = 1
    while d < ts:
        u = a * _shift_rows(u, d, 0.0) + u
        a = a * _shift_rows(a, d, 1.0)
        d *= 2
    h = a * hcar[...] + u
    hcar[...] = h[ts - 1:ts, :]

    @pl.when(i == last)
    def _():
        hout_ref[...] = h[ts - 1:ts, :]

    out_a = h * jax.nn.gelu(y)
    a_ref[...] = _rms(out_a, na_ref[...]).astype(a_ref.dtype)


def _lru_call(zl, row_off, B, S, cbuf, h0, lw_, out_dtype):
    lw = h0.shape[-1]
    ts = min(S, SEQ_TILE)
    assert S % ts == 0 and row_off % ts == 0 and S >= CONV_WIDTH - 1
    n = S // ts
    off = row_off // ts
    full = lambda b, i: (0, 0)
    ws = [lw_['conv_w'], lw_['conv_b'], lw_['lru_wa'], lw_['lru_ba'], lw_['lru_wi'], lw_['lru_bi'],
          lw_['lru_lam'], lw_['n_a']]
    in_specs = [
        pl.BlockSpec((ts, 2 * lw), lambda b, i: (off + b * n + i, 0)),
        pl.BlockSpec((None, CONV_WIDTH - 1, lw), lambda b, i: (b, 0, 0)),
        pl.BlockSpec((None, 1, lw), lambda b, i: (b, 0, 0)),
    ] + [pl.BlockSpec(a.shape, full) for a in ws]
    out_shape = (
        jax.ShapeDtypeStruct((B * S, lw), out_dtype),
        jax.ShapeDtypeStruct((B, CONV_WIDTH - 1, lw), F32),
        jax.ShapeDtypeStruct((B, 1, lw), F32),
    )
    out_specs = (
        pl.BlockSpec((ts, lw), lambda b, i: (b * n + i, 0)),
        pl.BlockSpec((None, CONV_WIDTH - 1, lw), lambda b, i: (b, 0, 0)),
        pl.BlockSpec((None, 1, lw), lambda b, i: (b, 0, 0)),
    )
    return pl.pallas_call(
        functools.partial(_lru_kernel, ts=ts, lw=lw),
        grid=(B, n), in_specs=in_specs, out_specs=out_specs, out_shape=out_shape,
        scratch_shapes=[pltpu.VMEM((ts + SUBLANES, lw), F32), pltpu.VMEM((1, lw), F32)],
        compiler_params=_cparams("parallel", "arbitrary"), name="rg_lru",
    )(zl, cbuf, h0.reshape(B, 1, lw), *ws)


def _lru_seg_kernel(zl_ref, hist_ref, h0_ref, cw_ref, cb_ref, wa_ref, ba_ref, wi_ref, bi_ref, lam_ref, na_ref,
                    a_ref, h_ref, *, seg, lw):
    nbuf = CONV_WIDTH - 1
    x = zl_ref[:, :lw]
    y = zl_ref[:, lw:]
    n = x.shape[0]
    pos = lax.broadcasted_iota(jnp.int32, (n, 1), 0) & (seg - 1)
    hist = hist_ref[...]
    xc = cb_ref[...] + cw_ref[nbuf:nbuf + 1, :] * x
    for k in range(nbuf):
        j = nbuf - k
        prev = jnp.where(pos >= j, pltpu.roll(x, j, 0), pltpu.roll(hist, (j - seg) % n, 0))
        xc = xc + cw_ref[k:k + 1, :] * prev
    xb = xc.astype(BF16)
    r = jax.nn.sigmoid(_dot(xb, wa_ref[...]) + ba_ref[...])
    gi = jax.nn.sigmoid(_dot(xb, wi_ref[...]) + bi_ref[...])
    log_a = (-LRU_C) * r * _softplus(-lam_ref[...])
    a = jnp.exp(log_a)
    th = jnp.tanh(log_a)
    u = jnp.sqrt(-2.0 * th / (1.0 - th)) * (gi * xc)
    d = 1
    while d < seg:
        u = a * jnp.where(pos >= d, pltpu.roll(u, d, 0), 0.0) + u
        a = a * jnp.where(pos >= d, pltpu.roll(a, d, 0), 1.0)
        d *= 2
    h = a * h0_ref[...] + u
    h_ref[...] = h
    a_ref[...] = _rms(h * jax.nn.gelu(y), na_ref[...]).astype(a_ref.dtype)


def _lru_seg_call(zl, row_off, B, S, cbuf, h0, lw_):
    lw = h0.shape[-1]
    n = B * S
    tm = min(n, TOKEN_TILE)
    assert S & (S - 1) == 0 and S >= CONV_WIDTH - 1 and tm % S == 0 and n % tm == 0 and row_off % tm == 0
    off = row_off // tm
    hist = jnp.pad(cbuf, ((0, 0), (S - (CONV_WIDTH - 1), 0), (0, 0))).reshape(n, lw)
    h0_rows = jnp.repeat(h0, S, axis=0)
    full = lambda i: (0, 0)
    row = lambda i: (i, 0)
    ws = [lw_['conv_w'], lw_['conv_b'], lw_['lru_wa'], lw_['lru_ba'], lw_['lru_wi'], lw_['lru_bi'],
          lw_['lru_lam'], lw_['n_a']]
    a, h = pl.pallas_call(
        functools.partial(_lru_seg_kernel, seg=S, lw=lw),
        grid=(n // tm,),
        in_specs=[pl.BlockSpec((tm, 2 * lw), lambda i: (off + i, 0)), pl.BlockSpec((tm, lw), row),
                  pl.BlockSpec((tm, lw), row)] + [pl.BlockSpec(w.shape, full) for w in ws],
        out_specs=(pl.BlockSpec((tm, lw), row), pl.BlockSpec((tm, lw), row)),
        out_shape=(jax.ShapeDtypeStruct((n, lw), BF16), jax.ShapeDtypeStruct((n, lw), F32)),
        compiler_params=_cparams("parallel"), name="rg_lru_seg",
    )(zl, hist, h0_rows, *ws)
    return a, h.reshape(B, S, lw)[:, S - 1]


def _gla_kernel(g_ref, s0_ref, gn_ref, nc_ref, c_ref, sout_ref, st, *, ts, nseq, gk_w, gv_w):
    i = pl.program_id(1)
    last = pl.num_programs(1) - 1
    C = GLA_CHUNK
    rows = min(ts, C)
    n_chunks = max(ts // C, 1)
    log2c = int(math.log2(C))

    @pl.when(i == 0)
    def _():
        st[...] = s0_ref[...]

    dk_sh, dv_sh = int(math.log2(GLA_DK)), int(math.log2(GLA_DV))
    hd = lax.broadcasted_iota(jnp.int32, (gk_w, gv_w), 0) >> dk_sh
    he = lax.broadcasted_iota(jnp.int32, (gk_w, gv_w), 1) >> dv_sh
    same = (hd == he).astype(BF16)
    he_t = lax.broadcasted_iota(jnp.int32, (gv_w, gk_w), 0) >> dv_sh
    hd_t = lax.broadcasted_iota(jnp.int32, (gv_w, gk_w), 1) >> dk_sh
    same_t = (he_t == hd_t).astype(F32)
    sel_t = lax.broadcasted_iota(jnp.int32, (C, C * C), 0)
    sel_r = lax.broadcasted_iota(jnp.int32, (C, C * C), 1) >> log2c
    sel = (sel_t == sel_r).astype(BF16)
    srow = lax.broadcasted_iota(jnp.int32, (C, 1), 0)

    seq_out = []
    for sq in range(nseq):
        blk = g_ref[sq * ts:(sq + 1) * ts, :]
        if rows < C:
            blk = jnp.concatenate([blk, jnp.zeros((C - rows, blk.shape[1]), F32)], axis=0)
        q = blk[:, :gk_w]
        k = blk[:, gk_w:2 * gk_w]
        la = blk[:, 2 * gk_w:3 * gk_w]
        v = blk[:, 3 * gk_w:3 * gk_w + gv_w]
        nrow = n_chunks * C
        pos = lax.broadcasted_iota(jnp.int32, (nrow, 1), 0) & (C - 1)
        cum = la
        d = 1
        while d < C:
            cum = cum + jnp.where(pos >= d, pltpu.roll(cum, d, 0), 0.0)
            d *= 2
        tot = jnp.where(pos == C - 1, cum, 0.0)
        d = 1
        while d < C:
            tot = tot + jnp.where(pos < C - d, pltpu.roll(tot, nrow - d, 0), 0.0)
            d *= 2
        qe = (q * jnp.exp(cum)).astype(BF16)
        kdec = (k * jnp.exp(tot - cum)).astype(BF16)
        dec = jnp.exp(tot)
        vb = v.astype(BF16)

        o_intra, upd = [], []
        for c in range(n_chunks):
            sl = slice(c * C, (c + 1) * C)
            cum_c, q_c, k_c, v_c = cum[sl], q[sl], k[sl], v[sl]
            pieces = []
            for t in range(C):
                diff = jnp.where(srow <= t, cum_c[t:t + 1, :] - cum_c, NEG_BIG)
                pieces.append(q_c[t:t + 1, :] * k_c * jnp.exp(diff))
            w = jnp.concatenate(pieces, axis=0)
            att = _dot(w.astype(BF16), same)
            xv = att * jnp.concatenate([v_c] * C, axis=0)
            o_intra.append(_dot(sel, xv.astype(BF16)))
            upd.append(lax.dot_general(vb[sl], kdec[sl], (((0,), (0,)), ((), ())),
                                       preferred_element_type=F32) * same_t)

        s_t = st[sq]
        outs = []
        for c in range(n_chunks):
            sl = slice(c * C, (c + 1) * C)
            outs.append(o_intra[c] + _dot_nt(qe[sl], s_t.astype(BF16)))
            s_t = s_t * dec[c * C:c * C + 1, :] + upd[c]
        st[sq] = s_t
        seq_out.append(jnp.concatenate(outs, axis=0)[:ts] if n_chunks > 1 else outs[0][:ts])

    o = jnp.concatenate(seq_out, axis=0) if nseq > 1 else seq_out[0]

    @pl.when(i == last)
    def _():
        sout_ref[...] = st[...]

    go = g_ref[:, 3 * gk_w + gv_w:]
    e64 = (lax.broadcasted_iota(jnp.int32, (gv_w, gv_w), 0) >> dv_sh
           == lax.broadcasted_iota(jnp.int32, (gv_w, gv_w), 1) >> dv_sh).astype(BF16)
    osq = o * o
    osq_hi = osq.astype(BF16)
    osq_lo = (osq - osq_hi.astype(F32)).astype(BF16)
    ms = (_dot(osq_hi, e64) + _dot(osq_lo, e64)) * (1.0 / GLA_DV)
    out_c = o * lax.rsqrt(ms + EPS) * gn_ref[...] * (go * jax.nn.sigmoid(go))
    c_ref[...] = _rms(out_c, nc_ref[...]).astype(c_ref.dtype)


def _gla_call(g, row_off, B, S, s0t, lw_, out_dtype, gk_w, gv_w):
    ts = min(S, SEQ_TILE)
    nseq = math.gcd(B, max(SEQ_TILE // (2 * ts), 1)) if ts == S else 1
    rows = nseq * ts
    assert S % ts == 0 and row_off % rows == 0 and (ts % GLA_CHUNK == 0 or ts < GLA_CHUNK)
    n = S // ts
    off = row_off // rows
    gw = g.shape[1]
    full = lambda b, i: (0, 0)
    out_shape = (jax.ShapeDtypeStruct((B * S, gv_w), out_dtype),
                 jax.ShapeDtypeStruct((B, gv_w, gk_w), F32))
    return pl.pallas_call(
        functools.partial(_gla_kernel, ts=ts, nseq=nseq, gk_w=gk_w, gv_w=gv_w),
        grid=(B // nseq, n),
        in_specs=[pl.BlockSpec((rows, gw), lambda b, i: (off + b * n + i, 0)),
                  pl.BlockSpec((nseq, gv_w, gk_w), lambda b, i: (b, 0, 0)),
                  pl.BlockSpec((1, gv_w), full), pl.BlockSpec((1, gv_w), full)],
        out_specs=(pl.BlockSpec((rows, gv_w), lambda b, i: (b * n + i, 0)),
                   pl.BlockSpec((nseq, gv_w, gk_w), lambda b, i: (b, 0, 0))),
        out_shape=out_shape,
        scratch_shapes=[pltpu.VMEM((nseq, gv_w, gk_w), F32)],
        compiler_params=_cparams("parallel", "arbitrary"), name="gla",
    )(g, s0t, lw_['gla_g'], lw_['n_c'])


def _tree(op, xs):
    xs = list(xs)
    while len(xs) > 1:
        xs = [op(xs[i], xs[i + 1]) if i + 1 < len(xs) else xs[i] for i in range(0, len(xs), 2)]
    return xs[0]


def _attn_prompt_kernel(q_ref, k_ref, kt_ref, wuvt_ref, nb_ref, o_ref, m_scr, acc_scr, *, tq, kvr):
    i = pl.program_id(1)
    kw = 2 * LANES
    m_scr[...] = jnp.full(m_scr.shape, NEG_BIG, F32)
    acc_scr[...] = jnp.zeros(acc_scr.shape, F32)

    def block(j, masked):
        kb = k_ref[pl.ds(pl.multiple_of(j * tq, tq), tq), :]
        kbt = kt_ref[j]
        if masked:
            key = lax.broadcasted_iota(jnp.int32, (tq, tq), 0)
            qry = lax.broadcasted_iota(jnp.int32, (tq, tq), 1)
            keep = key <= qry
        qk = lambda h: _dot_nt(kb, q_ref[:, h * kw:(h + 1) * kw])
        ahead = [qk(h) for h in range(QK_LOOKAHEAD)]
        for h in range(MLA_HEADS):
            st = ahead.pop(0)
            if h + QK_LOOKAHEAD < MLA_HEADS:
                ahead.append(qk(h + QK_LOOKAHEAD))
            if masked:
                st = jnp.where(keep, st, NEG_BIG)
            m_old = m_scr[h]
            m_new = jnp.maximum(m_old, jnp.max(st, axis=0, keepdims=True))
            alpha = jnp.exp2(m_old - m_new)
            pt = jnp.exp2(st - m_new).astype(BF16)
            acc_scr[h] = alpha * acc_scr[h] + _dot(kbt, pt)
            m_scr[h] = m_new

    def body(j, carry):
        block(j, False)
        return carry

    lax.fori_loop(0, i, body, 0)
    block(i, True)

    out_t = jnp.zeros((wuvt_ref.shape[0], tq), F32)
    for h in range(MLA_HEADS):
        acc = acc_scr[h]
        o_t = (acc[:kvr, :] / acc[kvr:kvr + 1, :]).astype(BF16)
        out_t = out_t + _dot(wuvt_ref[:, h * kvr:(h + 1) * kvr], o_t)
    ms = jnp.mean(out_t * out_t, axis=0, keepdims=True)
    out_t = out_t * lax.rsqrt(ms + EPS) * nb_ref[...]
    o_ref[...] = out_t.T.astype(o_ref.dtype)


def _attn_prompt_call(qcat, kcat, kcat_t, B, S, lw_, kvr):
    tq = kcat_t.shape[2]
    assert S % tq == 0
    n = S // tq
    mw = lw_['w_uv_rows_t'].shape[0]
    full = lambda b, i: (0, 0)
    return pl.pallas_call(
        functools.partial(_attn_prompt_kernel, tq=tq, kvr=kvr),
        grid=(B, n),
        in_specs=[pl.BlockSpec((tq, qcat.shape[1]), lambda b, i: (b * n + i, 0)),
                  pl.BlockSpec((S, kcat.shape[1]), lambda b, i: (b, 0)),
                  pl.BlockSpec((n, kcat_t.shape[1], tq), lambda b, i: (b, 0, 0)),
                  pl.BlockSpec(lw_['w_uv_rows_t'].shape, full), pl.BlockSpec((mw, 1), full)],
        out_specs=pl.BlockSpec((tq, mw), lambda b, i: (b * n + i, 0)),
        out_shape=jax.ShapeDtypeStruct((B * S, mw), BF16),
        scratch_shapes=[pltpu.VMEM((MLA_HEADS, 1, tq), F32),
                        pltpu.VMEM((MLA_HEADS, kcat_t.shape[1], tq), F32)],
        compiler_params=_cparams("parallel", "arbitrary"), name="attn_prompt",
    )(qcat, kcat, kcat_t, lw_['w_uv_rows_t'], lw_['n_b'].reshape(mw, 1))


def _attn_sample_kernel(pt_ref, q_ref, nckv_ref, nkpe_ref, ckv_hbm, kpe_hbm, wuv_ref, nb_ref, o_ref,
                        q_scr, s_scr, v_scr, ckv_buf, kpe_buf, sem, *, layer, n_pages, sq, kvr, page):
    b = pl.program_id(0)
    nrow = MLA_HEADS * sq

    def page_copies(bb):
        slot = lax.rem(bb, 2)
        out = []
        for j in range(n_pages):
            pg = pt_ref[bb * n_pages + j]
            out.append(pltpu.make_async_copy(ckv_hbm.at[layer, pg], ckv_buf.at[slot, j], sem.at[0, slot]))
            out.append(pltpu.make_async_copy(kpe_hbm.at[layer, pg], kpe_buf.at[slot, j], sem.at[1, slot]))
        return out

    @pl.when(b == 0)
    def _():
        for c in page_copies(b):
            c.start()

    @pl.when(b + 1 < pl.num_programs(0))
    def _():
        for c in page_copies(b + 1):
            c.start()

    for h in range(MLA_HEADS):
        q_scr[h * sq:(h + 1) * sq, :] = q_ref[:, 2 * h * LANES:(2 * h + 2) * LANES].astype(F32)
    qa = q_scr[:, :kvr].astype(BF16)
    qp = q_scr[:, kvr:kvr + QK_ROPE].astype(BF16)

    for c in page_copies(b):
        c.wait()
    slot = lax.rem(b, 2)
    gp = SCORE_PAGES
    for j in range(0, n_pages, gp):
        ck = ckv_buf[slot, j:j + gp].reshape(gp * page, kvr).astype(BF16)
        kp = jnp.concatenate([kpe_buf[slot, j + t] for t in range(gp)], axis=1).astype(BF16)
        v_scr[j * page:(j + gp) * page, :] = ck
        s_scr[:, j * page:(j + gp) * page] = _dot_nt(qa, ck) + _dot(qp, kp)

    zpad = lambda a: jnp.concatenate([a, jnp.zeros((page - sq, a.shape[1]), F32)], axis=0)
    ck_new = zpad(nckv_ref[...]).astype(BF16)
    kp_new = zpad(nkpe_ref[...]).astype(BF16)
    tok = lax.broadcasted_iota(jnp.int32, (nrow, page), 0) & (sq - 1)
    key = lax.broadcasted_iota(jnp.int32, (nrow, page), 1)
    s_new = jnp.where(key <= tok, _dot_nt(qa, ck_new) + _dot_nt(qp, kp_new), NEG_BIG)
    tile = lambda i: s_scr[:, i * page:(i + 1) * page]
    m = jnp.max(_tree(jnp.maximum, [tile(i) for i in range(n_pages)] + [s_new]), axis=-1, keepdims=True)
    p_new = jnp.exp2(s_new - m)
    acc = _dot(p_new.astype(BF16), ck_new)
    psum = p_new
    vp = VALUE_PAGES
    for i in range(0, n_pages, vp):
        p = jnp.exp2(s_scr[:, i * page:(i + vp) * page] - m)
        psum = psum + _tree(jnp.add, [p[:, t * page:(t + 1) * page] for t in range(vp)])
        acc = acc + _dot(p.astype(BF16), v_scr[i * page:(i + vp) * page, :])
    l = jnp.sum(psum, axis=-1, keepdims=True)
    o_lat = (acc / l).astype(BF16)
    mw = wuv_ref.shape[1]
    r = _dot(o_lat, wuv_ref[...])
    rh = lax.broadcasted_iota(jnp.int32, (nrow, mw), 0) >> int(math.log2(sq))
    ch = lax.broadcasted_iota(jnp.int32, (nrow, mw), 1) >> int(math.log2(V_HEAD))
    r = jnp.where(rh == ch, r, 0.0)
    out = r[0:sq, :]
    for h in range(1, MLA_HEADS):
        out = out + r[h * sq:(h + 1) * sq, :]
    o_ref[...] = _rms(out, nb_ref[...])


def _attn_sample_call(page_table, qs, nckv, nkpe, cache_ckv, cache_kpe_t, layer, lw_, kvr):
    B, sq, qw = qs.shape
    n_pages = page_table.shape[1]
    page = cache_ckv.shape[2]
    mw = lw_['w_uv_cat'].shape[1]
    nrow = MLA_HEADS * sq
    assert sq == SUBLANES and sq <= page and n_pages % SCORE_PAGES == 0 and n_pages % VALUE_PAGES == 0
    hbm = pl.BlockSpec(memory_space=pl.ANY)
    in_specs = [pl.BlockSpec((None, sq, qw), lambda b, pt: (b, 0, 0)),
                pl.BlockSpec((None, sq, kvr), lambda b, pt: (b, 0, 0)),
                pl.BlockSpec((None, sq, QK_ROPE), lambda b, pt: (b, 0, 0)),
                hbm, hbm,
                pl.BlockSpec(lw_['w_uv_cat'].shape, lambda b, pt: (0, 0)),
                pl.BlockSpec((1, mw), lambda b, pt: (0, 0))]
    grid_spec = pltpu.PrefetchScalarGridSpec(
        num_scalar_prefetch=1, grid=(B,), in_specs=in_specs,
        out_specs=pl.BlockSpec((None, sq, mw), lambda b, pt: (b, 0, 0)),
        scratch_shapes=[pltpu.VMEM((nrow, 2 * LANES), F32), pltpu.VMEM((nrow, n_pages * page), F32),
                        pltpu.VMEM((n_pages * page, kvr), BF16),
                        pltpu.VMEM((2, n_pages, page, kvr), F32), pltpu.VMEM((2, n_pages, QK_ROPE, page), F32),
                        pltpu.SemaphoreType.DMA((2, 2))])
    return pl.pallas_call(
        functools.partial(_attn_sample_kernel, layer=layer, n_pages=n_pages, sq=sq, kvr=kvr, page=page),
        grid_spec=grid_spec, out_shape=jax.ShapeDtypeStruct((B, sq, mw), F32),
        compiler_params=_cparams("arbitrary"), name="attn_sample",
    )(page_table.reshape(-1), qs, nckv, nkpe, cache_ckv, cache_kpe_t, lw_['w_uv_cat'], lw_['n_b'])


def _post_kernel(a_ref, b_ref, c_ref, *refs, widths, n_first):
    n_x = 1 if n_first is None else 2
    wo_ref, g2_ref, wr_ref, br_ref, x1_ref, hn_ref, rt_ref, cnt_ref, cnt_scr = refs[n_x:]
    wa, wb, wc = widths

    @pl.when(pl.program_id(0) == 0)
    def _():
        cnt_scr[...] = jnp.zeros(cnt_scr.shape, F32)

    mix = _dot(a_ref[...], wo_ref[:wa, :])
    mix = mix + _dot(b_ref[...], wo_ref[wa:wa + wb, :])
    mix = mix + _dot(c_ref[...], wo_ref[wa + wb:, :])
    x1 = _token_tile(refs[:n_x], n_first) + mix
    x1_ref[...] = x1
    hn = _rms(x1, g2_ref[...])
    for sub in range(hn_ref.shape[1]):
        hn_ref[:, sub, :] = hn[:, sub * LANES:(sub + 1) * LANES]
    hn_hi = hn.astype(BF16)
    hn_lo = (hn - hn_hi.astype(F32)).astype(BF16)
    hh = _dot(hn_hi, wr_ref[...])
    logits = (hh[:, :LANES] + hh[:, LANES:] + _dot(hn_lo, wr_ref[:, :LANES])) + br_ref[...]

    lane = lax.broadcasted_iota(jnp.int32, logits.shape, 1)
    lane_f = lane.astype(F32)
    big = float(LANES)
    is_g = lane < N_GROUPS
    gl = jnp.where(is_g, logits, NEG_BIG)
    gmax = jnp.max(gl, axis=-1, keepdims=True)
    gsel = jnp.min(jnp.where(gl == gmax, lane_f, big), axis=-1, keepdims=True)
    gprob = 1.0 / jnp.sum(jnp.where(is_g, jnp.exp(gl - gmax), 0.0), axis=-1, keepdims=True)
    lo = N_GROUPS + gsel * EXPERTS_PER_GROUP
    el = jnp.where((lane_f >= lo) & (lane_f < lo + EXPERTS_PER_GROUP), logits, NEG_BIG)
    v1 = jnp.max(el, axis=-1, keepdims=True)
    i1 = jnp.min(jnp.where(el == v1, lane_f, big), axis=-1, keepdims=True)
    el2 = jnp.where(lane_f == i1, NEG_BIG, el)
    v2 = jnp.max(el2, axis=-1, keepdims=True)
    i2 = jnp.min(jnp.where(el2 == v2, lane_f, big), axis=-1, keepdims=True)
    e21 = jnp.exp(v2 - v1)
    w1 = gprob / (1.0 + e21)
    w2 = w1 * e21
    e1 = i1 - N_GROUPS
    e2 = i2 - N_GROUPS

    oh1 = (lane_f == e1).astype(F32)
    oh2 = (lane_f == e2).astype(F32)
    both = oh1 + oh2
    tm = both.shape[0]
    tri = (lax.broadcasted_iota(jnp.int32, (tm, tm), 0) >= lax.broadcasted_iota(jnp.int32, (tm, tm), 1))
    incl = _dot(tri.astype(BF16), both.astype(BF16))
    base = cnt_scr[0:1, :] + incl - both
    r1 = jnp.sum(oh1 * base, axis=-1, keepdims=True)
    r2 = jnp.sum(oh2 * base, axis=-1, keepdims=True)
    cnt_scr[...] = cnt_scr[...] + incl[tm - 1:tm, :]
    cnt_ref[...] = cnt_scr[...]

    rt = jnp.where(lane == 0, e1, 0.0)
    rt = jnp.where(lane == 1, e2, rt)
    rt = jnp.where(lane == 2, w1, rt)
    rt = jnp.where(lane == 3, w2, rt)
    rt = jnp.where(lane == 4, r1, rt)
    rt = jnp.where(lane == 5, r2, rt)
    rt_ref[...] = rt


def _post_call(a, b, c, x, lw_):
    tm = TOKEN_TILE
    xs, x_specs, n_first, T = _token_specs(x, tm)
    D = xs[0].shape[1]
    row = lambda i: (i, 0)
    full = lambda i: (0, 0)
    widths = (a.shape[1], b.shape[1], c.shape[1])
    ws = [lw_['w_out'], lw_['g2'], lw_['w_r'], lw_['b_r']]
    out_shape = (jax.ShapeDtypeStruct((T, D), F32), jax.ShapeDtypeStruct((T, D // LANES, LANES), F32),
                 jax.ShapeDtypeStruct((T, LANES), F32), jax.ShapeDtypeStruct((SUBLANES, LANES), F32))
    return pl.pallas_call(
        functools.partial(_post_kernel, widths=widths, n_first=n_first),
        grid=(T // tm,),
        in_specs=[pl.BlockSpec((tm, w), row) for w in widths] + x_specs
        + [pl.BlockSpec(w.shape, full) for w in ws],
        out_specs=tuple(pl.BlockSpec((tm,) + s.shape[1:], (lambda i: (i, 0, 0)) if len(s.shape) == 3 else row)
                        for s in out_shape[:3])
        + (pl.BlockSpec((SUBLANES, LANES), full),),
        out_shape=out_shape, scratch_shapes=[pltpu.VMEM((SUBLANES, LANES), F32)],
        compiler_params=_cparams("arbitrary"), name="post_proj",
    )(a, b, c, *xs, *ws)


def _dest_kernel(rt_ref, ps_ref, d_ref):
    rt = rt_ref[...]
    lane = lax.broadcasted_iota(jnp.int32, rt.shape, 1)
    lane_f = lane.astype(F32)
    pick = lambda k: jnp.sum(jnp.where(lane == k, rt, 0.0), axis=-1, keepdims=True)
    ps = ps_ref[...]
    d1 = jnp.sum(jnp.where(lane_f == pick(0), ps, 0.0), axis=-1, keepdims=True) + pick(2 * TOP_K)
    d2 = jnp.sum(jnp.where(lane_f == pick(1), ps, 0.0), axis=-1, keepdims=True) + pick(2 * TOP_K + 1)
    d_ref[...] = jnp.where(lane == 0, d1, jnp.where(lane == 1, d2, 0.0))


def _dest_call(route, pstart_row):
    T = route.shape[0]
    tm = TOKEN_TILE
    return pl.pallas_call(
        _dest_kernel, grid=(T // tm,),
        in_specs=[pl.BlockSpec((tm, LANES), lambda i: (i, 0)), pl.BlockSpec((1, LANES), lambda i: (0, 0))],
        out_specs=pl.BlockSpec((tm, LANES), lambda i: (i, 0)),
        out_shape=jax.ShapeDtypeStruct((T, LANES), F32), compiler_params=_cparams("parallel"), name="moe_dest",
    )(route, pstart_row)


def _moe_kernel(te_ref, nt_ref, cnt_ref, tok_ref, hn_hbm, w_ref, wg_ref, wu_ref, wd_ref, y_ref, xbuf, sem):
    i = pl.program_id(0)
    tm = xbuf.shape[1]
    nt = nt_ref[0]
    n_groups = tm // MOE_ROW_GROUP

    def start_rows(t):
        slot = lax.rem(t, 2)

        def body(g, carry):
            @pl.when(g * MOE_ROW_GROUP < cnt_ref[t])
            def _():
                for k in range(MOE_ROW_GROUP):
                    r = g * MOE_ROW_GROUP + k
                    pltpu.make_async_copy(hn_hbm.at[tok_ref[t * tm + r]], xbuf.at[slot, r],
                                          sem.at[slot]).start(priority=k % 2)
            return carry

        lax.fori_loop(0, n_groups, body, 0)

    @pl.when(i == 0)
    def _():
        xbuf[...] = jnp.zeros(xbuf.shape, F32)
        start_rows(i)

    @pl.when(i + 1 < nt)
    def _():
        start_rows(i + 1)

    @pl.when(i < nt)
    def _():
        slot = lax.rem(i, 2)

        def wait_group(g, carry):
            @pl.when(g * MOE_ROW_GROUP < cnt_ref[i])
            def _():
                grp = pl.ds(0, MOE_ROW_GROUP)
                pltpu.make_async_copy(hn_hbm.at[grp], xbuf.at[slot, grp], sem.at[slot]).wait()
            return carry

        lax.fori_loop(0, n_groups, wait_group, 0)
        x = jnp.concatenate([xbuf[slot, :, sub, :] for sub in range(xbuf.shape[2])], axis=1).astype(BF16)
        hg = _dot(x, wg_ref[...].astype(BF16))
        hu = _dot(x, wu_ref[...].astype(BF16))
        act = hg * jax.nn.sigmoid(hg) * hu * w_ref[...]
        y_ref[...] = _dot(act.astype(BF16), wd_ref[...].astype(BF16))

    @pl.when(i >= nt)
    def _():
        y_ref[...] = jnp.zeros(y_ref.shape, F32)


def _moe_call(hn, row_tok, row_w, tile_e, n_used, tile_cnt, w_gate, w_up, w_down, layer):
    D = w_gate.shape[-2]
    tm = MOE_TILE
    F = w_gate.shape[-1]
    n_tiles = tile_e.shape[0]
    grid_spec = pltpu.PrefetchScalarGridSpec(
        num_scalar_prefetch=4, grid=(n_tiles,),
        in_specs=[pl.BlockSpec(memory_space=pl.ANY),
                  pl.BlockSpec((tm, 1), lambda i, te, nt, cnt, tok: (i, 0)),
                  pl.BlockSpec((None, None, D, F), lambda i, te, nt, cnt, tok: (layer, te[i], 0, 0)),
                  pl.BlockSpec((None, None, D, F), lambda i, te, nt, cnt, tok: (layer, te[i], 0, 0)),
                  pl.BlockSpec((None, None, F, D), lambda i, te, nt, cnt, tok: (layer, te[i], 0, 0))],
        out_specs=pl.BlockSpec((tm, D), lambda i, te, nt, cnt, tok: (i, 0)),
        scratch_shapes=[pltpu.VMEM((2, tm) + hn.shape[1:], F32), pltpu.SemaphoreType.DMA((2,))])
    return pl.pallas_call(
        _moe_kernel, grid_spec=grid_spec, out_shape=jax.ShapeDtypeStruct((n_tiles * tm, D), F32),
        compiler_params=_cparams("arbitrary"), name="moe_experts",
    )(tile_e, n_used, tile_cnt, row_tok, hn, row_w, w_gate, w_up, w_down)


def _route_meta(route, counts_f, tm, n_tiles):
    T = route.shape[0]
    n_assign = TOP_K * T
    e = route[:, :TOP_K].astype(jnp.int32).reshape(-1)
    w = route[:, TOP_K:2 * TOP_K].reshape(-1)
    order = jnp.argsort(e, stable=True).astype(jnp.int32)
    counts = counts_f.astype(jnp.int32)
    zero = jnp.zeros((1,), jnp.int32)
    start = jnp.concatenate([zero, jnp.cumsum(counts)])
    pstart = jnp.concatenate([zero, jnp.cumsum(((counts + tm - 1) // tm) * tm)])
    n_used = (pstart[N_EXPERTS] // tm).reshape(1)
    tile_lo = jnp.arange(n_tiles, dtype=jnp.int32) * tm
    tile_e = jnp.sum((pstart[None, 1:] <= tile_lo[:, None]).astype(jnp.int32), axis=1)
    tile_e = jnp.minimum(tile_e, N_EXPERTS - 1)
    oh = (tile_e[:, None] == jnp.arange(N_EXPERTS, dtype=jnp.int32)[None, :]).astype(jnp.int32)
    pick = lambda tab: jnp.sum(oh * tab[None, :N_EXPERTS], axis=1)
    k0 = tile_lo - pick(pstart)
    tile_cnt = jnp.clip(pick(counts) - k0, 0, tm)
    k = k0[:, None] + jnp.arange(tm, dtype=jnp.int32)[None, :]
    valid = (k < pick(counts)[:, None]).reshape(-1)
    idx = jnp.clip(pick(start)[:, None] + k, 0, n_assign - 1).reshape(-1)
    src = order[idx]
    row_tok = jnp.where(valid, src // TOP_K, 0)
    row_w = jnp.where(valid, w[src], 0.0)
    ps_row = jnp.pad(pstart[:N_EXPERTS].astype(F32), (0, LANES - N_EXPERTS)).reshape(1, LANES)
    pos = _dest_call(route, ps_row)[:, :TOP_K].astype(jnp.int32)
    return row_tok, row_w.reshape(-1, 1), pos, tile_e, n_used, tile_cnt


def _final_kernel(x_ref, g_ref, o_ref):
    o_ref[...] = _rms(x_ref[...], g_ref[...])


def _final_call(x, g, row_off, n_rows):
    D = x.shape[1]
    tm = TOKEN_TILE
    assert row_off % tm == 0 and n_rows % tm == 0
    off = row_off // tm
    return pl.pallas_call(
        _final_kernel, grid=(n_rows // tm,),
        in_specs=[pl.BlockSpec((tm, D), lambda i: (off + i, 0)), pl.BlockSpec((1, D), lambda i: (0, 0))],
        out_specs=pl.BlockSpec((tm, D), lambda i: (i, 0)),
        out_shape=jax.ShapeDtypeStruct((n_rows, D), F32), compiler_params=_cparams("parallel"), name="final_norm",
    )(x, g)


def _block_diag(w):
    n, c, d = w.shape
    return jnp.einsum('ncd,nm->ncmd', w, jnp.eye(n, dtype=w.dtype)).reshape(n * c, n * d)


def _half_swap(w):
    half = w.shape[-1] // 2
    return jnp.concatenate([w[..., half:], w[..., :half]], axis=-1)


def _layer_weights(l, p, dims):
    lw, qr, kvr, gk_w, gv_w = dims
    D = p['w_in'].shape[1]
    row = lambda a: a.reshape(1, -1).astype(F32)
    w_in = p['w_in'][l]
    sizes = (lw, lw, qr, kvr, QK_ROPE, gk_w, gk_w, gv_w, p['gla_wa2'].shape[1], gv_w)
    x_lru, y_lru, c_q, c_kv, k_pe, g_q, g_k, g_v, g_a, g_o = jnp.split(w_in, list(np.cumsum(sizes)[:-1]), axis=1)
    tail_pad = LANES - 2 * QK_ROPE - g_a.shape[1]
    tail = jnp.concatenate([k_pe, _half_swap(k_pe), g_a, jnp.zeros((D, tail_pad), F32)], axis=1)
    w_in_p = jnp.concatenate([x_lru, y_lru, c_q, c_kv, g_q, g_k, g_v, g_o, tail], axis=1).astype(BF16)

    w_uq = p['w_uq'][l].reshape(qr, MLA_HEADS, QK_NOPE + QK_ROPE)
    nope = w_uq[:, :, :QK_NOPE].reshape(qr, MLA_HEADS * QK_NOPE)
    pe = w_uq[:, :, QK_NOPE:]
    widen = lambda a: jnp.pad(a, ((0, 0), (0, 0), (0, LANES - QK_ROPE))).reshape(qr, MLA_HEADS * LANES)
    w_q = jnp.concatenate([nope, widen(pe), widen(_half_swap(pe))], axis=1).astype(BF16)
    w_abs = _block_diag(jnp.transpose(p['w_uk'][l], (1, 2, 0))).astype(BF16)
    wa2 = jnp.zeros((LANES, gk_w), F32).at[2 * QK_ROPE:2 * QK_ROPE + g_a.shape[1]].set(p['gla_wa2'][l]).astype(BF16)

    w_uv = p['w_uv'][l]
    mw = MLA_HEADS * V_HEAD
    w_uv_cat = w_uv.reshape(kvr, mw)
    w_uv_rows = _block_diag(jnp.transpose(w_uv, (1, 0, 2)))
    n_a, n_b, n_c = jnp.split(p['out_norm_g'][l], [lw, lw + mw])
    n_r = N_GROUPS + N_EXPERTS
    w_r = jnp.concatenate([p['router_wg'][l], p['router_we'][l], jnp.zeros((D, LANES - n_r), F32)], axis=1)
    b_r = jnp.concatenate([p['router_bg'][l], p['router_be'][l], jnp.zeros((LANES - n_r,), F32)])
    return dict(
        g1=row(p['norm1_g'][l]), w_in=w_in_p, q_g=row(p['q_norm_g'][l]), w_q=w_q, w_abs=w_abs,
        kv_g=row(p['kv_norm_g'][l]), wa2=wa2, ba=row(p['gla_ba'][l]),
        conv_w=p['conv_w'][l], conv_b=row(p['conv_b'][l]),
        lru_wa=_block_diag(p['lru_wa'][l]).astype(BF16), lru_ba=row(p['lru_ba'][l]),
        lru_wi=_block_diag(p['lru_wi'][l]).astype(BF16), lru_bi=row(p['lru_bi'][l]),
        lru_lam=row(p['lru_lambda'][l]), n_a=row(n_a), n_b=row(n_b), n_c=row(n_c),
        gla_g=row(jnp.tile(p['gla_norm_g'][l], GLA_HEADS)),
        w_uv_cat=w_uv_cat.astype(BF16), w_uv_rows_t=w_uv_rows.T.astype(BF16),
        w_out=p['w_out'][l].astype(BF16), g2=row(p['norm2_g'][l]),
        w_r=jnp.concatenate([w_r.astype(BF16), (w_r - w_r.astype(BF16).astype(F32)).astype(BF16)], axis=1),
        b_r=row(b_r),
    )


def _rope_tables(positions):
    half = QK_ROPE // 2
    inv = ROPE_THETA ** (-np.arange(half, dtype=np.float64) / half)
    ang = np.asarray(positions, np.float64)[:, None] * inv
    zeros = np.zeros((ang.shape[0], LANES - QK_ROPE))
    cos = np.concatenate([np.cos(ang), np.cos(ang), zeros], axis=1)
    sin = np.concatenate([-np.sin(ang), np.sin(ang), zeros], axis=1)
    return jnp.asarray(cos, F32), jnp.asarray(sin, F32)


def _gla_state_to_rows(s):
    B = s.shape[0]
    eye = jnp.eye(GLA_HEADS, dtype=s.dtype)
    return jnp.einsum('bhde,hg->bhegd', s, eye).reshape(B, GLA_HEADS * GLA_DV, GLA_HEADS * GLA_DK)


def _gla_rows_to_state(st):
    B = st.shape[0]
    s5 = st.reshape(B, GLA_HEADS, GLA_DV, GLA_HEADS, GLA_DK)
    blocks = jnp.stack([s5[:, h, :, h, :] for h in range(GLA_HEADS)], axis=1)
    return jnp.swapaxes(blocks, -1, -2)


def kernel(x_prompt, x_sample, cache_ckv, cache_kpe, page_table, state_conv, state_lru, state_gla, norm1_g, w_in, conv_w, conv_b, lru_wa, lru_ba, lru_wi, lru_bi, lru_lambda, q_norm_g, w_uq, kv_norm_g, w_uk, w_uv, gla_wa2, gla_ba, gla_norm_g, out_norm_g, w_out, norm2_g, router_wg, router_bg, router_we, router_be, w_gate, w_up, w_down, final_norm_g):
    p = dict(norm1_g=norm1_g, w_in=w_in, conv_w=conv_w, conv_b=conv_b, lru_wa=lru_wa, lru_ba=lru_ba,
             lru_wi=lru_wi, lru_bi=lru_bi, lru_lambda=lru_lambda, q_norm_g=q_norm_g, w_uq=w_uq,
             kv_norm_g=kv_norm_g, w_uk=w_uk, w_uv=w_uv, gla_wa2=gla_wa2, gla_ba=gla_ba,
             gla_norm_g=gla_norm_g, out_norm_g=out_norm_g, w_out=w_out, norm2_g=norm2_g,
             router_wg=router_wg, router_bg=router_bg, router_we=router_we, router_be=router_be)
    Bp, Sp, D = x_prompt.shape
    Bs, Ss, _ = x_sample.shape
    depth = w_in.shape[0]
    lw = state_lru.shape[-1]
    qr = q_norm_g.shape[-1]
    kvr = kv_norm_g.shape[-1]
    gk_w = GLA_HEADS * GLA_DK
    gv_w = GLA_HEADS * GLA_DV
    dims = (lw, qr, kvr, gk_w, gv_w)
    assert kvr == LANES and gk_w == LANES
    Tp, Ts = Bp * Sp, Bs * Ss
    T = Tp + Ts
    past = page_table.shape[1] * cache_ckv.shape[2]

    cos_t, sin_t = _rope_tables(np.concatenate([np.tile(np.arange(Sp), Bp), np.tile(past + np.arange(Ss), Bs)]))
    cache_kpe_t = jnp.swapaxes(cache_kpe, 2, 3)
    n_tiles = (TOP_K * T + N_EXPERTS * (MOE_TILE - 1)) // MOE_TILE + 1

    x = (x_prompt.reshape(Tp, D), x_sample.reshape(Ts, D))
    outs = {k: [] for k in ('ckv_p', 'kpe_p', 'ckv_s', 'kpe_s', 'conv_p', 'conv_s', 'lru_p', 'lru_s', 'gla_p', 'gla_s')}
    for l in range(depth):
        lw_ = _layer_weights(l, p, dims)
        zl, qcat, kcat, kcat_t, ckv_n, kpe_n, g = _pre_call(x, lw_, cos_t, sin_t, dims)

        a_p, conv_p, lru_p = _lru_call(zl, 0, Bp, Sp, jnp.zeros((Bp, CONV_WIDTH - 1, lw), F32),
                                       jnp.zeros((Bp, lw), F32), lw_, BF16)
        a_s, lru_s = _lru_seg_call(zl, Tp, Bs, Ss, state_conv[l], state_lru[l], lw_)
        conv_s = zl[Tp:, :lw].reshape(Bs, Ss, lw)[:, Ss - (CONV_WIDTH - 1):]

        c_p, gla_p = _gla_call(g, 0, Bp, Sp, jnp.zeros((Bp, gv_w, gk_w), F32), lw_, BF16, gk_w, gv_w)
        c_s, gla_s = _gla_call(g, Tp, Bs, Ss, _gla_state_to_rows(state_gla[l]), lw_, BF16, gk_w, gv_w)

        b_p = _attn_prompt_call(qcat, kcat, kcat_t, Bp, Sp, lw_, kvr)
        b_s = _attn_sample_call(page_table, qcat[Tp:].reshape(Bs, Ss, -1), ckv_n[Tp:].reshape(Bs, Ss, kvr),
                                kpe_n[Tp:].reshape(Bs, Ss, QK_ROPE), cache_ckv, cache_kpe_t, l, lw_, kvr)

        a = jnp.concatenate([a_p, a_s], axis=0)
        b = jnp.concatenate([b_p, b_s.reshape(Ts, -1).astype(BF16)], axis=0)
        c = jnp.concatenate([c_p, c_s], axis=0)
        x1, hn, route, cnt = _post_call(a, b, c, x, lw_)

        row_tok, row_w, pos, tile_e, n_used, tile_cnt = _route_meta(route, cnt[0, :N_EXPERTS], MOE_TILE, n_tiles)
        ys = _moe_call(hn, row_tok, row_w, tile_e, n_used, tile_cnt, w_gate, w_up, w_down, l)
        x = x1 + ys[pos[:, 0]] + ys[pos[:, 1]]

        outs['ckv_p'].append(ckv_n[:Tp].reshape(Bp, Sp, kvr))
        outs['kpe_p'].append(kpe_n[:Tp].reshape(Bp, Sp, QK_ROPE))
        outs['ckv_s'].append(ckv_n[Tp:].reshape(Bs, Ss, kvr))
        outs['kpe_s'].append(kpe_n[Tp:].reshape(Bs, Ss, QK_ROPE))
        outs['conv_p'].append(conv_p)
        outs['conv_s'].append(conv_s)
        outs['lru_p'].append(lru_p.reshape(Bp, lw))
        outs['lru_s'].append(lru_s.reshape(Bs, lw))
        outs['gla_p'].append(_gla_rows_to_state(gla_p))
        outs['gla_s'].append(_gla_rows_to_state(gla_s))

    g_fin = final_norm_g.reshape(1, D)
    y_p, y_s = _final_call(x, g_fin, 0, Tp), _final_call(x, g_fin, Tp, Ts)
    st = {k: jnp.stack(v) for k, v in outs.items()}
    return (y_p.reshape(Bp, Sp, D), y_s.reshape(Bs, Ss, D),
            st['ckv_p'], st['kpe_p'], st['ckv_s'], st['kpe_s'], st['conv_p'], st['conv_s'],
            st['lru_p'], st['lru_s'], st['gla_p'], st['gla_s'])
```
